```python
import math
import jax
import jax.numpy as jnp
from jax import lax
import numpy as np


D_MODEL = 2048
BATCH = 2
SEQ = 16384
DEPTH = 2

GRID_W = 64
CTX_LEN = 256
N_EVEN = (DEPTH + 1) // 2
N_ODD = DEPTH // 2
NORM_EPS = 1e-6
DIFF_HEADS = 8
DIFF_QK_DIM = 64
DIFF_V_DIM = 2 * DIFF_QK_DIM
DIFF_SCALE = DIFF_QK_DIM ** -0.5
A_WIDTH = DIFF_HEADS * DIFF_V_DIM
Q_COLS = DIFF_HEADS * 2 * DIFF_QK_DIM
ROPE_THETA = 10000.0
ROPE_AXIS_DIM = DIFF_QK_DIM // 2
Q_BLOCK = 128
SUBLN_EPS = 1e-5
SGU_GROUPS = 8
SGU_CHUNK = 128
SGU_CH = 128
B_WIDTH = SGU_GROUPS * SGU_CH
LN_EPS = 1e-5
EVEN_SPLITS = (Q_COLS, 2 * Q_COLS, 2 * Q_COLS + A_WIDTH, 2 * Q_COLS + A_WIDTH + B_WIDTH)
EVEN_IN = 2 * Q_COLS + A_WIDTH + 2 * B_WIDTH
EVEN_OUT = A_WIDTH + B_WIDTH
HYENA_ORDER = 2
HYENA_WIDTH = D_MODEL
SHORT_CONV = 3
FILTER_EMB = 33
FILTER_BANDS = (FILTER_EMB - 1) // 2
FILTER_HIDDEN = 64
DECAY_TARGET = 1e-2
FAST_DECAY_PCT = 0.3
SLOW_DECAY_PCT = 1.5
MAX_DECAY = math.log(DECAY_TARGET) / FAST_DECAY_PCT
MIN_DECAY = math.log(DECAY_TARGET) / SLOW_DECAY_PCT
D_FF = ((8 * D_MODEL + 3 * 256 - 1) // (3 * 256)) * 256

kernel_name = 'hybrid_diffattn_sgu_hyena_block'


def rms_norm(x, g, eps=NORM_EPS):
    x32 = x.astype(jnp.float32)
    y = x32 * lax.rsqrt(jnp.mean(x32 * x32, axis=-1, keepdims=True) + eps)
    return (y * g.astype(jnp.float32)).astype(x.dtype)


def layer_norm(x, g, b, eps=LN_EPS):
    x32 = x.astype(jnp.float32)
    mu = jnp.mean(x32, axis=-1, keepdims=True)
    var = jnp.mean(jnp.square(x32 - mu), axis=-1, keepdims=True)
    y = (x32 - mu) * lax.rsqrt(var + eps)
    return (y * g.astype(jnp.float32) + b.astype(jnp.float32)).astype(x.dtype)


def adaln(cond, w, b):
    m = jax.nn.silu(cond) @ w + b
    parts = jnp.split(m, 6, axis=-1)
    if m.ndim == 2:
        parts = [p[:, None, :] for p in parts]
    return parts


def modulate(h, shift, scale):
    return h * (1.0 + scale) + shift


def swiglu(h, w_gate, w_up, w_down):
    return (jax.nn.silu(h @ w_gate) * (h @ w_up)) @ w_down


def axial_rope_tables(n_tokens):
    rows = n_tokens // GRID_W
    row = jnp.repeat(jnp.arange(rows), GRID_W).astype(jnp.float32)
    col = jnp.tile(jnp.arange(GRID_W), rows).astype(jnp.float32)
    inv = 1.0 / (ROPE_THETA ** (jnp.arange(0, ROPE_AXIS_DIM, 2, dtype=jnp.float32) / ROPE_AXIS_DIM))
    ang_r = row[:, None] * inv
    ang_c = col[:, None] * inv
    return (jnp.cos(ang_r), jnp.sin(ang_r), jnp.cos(ang_c), jnp.sin(ang_c))


def _rotate(x, cos, sin):
    x1, x2 = jnp.split(x, 2, axis=-1)
    return jnp.concatenate([x1 * cos - x2 * sin, x2 * cos + x1 * sin], axis=-1)


def apply_axial_rope(x, tables):
    cr, sr, cc, sc = [t[None, :, None, None, :].astype(x.dtype) for t in tables]
    xr, xc = x[..., :ROPE_AXIS_DIM], x[..., ROPE_AXIS_DIM:]
    return jnp.concatenate([_rotate(xr, cr, sr), _rotate(xc, cc, sc)], axis=-1)


def diff_attend(q, k, v, lam):
    s = jnp.einsum('bqhmd,bkhmd->bhmqk', q, k).astype(jnp.float32) * DIFF_SCALE
    p = jax.nn.softmax(s, axis=-1)
    w = (p[:, :, 0] - lam * p[:, :, 1]).astype(v.dtype)
    return jnp.einsum('bhqk,bkhe->bqhe', w, v)


def diff_attention_blocked(q, k, v, lam):
    b, s = q.shape[0], q.shape[1]
    nblk = s // Q_BLOCK
    qb = q.reshape(b, nblk, Q_BLOCK, DIFF_HEADS, 2, DIFF_QK_DIM).swapaxes(0, 1)
    ob = lax.map(lambda qq: diff_attend(qq, k, v, lam), qb)
    return ob.swapaxes(0, 1).reshape(b, s, DIFF_HEADS, DIFF_V_DIM)


def diff_post(o, subln_g, lam_init):
    b, l = o.shape[0], o.shape[1]
    o = rms_norm(o, subln_g, eps=SUBLN_EPS) * (1.0 - lam_init)
    return o.reshape(b, l, A_WIDTH)


def spatial_gating(u, vg, norm_g, norm_b, w_s, b_s):
    b, l = u.shape[0], u.shape[1]
    n = l // SGU_CHUNK
    vv = layer_norm(vg.reshape(b, n, SGU_CHUNK, SGU_GROUPS, SGU_CH), norm_g, norm_b)
    mixed = jnp.einsum('gpq,bnqgc->bnpgc', w_s, vv) + b_s.T[:, :, None]
    return u * mixed.reshape(b, l, B_WIDTH)


def even_mixer(h_ctx, h_lat, w_in, w_out, lam_params, subln_g, sgu_norm_g, sgu_norm_b, sgu_w, sgu_b, layer_idx, need_ctx):
    lam_init = 0.8 - 0.6 * math.exp(-0.3 * layer_idx)
    lp = lam_params.astype(jnp.float32)
    lam = jnp.exp(jnp.sum(lp[0] * lp[1])) - jnp.exp(jnp.sum(lp[2] * lp[3])) + lam_init
    b, s = h_lat.shape[0], h_lat.shape[1]
    n_ctx = h_ctx.shape[1]
    q_l, k_l, v_l, u_l, g_l = jnp.split(h_lat @ w_in, EVEN_SPLITS, axis=-1)
    rope = axial_rope_tables(s)
    q_l = apply_axial_rope(q_l.reshape(b, s, DIFF_HEADS, 2, DIFF_QK_DIM), rope)
    k_l = apply_axial_rope(k_l.reshape(b, s, DIFF_HEADS, 2, DIFF_QK_DIM), rope)
    v_l = v_l.reshape(b, s, DIFF_HEADS, DIFF_V_DIM)
    if need_ctx:
        q_c, k_c, v_c, u_c, g_c = jnp.split(h_ctx @ w_in, EVEN_SPLITS, axis=-1)
    else:
        k_c, v_c = jnp.split(h_ctx @ w_in[:, Q_COLS:2 * Q_COLS + A_WIDTH], [Q_COLS], axis=-1)
    k_c = k_c.reshape(b, n_ctx, DIFF_HEADS, 2, DIFF_QK_DIM)
    v_c = v_c.reshape(b, n_ctx, DIFF_HEADS, DIFF_V_DIM)
    k_all = jnp.concatenate([k_c, k_l], axis=1)
    v_all = jnp.concatenate([v_c, v_l], axis=1)
    a_l = diff_post(diff_attention_blocked(q_l, k_all, v_all, lam), subln_g, lam_init)
    s_l = spatial_gating(jax.nn.gelu(u_l), jax.nn.gelu(g_l), sgu_norm_g, sgu_norm_b, sgu_w, sgu_b)
    y_lat = jnp.concatenate([a_l, s_l], axis=-1) @ w_out
    y_ctx = None
    if need_ctx:
        q_c = q_c.reshape(b, n_ctx, DIFF_HEADS, 2, DIFF_QK_DIM)
        a_c = diff_post(diff_attend(q_c, k_c, v_c, lam), subln_g, lam_init)
        s_c = spatial_gating(jax.nn.gelu(u_c), jax.nn.gelu(g_c), sgu_norm_g, sgu_norm_b, sgu_w, sgu_b)
        y_ctx = jnp.concatenate([a_c, s_c], axis=-1) @ w_out
    return y_ctx, y_lat


def short_conv(z, w, b):
    l = z.shape[1]
    half = SHORT_CONV // 2
    zp = jnp.pad(z, ((0, 0), (half, half), (0, 0)))
    y = b
    for j in range(SHORT_CONV):
        y = y + zp[:, j:j + l] * w[j]
    return y


def hyena_filters(l, w1, b1, w2, b2, w3, b3, freq, w4):
    f32 = jnp.float32
    t = jnp.linspace(0.0, 1.0, l, dtype=f32)[:, None]
    w = 2.0 * math.pi * jnp.arange(l, dtype=f32)[:, None] / l
    f = jnp.linspace(1e-4, FILTER_BANDS - 1, FILTER_BANDS, dtype=f32)[None, :]
    feats = jnp.concatenate([t, jnp.cos(f * w), -jnp.sin(f * w)], axis=-1)
    fr = freq.astype(f32)
    h = jnp.sin(fr[0] * (feats @ w1.astype(f32) + b1.astype(f32)))
    h = jnp.sin(fr[1] * (h @ w2.astype(f32) + b2.astype(f32)))
    h = jnp.sin(fr[2] * (h @ w3.astype(f32) + b3.astype(f32)))
    h = (h @ w4.astype(f32)).reshape(l, HYENA_ORDER, 2, HYENA_WIDTH)
    deltas = jnp.abs(jnp.linspace(MIN_DECAY, MAX_DECAY, HYENA_WIDTH, dtype=f32))
    h = h * jnp.exp(-t * deltas)[:, None, None, :]
    fwd, bwd = h[:, :, 0], h[:, :, 1]
    k2 = jnp.concatenate([fwd, jnp.zeros((1, HYENA_ORDER, HYENA_WIDTH), f32), bwd[1:][::-1]], axis=0)
    k2 = k2 / jnp.sum(jnp.abs(k2), axis=0, keepdims=True)
    return jnp.fft.rfft(k2, axis=0)


def fft_conv(z, kf):
    l = z.shape[1]
    zf = jnp.fft.rfft(z.astype(jnp.float32), n=2 * l, axis=1)
    y = jnp.fft.irfft(zf * kf[None], n=2 * l, axis=1)[:, :l]
    return y.astype(z.dtype)


def hyena_mixer(h, w_in, conv_w, conv_b, fw1, fb1, fw2, fb2, fw3, fb3, ffreq, fw4, fbias, w_out):
    l = h.shape[1]
    v, x1, x2 = jnp.split(short_conv(h @ w_in, conv_w, conv_b), 3, axis=-1)
    kf = hyena_filters(l, fw1, fb1, fw2, fb2, fw3, fb3, ffreq, fw4)
    y = v
    for n, gate in enumerate((x1, x2)):
        y = gate * (fft_conv(y, kf[:, n]) + y * fbias[n])
    return y @ w_out


def setup_inputs(seed: int = 0) -> dict:
    key = jax.random.key(seed)
    ks = iter(jax.random.split(key, 40))
    f32 = jnp.float32

    def nrm(shape, scale):
        return jax.random.normal(next(ks), shape, f32) * scale

    D = D_MODEL
    return {
        'x': nrm((BATCH, SEQ, D), 1.0),
        'c': nrm((BATCH, D), 1.0),
        'ctx': nrm((BATCH, CTX_LEN, D), 1.0),
        'c_ctx': nrm((D,), 1.0),
        'ada_w': nrm((DEPTH, D, 6 * D), 0.5 * D ** -0.5),
        'ada_b': nrm((DEPTH, 6 * D), 0.01),
        'norm1': 1.0 + nrm((DEPTH, D), 0.02),
        'norm2': 1.0 + nrm((DEPTH, D), 0.02),
        'ffn_w_gate': nrm((DEPTH, D, D_FF), D ** -0.5),
        'ffn_w_up': nrm((DEPTH, D, D_FF), D ** -0.5),
        'ffn_w_down': nrm((DEPTH, D_FF, D), D_FF ** -0.5),
        'e_w_in': nrm((N_EVEN, D, EVEN_IN), D ** -0.5),
        'e_w_out': nrm((N_EVEN, EVEN_OUT, D), EVEN_OUT ** -0.5),
        'e_lambda': nrm((N_EVEN, 4, DIFF_QK_DIM), 0.1),
        'e_subln': 1.0 + nrm((N_EVEN, DIFF_V_DIM), 0.02),
        'e_sgu_norm_g': 1.0 + nrm((N_EVEN, SGU_GROUPS, SGU_CH), 0.02),
        'e_sgu_norm_b': nrm((N_EVEN, SGU_GROUPS, SGU_CH), 0.02),
        'e_sgu_w': nrm((N_EVEN, SGU_GROUPS, SGU_CHUNK, SGU_CHUNK), SGU_CHUNK ** -0.5),
        'e_sgu_b': 1.0 + nrm((N_EVEN, SGU_GROUPS, SGU_CHUNK), 0.02),
        'o_w_in': nrm((N_ODD, D, 3 * HYENA_WIDTH), D ** -0.5),
        'o_conv_w': nrm((N_ODD, SHORT_CONV, 3 * HYENA_WIDTH), SHORT_CONV ** -0.5),
        'o_conv_b': nrm((N_ODD, 3 * HYENA_WIDTH), 0.02),
        'o_filt_w1': nrm((N_ODD, FILTER_EMB, FILTER_HIDDEN), FILTER_EMB ** -0.5),
        'o_filt_b1': nrm((N_ODD, FILTER_HIDDEN), 0.02),
        'o_filt_w2': nrm((N_ODD, FILTER_HIDDEN, FILTER_HIDDEN), FILTER_HIDDEN ** -0.5),
        'o_filt_b2': nrm((N_ODD, FILTER_HIDDEN), 0.02),
        'o_filt_w3': nrm((N_ODD, FILTER_HIDDEN, FILTER_HIDDEN), FILTER_HIDDEN ** -0.5),
        'o_filt_b3': nrm((N_ODD, FILTER_HIDDEN), 0.02),
        'o_filt_freq': 1.0 + nrm((N_ODD, 3, FILTER_HIDDEN), 0.02),
        'o_filt_w4': nrm((N_ODD, FILTER_HIDDEN, HYENA_ORDER * 2 * HYENA_WIDTH), FILTER_HIDDEN ** -0.5),
        'o_filt_bias': nrm((N_ODD, HYENA_ORDER, HYENA_WIDTH), 0.5),
        'o_w_out': nrm((N_ODD, HYENA_WIDTH, D), HYENA_WIDTH ** -0.5),
        'final_norm': 1.0 + nrm((D,), 0.02),
    }


def reference(x, c, ctx, c_ctx, ada_w, ada_b, norm1, norm2, ffn_w_gate, ffn_w_up, ffn_w_down,
              e_w_in, e_w_out, e_lambda, e_subln, e_sgu_norm_g, e_sgu_norm_b, e_sgu_w, e_sgu_b,
              o_w_in, o_conv_w, o_conv_b, o_filt_w1, o_filt_b1, o_filt_w2, o_filt_b2, o_filt_w3,
              o_filt_b3, o_filt_freq, o_filt_w4, o_filt_bias, o_w_out, final_norm):
    xc = ctx
    for i in range(DEPTH):
        is_even = i % 2 == 0
        j = i // 2
        ctx_out = i < DEPTH - 1
        ctx_in = ctx_out or is_even
        sh1, sc1, g1, sh2, sc2, g2 = adaln(c, ada_w[i], ada_b[i])
        h_lat = modulate(rms_norm(x, norm1[i]), sh1, sc1)
        h_ctx = None
        if ctx_in:
            csh1, csc1, cg1, csh2, csc2, cg2 = adaln(c_ctx, ada_w[i], ada_b[i])
            h_ctx = modulate(rms_norm(xc, norm1[i]), csh1, csc1)
        if is_even:
            y_ctx, y_lat = even_mixer(h_ctx, h_lat, e_w_in[j], e_w_out[j], e_lambda[j], e_subln[j],
                                      e_sgu_norm_g[j], e_sgu_norm_b[j], e_sgu_w[j], e_sgu_b[j], i, ctx_out)
        else:
            hy = (o_w_in[j], o_conv_w[j], o_conv_b[j], o_filt_w1[j], o_filt_b1[j], o_filt_w2[j],
                  o_filt_b2[j], o_filt_w3[j], o_filt_b3[j], o_filt_freq[j], o_filt_w4[j],
                  o_filt_bias[j], o_w_out[j])
            y_lat = hyena_mixer(h_lat, *hy)
            y_ctx = hyena_mixer(h_ctx, *hy) if ctx_out else None
        x = x + g1 * y_lat
        x = x + g2 * swiglu(modulate(rms_norm(x, norm2[i]), sh2, sc2), ffn_w_gate[i], ffn_w_up[i], ffn_w_down[i])
        if ctx_out:
            xc = xc + cg1 * y_ctx
            xc = xc + cg2 * swiglu(modulate(rms_norm(xc, norm2[i]), csh2, csc2), ffn_w_gate[i], ffn_w_up[i], ffn_w_down[i])
    return rms_norm(x, final_norm)
```

```python
import functools
import math

import numpy as np
import jax
import jax.numpy as jnp
from jax import lax
from jax.experimental import pallas as pl
from jax.experimental.pallas import tpu as pltpu

F32 = jnp.float32
BF16 = jnp.bfloat16
HIGHEST = lax.Precision.HIGHEST

GRID_W = 64
NORM_EPS = 1e-6
DIFF_HEADS = 8
DIFF_QK_DIM = 64
DIFF_V_DIM = 2 * DIFF_QK_DIM
DIFF_SCALE = DIFF_QK_DIM ** -0.5
A_WIDTH = DIFF_HEADS * DIFF_V_DIM
Q_COLS = DIFF_HEADS * 2 * DIFF_QK_DIM
ROPE_THETA = 10000.0
ROPE_AXIS_DIM = DIFF_QK_DIM // 2
SUBLN_EPS = 1e-5
SGU_GROUPS = 8
SGU_CHUNK = 128
SGU_CH = 128
B_WIDTH = SGU_GROUPS * SGU_CH
LN_EPS = 1e-5
HYENA_ORDER = 2
SHORT_CONV = 3
FILTER_EMB = 33
FILTER_BANDS = (FILTER_EMB - 1) // 2
FILTER_HIDDEN = 64
DECAY_TARGET = 1e-2
MAX_DECAY = math.log(DECAY_TARGET) / 0.3
MIN_DECAY = math.log(DECAY_TARGET) / 1.5

LANES = 128
DFT_MINOR = 256
VMEM_LIMIT = 48 * 1024 * 1024

NT_DIMS = (((1,), (1,)), ((), ()))
TN_DIMS = (((0,), (0,)), ((), ()))


def _params(*sem):
    return pltpu.CompilerParams(dimension_semantics=sem, vmem_limit_bytes=VMEM_LIMIT)


def _tile(n, pref):
    return pref if n % pref == 0 else n


def _adaln_kernel(c_ref, w_ref, b_ref, o_ref):
    a = c_ref[...]
    a = a * jax.nn.sigmoid(a)
    o_ref[...] = jnp.dot(a, w_ref[...], preferred_element_type=F32, precision=HIGHEST) + b_ref[...]


def _adaln_all(cond, ada_w, ada_b):
    depth, d, n6 = ada_w.shape
    tn = _tile(n6, 1024)
    return pl.pallas_call(
        _adaln_kernel,
        grid=(depth, n6 // tn),
        in_specs=[pl.BlockSpec((8, d), lambda l, j: (0, 0)),
                  pl.BlockSpec((None, d, tn), lambda l, j: (l, 0, j)),
                  pl.BlockSpec((None, 1, tn), lambda l, j: (l, 0, j))],
        out_specs=pl.BlockSpec((None, 8, tn), lambda l, j: (l, 0, j)),
        out_shape=jax.ShapeDtypeStruct((depth, 8, n6), F32),
        compiler_params=_params("parallel", "parallel"),
        name="adaln",
    )(cond, ada_w, ada_b.reshape(depth, 1, n6))


def _norm_mod(x_ref, g_ref, sh_ref, sc_ref):
    x = x_ref[...]
    ms = jnp.mean(x * x, axis=-1, keepdims=True)
    y = x * lax.rsqrt(ms + NORM_EPS) * g_ref[...]
    return (y * (1.0 + sc_ref[...]) + sh_ref[...]).astype(BF16)


def _inproj_kernel(x_ref, g_ref, sh_ref, sc_ref, w_ref, cos_ref, sin_ref, o_ref, hs_ref, *,
                   n_q, n_qk, rope):
    j = pl.program_id(2)

    @pl.when(j == 0)
    def _():
        hs_ref[...] = _norm_mod(x_ref, g_ref, sh_ref, sc_ref)

    acc = jnp.dot(hs_ref[...], w_ref[...], preferred_element_type=F32)

    @pl.when(j < n_qk)
    def _():
        a = acc
        if rope:
            tn = a.shape[1]
            lane = lax.broadcasted_iota(jnp.int32, a.shape, 1)
            first = (lane & 31) < 16
            partner = jnp.where(first, pltpu.roll(a, tn - 16, 1), pltpu.roll(a, 16, 1))
            reps = tn // LANES
            a = a * jnp.tile(cos_ref[...], (1, reps)) + partner * jnp.tile(sin_ref[...], (1, reps))
        a = jnp.where(j < n_q, a * DIFF_SCALE, a)
        o_ref[...] = a.astype(o_ref.dtype)

    @pl.when(j >= n_qk)
    def _():
        o_ref[...] = jax.nn.gelu(acc).astype(o_ref.dtype)


def _inproj(x, g, sh, sc, w, cos_t, sin_t, rope):
    b, s, d = x.shape
    n = w.shape[1]
    tm = _tile(s, 1024)
    tn = 512
    kern = functools.partial(_inproj_kernel, n_q=Q_COLS // tn, n_qk=2 * Q_COLS // tn, rope=rope)
    return pl.pallas_call(
        kern,
        grid=(b, s // tm, n // tn),
        in_specs=[pl.BlockSpec((None, tm, d), lambda bb, i, j: (bb, i, 0)),
                  pl.BlockSpec((1, d), lambda bb, i, j: (0, 0)),
                  pl.BlockSpec((None, 1, d), lambda bb, i, j: (bb, 0, 0)),
                  pl.BlockSpec((None, 1, d), lambda bb, i, j: (bb, 0, 0)),
                  pl.BlockSpec((d, tn), lambda bb, i, j: (0, j)),
                  pl.BlockSpec((tm, LANES), lambda bb, i, j: (i, 0)),
                  pl.BlockSpec((tm, LANES), lambda bb, i, j: (i, 0))],
        out_specs=pl.BlockSpec((None, tm, tn), lambda bb, i, j: (bb, i, j)),
        out_shape=jax.ShapeDtypeStruct((b, s, n), BF16),
        scratch_shapes=[pltpu.VMEM((tm, d), BF16)],
        compiler_params=_params("parallel", "parallel", "arbitrary"),
        name="inproj",
    )(x, g, sh, sc, w, cos_t, sin_t)


def _inproj_nt_kernel(x_ref, g_ref, sh_ref, sc_ref, wt_ref, o_ref, hs_ref, *, tl):
    j = pl.program_id(2)

    @pl.when(j == 0)
    def _():
        hs_ref[...] = _norm_mod(x_ref, g_ref, sh_ref, sc_ref)

    acc = lax.dot_general(wt_ref[...], hs_ref[...], NT_DIMS, preferred_element_type=F32)
    for c in range(o_ref.shape[0]):
        o_ref[c] = acc[:, c * tl:(c + 1) * tl].astype(o_ref.dtype)


def _inproj_nt(x, g, sh, sc, wt, tl):
    b, s, d = x.shape
    n = wt.shape[0]
    tm = _tile(s, 1024)
    tn = 512
    kern = functools.partial(_inproj_nt_kernel, tl=tl)
    return pl.pallas_call(
        kern,
        grid=(b, s // tm, n // tn),
        in_specs=[pl.BlockSpec((None, tm, d), lambda bb, i, j: (bb, i, 0)),
                  pl.BlockSpec((1, d), lambda bb, i, j: (0, 0)),
                  pl.BlockSpec((None, 1, d), lambda bb, i, j: (bb, 0, 0)),
                  pl.BlockSpec((None, 1, d), lambda bb, i, j: (bb, 0, 0)),
                  pl.BlockSpec((tn, d), lambda bb, i, j: (j, 0))],
        out_specs=pl.BlockSpec((None, tm // tl, tn, tl), lambda bb, i, j: (bb, i, j, 0)),
        out_shape=jax.ShapeDtypeStruct((b, s // tl, n, tl), BF16),
        scratch_shapes=[pltpu.VMEM((tm, d), BF16)],
        compiler_params=_params("parallel", "parallel", "arbitrary"),
        name="inproj_nt",
    )(x, g, sh, sc, wt)


def _ffn_up_kernel(x_ref, g_ref, sh_ref, sc_ref, wg_ref, wu_ref, o_ref, hs_ref):
    j = pl.program_id(2)

    @pl.when(j == 0)
    def _():
        hs_ref[...] = _norm_mod(x_ref, g_ref, sh_ref, sc_ref)

    hs = hs_ref[...]
    gate = jnp.dot(hs, wg_ref[...], preferred_element_type=F32)
    up = jnp.dot(hs, wu_ref[...], preferred_element_type=F32)
    o_ref[...] = (gate * jax.nn.sigmoid(gate) * up).astype(o_ref.dtype)


def _ffn_up(x, g, sh, sc, wg, wu):
    b, s, d = x.shape
    n = wg.shape[1]
    tm = _tile(s, 1024)
    tn = 512
    return pl.pallas_call(
        _ffn_up_kernel,
        grid=(b, s // tm, n // tn),
        in_specs=[pl.BlockSpec((None, tm, d), lambda bb, i, j: (bb, i, 0)),
                  pl.BlockSpec((1, d), lambda bb, i, j: (0, 0)),
                  pl.BlockSpec((None, 1, d), lambda bb, i, j: (bb, 0, 0)),
                  pl.BlockSpec((None, 1, d), lambda bb, i, j: (bb, 0, 0)),
                  pl.BlockSpec((d, tn), lambda bb, i, j: (0, j)),
                  pl.BlockSpec((d, tn), lambda bb, i, j: (0, j))],
        out_specs=pl.BlockSpec((None, tm, tn), lambda bb, i, j: (bb, i, j)),
        out_shape=jax.ShapeDtypeStruct((b, s, n), BF16),
        scratch_shapes=[pltpu.VMEM((tm, d), BF16)],
        compiler_params=_params("parallel", "parallel", "arbitrary"),
        name="ffn_up",
    )(x, g, sh, sc, wg, wu)


def _proj_res_kernel(*refs, ksizes, transposed):
    n = len(ksizes)
    a_refs = refs[:n]
    w_ref, x_ref, gate_ref, o_ref = refs[n:]
    acc = None
    off = 0
    for a_ref, ks in zip(a_refs, ksizes):
        w = w_ref[off:off + ks, :]
        if transposed:
            part = lax.dot_general(a_ref[...], w, TN_DIMS, preferred_element_type=F32)
        else:
            part = jnp.dot(a_ref[...], w, preferred_element_type=F32)
        acc = part if acc is None else acc + part
        off += ks
    o_ref[...] = x_ref[...] + gate_ref[...] * acc


def _proj_res(a_list, w, x, gate, transposed=False, tm_pref=512, tn_pref=512):
    b, s, d = x.shape
    ksizes = tuple(a.shape[1] if transposed else a.shape[2] for a in a_list)
    ktot = sum(ksizes)
    tm = _tile(s, tm_pref)
    tn = _tile(d, tn_pref)
    if transposed:
        a_specs = [pl.BlockSpec((None, ks, tm), lambda bb, i, j: (bb, 0, i)) for ks in ksizes]
    else:
        a_specs = [pl.BlockSpec((None, tm, ks), lambda bb, i, j: (bb, i, 0)) for ks in ksizes]
    kern = functools.partial(_proj_res_kernel, ksizes=ksizes, transposed=transposed)
    return pl.pallas_call(
        kern,
        grid=(b, s // tm, d // tn),
        in_specs=a_specs + [pl.BlockSpec((ktot, tn), lambda bb, i, j: (0, j)),
                            pl.BlockSpec((None, tm, tn), lambda bb, i, j: (bb, i, j)),
                            pl.BlockSpec((None, 1, tn), lambda bb, i, j: (bb, 0, j))],
        out_specs=pl.BlockSpec((None, tm, tn), lambda bb, i, j: (bb, i, j)),
        out_shape=jax.ShapeDtypeStruct((b, s, d), F32),
        compiler_params=_params("parallel", "parallel", "parallel"),
        name="proj_res",
    )(*a_list, w, x, gate)


def _attn_kernel(*refs, n_chunks, lam_init):
    if n_chunks:
        lam_ref, q_ref, kc_ref, vct_ref, k_ref, vt_ref, g_ref, o_ref, acc_ref = refs
    else:
        lam_ref, q_ref, kc_ref, vct_ref, g_ref, o_ref, acc_ref = refs
    q = q_ref[...]
    tq = q.shape[0]
    qm = (q[:, :DIFF_QK_DIM], q[:, DIFF_QK_DIM:])

    def absorb(kblk, vtblk, stats):
        out = []
        for m in range(2):
            m_old, l_old = stats[m]
            km = kblk[:, m * DIFF_QK_DIM:(m + 1) * DIFF_QK_DIM]
            s = lax.dot_general(km, qm[m], NT_DIMS, preferred_element_type=F32)
            m_new = jnp.maximum(m_old, jnp.max(s, axis=0, keepdims=True))
            alpha = jnp.exp(m_old - m_new)
            p = jnp.exp(s - m_new)
            l_new = alpha * l_old + jnp.sum(p, axis=0, keepdims=True)
            pv = jnp.dot(vtblk, p.astype(BF16), preferred_element_type=F32)
            acc_ref[m] = alpha * acc_ref[m] + pv
            out.append((m_new, l_new))
        return tuple(out)

    acc_ref[...] = jnp.zeros_like(acc_ref)
    init = (jnp.full((1, tq), -1e30, F32), jnp.zeros((1, tq), F32))
    stats = absorb(kc_ref[...], vct_ref[0], (init, init))
    if n_chunks:
        def body(i, st):
            return absorb(k_ref[i], vt_ref[i], st)
        stats = lax.fori_loop(0, n_chunks, body, stats)

    lp = lam_ref[...]
    lam = (jnp.exp(jnp.sum(lp[0:1] * lp[1:2], axis=-1, keepdims=True))
           - jnp.exp(jnp.sum(lp[2:3] * lp[3:4], axis=-1, keepdims=True)) + lam_init)
    o = acc_ref[0] / stats[0][1] - lam * (acc_ref[1] / stats[1][1])
    ot = o.T
    ms = jnp.mean(ot * ot, axis=-1, keepdims=True)
    on = ot * lax.rsqrt(ms + SUBLN_EPS) * g_ref[...] * (1.0 - lam_init)
    o_ref[...] = on.astype(o_ref.dtype)


def _attention(lam_p, subln, q_arr, kc_arr, vct_arr, k4=None, vt4=None, *, lam_init):
    b, sq = q_arr.shape[0], q_arr.shape[1]
    sc = kc_arr.shape[1]
    h = DIFF_HEADS
    tq = _tile(sq, 256)
    dv = DIFF_V_DIM
    n_chunks = 0 if k4 is None else k4.shape[1]
    in_specs = [pl.BlockSpec((4, DIFF_QK_DIM), lambda bb, hh, i: (0, 0)),
                pl.BlockSpec((None, tq, dv), lambda bb, hh, i: (bb, i, hh)),
                pl.BlockSpec((None, sc, dv), lambda bb, hh, i: (bb, 0, h + hh)),
                pl.BlockSpec((None, 1, dv, sc), lambda bb, hh, i: (bb, 0, hh, 0))]
    args = [lam_p, q_arr, kc_arr, vct_arr]
    if n_chunks:
        tk = k4.shape[2]
        in_specs += [pl.BlockSpec((None, n_chunks, tk, dv), lambda bb, hh, i: (bb, 0, 0, h + hh)),
                     pl.BlockSpec((None, n_chunks, dv, tk), lambda bb, hh, i: (bb, 0, hh, 0))]
        args += [k4, vt4]
    in_specs.append(pl.BlockSpec((1, dv), lambda bb, hh, i: (0, 0)))
    args.append(subln)
    kern = functools.partial(_attn_kernel, n_chunks=n_chunks, lam_init=lam_init)
    return pl.pallas_call(
        kern,
        grid=(b, h, sq // tq),
        in_specs=in_specs,
        out_specs=pl.BlockSpec((None, tq, dv), lambda bb, hh, i: (bb, i, hh)),
        out_shape=jax.ShapeDtypeStruct((b, sq, A_WIDTH), BF16),
        scratch_shapes=[pltpu.VMEM((2, dv, tq), F32)],
        compiler_params=_params("parallel", "parallel", "parallel"),
        name="diff_attn",
    )(*args)


def _sgu_kernel(u_ref, g_ref, ng_ref, nb_ref, w_ref, bs_ref, o_ref):
    w = w_ref[...]
    for c in range(u_ref.shape[0] // SGU_CHUNK):
        sl = slice(c * SGU_CHUNK, (c + 1) * SGU_CHUNK)
        gg = g_ref[sl, :].astype(F32)
        mu = jnp.mean(gg, axis=-1, keepdims=True)
        dev = gg - mu
        var = jnp.mean(dev * dev, axis=-1, keepdims=True)
        vv = dev * lax.rsqrt(var + LN_EPS) * ng_ref[...] + nb_ref[...]
        mixed = jnp.dot(w, vv.astype(BF16), preferred_element_type=F32) + bs_ref[...]
        o_ref[sl, :] = (u_ref[sl, :].astype(F32) * mixed).astype(o_ref.dtype)


def _sgu(qkug, norm_g, norm_b, w_s, b_s):
    b, s = qkug.shape[0], qkug.shape[1]
    tm = _tile(s, 1024)
    gcount = SGU_GROUPS
    ublk = 2 * Q_COLS // SGU_CH
    return pl.pallas_call(
        _sgu_kernel,
        grid=(b, s // tm, gcount),
        in_specs=[pl.BlockSpec((None, tm, SGU_CH), lambda bb, i, gi: (bb, i, ublk + gi)),
                  pl.BlockSpec((None, tm, SGU_CH), lambda bb, i, gi: (bb, i, ublk + gcount + gi)),
                  pl.BlockSpec((None, 1, SGU_CH), lambda bb, i, gi: (gi, 0, 0)),
                  pl.BlockSpec((None, 1, SGU_CH), lambda bb, i, gi: (gi, 0, 0)),
                  pl.BlockSpec((None, SGU_CHUNK, SGU_CHUNK), lambda bb, i, gi: (gi, 0, 0)),
                  pl.BlockSpec((None, SGU_CHUNK, 1), lambda bb, i, gi: (gi, 0, 0))],
        out_specs=pl.BlockSpec((None, tm, SGU_CH), lambda bb, i, gi: (bb, i, gi)),
        out_shape=jax.ShapeDtypeStruct((b, s, B_WIDTH), BF16),
        compiler_params=_params("parallel", "parallel", "parallel"),
        name="sgu",
    )(qkug, qkug, norm_g.reshape(gcount, 1, SGU_CH), norm_b.reshape(gcount, 1, SGU_CH),
      w_s.astype(BF16), b_s.reshape(gcount, SGU_CHUNK, 1))


def _filter_mlp_kernel(f_ref, w1_ref, b1_ref, w2_ref, b2_ref, w3_ref, b3_ref, fr_ref, o_ref):
    def lin(a, w_ref, b_ref):
        return jnp.dot(a, w_ref[...], preferred_element_type=F32, precision=HIGHEST) + b_ref[...]
    fr = fr_ref[...]
    hcur = jnp.sin(fr[0:1] * lin(f_ref[...], w1_ref, b1_ref))
    hcur = jnp.sin(fr[1:2] * lin(hcur, w2_ref, b2_ref))
    o_ref[...] = jnp.sin(fr[2:3] * lin(hcur, w3_ref, b3_ref))


def _filter_mlp(feats, w1, b1, w2, b2, w3, b3, freq):
    rows, emb = feats.shape
    hid = FILTER_HIDDEN
    tr = _tile(rows, 2048)
    full = lambda shape: pl.BlockSpec(shape, lambda i: (0,) * len(shape))
    return pl.pallas_call(
        _filter_mlp_kernel,
        grid=(rows // tr,),
        in_specs=[pl.BlockSpec((tr, emb), lambda i: (i, 0)),
                  full((emb, hid)), full((1, hid)), full((hid, hid)), full((1, hid)),
                  full((hid, hid)), full((1, hid)), full((3, hid))],
        out_specs=pl.BlockSpec((tr, hid), lambda i: (i, 0)),
        out_shape=jax.ShapeDtypeStruct((rows, hid), F32),
        compiler_params=_params("parallel"),
        name="filter_mlp",
    )(feats, w1, b1.reshape(1, hid), w2, b2.reshape(1, hid), w3, b3.reshape(1, hid), freq)


def _filter_raw_kernel(w4t_ref, h_ref, t_ref, delta_ref, o_ref, *, zero_tile):
    rt = pl.program_id(2)
    raw = lax.dot_general(w4t_ref[...], h_ref[...], NT_DIMS, preferred_element_type=F32,
                          precision=HIGHEST)
    raw = raw * jnp.exp(-(delta_ref[...] * t_ref[...]))
    for c in range(o_ref.shape[0]):
        o_ref[c] = raw[:, c * DFT_MINOR:(c + 1) * DFT_MINOR]

    @pl.when(rt == zero_tile)
    def _():
        col = lax.broadcasted_iota(jnp.int32, (raw.shape[0], DFT_MINOR), 1)
        o_ref[0] = jnp.where(col == 0, 0.0, raw[:, :DFT_MINOR])


def _filter_raw(w4t, h3, t_row, deltas, seq):
    c = w4t.shape[2]
    rows = h3.shape[0]
    tr = _tile(rows // 2, 2048)
    tc = _tile(c, 512)
    half_tiles = seq // tr
    kern = functools.partial(_filter_raw_kernel, zero_tile=half_tiles)
    return pl.pallas_call(
        kern,
        grid=(HYENA_ORDER, c // tc, rows // tr),
        in_specs=[pl.BlockSpec((None, None, tc, FILTER_HIDDEN),
                               lambda n, ci, rt: (n, rt // half_tiles, ci, 0)),
                  pl.BlockSpec((tr, FILTER_HIDDEN), lambda n, ci, rt: (rt, 0)),
                  pl.BlockSpec((1, tr), lambda n, ci, rt: (0, rt)),
                  pl.BlockSpec((tc, 1), lambda n, ci, rt: (ci, 0))],
        out_specs=pl.BlockSpec((None, tr // DFT_MINOR, tc, DFT_MINOR), lambda n, ci, rt: (n, rt, ci, 0)),
        out_shape=jax.ShapeDtypeStruct((HYENA_ORDER, rows // DFT_MINOR, c, DFT_MINOR), F32),
        compiler_params=_params("parallel", "parallel", "parallel"),
        name="filter_raw",
    )(w4t, h3, t_row, deltas)


def _cmul(ar, ai, br, bi):
    return ar * br - ai * bi, ar * bi + ai * br


def _filter_spec_kernel(k_ref, f1_ref, twr_ref, twi_ref, f2_ref, o_ref, *, inv_n):
    n1 = k_ref.shape[1]

    def body(c, carry):
        k = k_ref[c]
        nrm = jnp.sum(jnp.sum(jnp.abs(k), axis=1, keepdims=True), axis=0, keepdims=True)
        kn = (k * (inv_n / nrm)).astype(BF16)
        a = jnp.dot(f1_ref[...], kn, preferred_element_type=F32)
        ar, ai = _cmul(a[:n1], a[n1:], twr_ref[...], twi_ref[...])
        a2 = jnp.concatenate([ar, ai], axis=1).astype(BF16)
        z = jnp.dot(a2, f2_ref[...], preferred_element_type=F32)
        o_ref[c, 0] = z[:, :DFT_MINOR].astype(o_ref.dtype)
        o_ref[c, 1] = z[:, DFT_MINOR:].astype(o_ref.dtype)
        return carry

    lax.fori_loop(0, k_ref.shape[0], body, 0)


def _filter_spec(k2, f1_full, twr, twi, f2e):
    nc, n1, n2 = k2.shape
    tc = 16
    kern = functools.partial(_filter_spec_kernel, inv_n=1.0 / (n1 * n2))
    full = lambda shape: pl.BlockSpec(shape, lambda i: (0,) * len(shape))
    return pl.pallas_call(
        kern,
        grid=(nc // tc,),
        in_specs=[pl.BlockSpec((tc, n1, n2), lambda i: (i, 0, 0)),
                  full((2 * n1, n1)), full((n1, n2)), full((n1, n2)), full((2 * n2, 2 * n2))],
        out_specs=pl.BlockSpec((tc, 2, n1, n2), lambda i: (i, 0, 0, 0)),
        out_shape=jax.ShapeDtypeStruct((nc, 2, n1, n2), BF16),
        compiler_params=_params("parallel"),
        name="filter_spec",
    )(k2, f1_full, twr, twi, f2e)


def _hyena_kernel(zv_ref, z1_ref, z2_ref, kf_ref, pc_ref, f1_ref, twr_ref, twi_ref, f2_ref, f2c_ref,
                  g1_ref, o_ref):
    nb, tc, hr, n2 = zv_ref.shape
    n1 = 2 * hr
    row = lax.broadcasted_iota(jnp.int32, (hr, n2), 0)
    lane = lax.broadcasted_iota(jnp.int32, (hr, n2), 1)
    first_lane, last_lane = lane == 0, lane == n2 - 1
    first_elem = first_lane & (row == 0)
    last_elem = last_lane & (row == hr - 1)

    def short_conv(z, w0, w1, w2, bias):
        prev = pltpu.roll(z, 1, 1)
        prev = jnp.where(first_lane, pltpu.roll(prev, 1, 0), prev)
        prev = jnp.where(first_elem, 0.0, prev)
        nxt = pltpu.roll(z, n2 - 1, 1)
        nxt = jnp.where(last_lane, pltpu.roll(nxt, hr - 1, 0), nxt)
        nxt = jnp.where(last_elem, 0.0, nxt)
        return bias + prev * w0 + z * w1 + nxt * w2

    def long_conv(ur, ui, kr, ki):
        xs = jnp.concatenate([ur, ui], axis=0).astype(BF16)
        a = jnp.dot(f1_ref[...], xs, preferred_element_type=F32)
        ar, ai = _cmul(a[:n1], a[n1:], twr_ref[...], twi_ref[...])
        a2 = jnp.concatenate([ar, ai], axis=1).astype(BF16)
        z = jnp.dot(a2, f2_ref[...], preferred_element_type=F32)
        wr, wi = _cmul(z[:, :n2], z[:, n2:], kr, ki)
        w2 = jnp.concatenate([wr, wi], axis=1).astype(BF16)
        bm = jnp.dot(w2, f2c_ref[...], preferred_element_type=F32)
        br, bi = _cmul(bm[:, :n2], bm[:, n2:], twr_ref[...], -twi_ref[...])
        bs = jnp.concatenate([br, bi], axis=0).astype(BF16)
        y = jnp.dot(g1_ref[...], bs, preferred_element_type=F32)
        return y[:hr], y[hr:]

    def body(c, carry):
        pc = pc_ref[pl.ds(c, 1), :]
        par = lambda idx: pc[:, idx:idx + 1]
        sig = []
        for part, ref in enumerate((zv_ref, z1_ref, z2_ref)):
            o4 = 4 * part
            sig.append([short_conv(ref[bb, c].astype(F32), par(o4), par(o4 + 1), par(o4 + 2), par(o4 + 3))
                        for bb in range(nb)])
        u = sig[0]
        for n in range(HYENA_ORDER):
            kr = kf_ref[n, c, 0].astype(F32)
            ki = kf_ref[n, c, 1].astype(F32)
            yr, yi = long_conv(u[0], u[1], kr, ki)
            fb = par(12 + n)
            gate = sig[1 + n]
            u = [gate[0] * (yr + u[0] * fb), gate[1] * (yi + u[1] * fb)]
        for bb in range(nb):
            o_ref[bb, c] = u[bb].astype(o_ref.dtype)
        return carry

    lax.fori_loop(0, tc, body, 0)


def _hyena_core(z4, kf, pc, f1h, twr, twi, f2e, f2c, g1e):
    nb, c3, hr, n2 = z4.shape
    c = c3 // 3
    n1 = 2 * hr
    tc = 16
    nct = c // tc
    full = lambda shape: pl.BlockSpec(shape, lambda i: (0,) * len(shape))
    zspec = lambda part: pl.BlockSpec((nb, tc, hr, n2), lambda i: (0, i + part * nct, 0, 0))
    return pl.pallas_call(
        _hyena_kernel,
        grid=(nct,),
        in_specs=[zspec(0), zspec(1), zspec(2),
                  pl.BlockSpec((HYENA_ORDER, tc, 2, n1, n2), lambda i: (0, i, 0, 0, 0)),
                  pl.BlockSpec((tc, 16), lambda i: (i, 0)),
                  full((2 * n1, n1)), full((n1, n2)), full((n1, n2)),
                  full((2 * n2, 2 * n2)), full((2 * n2, 2 * n2)), full((n1, 2 * n1))],
        out_specs=pl.BlockSpec((nb, tc, hr, n2), lambda i: (0, i, 0, 0)),
        out_shape=jax.ShapeDtypeStruct((nb, c, hr, n2), BF16),
        compiler_params=_params("parallel"),
        name="hyena_core",
    )(z4, z4, z4, kf, pc, f1h, twr, twi, f2e, f2c, g1e)


def _dft_tables(n1):
    n2 = DFT_MINOR
    n = n1 * n2
    a1 = 2.0 * np.pi * np.outer(np.arange(n1), np.arange(n1)) / n1
    f1r, f1i = np.cos(a1), -np.sin(a1)
    a2 = 2.0 * np.pi * np.outer(np.arange(n2), np.arange(n2)) / n2
    f2r, f2i = np.cos(a2), -np.sin(a2)
    at = 2.0 * np.pi * np.outer(np.arange(n1), np.arange(n2)) / n
    twr, twi = np.cos(at), -np.sin(at)
    hr = n1 // 2
    f1_full = np.concatenate([f1r, f1i], axis=0)
    f1_half = np.block([[f1r[:, :hr], -f1i[:, :hr]], [f1i[:, :hr], f1r[:, :hr]]])
    f2e = np.block([[f2r, f2i], [-f2i, f2r]])
    f2c = np.block([[f2r, -f2i], [f2i, f2r]])
    gr, gi = f1r[:hr, :], -f1i[:hr, :]
    g1e = np.block([[gr, -gi], [gi, gr]])
    bf = lambda m: jnp.asarray(m, F32).astype(BF16)
    return dict(f1_full=bf(f1_full), f1_half=bf(f1_half), f2e=bf(f2e), f2c=bf(f2c), g1e=bf(g1e),
                twr=jnp.asarray(twr, F32), twi=jnp.asarray(twi, F32))


def _hyena_spectra(seq, fw1, fb1, fw2, fb2, fw3, fb3, ffreq, fw4, tabs):
    l = seq
    width = fw4.shape[1] // (2 * HYENA_ORDER)
    idx = jnp.concatenate([jnp.arange(l), l - jnp.arange(l)]) % l
    t_lin = jnp.linspace(0.0, 1.0, l, dtype=F32)[idx]
    w = 2.0 * math.pi * idx.astype(F32)[:, None] / l
    f = jnp.linspace(1e-4, FILTER_BANDS - 1, FILTER_BANDS, dtype=F32)[None, :]
    feats = jnp.concatenate([t_lin[:, None], jnp.cos(f * w), -jnp.sin(f * w)], axis=-1)
    emb_pad = FILTER_HIDDEN
    feats = jnp.pad(feats, ((0, 0), (0, emb_pad - FILTER_EMB)))
    w1p = jnp.pad(fw1, ((0, emb_pad - FILTER_EMB), (0, 0)))
    h3 = _filter_mlp(feats, w1p, fb1, fw2, fb2, fw3, fb3, ffreq)
    w4t = fw4.reshape(FILTER_HIDDEN, HYENA_ORDER, 2, width).transpose(1, 2, 3, 0)
    deltas = jnp.abs(jnp.linspace(MIN_DECAY, MAX_DECAY, width, dtype=F32))[:, None]
    raw = _filter_raw(w4t, h3, t_lin[None, :], deltas, l)
    n1 = raw.shape[1]
    k2 = raw.transpose(0, 2, 1, 3).reshape(HYENA_ORDER * width, n1, DFT_MINOR)
    kf = _filter_spec(k2, tabs["f1_full"], tabs["twr"], tabs["twi"], tabs["f2e"])
    return kf.reshape(HYENA_ORDER, width, 2, n1, DFT_MINOR)


def _rms_kernel(x_ref, g_ref, o_ref):
    x = x_ref[...]
    ms = jnp.mean(x * x, axis=-1, keepdims=True)
    o_ref[...] = x * lax.rsqrt(ms + NORM_EPS) * g_ref[...]


def _final_norm(x, g):
    b, s, d = x.shape
    tm = _tile(s, 1024)
    return pl.pallas_call(
        _rms_kernel,
        grid=(b, s // tm),
        in_specs=[pl.BlockSpec((None, tm, d), lambda bb, i: (bb, i, 0)),
                  pl.BlockSpec((1, d), lambda bb, i: (0, 0))],
        out_specs=pl.BlockSpec((None, tm, d), lambda bb, i: (bb, i, 0)),
        out_shape=jax.ShapeDtypeStruct((b, s, d), F32),
        compiler_params=_params("parallel", "parallel"),
        name="final_norm",
    )(x, g.reshape(1, d))


def _rope_tables(n_tokens):
    tok = jnp.arange(n_tokens)
    row = (tok // GRID_W).astype(F32)
    col = (tok % GRID_W).astype(F32)
    half = ROPE_AXIS_DIM // 2
    inv = 1.0 / (ROPE_THETA ** (jnp.arange(0, ROPE_AXIS_DIM, 2, dtype=F32) / ROPE_AXIS_DIM))
    ang_r = row[:, None] * inv
    ang_c = col[:, None] * inv
    cos64 = jnp.concatenate([jnp.cos(ang_r), jnp.cos(ang_r), jnp.cos(ang_c), jnp.cos(ang_c)], axis=-1)
    sin64 = jnp.concatenate([-jnp.sin(ang_r), jnp.sin(ang_r), -jnp.sin(ang_c), jnp.sin(ang_c)], axis=-1)
    assert cos64.shape[1] == 4 * half == DIFF_QK_DIM
    return jnp.tile(cos64, (1, 2)), jnp.tile(sin64, (1, 2))


def _ffn(x, norm_g, sh, sc, gate, wg, wu, wd):
    hidden = _ffn_up(x, norm_g, sh, sc, wg, wu)
    return _proj_res([hidden], wd, x, gate)


def _even_layer(x, xc, mods, cmods, n1g, n2g, w_in, w_out, lam_p, subln, sgu_ng, sgu_nb, sgu_w, sgu_b,
                wg, wu, wd, layer_idx):
    b, s, d = x.shape
    lam_init = 0.8 - 0.6 * math.exp(-0.3 * layer_idx)
    sh1, sc1, g1, sh2, sc2, g2 = mods
    csh1, csc1, cg1, csh2, csc2, cg2 = cmods
    o_k, o_v, o_u = Q_COLS, 2 * Q_COLS, 2 * Q_COLS + A_WIDTH
    w_qkug = jnp.concatenate([w_in[:, :o_v], w_in[:, o_u:]], axis=1).astype(BF16)
    w_vt = w_in[:, o_v:o_u].T.astype(BF16)
    wo = w_out.astype(BF16)
    n1g2 = n1g.reshape(1, d)
    cos_t, sin_t = _rope_tables(s)
    sc_len = xc.shape[1]

    qkug = _inproj(x, n1g2, sh1, sc1, w_qkug, cos_t, sin_t, rope=True)
    cqkug = _inproj(xc, n1g2, csh1, csc1, w_qkug, cos_t[:sc_len], sin_t[:sc_len], rope=False)
    tk = _tile(s, 512)
    vt4 = _inproj_nt(x, n1g2, sh1, sc1, w_vt, tk)
    cvt4 = _inproj_nt(xc, n1g2, csh1, csc1, w_vt, sc_len)
    k4 = qkug.reshape(b, s // tk, tk, qkug.shape[2])
    subln2 = subln.reshape(1, DIFF_V_DIM)

    a_l = _attention(lam_p, subln2, qkug, cqkug, cvt4, k4, vt4, lam_init=lam_init)
    s_l = _sgu(qkug, sgu_ng, sgu_nb, sgu_w, sgu_b)
    x = _proj_res([a_l, s_l], wo, x, g1)
    x = _ffn(x, n2g.reshape(1, d), sh2, sc2, g2, wg, wu, wd)

    a_c = _attention(lam_p, subln2, cqkug, cqkug, cvt4, lam_init=lam_init)
    s_c = _sgu(cqkug, sgu_ng, sgu_nb, sgu_w, sgu_b)
    xc = _proj_res([a_c, s_c], wo, xc, cg1)
    xc = _ffn(xc, n2g.reshape(1, d), csh2, csc2, cg2, wg, wu, wd)
    return x, xc


def _odd_layer(x, mods, n1g, n2g, w_in, conv_w, conv_b, fw1, fb1, fw2, fb2, fw3, fb3, ffreq, fw4, fbias,
               w_out, wg, wu, wd):
    b, s, d = x.shape
    sh1, sc1, g1, sh2, sc2, g2 = mods
    width = w_out.shape[0]
    n1 = 2 * s // DFT_MINOR
    tabs = _dft_tables(n1)
    kf = _hyena_spectra(s, fw1, fb1, fw2, fb2, fw3, fb3, ffreq, fw4, tabs)

    z4 = _inproj_nt(x, n1g.reshape(1, d), sh1, sc1, w_in.T.astype(BF16), DFT_MINOR)
    z4 = z4.transpose(0, 2, 1, 3)
    cw = conv_w.reshape(SHORT_CONV, 3, width)
    cb = conv_b.reshape(3, width)
    cols = []
    for part in range(3):
        cols += [cw[0, part], cw[1, part], cw[2, part], cb[part]]
    cols += [fbias[0], fbias[1], jnp.zeros_like(fbias[0]), jnp.zeros_like(fbias[0])]
    pc = jnp.stack(cols, axis=1)
    y4 = _hyena_core(z4, kf, pc, tabs["f1_half"], tabs["twr"], tabs["twi"], tabs["f2e"], tabs["f2c"],
                     tabs["g1e"])
    yt = y4.reshape(b, width, s)
    x = _proj_res([yt], w_out.astype(BF16), x, g1, transposed=True)
    return _ffn(x, n2g.reshape(1, d), sh2, sc2, g2, wg, wu, wd)


def kernel(x, c, ctx, c_ctx, ada_w, ada_b, norm1, norm2, ffn_w_gate, ffn_w_up, ffn_w_down, e_w_in, e_w_out, e_lambda, e_subln, e_sgu_norm_g, e_sgu_norm_b, e_sgu_w, e_sgu_b, o_w_in, o_conv_w, o_conv_b, o_filt_w1, o_filt_b1, o_filt_w2, o_filt_b2, o_filt_w3, o_filt_b3, o_filt_freq, o_filt_w4, o_filt_bias, o_w_out, final_norm):
    b, s, d = x.shape
    depth = ada_w.shape[0]
    assert b == 2, "the long convolution packs exactly two batches into one complex signal"
    assert depth == 2, "odd layers here never carry the context stream"
    cond = jnp.zeros((8, d), F32).at[:b].set(c).at[b].set(c_ctx)
    mod_all = _adaln_all(cond, ada_w, ada_b)
    xc = ctx
    for i in range(depth):
        j = i // 2
        parts = jnp.split(mod_all[i], 6, axis=-1)
        mods = [p[:b, None, :] for p in parts]
        cmods = [jnp.broadcast_to(p[b:b + 1, None, :], (b, 1, d)) for p in parts]
        wg, wu, wd = (ffn_w_gate[i].astype(BF16), ffn_w_up[i].astype(BF16), ffn_w_down[i].astype(BF16))
        if i % 2 == 0:
            x, xc = _even_layer(x, xc, mods, cmods, norm1[i], norm2[i], e_w_in[j], e_w_out[j], e_lambda[j],
                                e_subln[j], e_sgu_norm_g[j], e_sgu_norm_b[j], e_sgu_w[j], e_sgu_b[j],
                                wg, wu, wd, i)
        else:
            x = _odd_layer(x, mods, norm1[i], norm2[i], o_w_in[j], o_conv_w[j], o_conv_b[j], o_filt_w1[j],
                           o_filt_b1[j], o_filt_w2[j], o_filt_b2[j], o_filt_w3[j], o_filt_b3[j],
                           o_filt_freq[j], o_filt_w4[j], o_filt_bias[j], o_w_out[j], wg, wu, wd)
    return _final_norm(x, final_norm)
```

```python
import functools
import math

import numpy as np
import jax
import jax.numpy as jnp
from jax import lax
from jax.experimental import pallas as pl
from jax.experimental.pallas import tpu as pltpu

F32 = jnp.float32
BF16 = jnp.bfloat16
HIGHEST = lax.Precision.HIGHEST

GRID_W = 64
NORM_EPS = 1e-6
DIFF_HEADS = 8
DIFF_QK_DIM = 64
DIFF_V_DIM = 2 * DIFF_QK_DIM
DIFF_SCALE = DIFF_QK_DIM ** -0.5
A_WIDTH = DIFF_HEADS * DIFF_V_DIM
Q_COLS = DIFF_HEADS * 2 * DIFF_QK_DIM
ROPE_THETA = 10000.0
ROPE_AXIS_DIM = DIFF_QK_DIM // 2
SUBLN_EPS = 1e-5
SGU_GROUPS = 8
SGU_CHUNK = 128
SGU_CH = 128
B_WIDTH = SGU_GROUPS * SGU_CH
LN_EPS = 1e-5
HYENA_ORDER = 2
SHORT_CONV = 3
FILTER_EMB = 33
FILTER_BANDS = (FILTER_EMB - 1) // 2
FILTER_HIDDEN = 64
DECAY_TARGET = 1e-2
MAX_DECAY = math.log(DECAY_TARGET) / 0.3
MIN_DECAY = math.log(DECAY_TARGET) / 1.5

LANES = 128
ONES_ROWS = 16
DFT_MINOR = 256
VMEM_LIMIT = 48 * 1024 * 1024

NT_DIMS = (((1,), (1,)), ((), ()))
TN_DIMS = (((0,), (0,)), ((), ()))


def _params(*sem):
    return pltpu.CompilerParams(dimension_semantics=sem, vmem_limit_bytes=VMEM_LIMIT)


def _tile(n, pref):
    return pref if n % pref == 0 else n


def _adaln_kernel(c_ref, w_ref, b_ref, o_ref):
    a = c_ref[...]
    a = a * jax.nn.sigmoid(a)
    o_ref[...] = jnp.dot(a, w_ref[...], preferred_element_type=F32, precision=HIGHEST) + b_ref[...]


def _adaln_all(cond, ada_w, ada_b):
    depth, d, n6 = ada_w.shape
    tn = _tile(n6, 1024)
    return pl.pallas_call(
        _adaln_kernel,
        grid=(depth, n6 // tn),
        in_specs=[pl.BlockSpec((8, d), lambda l, j: (0, 0)),
                  pl.BlockSpec((None, d, tn), lambda l, j: (l, 0, j)),
                  pl.BlockSpec((None, 1, tn), lambda l, j: (l, 0, j))],
        out_specs=pl.BlockSpec((None, 8, tn), lambda l, j: (l, 0, j)),
        out_shape=jax.ShapeDtypeStruct((depth, 8, n6), F32),
        compiler_params=_params("parallel", "parallel"),
        name="adaln",
    )(cond, ada_w, ada_b.reshape(depth, 1, n6))


def _norm_mod(x_ref, g_ref, sh_ref, sc_ref):
    x = x_ref[...]
    ms = jnp.mean(x * x, axis=-1, keepdims=True)
    y = x * lax.rsqrt(ms + NORM_EPS) * g_ref[...]
    return (y * (1.0 + sc_ref[...]) + sh_ref[...]).astype(BF16)


def _inproj_kernel(x_ref, g_ref, sh_ref, sc_ref, w_ref, cos_ref, sin_ref, o_ref, hs_ref, *,
                   n_q, n_qk, rope):
    j = pl.program_id(2)

    @pl.when(j == 0)
    def _():
        hs_ref[...] = _norm_mod(x_ref, g_ref, sh_ref, sc_ref)

    acc = jnp.dot(hs_ref[...], w_ref[...], preferred_element_type=F32)

    @pl.when(j < n_qk)
    def _():
        a = acc
        if rope:
            tn = a.shape[1]
            lane = lax.broadcasted_iota(jnp.int32, a.shape, 1)
            first = (lane & 31) < 16
            partner = jnp.where(first, pltpu.roll(a, tn - 16, 1), pltpu.roll(a, 16, 1))
            reps = tn // LANES
            a = a * jnp.tile(cos_ref[...], (1, reps)) + partner * jnp.tile(sin_ref[...], (1, reps))
        a = jnp.where(j < n_q, a * DIFF_SCALE, a)
        o_ref[...] = a.astype(o_ref.dtype)

    @pl.when(j >= n_qk)
    def _():
        o_ref[...] = jax.nn.gelu(acc).astype(o_ref.dtype)


def _inproj(x, g, sh, sc, w, cos_t, sin_t, rope):
    b, s, d = x.shape
    n = w.shape[1]
    tm = _tile(s, 1024)
    tn = 512
    kern = functools.partial(_inproj_kernel, n_q=Q_COLS // tn, n_qk=2 * Q_COLS // tn, rope=rope)
    return pl.pallas_call(
        kern,
        grid=(b, s // tm, n // tn),
        in_specs=[pl.BlockSpec((None, tm, d), lambda bb, i, j: (bb, i, 0)),
                  pl.BlockSpec((1, d), lambda bb, i, j: (0, 0)),
                  pl.BlockSpec((None, 1, d), lambda bb, i, j: (bb, 0, 0)),
                  pl.BlockSpec((None, 1, d), lambda bb, i, j: (bb, 0, 0)),
                  pl.BlockSpec((d, tn), lambda bb, i, j: (0, j)),
                  pl.BlockSpec((tm, LANES), lambda bb, i, j: (i, 0)),
                  pl.BlockSpec((tm, LANES), lambda bb, i, j: (i, 0))],
        out_specs=pl.BlockSpec((None, tm, tn), lambda bb, i, j: (bb, i, j)),
        out_shape=jax.ShapeDtypeStruct((b, s, n), BF16),
        scratch_shapes=[pltpu.VMEM((tm, d), BF16)],
        compiler_params=_params("parallel", "parallel", "arbitrary"),
        name="inproj",
    )(x, g, sh, sc, w, cos_t, sin_t)


def _inproj_nt_kernel(x_ref, g_ref, sh_ref, sc_ref, wt_ref, o_ref, hs_ref, *, tl):
    j = pl.program_id(2)

    @pl.when(j == 0)
    def _():
        hs_ref[...] = _norm_mod(x_ref, g_ref, sh_ref, sc_ref)

    acc = lax.dot_general(wt_ref[...], hs_ref[...], NT_DIMS, preferred_element_type=F32)
    for c in range(o_ref.shape[0]):
        o_ref[c] = acc[:, c * tl:(c + 1) * tl].astype(o_ref.dtype)


def _inproj_nt(x, g, sh, sc, wt, tl):
    b, s, d = x.shape
    n = wt.shape[0]
    tm = _tile(s, 1024)
    tn = 512
    kern = functools.partial(_inproj_nt_kernel, tl=tl)
    return pl.pallas_call(
        kern,
        grid=(b, s // tm, n // tn),
        in_specs=[pl.BlockSpec((None, tm, d), lambda bb, i, j: (bb, i, 0)),
                  pl.BlockSpec((1, d), lambda bb, i, j: (0, 0)),
                  pl.BlockSpec((None, 1, d), lambda bb, i, j: (bb, 0, 0)),
                  pl.BlockSpec((None, 1, d), lambda bb, i, j: (bb, 0, 0)),
                  pl.BlockSpec((tn, d), lambda bb, i, j: (j, 0))],
        out_specs=pl.BlockSpec((None, tm // tl, tn, tl), lambda bb, i, j: (bb, i, j, 0)),
        out_shape=jax.ShapeDtypeStruct((b, s // tl, n, tl), BF16),
        scratch_shapes=[pltpu.VMEM((tm, d), BF16)],
        compiler_params=_params("parallel", "parallel", "arbitrary"),
        name="inproj_nt",
    )(x, g, sh, sc, wt)


def _ffn_up_kernel(x_ref, g_ref, sh_ref, sc_ref, wg_ref, wu_ref, o_ref, hs_ref):
    j = pl.program_id(2)

    @pl.when(j == 0)
    def _():
        hs_ref[...] = _norm_mod(x_ref, g_ref, sh_ref, sc_ref)

    hs = hs_ref[...]
    gate = jnp.dot(hs, wg_ref[...], preferred_element_type=F32)
    up = jnp.dot(hs, wu_ref[...], preferred_element_type=F32)
    o_ref[...] = (gate * jax.nn.sigmoid(gate) * up).astype(o_ref.dtype)


def _ffn_up(x, g, sh, sc, wg, wu):
    b, s, d = x.shape
    n = wg.shape[1]
    tm = _tile(s, 1024)
    tn = 512
    return pl.pallas_call(
        _ffn_up_kernel,
        grid=(b, s // tm, n // tn),
        in_specs=[pl.BlockSpec((None, tm, d), lambda bb, i, j: (bb, i, 0)),
                  pl.BlockSpec((1, d), lambda bb, i, j: (0, 0)),
                  pl.BlockSpec((None, 1, d), lambda bb, i, j: (bb, 0, 0)),
                  pl.BlockSpec((None, 1, d), lambda bb, i, j: (bb, 0, 0)),
                  pl.BlockSpec((d, tn), lambda bb, i, j: (0, j)),
                  pl.BlockSpec((d, tn), lambda bb, i, j: (0, j))],
        out_specs=pl.BlockSpec((None, tm, tn), lambda bb, i, j: (bb, i, j)),
        out_shape=jax.ShapeDtypeStruct((b, s, n), BF16),
        scratch_shapes=[pltpu.VMEM((tm, d), BF16)],
        compiler_params=_params("parallel", "parallel", "arbitrary"),
        name="ffn_up",
    )(x, g, sh, sc, wg, wu)


def _proj_res_kernel(*refs, ksizes, transposed):
    n = len(ksizes)
    a_refs = refs[:n]
    w_ref, x_ref, gate_ref, o_ref = refs[n:]
    acc = None
    off = 0
    for a_ref, ks in zip(a_refs, ksizes):
        w = w_ref[off:off + ks, :]
        if transposed:
            part = lax.dot_general(a_ref[...], w, TN_DIMS, preferred_element_type=F32)
        else:
            part = jnp.dot(a_ref[...], w, preferred_element_type=F32)
        acc = part if acc is None else acc + part
        off += ks
    o_ref[...] = x_ref[...] + gate_ref[...] * acc


def _proj_res(a_list, w, x, gate, transposed=False, tm_pref=512, tn_pref=512):
    b, s, d = x.shape
    ksizes = tuple(a.shape[1] if transposed else a.shape[2] for a in a_list)
    ktot = sum(ksizes)
    tm = _tile(s, tm_pref)
    tn = _tile(d, tn_pref)
    if transposed:
        a_specs = [pl.BlockSpec((None, ks, tm), lambda bb, i, j: (bb, 0, i)) for ks in ksizes]
    else:
        a_specs = [pl.BlockSpec((None, tm, ks), lambda bb, i, j: (bb, i, 0)) for ks in ksizes]
    kern = functools.partial(_proj_res_kernel, ksizes=ksizes, transposed=transposed)
    return pl.pallas_call(
        kern,
        grid=(b, s // tm, d // tn),
        in_specs=a_specs + [pl.BlockSpec((ktot, tn), lambda bb, i, j: (0, j)),
                            pl.BlockSpec((None, tm, tn), lambda bb, i, j: (bb, i, j)),
                            pl.BlockSpec((None, 1, tn), lambda bb, i, j: (bb, 0, j))],
        out_specs=pl.BlockSpec((None, tm, tn), lambda bb, i, j: (bb, i, j)),
        out_shape=jax.ShapeDtypeStruct((b, s, d), F32),
        compiler_params=_params("parallel", "parallel", "parallel"),
        name="proj_res",
    )(*a_list, w, x, gate)


def _attn_kernel(*refs, n_chunks, lam_init):
    if n_chunks:
        lam_ref, q_ref, kc_ref, vct_ref, k_ref, vt_ref, g_ref, o_ref, acc_ref, sa_ref, sb_ref = refs
    else:
        lam_ref, q_ref, kc_ref, vct_ref, g_ref, o_ref, acc_ref = refs
    q = q_ref[...]
    tq = q.shape[0]
    dv = DIFF_V_DIM
    qm = (q[:, :DIFF_QK_DIM], q[:, DIFF_QK_DIM:])

    def scores(kblk):
        return tuple(lax.dot_general(kblk[:, m * DIFF_QK_DIM:(m + 1) * DIFF_QK_DIM], qm[m], NT_DIMS,
                                     preferred_element_type=F32) for m in range(2))

    def absorb(s_pair, vtblk, m_pair):
        vext = jnp.concatenate([vtblk, jnp.ones((ONES_ROWS, vtblk.shape[1]), BF16)], axis=0)
        out = []
        for m in range(2):
            m_old = m_pair[m]
            m_new = jnp.maximum(m_old, jnp.max(s_pair[m], axis=0, keepdims=True))
            alpha = jnp.exp(m_old - m_new)
            p = jnp.exp((s_pair[m] - m_new).astype(BF16))
            pv = jnp.dot(vext, p, preferred_element_type=F32)
            acc_ref[m] = alpha * acc_ref[m] + pv
            out.append(m_new)
        return tuple(out)

    def store(ref, s_pair):
        ref[0] = s_pair[0]
        ref[1] = s_pair[1]

    acc_ref[...] = jnp.zeros_like(acc_ref)
    init = jnp.full((1, tq), -1e30, F32)
    m_pair = absorb(scores(kc_ref[...]), vct_ref[0], (init, init))
    if n_chunks:
        assert n_chunks % 2 == 0
        store(sa_ref, scores(k_ref[0]))

        def pair(j, mp, lookahead):
            a = 2 * j
            store(sb_ref, scores(k_ref[a + 1]))
            mp = absorb((sa_ref[0], sa_ref[1]), vt_ref[a], mp)
            if lookahead:
                store(sa_ref, scores(k_ref[a + 2]))
            return absorb((sb_ref[0], sb_ref[1]), vt_ref[a + 1], mp)

        n_pairs = n_chunks // 2
        m_pair = lax.fori_loop(0, n_pairs - 1, lambda j, mp: pair(j, mp, True), m_pair)
        m_pair = pair(n_pairs - 1, m_pair, False)

    lp = lam_ref[...]
    lam = (jnp.exp(jnp.sum(lp[0:1] * lp[1:2], axis=-1, keepdims=True))
           - jnp.exp(jnp.sum(lp[2:3] * lp[3:4], axis=-1, keepdims=True)) + lam_init)
    acc0, acc1 = acc_ref[0], acc_ref[1]
    o = acc0[:dv] / acc0[dv:dv + 1] - lam * (acc1[:dv] / acc1[dv:dv + 1])
    ot = o.T
    ms = jnp.mean(ot * ot, axis=-1, keepdims=True)
    on = ot * lax.rsqrt(ms + SUBLN_EPS) * g_ref[...] * (1.0 - lam_init)
    o_ref[...] = on.astype(o_ref.dtype)


def _attention(lam_p, subln, q_arr, kc_arr, vct_arr, k4=None, vt4=None, *, lam_init):
    b, sq = q_arr.shape[0], q_arr.shape[1]
    sc = kc_arr.shape[1]
    h = DIFF_HEADS
    tq = _tile(sq, 256)
    dv = DIFF_V_DIM
    n_chunks = 0 if k4 is None else k4.shape[1]
    in_specs = [pl.BlockSpec((4, DIFF_QK_DIM), lambda bb, hh, i: (0, 0)),
                pl.BlockSpec((None, tq, dv), lambda bb, hh, i: (bb, i, hh)),
                pl.BlockSpec((None, sc, dv), lambda bb, hh, i: (bb, 0, h + hh)),
                pl.BlockSpec((None, 1, dv, sc), lambda bb, hh, i: (bb, 0, hh, 0))]
    args = [lam_p, q_arr, kc_arr, vct_arr]
    if n_chunks:
        tk = k4.shape[2]
        in_specs += [pl.BlockSpec((None, n_chunks, tk, dv), lambda bb, hh, i: (bb, 0, 0, h + hh)),
                     pl.BlockSpec((None, n_chunks, dv, tk), lambda bb, hh, i: (bb, 0, hh, 0))]
        args += [k4, vt4]
    in_specs.append(pl.BlockSpec((1, dv), lambda bb, hh, i: (0, 0)))
    args.append(subln)
    kern = functools.partial(_attn_kernel, n_chunks=n_chunks, lam_init=lam_init)
    scratch = [pltpu.VMEM((2, dv + ONES_ROWS, tq), F32)]
    if n_chunks:
        scratch += [pltpu.VMEM((2, tk, tq), F32), pltpu.VMEM((2, tk, tq), F32)]
    return pl.pallas_call(
        kern,
        grid=(b, h, sq // tq),
        in_specs=in_specs,
        out_specs=pl.BlockSpec((None, tq, dv), lambda bb, hh, i: (bb, i, hh)),
        out_shape=jax.ShapeDtypeStruct((b, sq, A_WIDTH), BF16),
        scratch_shapes=scratch,
        compiler_params=_params("parallel", "parallel", "parallel"),
        name="diff_attn",
    )(*args)


def _sgu_kernel(u_ref, g_ref, ng_ref, nb_ref, w_ref, bs_ref, o_ref):
    w = w_ref[...]
    for c in range(u_ref.shape[0] // SGU_CHUNK):
        sl = slice(c * SGU_CHUNK, (c + 1) * SGU_CHUNK)
        gg = g_ref[sl, :].astype(F32)
        mu = jnp.mean(gg, axis=-1, keepdims=True)
        dev = gg - mu
        var = jnp.mean(dev * dev, axis=-1, keepdims=True)
        vv = dev * lax.rsqrt(var + LN_EPS) * ng_ref[...] + nb_ref[...]
        mixed = jnp.dot(w, vv.astype(BF16), preferred_element_type=F32) + bs_ref[...]
        o_ref[sl, :] = (u_ref[sl, :].astype(F32) * mixed).astype(o_ref.dtype)


def _sgu(qkug, norm_g, norm_b, w_s, b_s):
    b, s = qkug.shape[0], qkug.shape[1]
    tm = _tile(s, 1024)
    gcount = SGU_GROUPS
    ublk = 2 * Q_COLS // SGU_CH
    return pl.pallas_call(
        _sgu_kernel,
        grid=(b, s // tm, gcount),
        in_specs=[pl.BlockSpec((None, tm, SGU_CH), lambda bb, i, gi: (bb, i, ublk + gi)),
                  pl.BlockSpec((None, tm, SGU_CH), lambda bb, i, gi: (bb, i, ublk + gcount + gi)),
                  pl.BlockSpec((None, 1, SGU_CH), lambda bb, i, gi: (gi, 0, 0)),
                  pl.BlockSpec((None, 1, SGU_CH), lambda bb, i, gi: (gi, 0, 0)),
                  pl.BlockSpec((None, SGU_CHUNK, SGU_CHUNK), lambda bb, i, gi: (gi, 0, 0)),
                  pl.BlockSpec((None, SGU_CHUNK, 1), lambda bb, i, gi: (gi, 0, 0))],
        out_specs=pl.BlockSpec((None, tm, SGU_CH), lambda bb, i, gi: (bb, i, gi)),
        out_shape=jax.ShapeDtypeStruct((b, s, B_WIDTH), BF16),
        compiler_params=_params("parallel", "parallel", "parallel"),
        name="sgu",
    )(qkug, qkug, norm_g.reshape(gcount, 1, SGU_CH), norm_b.reshape(gcount, 1, SGU_CH),
      w_s.astype(BF16), b_s.reshape(gcount, SGU_CHUNK, 1))


def _filter_mlp_kernel(f_ref, w1_ref, b1_ref, w2_ref, b2_ref, w3_ref, b3_ref, fr_ref, o_ref):
    def lin(a, w_ref, b_ref):
        return jnp.dot(a, w_ref[...], preferred_element_type=F32, precision=HIGHEST) + b_ref[...]
    fr = fr_ref[...]
    hcur = jnp.sin(fr[0:1] * lin(f_ref[...], w1_ref, b1_ref))
    hcur = jnp.sin(fr[1:2] * lin(hcur, w2_ref, b2_ref))
    o_ref[...] = jnp.sin(fr[2:3] * lin(hcur, w3_ref, b3_ref))


def _filter_mlp(feats, w1, b1, w2, b2, w3, b3, freq):
    rows, emb = feats.shape
    hid = FILTER_HIDDEN
    tr = _tile(rows, 2048)
    full = lambda shape: pl.BlockSpec(shape, lambda i: (0,) * len(shape))
    return pl.pallas_call(
        _filter_mlp_kernel,
        grid=(rows // tr,),
        in_specs=[pl.BlockSpec((tr, emb), lambda i: (i, 0)),
                  full((emb, hid)), full((1, hid)), full((hid, hid)), full((1, hid)),
                  full((hid, hid)), full((1, hid)), full((3, hid))],
        out_specs=pl.BlockSpec((tr, hid), lambda i: (i, 0)),
        out_shape=jax.ShapeDtypeStruct((rows, hid), F32),
        compiler_params=_params("parallel"),
        name="filter_mlp",
    )(feats, w1, b1.reshape(1, hid), w2, b2.reshape(1, hid), w3, b3.reshape(1, hid), freq)


def _filter_raw_kernel(w4t_ref, h_ref, t_ref, delta_ref, o_ref, *, zero_tile):
    rt = pl.program_id(2)
    raw = lax.dot_general(w4t_ref[...], h_ref[...], NT_DIMS, preferred_element_type=F32,
                          precision=HIGHEST)
    raw = raw * jnp.exp(-(delta_ref[...] * t_ref[...]))
    for c in range(o_ref.shape[0]):
        o_ref[c] = raw[:, c * DFT_MINOR:(c + 1) * DFT_MINOR]

    @pl.when(rt == zero_tile)
    def _():
        col = lax.broadcasted_iota(jnp.int32, (raw.shape[0], DFT_MINOR), 1)
        o_ref[0] = jnp.where(col == 0, 0.0, raw[:, :DFT_MINOR])


def _filter_raw(w4t, h3, t_row, deltas, seq):
    c = w4t.shape[2]
    rows = h3.shape[0]
    tr = _tile(rows // 2, 2048)
    tc = _tile(c, 512)
    half_tiles = seq // tr
    kern = functools.partial(_filter_raw_kernel, zero_tile=half_tiles)
    return pl.pallas_call(
        kern,
        grid=(HYENA_ORDER, c // tc, rows // tr),
        in_specs=[pl.BlockSpec((None, None, tc, FILTER_HIDDEN),
                               lambda n, ci, rt: (n, rt // half_tiles, ci, 0)),
                  pl.BlockSpec((tr, FILTER_HIDDEN), lambda n, ci, rt: (rt, 0)),
                  pl.BlockSpec((1, tr), lambda n, ci, rt: (0, rt)),
                  pl.BlockSpec((tc, 1), lambda n, ci, rt: (ci, 0))],
        out_specs=pl.BlockSpec((None, tr // DFT_MINOR, tc, DFT_MINOR), lambda n, ci, rt: (n, rt, ci, 0)),
        out_shape=jax.ShapeDtypeStruct((HYENA_ORDER, rows // DFT_MINOR, c, DFT_MINOR), F32),
        compiler_params=_params("parallel", "parallel", "parallel"),
        name="filter_raw",
    )(w4t, h3, t_row, deltas)


def _cmul(ar, ai, br, bi):
    return ar * br - ai * bi, ar * bi + ai * br


def _filter_spec_kernel(k_ref, f1_ref, twr_ref, twi_ref, f2_ref, o_ref, *, inv_n):
    n1 = k_ref.shape[1]

    def body(c, carry):
        k = k_ref[c]
        nrm = jnp.sum(jnp.sum(jnp.abs(k), axis=1, keepdims=True), axis=0, keepdims=True)
        kn = (k * (inv_n / nrm)).astype(BF16)
        a = jnp.dot(f1_ref[...], kn, preferred_element_type=F32)
        ar, ai = _cmul(a[:n1], a[n1:], twr_ref[...], twi_ref[...])
        a2 = jnp.concatenate([ar, ai], axis=1).astype(BF16)
        z = jnp.dot(a2, f2_ref[...], preferred_element_type=F32)
        o_ref[c, 0] = z[:, :DFT_MINOR].astype(o_ref.dtype)
        o_ref[c, 1] = z[:, DFT_MINOR:].astype(o_ref.dtype)
        return carry

    lax.fori_loop(0, k_ref.shape[0], body, 0, unroll=2)


def _filter_spec(k2, f1_full, twr, twi, f2e):
    nc, n1, n2 = k2.shape
    tc = 16
    kern = functools.partial(_filter_spec_kernel, inv_n=1.0 / (n1 * n2))
    full = lambda shape: pl.BlockSpec(shape, lambda i: (0,) * len(shape))
    return pl.pallas_call(
        kern,
        grid=(nc // tc,),
        in_specs=[pl.BlockSpec((tc, n1, n2), lambda i: (i, 0, 0)),
                  full((2 * n1, n1)), full((n1, n2)), full((n1, n2)), full((2 * n2, 2 * n2))],
        out_specs=pl.BlockSpec((tc, 2, n1, n2), lambda i: (i, 0, 0, 0)),
        out_shape=jax.ShapeDtypeStruct((nc, 2, n1, n2), BF16),
        compiler_params=_params("parallel"),
        name="filter_spec",
    )(k2, f1_full, twr, twi, f2e)


def _hyena_kernel(zv_ref, z1_ref, z2_ref, kf_ref, pc_ref, f1_ref, twr_ref, twi_ref, f2_ref, f2c_ref,
                  g1_ref, o_ref):
    nb, tc, hr, n2 = zv_ref.shape
    n1 = 2 * hr
    row = lax.broadcasted_iota(jnp.int32, (hr, n2), 0)
    lane = lax.broadcasted_iota(jnp.int32, (hr, n2), 1)
    first_lane, last_lane = lane == 0, lane == n2 - 1
    first_elem = first_lane & (row == 0)
    last_elem = last_lane & (row == hr - 1)

    def short_conv(z, w0, w1, w2, bias):
        prev = pltpu.roll(z, 1, 1)
        prev = jnp.where(first_lane, pltpu.roll(prev, 1, 0), prev)
        prev = jnp.where(first_elem, 0.0, prev)
        nxt = pltpu.roll(z, n2 - 1, 1)
        nxt = jnp.where(last_lane, pltpu.roll(nxt, hr - 1, 0), nxt)
        nxt = jnp.where(last_elem, 0.0, nxt)
        return bias + prev * w0 + z * w1 + nxt * w2

    def long_conv(ur, ui, kr, ki):
        xs = jnp.concatenate([ur, ui], axis=0).astype(BF16)
        a = jnp.dot(f1_ref[...], xs, preferred_element_type=F32)
        ar, ai = _cmul(a[:n1], a[n1:], twr_ref[...], twi_ref[...])
        a2 = jnp.concatenate([ar, ai], axis=1).astype(BF16)
        z = jnp.dot(a2, f2_ref[...], preferred_element_type=F32)
        wr, wi = _cmul(z[:, :n2], z[:, n2:], kr, ki)
        w2 = jnp.concatenate([wr, wi], axis=1).astype(BF16)
        bm = jnp.dot(w2, f2c_ref[...], preferred_element_type=F32)
        br, bi = _cmul(bm[:, :n2], bm[:, n2:], twr_ref[...], -twi_ref[...])
        bs = jnp.concatenate([br, bi], axis=0).astype(BF16)
        y = jnp.dot(g1_ref[...], bs, preferred_element_type=F32)
        return y[:hr], y[hr:]

    def body(c, carry):
        pc = pc_ref[pl.ds(c, 1), :]
        par = lambda idx: pc[:, idx:idx + 1]
        sig = []
        for part, ref in enumerate((zv_ref, z1_ref, z2_ref)):
            o4 = 4 * part
            sig.append([short_conv(ref[bb, c].astype(F32), par(o4), par(o4 + 1), par(o4 + 2), par(o4 + 3))
                        for bb in range(nb)])
        u = sig[0]
        for n in range(HYENA_ORDER):
            kr = kf_ref[n, c, 0].astype(F32)
            ki = kf_ref[n, c, 1].astype(F32)
            yr, yi = long_conv(u[0], u[1], kr, ki)
            fb = par(12 + n)
            gate = sig[1 + n]
            u = [gate[0] * (yr + u[0] * fb), gate[1] * (yi + u[1] * fb)]
        for bb in range(nb):
            o_ref[bb, c] = u[bb].astype(o_ref.dtype)
        return carry

    lax.fori_loop(0, tc, body, 0, unroll=2)


def _hyena_core(z4, kf, pc, f1h, twr, twi, f2e, f2c, g1e):
    nb, c3, hr, n2 = z4.shape
    c = c3 // 3
    n1 = 2 * hr
    tc = 16
    nct = c // tc
    full = lambda shape: pl.BlockSpec(shape, lambda i: (0,) * len(shape))
    zspec = lambda part: pl.BlockSpec((nb, tc, hr, n2), lambda i: (0, i + part * nct, 0, 0))
    return pl.pallas_call(
        _hyena_kernel,
        grid=(nct,),
        in_specs=[zspec(0), zspec(1), zspec(2),
                  pl.BlockSpec((HYENA_ORDER, tc, 2, n1, n2), lambda i: (0, i, 0, 0, 0)),
                  pl.BlockSpec((tc, 16), lambda i: (i, 0)),
                  full((2 * n1, n1)), full((n1, n2)), full((n1, n2)),
                  full((2 * n2, 2 * n2)), full((2 * n2, 2 * n2)), full((n1, 2 * n1))],
        out_specs=pl.BlockSpec((nb, tc, hr, n2), lambda i: (0, i, 0, 0)),
        out_shape=jax.ShapeDtypeStruct((nb, c, hr, n2), BF16),
        compiler_params=_params("parallel"),
        name="hyena_core",
    )(z4, z4, z4, kf, pc, f1h, twr, twi, f2e, f2c, g1e)


def _dft_tables(n1):
    n2 = DFT_MINOR
    n = n1 * n2
    a1 = 2.0 * np.pi * np.outer(np.arange(n1), np.arange(n1)) / n1
    f1r, f1i = np.cos(a1), -np.sin(a1)
    a2 = 2.0 * np.pi * np.outer(np.arange(n2), np.arange(n2)) / n2
    f2r, f2i = np.cos(a2), -np.sin(a2)
    at = 2.0 * np.pi * np.outer(np.arange(n1), np.arange(n2)) / n
    twr, twi = np.cos(at), -np.sin(at)
    hr = n1 // 2
    f1_full = np.concatenate([f1r, f1i], axis=0)
    f1_half = np.block([[f1r[:, :hr], -f1i[:, :hr]], [f1i[:, :hr], f1r[:, :hr]]])
    f2e = np.block([[f2r, f2i], [-f2i, f2r]])
    f2c = np.block([[f2r, -f2i], [f2i, f2r]])
    gr, gi = f1r[:hr, :], -f1i[:hr, :]
    g1e = np.block([[gr, -gi], [gi, gr]])
    bf = lambda m: jnp.asarray(m, F32).astype(BF16)
    return dict(f1_full=bf(f1_full), f1_half=bf(f1_half), f2e=bf(f2e), f2c=bf(f2c), g1e=bf(g1e),
                twr=jnp.asarray(twr, F32), twi=jnp.asarray(twi, F32))


def _hyena_spectra(seq, fw1, fb1, fw2, fb2, fw3, fb3, ffreq, fw4, tabs):
    l = seq
    width = fw4.shape[1] // (2 * HYENA_ORDER)
    idx = jnp.concatenate([jnp.arange(l), l - jnp.arange(l)]) % l
    t_lin = jnp.linspace(0.0, 1.0, l, dtype=F32)[idx]
    w = 2.0 * math.pi * idx.astype(F32)[:, None] / l
    f = jnp.linspace(1e-4, FILTER_BANDS - 1, FILTER_BANDS, dtype=F32)[None, :]
    feats = jnp.concatenate([t_lin[:, None], jnp.cos(f * w), -jnp.sin(f * w)], axis=-1)
    emb_pad = FILTER_HIDDEN
    feats = jnp.pad(feats, ((0, 0), (0, emb_pad - FILTER_EMB)))
    w1p = jnp.pad(fw1, ((0, emb_pad - FILTER_EMB), (0, 0)))
    h3 = _filter_mlp(feats, w1p, fb1, fw2, fb2, fw3, fb3, ffreq)
    w4t = fw4.reshape(FILTER_HIDDEN, HYENA_ORDER, 2, width).transpose(1, 2, 3, 0)
    deltas = jnp.abs(jnp.linspace(MIN_DECAY, MAX_DECAY, width, dtype=F32))[:, None]
    raw = _filter_raw(w4t, h3, t_lin[None, :], deltas, l)
    n1 = raw.shape[1]
    k2 = raw.transpose(0, 2, 1, 3).reshape(HYENA_ORDER * width, n1, DFT_MINOR)
    kf = _filter_spec(k2, tabs["f1_full"], tabs["twr"], tabs["twi"], tabs["f2e"])
    return kf.reshape(HYENA_ORDER, width, 2, n1, DFT_MINOR)


def _rms_kernel(x_ref, g_ref, o_ref):
    x = x_ref[...]
    ms = jnp.mean(x * x, axis=-1, keepdims=True)
    o_ref[...] = x * lax.rsqrt(ms + NORM_EPS) * g_ref[...]


def _final_norm(x, g):
    b, s, d = x.shape
    tm = _tile(s, 1024)
    return pl.pallas_call(
        _rms_kernel,
        grid=(b, s // tm),
        in_specs=[pl.BlockSpec((None, tm, d), lambda bb, i: (bb, i, 0)),
                  pl.BlockSpec((1, d), lambda bb, i: (0, 0))],
        out_specs=pl.BlockSpec((None, tm, d), lambda bb, i: (bb, i, 0)),
        out_shape=jax.ShapeDtypeStruct((b, s, d), F32),
        compiler_params=_params("parallel", "parallel"),
        name="final_norm",
    )(x, g.reshape(1, d))


def _rope_tables(n_tokens):
    tok = jnp.arange(n_tokens)
    row = (tok // GRID_W).astype(F32)
    col = (tok % GRID_W).astype(F32)
    half = ROPE_AXIS_DIM // 2
    inv = 1.0 / (ROPE_THETA ** (jnp.arange(0, ROPE_AXIS_DIM, 2, dtype=F32) / ROPE_AXIS_DIM))
    ang_r = row[:, None] * inv
    ang_c = col[:, None] * inv
    cos64 = jnp.concatenate([jnp.cos(ang_r), jnp.cos(ang_r), jnp.cos(ang_c), jnp.cos(ang_c)], axis=-1)
    sin64 = jnp.concatenate([-jnp.sin(ang_r), jnp.sin(ang_r), -jnp.sin(ang_c), jnp.sin(ang_c)], axis=-1)
    assert cos64.shape[1] == 4 * half == DIFF_QK_DIM
    return jnp.tile(cos64, (1, 2)), jnp.tile(sin64, (1, 2))


def _ffn(x, norm_g, sh, sc, gate, wg, wu, wd):
    hidden = _ffn_up(x, norm_g, sh, sc, wg, wu)
    return _proj_res([hidden], wd, x, gate)


def _even_layer(x, xc, mods, cmods, n1g, n2g, w_in, w_out, lam_p, subln, sgu_ng, sgu_nb, sgu_w, sgu_b,
                wg, wu, wd, layer_idx):
    b, s, d = x.shape
    lam_init = 0.8 - 0.6 * math.exp(-0.3 * layer_idx)
    sh1, sc1, g1, sh2, sc2, g2 = mods
    csh1, csc1, cg1, csh2, csc2, cg2 = cmods
    o_k, o_v, o_u = Q_COLS, 2 * Q_COLS, 2 * Q_COLS + A_WIDTH
    w_qkug = jnp.concatenate([w_in[:, :o_v], w_in[:, o_u:]], axis=1).astype(BF16)
    w_vt = w_in[:, o_v:o_u].T.astype(BF16)
    wo = w_out.astype(BF16)
    n1g2 = n1g.reshape(1, d)
    cos_t, sin_t = _rope_tables(s)
    sc_len = xc.shape[1]

    qkug = _inproj(x, n1g2, sh1, sc1, w_qkug, cos_t, sin_t, rope=True)
    cqkug = _inproj(xc, n1g2, csh1, csc1, w_qkug, cos_t[:sc_len], sin_t[:sc_len], rope=False)
    tk = _tile(s, 512)
    vt4 = _inproj_nt(x, n1g2, sh1, sc1, w_vt, tk)
    cvt4 = _inproj_nt(xc, n1g2, csh1, csc1, w_vt, sc_len)
    k4 = qkug.reshape(b, s // tk, tk, qkug.shape[2])
    subln2 = subln.reshape(1, DIFF_V_DIM)

    a_l = _attention(lam_p, subln2, qkug, cqkug, cvt4, k4, vt4, lam_init=lam_init)
    s_l = _sgu(qkug, sgu_ng, sgu_nb, sgu_w, sgu_b)
    x = _proj_res([a_l, s_l], wo, x, g1)
    x = _ffn(x, n2g.reshape(1, d), sh2, sc2, g2, wg, wu, wd)

    a_c = _attention(lam_p, subln2, cqkug, cqkug, cvt4, lam_init=lam_init)
    s_c = _sgu(cqkug, sgu_ng, sgu_nb, sgu_w, sgu_b)
    xc = _proj_res([a_c, s_c], wo, xc, cg1)
    xc = _ffn(xc, n2g.reshape(1, d), csh2, csc2, cg2, wg, wu, wd)
    return x, xc


def _odd_layer(x, mods, n1g, n2g, w_in, conv_w, conv_b, fw1, fb1, fw2, fb2, fw3, fb3, ffreq, fw4, fbias,
               w_out, wg, wu, wd):
    b, s, d = x.shape
    sh1, sc1, g1, sh2, sc2, g2 = mods
    width = w_out.shape[0]
    n1 = 2 * s // DFT_MINOR
    tabs = _dft_tables(n1)
    kf = _hyena_spectra(s, fw1, fb1, fw2, fb2, fw3, fb3, ffreq, fw4, tabs)

    z4 = _inproj_nt(x, n1g.reshape(1, d), sh1, sc1, w_in.T.astype(BF16), DFT_MINOR)
    z4 = z4.transpose(0, 2, 1, 3)
    cw = conv_w.reshape(SHORT_CONV, 3, width)
    cb = conv_b.reshape(3, width)
    cols = []
    for part in range(3):
        cols += [cw[0, part], cw[1, part], cw[2, part], cb[part]]
    cols += [fbias[0], fbias[1], jnp.zeros_like(fbias[0]), jnp.zeros_like(fbias[0])]
    pc = jnp.stack(cols, axis=1)
    y4 = _hyena_core(z4, kf, pc, tabs["f1_half"], tabs["twr"], tabs["twi"], tabs["f2e"], tabs["f2c"],
                     tabs["g1e"])
    yt = y4.reshape(b, width, s)
    x = _proj_res([yt], w_out.astype(BF16), x, g1, transposed=True)
    return _ffn(x, n2g.reshape(1, d), sh2, sc2, g2, wg, wu, wd)


def kernel(x, c, ctx, c_ctx, ada_w, ada_b, norm1, norm2, ffn_w_gate, ffn_w_up, ffn_w_down, e_w_in, e_w_out, e_lambda, e_subln, e_sgu_norm_g, e_sgu_norm_b, e_sgu_w, e_sgu_b, o_w_in, o_conv_w, o_conv_b, o_filt_w1, o_filt_b1, o_filt_w2, o_filt_b2, o_filt_w3, o_filt_b3, o_filt_freq, o_filt_w4, o_filt_bias, o_w_out, final_norm):
    b, s, d = x.shape
    depth = ada_w.shape[0]
    assert b == 2, "the long convolution packs exactly two batches into one complex signal"
    assert depth == 2, "odd layers here never carry the context stream"
    cond = jnp.zeros((8, d), F32).at[:b].set(c).at[b].set(c_ctx)
    mod_all = _adaln_all(cond, ada_w, ada_b)
    xc = ctx
    for i in range(depth):
        j = i // 2
        parts = jnp.split(mod_all[i], 6, axis=-1)
        mods = [p[:b, None, :] for p in parts]
        cmods = [jnp.broadcast_to(p[b:b + 1, None, :], (b, 1, d)) for p in parts]
        wg, wu, wd = (ffn_w_gate[i].astype(BF16), ffn_w_up[i].astype(BF16), ffn_w_down[i].astype(BF16))
        if i % 2 == 0:
            x, xc = _even_layer(x, xc, mods, cmods, norm1[i], norm2[i], e_w_in[j], e_w_out[j], e_lambda[j],
                                e_subln[j], e_sgu_norm_g[j], e_sgu_norm_b[j], e_sgu_w[j], e_sgu_b[j],
                                wg, wu, wd, i)
        else:
            x = _odd_layer(x, mods, norm1[i], norm2[i], o_w_in[j], o_conv_w[j], o_conv_b[j], o_filt_w1[j],
                           o_filt_b1[j], o_filt_w2[j], o_filt_b2[j], o_filt_w3[j], o_filt_b3[j],
                           o_filt_freq[j], o_filt_w4[j], o_filt_bias[j], o_w_out[j], wg, wu, wd)
    return _final_norm(x, final_norm)
```

```python
import functools
import math

import numpy as np
import jax
import jax.numpy as jnp
from jax import lax
from jax.experimental import pallas as pl
from jax.experimental.pallas import tpu as pltpu

F32 = jnp.float32
BF16 = jnp.bfloat16
HIGHEST = lax.Precision.HIGHEST

GRID_W = 64
NORM_EPS = 1e-6
DIFF_HEADS = 8
DIFF_QK_DIM = 64
DIFF_V_DIM = 2 * DIFF_QK_DIM
DIFF_SCALE = DIFF_QK_DIM ** -0.5
A_WIDTH = DIFF_HEADS * DIFF_V_DIM
Q_COLS = DIFF_HEADS * 2 * DIFF_QK_DIM
ROPE_THETA = 10000.0
ROPE_AXIS_DIM = DIFF_QK_DIM // 2
SUBLN_EPS = 1e-5
SGU_GROUPS = 8
SGU_CHUNK = 128
SGU_CH = 128
B_WIDTH = SGU_GROUPS * SGU_CH
LN_EPS = 1e-5
HYENA_ORDER = 2
SHORT_CONV = 3
FILTER_EMB = 33
FILTER_BANDS = (FILTER_EMB - 1) // 2
FILTER_HIDDEN = 64
DECAY_TARGET = 1e-2
MAX_DECAY = math.log(DECAY_TARGET) / 0.3
MIN_DECAY = math.log(DECAY_TARGET) / 1.5

LANES = 128
ONES_ROWS = 16
DFT_MINOR = 256
VMEM_LIMIT = 48 * 1024 * 1024

NT_DIMS = (((1,), (1,)), ((), ()))
TN_DIMS = (((0,), (0,)), ((), ()))


def _params(*sem):
    return pltpu.CompilerParams(dimension_semantics=sem, vmem_limit_bytes=VMEM_LIMIT)


def _tile(n, pref):
    return pref if n % pref == 0 else n


def _adaln_kernel(c_ref, w_ref, b_ref, o_ref):
    a = c_ref[...]
    a = a * jax.nn.sigmoid(a)
    o_ref[...] = jnp.dot(a, w_ref[...], preferred_element_type=F32, precision=HIGHEST) + b_ref[...]


def _adaln_all(cond, ada_w, ada_b):
    depth, d, n6 = ada_w.shape
    tn = _tile(n6, 1024)
    return pl.pallas_call(
        _adaln_kernel,
        grid=(depth, n6 // tn),
        in_specs=[pl.BlockSpec((8, d), lambda l, j: (0, 0)),
                  pl.BlockSpec((None, d, tn), lambda l, j: (l, 0, j)),
                  pl.BlockSpec((None, 1, tn), lambda l, j: (l, 0, j))],
        out_specs=pl.BlockSpec((None, 8, tn), lambda l, j: (l, 0, j)),
        out_shape=jax.ShapeDtypeStruct((depth, 8, n6), F32),
        compiler_params=_params("parallel", "parallel"),
        name="adaln",
    )(cond, ada_w, ada_b.reshape(depth, 1, n6))


def _norm_mod(x_ref, g_ref, sh_ref, sc_ref):
    x = x_ref[...]
    ms = jnp.mean(x * x, axis=-1, keepdims=True)
    y = x * lax.rsqrt(ms + NORM_EPS) * g_ref[...]
    return (y * (1.0 + sc_ref[...]) + sh_ref[...]).astype(BF16)


def _inproj_kernel(x_ref, g_ref, sh_ref, sc_ref, w_ref, cos_ref, sin_ref, o_ref, hs_ref, *,
                   n_q, n_qk, rope):
    j = pl.program_id(2)

    @pl.when(j == 0)
    def _():
        hs_ref[...] = _norm_mod(x_ref, g_ref, sh_ref, sc_ref)

    acc = jnp.dot(hs_ref[...], w_ref[...], preferred_element_type=F32)

    @pl.when(j < n_qk)
    def _():
        a = acc
        if rope:
            tn = a.shape[1]
            lane = lax.broadcasted_iota(jnp.int32, a.shape, 1)
            first = (lane & 31) < 16
            partner = jnp.where(first, pltpu.roll(a, tn - 16, 1), pltpu.roll(a, 16, 1))
            reps = tn // LANES
            a = a * jnp.tile(cos_ref[...], (1, reps)) + partner * jnp.tile(sin_ref[...], (1, reps))
        a = jnp.where(j < n_q, a * DIFF_SCALE, a)
        o_ref[...] = a.astype(o_ref.dtype)

    @pl.when(j >= n_qk)
    def _():
        o_ref[...] = jax.nn.gelu(acc).astype(o_ref.dtype)


def _inproj(x, g, sh, sc, w, cos_t, sin_t, rope):
    b, s, d = x.shape
    n = w.shape[1]
    tm = _tile(s, 1024)
    tn = 512
    kern = functools.partial(_inproj_kernel, n_q=Q_COLS // tn, n_qk=2 * Q_COLS // tn, rope=rope)
    return pl.pallas_call(
        kern,
        grid=(b, s // tm, n // tn),
        in_specs=[pl.BlockSpec((None, tm, d), lambda bb, i, j: (bb, i, 0)),
                  pl.BlockSpec((1, d), lambda bb, i, j: (0, 0)),
                  pl.BlockSpec((None, 1, d), lambda bb, i, j: (bb, 0, 0)),
                  pl.BlockSpec((None, 1, d), lambda bb, i, j: (bb, 0, 0)),
                  pl.BlockSpec((d, tn), lambda bb, i, j: (0, j)),
                  pl.BlockSpec((tm, LANES), lambda bb, i, j: (i, 0)),
                  pl.BlockSpec((tm, LANES), lambda bb, i, j: (i, 0))],
        out_specs=pl.BlockSpec((None, tm, tn), lambda bb, i, j: (bb, i, j)),
        out_shape=jax.ShapeDtypeStruct((b, s, n), BF16),
        scratch_shapes=[pltpu.VMEM((tm, d), BF16)],
        compiler_params=_params("parallel", "parallel", "arbitrary"),
        name="inproj",
    )(x, g, sh, sc, w, cos_t, sin_t)


def _inproj_nt_kernel(x_ref, g_ref, sh_ref, sc_ref, wt_ref, o_ref, hs_ref, *, tl):
    j = pl.program_id(2)

    @pl.when(j == 0)
    def _():
        hs_ref[...] = _norm_mod(x_ref, g_ref, sh_ref, sc_ref)

    acc = lax.dot_general(wt_ref[...], hs_ref[...], NT_DIMS, preferred_element_type=F32)
    for c in range(o_ref.shape[0]):
        o_ref[c] = acc[:, c * tl:(c + 1) * tl].astype(o_ref.dtype)


def _inproj_nt(x, g, sh, sc, wt, tl):
    b, s, d = x.shape
    n = wt.shape[0]
    tm = _tile(s, 1024)
    tn = 512
    kern = functools.partial(_inproj_nt_kernel, tl=tl)
    return pl.pallas_call(
        kern,
        grid=(b, s // tm, n // tn),
        in_specs=[pl.BlockSpec((None, tm, d), lambda bb, i, j: (bb, i, 0)),
                  pl.BlockSpec((1, d), lambda bb, i, j: (0, 0)),
                  pl.BlockSpec((None, 1, d), lambda bb, i, j: (bb, 0, 0)),
                  pl.BlockSpec((None, 1, d), lambda bb, i, j: (bb, 0, 0)),
                  pl.BlockSpec((tn, d), lambda bb, i, j: (j, 0))],
        out_specs=pl.BlockSpec((None, tm // tl, tn, tl), lambda bb, i, j: (bb, i, j, 0)),
        out_shape=jax.ShapeDtypeStruct((b, s // tl, n, tl), BF16),
        scratch_shapes=[pltpu.VMEM((tm, d), BF16)],
        compiler_params=_params("parallel", "parallel", "arbitrary"),
        name="inproj_nt",
    )(x, g, sh, sc, wt)


def _ffn_up_kernel(x_ref, g_ref, sh_ref, sc_ref, wg_ref, wu_ref, o_ref, hs_ref):
    j = pl.program_id(2)

    @pl.when(j == 0)
    def _():
        hs_ref[...] = _norm_mod(x_ref, g_ref, sh_ref, sc_ref)

    hs = hs_ref[...]
    gate = jnp.dot(hs, wg_ref[...], preferred_element_type=F32)
    up = jnp.dot(hs, wu_ref[...], preferred_element_type=F32)
    o_ref[...] = (gate * jax.nn.sigmoid(gate) * up).astype(o_ref.dtype)


def _ffn_up(x, g, sh, sc, wg, wu):
    b, s, d = x.shape
    n = wg.shape[1]
    tm = _tile(s, 1024)
    tn = 512
    return pl.pallas_call(
        _ffn_up_kernel,
        grid=(b, s // tm, n // tn),
        in_specs=[pl.BlockSpec((None, tm, d), lambda bb, i, j: (bb, i, 0)),
                  pl.BlockSpec((1, d), lambda bb, i, j: (0, 0)),
                  pl.BlockSpec((None, 1, d), lambda bb, i, j: (bb, 0, 0)),
                  pl.BlockSpec((None, 1, d), lambda bb, i, j: (bb, 0, 0)),
                  pl.BlockSpec((d, tn), lambda bb, i, j: (0, j)),
                  pl.BlockSpec((d, tn), lambda bb, i, j: (0, j))],
        out_specs=pl.BlockSpec((None, tm, tn), lambda bb, i, j: (bb, i, j)),
        out_shape=jax.ShapeDtypeStruct((b, s, n), BF16),
        scratch_shapes=[pltpu.VMEM((tm, d), BF16)],
        compiler_params=_params("parallel", "parallel", "arbitrary"),
        name="ffn_up",
    )(x, g, sh, sc, wg, wu)


def _proj_res_kernel(*refs, ksizes, transposed):
    n = len(ksizes)
    a_refs = refs[:n]
    w_ref, x_ref, gate_ref, o_ref = refs[n:]
    acc = None
    off = 0
    for a_ref, ks in zip(a_refs, ksizes):
        w = w_ref[off:off + ks, :]
        if transposed:
            part = lax.dot_general(a_ref[...], w, TN_DIMS, preferred_element_type=F32)
        else:
            part = jnp.dot(a_ref[...], w, preferred_element_type=F32)
        acc = part if acc is None else acc + part
        off += ks
    o_ref[...] = x_ref[...] + gate_ref[...] * acc


def _proj_res(a_list, w, x, gate, transposed=False, tm_pref=512, tn_pref=512):
    b, s, d = x.shape
    ksizes = tuple(a.shape[1] if transposed else a.shape[2] for a in a_list)
    ktot = sum(ksizes)
    tm = _tile(s, tm_pref)
    tn = _tile(d, tn_pref)
    if transposed:
        a_specs = [pl.BlockSpec((None, ks, tm), lambda bb, i, j: (bb, 0, i)) for ks in ksizes]
    else:
        a_specs = [pl.BlockSpec((None, tm, ks), lambda bb, i, j: (bb, i, 0)) for ks in ksizes]
    kern = functools.partial(_proj_res_kernel, ksizes=ksizes, transposed=transposed)
    return pl.pallas_call(
        kern,
        grid=(b, s // tm, d // tn),
        in_specs=a_specs + [pl.BlockSpec((ktot, tn), lambda bb, i, j: (0, j)),
                            pl.BlockSpec((None, tm, tn), lambda bb, i, j: (bb, i, j)),
                            pl.BlockSpec((None, 1, tn), lambda bb, i, j: (bb, 0, j))],
        out_specs=pl.BlockSpec((None, tm, tn), lambda bb, i, j: (bb, i, j)),
        out_shape=jax.ShapeDtypeStruct((b, s, d), F32),
        compiler_params=_params("parallel", "parallel", "parallel"),
        name="proj_res",
    )(*a_list, w, x, gate)


def _attn_kernel(*refs, n_chunks, lam_init):
    if n_chunks:
        lam_ref, q_ref, kc_ref, vct_ref, k_ref, vt_ref, g_ref, o_ref, acc_ref = refs[:9]
        s_refs = refs[9:]
    else:
        lam_ref, q_ref, kc_ref, vct_ref, g_ref, o_ref, acc_ref = refs
    q = q_ref[...]
    tq = q.shape[0]
    dv = DIFF_V_DIM
    qm = (q[:, :DIFF_QK_DIM], q[:, DIFF_QK_DIM:])

    def scores(kblk):
        return tuple(lax.dot_general(kblk[:, m * DIFF_QK_DIM:(m + 1) * DIFF_QK_DIM], qm[m], NT_DIMS,
                                     preferred_element_type=F32) for m in range(2))

    def absorb(s_pair, vtblk, m_pair):
        vext = jnp.concatenate([vtblk, jnp.ones((ONES_ROWS, vtblk.shape[1]), BF16)], axis=0)
        out = []
        for m in range(2):
            m_old = m_pair[m]
            m_new = jnp.maximum(m_old, jnp.max(s_pair[m], axis=0, keepdims=True))
            alpha = jnp.exp(m_old - m_new)
            p = jnp.exp((s_pair[m] - m_new).astype(BF16))
            pv = jnp.dot(vext, p, preferred_element_type=F32)
            acc_ref[m] = alpha * acc_ref[m] + pv
            out.append(m_new)
        return tuple(out)

    def store(ref, s_pair):
        ref[0] = s_pair[0]
        ref[1] = s_pair[1]

    acc_ref[...] = jnp.zeros_like(acc_ref)
    init = jnp.full((1, tq), -1e30, F32)
    m_pair = absorb(scores(kc_ref[...]), vct_ref[0], (init, init))
    if n_chunks:
        group = len(s_refs)
        assert n_chunks % group == 0
        store(s_refs[0], scores(k_ref[0]))

        def trip(j, mp, lookahead):
            a = group * j
            for k in range(group):
                if k + 1 < group:
                    store(s_refs[k + 1], scores(k_ref[a + k + 1]))
                mp = absorb((s_refs[k][0], s_refs[k][1]), vt_ref[a + k], mp)
                if k == 0 and lookahead:
                    store(s_refs[0], scores(k_ref[a + group]))
            return mp

        n_trips = n_chunks // group
        m_pair = lax.fori_loop(0, n_trips - 1, lambda j, mp: trip(j, mp, True), m_pair)
        m_pair = trip(n_trips - 1, m_pair, False)

    lp = lam_ref[...]
    lam = (jnp.exp(jnp.sum(lp[0:1] * lp[1:2], axis=-1, keepdims=True))
           - jnp.exp(jnp.sum(lp[2:3] * lp[3:4], axis=-1, keepdims=True)) + lam_init)
    acc0, acc1 = acc_ref[0], acc_ref[1]
    o = acc0[:dv] / acc0[dv:dv + 1] - lam * (acc1[:dv] / acc1[dv:dv + 1])
    ot = o.T
    ms = jnp.mean(ot * ot, axis=-1, keepdims=True)
    on = ot * lax.rsqrt(ms + SUBLN_EPS) * g_ref[...] * (1.0 - lam_init)
    o_ref[...] = on.astype(o_ref.dtype)


def _attention(lam_p, subln, q_arr, kc_arr, vct_arr, k4=None, vt4=None, *, lam_init):
    b, sq = q_arr.shape[0], q_arr.shape[1]
    sc = kc_arr.shape[1]
    h = DIFF_HEADS
    tq = _tile(sq, 256)
    dv = DIFF_V_DIM
    n_chunks = 0 if k4 is None else k4.shape[1]
    in_specs = [pl.BlockSpec((4, DIFF_QK_DIM), lambda bb, hh, i: (0, 0)),
                pl.BlockSpec((None, tq, dv), lambda bb, hh, i: (bb, i, hh)),
                pl.BlockSpec((None, sc, dv), lambda bb, hh, i: (bb, 0, h + hh)),
                pl.BlockSpec((None, 1, dv, sc), lambda bb, hh, i: (bb, 0, hh, 0))]
    args = [lam_p, q_arr, kc_arr, vct_arr]
    if n_chunks:
        tk = k4.shape[2]
        in_specs += [pl.BlockSpec((None, n_chunks, tk, dv), lambda bb, hh, i: (bb, 0, 0, h + hh)),
                     pl.BlockSpec((None, n_chunks, dv, tk), lambda bb, hh, i: (bb, 0, hh, 0))]
        args += [k4, vt4]
    in_specs.append(pl.BlockSpec((1, dv), lambda bb, hh, i: (0, 0)))
    args.append(subln)
    kern = functools.partial(_attn_kernel, n_chunks=n_chunks, lam_init=lam_init)
    scratch = [pltpu.VMEM((2, dv + ONES_ROWS, tq), F32)]
    if n_chunks:
        group = 4 if n_chunks % 4 == 0 else 2
        scratch += [pltpu.VMEM((2, tk, tq), F32) for _ in range(group)]
    return pl.pallas_call(
        kern,
        grid=(b, h, sq // tq),
        in_specs=in_specs,
        out_specs=pl.BlockSpec((None, tq, dv), lambda bb, hh, i: (bb, i, hh)),
        out_shape=jax.ShapeDtypeStruct((b, sq, A_WIDTH), BF16),
        scratch_shapes=scratch,
        compiler_params=_params("parallel", "parallel", "parallel"),
        name="diff_attn",
    )(*args)


def _sgu_kernel(u_ref, g_ref, ng_ref, nb_ref, w_ref, bs_ref, o_ref):
    w = w_ref[...]
    for c in range(u_ref.shape[0] // SGU_CHUNK):
        sl = slice(c * SGU_CHUNK, (c + 1) * SGU_CHUNK)
        gg = g_ref[sl, :].astype(F32)
        mu = jnp.mean(gg, axis=-1, keepdims=True)
        dev = gg - mu
        var = jnp.mean(dev * dev, axis=-1, keepdims=True)
        vv = dev * lax.rsqrt(var + LN_EPS) * ng_ref[...] + nb_ref[...]
        mixed = jnp.dot(w, vv.astype(BF16), preferred_element_type=F32) + bs_ref[...]
        o_ref[sl, :] = (u_ref[sl, :].astype(F32) * mixed).astype(o_ref.dtype)


def _sgu(qkug, norm_g, norm_b, w_s, b_s):
    b, s = qkug.shape[0], qkug.shape[1]
    tm = _tile(s, 1024)
    gcount = SGU_GROUPS
    ublk = 2 * Q_COLS // SGU_CH
    return pl.pallas_call(
        _sgu_kernel,
        grid=(b, s // tm, gcount),
        in_specs=[pl.BlockSpec((None, tm, SGU_CH), lambda bb, i, gi: (bb, i, ublk + gi)),
                  pl.BlockSpec((None, tm, SGU_CH), lambda bb, i, gi: (bb, i, ublk + gcount + gi)),
                  pl.BlockSpec((None, 1, SGU_CH), lambda bb, i, gi: (gi, 0, 0)),
                  pl.BlockSpec((None, 1, SGU_CH), lambda bb, i, gi: (gi, 0, 0)),
                  pl.BlockSpec((None, SGU_CHUNK, SGU_CHUNK), lambda bb, i, gi: (gi, 0, 0)),
                  pl.BlockSpec((None, SGU_CHUNK, 1), lambda bb, i, gi: (gi, 0, 0))],
        out_specs=pl.BlockSpec((None, tm, SGU_CH), lambda bb, i, gi: (bb, i, gi)),
        out_shape=jax.ShapeDtypeStruct((b, s, B_WIDTH), BF16),
        compiler_params=_params("parallel", "parallel", "parallel"),
        name="sgu",
    )(qkug, qkug, norm_g.reshape(gcount, 1, SGU_CH), norm_b.reshape(gcount, 1, SGU_CH),
      w_s.astype(BF16), b_s.reshape(gcount, SGU_CHUNK, 1))


def _filter_mlp_kernel(f_ref, w1_ref, b1_ref, w2_ref, b2_ref, w3_ref, b3_ref, fr_ref, o_ref):
    def lin(a, w_ref, b_ref):
        return jnp.dot(a, w_ref[...], preferred_element_type=F32, precision=HIGHEST) + b_ref[...]
    fr = fr_ref[...]
    hcur = jnp.sin(fr[0:1] * lin(f_ref[...], w1_ref, b1_ref))
    hcur = jnp.sin(fr[1:2] * lin(hcur, w2_ref, b2_ref))
    o_ref[...] = jnp.sin(fr[2:3] * lin(hcur, w3_ref, b3_ref))


def _filter_mlp(feats, w1, b1, w2, b2, w3, b3, freq):
    rows, emb = feats.shape
    hid = FILTER_HIDDEN
    tr = _tile(rows, 2048)
    full = lambda shape: pl.BlockSpec(shape, lambda i: (0,) * len(shape))
    return pl.pallas_call(
        _filter_mlp_kernel,
        grid=(rows // tr,),
        in_specs=[pl.BlockSpec((tr, emb), lambda i: (i, 0)),
                  full((emb, hid)), full((1, hid)), full((hid, hid)), full((1, hid)),
                  full((hid, hid)), full((1, hid)), full((3, hid))],
        out_specs=pl.BlockSpec((tr, hid), lambda i: (i, 0)),
        out_shape=jax.ShapeDtypeStruct((rows, hid), F32),
        compiler_params=_params("parallel"),
        name="filter_mlp",
    )(feats, w1, b1.reshape(1, hid), w2, b2.reshape(1, hid), w3, b3.reshape(1, hid), freq)


def _filter_raw_kernel(w4t_ref, h_ref, t_ref, delta_ref, o_ref, *, zero_tile):
    rt = pl.program_id(2)
    raw = lax.dot_general(w4t_ref[...], h_ref[...], NT_DIMS, preferred_element_type=F32,
                          precision=HIGHEST)
    raw = raw * jnp.exp(-(delta_ref[...] * t_ref[...]))
    for c in range(o_ref.shape[0]):
        o_ref[c] = raw[:, c * DFT_MINOR:(c + 1) * DFT_MINOR]

    @pl.when(rt == zero_tile)
    def _():
        col = lax.broadcasted_iota(jnp.int32, (raw.shape[0], DFT_MINOR), 1)
        o_ref[0] = jnp.where(col == 0, 0.0, raw[:, :DFT_MINOR])


def _filter_raw(w4t, h3, t_row, deltas, seq):
    c = w4t.shape[2]
    rows = h3.shape[0]
    tr = _tile(rows // 2, 2048)
    tc = _tile(c, 512)
    half_tiles = seq // tr
    kern = functools.partial(_filter_raw_kernel, zero_tile=half_tiles)
    return pl.pallas_call(
        kern,
        grid=(HYENA_ORDER, c // tc, rows // tr),
        in_specs=[pl.BlockSpec((None, None, tc, FILTER_HIDDEN),
                               lambda n, ci, rt: (n, rt // half_tiles, ci, 0)),
                  pl.BlockSpec((tr, FILTER_HIDDEN), lambda n, ci, rt: (rt, 0)),
                  pl.BlockSpec((1, tr), lambda n, ci, rt: (0, rt)),
                  pl.BlockSpec((tc, 1), lambda n, ci, rt: (ci, 0))],
        out_specs=pl.BlockSpec((None, tr // DFT_MINOR, tc, DFT_MINOR), lambda n, ci, rt: (n, rt, ci, 0)),
        out_shape=jax.ShapeDtypeStruct((HYENA_ORDER, rows // DFT_MINOR, c, DFT_MINOR), F32),
        compiler_params=_params("parallel", "parallel", "parallel"),
        name="filter_raw",
    )(w4t, h3, t_row, deltas)


def _cmul(ar, ai, br, bi):
    return ar * br - ai * bi, ar * bi + ai * br


def _store_complex(ref, c, n1, re, im):
    r0 = pl.multiple_of(c * n1, n1)
    ref[pl.ds(r0, n1), :DFT_MINOR] = re.astype(ref.dtype)
    ref[pl.ds(r0, n1), DFT_MINOR:] = im.astype(ref.dtype)


def _load_complex(ref, c, n1):
    r0 = pl.multiple_of(c * n1, n1)
    tile = ref[pl.ds(r0, n1), :]
    return tile[:, :DFT_MINOR], tile[:, DFT_MINOR:]


def _filter_spec_kernel(k_ref, f1_ref, twr_ref, twi_ref, f2_ref, o_ref, a2_ref, *, inv_n):
    tc, n1, n2 = k_ref.shape

    def left(c, carry):
        k = k_ref[c]
        nrm = jnp.sum(jnp.sum(jnp.abs(k), axis=1, keepdims=True), axis=0, keepdims=True)
        kn = (k * (inv_n / nrm)).astype(BF16)
        a = jnp.dot(f1_ref[...], kn, preferred_element_type=F32)
        ar, ai = _cmul(a[:n1], a[n1:], twr_ref[...], twi_ref[...])
        _store_complex(a2_ref, c, n1, ar, ai)
        return carry

    lax.fori_loop(0, tc, left, 0, unroll=4)
    z = jnp.dot(a2_ref[...], f2_ref[...], preferred_element_type=F32).reshape(tc, n1, 2 * n2)
    o_ref[:, 0] = z[:, :, :n2].astype(o_ref.dtype)
    o_ref[:, 1] = z[:, :, n2:].astype(o_ref.dtype)


def _filter_spec(k2, f1_full, twr, twi, f2e):
    nc, n1, n2 = k2.shape
    tc = 16
    kern = functools.partial(_filter_spec_kernel, inv_n=1.0 / (n1 * n2))
    full = lambda shape: pl.BlockSpec(shape, lambda i: (0,) * len(shape))
    return pl.pallas_call(
        kern,
        grid=(nc // tc,),
        in_specs=[pl.BlockSpec((tc, n1, n2), lambda i: (i, 0, 0)),
                  full((2 * n1, n1)), full((n1, n2)), full((n1, n2)), full((2 * n2, 2 * n2))],
        out_specs=pl.BlockSpec((tc, 2, n1, n2), lambda i: (i, 0, 0, 0)),
        out_shape=jax.ShapeDtypeStruct((nc, 2, n1, n2), BF16),
        scratch_shapes=[pltpu.VMEM((tc * n1, 2 * n2), BF16)],
        compiler_params=_params("parallel"),
        name="filter_spec",
    )(k2, f1_full, twr, twi, f2e)


def _hyena_kernel(zv_ref, z1_ref, z2_ref, kf_ref, pc_ref, f1_ref, twr_ref, twi_ref, f2_ref, f2c_ref,
                  g1_ref, o_ref, u_ref, gate_ref, a2_ref, z_ref):
    nb, tc, hr, n2 = zv_ref.shape
    n1 = 2 * hr
    row = lax.broadcasted_iota(jnp.int32, (hr, n2), 0)
    lane = lax.broadcasted_iota(jnp.int32, (hr, n2), 1)
    first_lane, last_lane = lane == 0, lane == n2 - 1
    first_elem = first_lane & (row == 0)
    last_elem = last_lane & (row == hr - 1)

    def short_conv(z, w0, w1, w2, bias):
        prev = pltpu.roll(z, 1, 1)
        prev = jnp.where(first_lane, pltpu.roll(prev, 1, 0), prev)
        prev = jnp.where(first_elem, 0.0, prev)
        nxt = pltpu.roll(z, n2 - 1, 1)
        nxt = jnp.where(last_lane, pltpu.roll(nxt, hr - 1, 0), nxt)
        nxt = jnp.where(last_elem, 0.0, nxt)
        return bias + prev * w0 + z * w1 + nxt * w2

    def params(c):
        pc = pc_ref[pl.ds(c, 1), :]
        return lambda idx: pc[:, idx:idx + 1]

    def convs(c, carry):
        par = params(c)
        for bb in range(nb):
            u_ref[bb, c] = short_conv(zv_ref[bb, c].astype(F32), par(0), par(1), par(2), par(3))
            gate_ref[0, bb, c] = short_conv(z1_ref[bb, c].astype(F32), par(4), par(5), par(6), par(7))
            gate_ref[1, bb, c] = short_conv(z2_ref[bb, c].astype(F32), par(8), par(9), par(10), par(11))
        return carry

    lax.fori_loop(0, tc, convs, 0, unroll=2)

    for n in range(HYENA_ORDER):
        last = n == HYENA_ORDER - 1

        def fwd_left(c, carry):
            xs = jnp.concatenate([u_ref[0, c], u_ref[1, c]], axis=0).astype(BF16)
            a = jnp.dot(f1_ref[...], xs, preferred_element_type=F32)
            ar, ai = _cmul(a[:n1], a[n1:], twr_ref[...], twi_ref[...])
            _store_complex(a2_ref, c, n1, ar, ai)
            return carry

        lax.fori_loop(0, tc, fwd_left, 0, unroll=4)
        z_ref[...] = jnp.dot(a2_ref[...], f2_ref[...], preferred_element_type=F32)

        def spectrum(c, carry):
            zr, zi = _load_complex(z_ref, c, n1)
            wr, wi = _cmul(zr, zi, kf_ref[n, c, 0].astype(F32), kf_ref[n, c, 1].astype(F32))
            _store_complex(a2_ref, c, n1, wr, wi)
            return carry

        lax.fori_loop(0, tc, spectrum, 0, unroll=4)
        z_ref[...] = jnp.dot(a2_ref[...], f2c_ref[...], preferred_element_type=F32)

        def inv_left(c, carry):
            br, bi = _load_complex(z_ref, c, n1)
            br, bi = _cmul(br, bi, twr_ref[...], -twi_ref[...])
            bs = jnp.concatenate([br, bi], axis=0).astype(BF16)
            y = jnp.dot(g1_ref[...], bs, preferred_element_type=F32)
            fb = params(c)(12 + n)
            for bb in range(nb):
                new = gate_ref[n, bb, c] * (y[bb * hr:(bb + 1) * hr] + u_ref[bb, c] * fb)
                if last:
                    o_ref[bb, c] = new.astype(o_ref.dtype)
                else:
                    u_ref[bb, c] = new
            return carry

        lax.fori_loop(0, tc, inv_left, 0, unroll=4)


def _hyena_core(z4, kf, pc, f1h, twr, twi, f2e, f2c, g1e):
    nb, c3, hr, n2 = z4.shape
    c = c3 // 3
    n1 = 2 * hr
    tc = 16
    nct = c // tc
    full = lambda shape: pl.BlockSpec(shape, lambda i: (0,) * len(shape))
    zspec = lambda part: pl.BlockSpec((nb, tc, hr, n2), lambda i: (0, i + part * nct, 0, 0))
    return pl.pallas_call(
        _hyena_kernel,
        grid=(nct,),
        in_specs=[zspec(0), zspec(1), zspec(2),
                  pl.BlockSpec((HYENA_ORDER, tc, 2, n1, n2), lambda i: (0, i, 0, 0, 0)),
                  pl.BlockSpec((tc, 16), lambda i: (i, 0)),
                  full((2 * n1, n1)), full((n1, n2)), full((n1, n2)),
                  full((2 * n2, 2 * n2)), full((2 * n2, 2 * n2)), full((n1, 2 * n1))],
        out_specs=pl.BlockSpec((nb, tc, hr, n2), lambda i: (0, i, 0, 0)),
        out_shape=jax.ShapeDtypeStruct((nb, c, hr, n2), BF16),
        scratch_shapes=[pltpu.VMEM((nb, tc, hr, n2), F32),
                        pltpu.VMEM((HYENA_ORDER, nb, tc, hr, n2), F32),
                        pltpu.VMEM((tc * n1, 2 * n2), BF16),
                        pltpu.VMEM((tc * n1, 2 * n2), F32)],
        compiler_params=_params("parallel"),
        name="hyena_core",
    )(z4, z4, z4, kf, pc, f1h, twr, twi, f2e, f2c, g1e)


def _dft_tables(n1):
    n2 = DFT_MINOR
    n = n1 * n2
    a1 = 2.0 * np.pi * np.outer(np.arange(n1), np.arange(n1)) / n1
    f1r, f1i = np.cos(a1), -np.sin(a1)
    a2 = 2.0 * np.pi * np.outer(np.arange(n2), np.arange(n2)) / n2
    f2r, f2i = np.cos(a2), -np.sin(a2)
    at = 2.0 * np.pi * np.outer(np.arange(n1), np.arange(n2)) / n
    twr, twi = np.cos(at), -np.sin(at)
    hr = n1 // 2
    f1_full = np.concatenate([f1r, f1i], axis=0)
    f1_half = np.block([[f1r[:, :hr], -f1i[:, :hr]], [f1i[:, :hr], f1r[:, :hr]]])
    f2e = np.block([[f2r, f2i], [-f2i, f2r]])
    f2c = np.block([[f2r, -f2i], [f2i, f2r]])
    gr, gi = f1r[:hr, :], -f1i[:hr, :]
    g1e = np.block([[gr, -gi], [gi, gr]])
    bf = lambda m: jnp.asarray(m, F32).astype(BF16)
    return dict(f1_full=bf(f1_full), f1_half=bf(f1_half), f2e=bf(f2e), f2c=bf(f2c), g1e=bf(g1e),
                twr=jnp.asarray(twr, F32), twi=jnp.asarray(twi, F32))


def _hyena_spectra(seq, fw1, fb1, fw2, fb2, fw3, fb3, ffreq, fw4, tabs):
    l = seq
    width = fw4.shape[1] // (2 * HYENA_ORDER)
    idx = jnp.concatenate([jnp.arange(l), l - jnp.arange(l)]) % l
    t_lin = jnp.linspace(0.0, 1.0, l, dtype=F32)[idx]
    w = 2.0 * math.pi * idx.astype(F32)[:, None] / l
    f = jnp.linspace(1e-4, FILTER_BANDS - 1, FILTER_BANDS, dtype=F32)[None, :]
    feats = jnp.concatenate([t_lin[:, None], jnp.cos(f * w), -jnp.sin(f * w)], axis=-1)
    emb_pad = FILTER_HIDDEN
    feats = jnp.pad(feats, ((0, 0), (0, emb_pad - FILTER_EMB)))
    w1p = jnp.pad(fw1, ((0, emb_pad - FILTER_EMB), (0, 0)))
    h3 = _filter_mlp(feats, w1p, fb1, fw2, fb2, fw3, fb3, ffreq)
    w4t = fw4.reshape(FILTER_HIDDEN, HYENA_ORDER, 2, width).transpose(1, 2, 3, 0)
    deltas = jnp.abs(jnp.linspace(MIN_DECAY, MAX_DECAY, width, dtype=F32))[:, None]
    raw = _filter_raw(w4t, h3, t_lin[None, :], deltas, l)
    n1 = raw.shape[1]
    k2 = raw.transpose(0, 2, 1, 3).reshape(HYENA_ORDER * width, n1, DFT_MINOR)
    kf = _filter_spec(k2, tabs["f1_full"], tabs["twr"], tabs["twi"], tabs["f2e"])
    return kf.reshape(HYENA_ORDER, width, 2, n1, DFT_MINOR)


def _rms_kernel(x_ref, g_ref, o_ref):
    x = x_ref[...]
    ms = jnp.mean(x * x, axis=-1, keepdims=True)
    o_ref[...] = x * lax.rsqrt(ms + NORM_EPS) * g_ref[...]


def _final_norm(x, g):
    b, s, d = x.shape
    tm = _tile(s, 1024)
    return pl.pallas_call(
        _rms_kernel,
        grid=(b, s // tm),
        in_specs=[pl.BlockSpec((None, tm, d), lambda bb, i: (bb, i, 0)),
                  pl.BlockSpec((1, d), lambda bb, i: (0, 0))],
        out_specs=pl.BlockSpec((None, tm, d), lambda bb, i: (bb, i, 0)),
        out_shape=jax.ShapeDtypeStruct((b, s, d), F32),
        compiler_params=_params("parallel", "parallel"),
        name="final_norm",
    )(x, g.reshape(1, d))


def _rope_tables(n_tokens):
    tok = jnp.arange(n_tokens)
    row = (tok // GRID_W).astype(F32)
    col = (tok % GRID_W).astype(F32)
    half = ROPE_AXIS_DIM // 2
    inv = 1.0 / (ROPE_THETA ** (jnp.arange(0, ROPE_AXIS_DIM, 2, dtype=F32) / ROPE_AXIS_DIM))
    ang_r = row[:, None] * inv
    ang_c = col[:, None] * inv
    cos64 = jnp.concatenate([jnp.cos(ang_r), jnp.cos(ang_r), jnp.cos(ang_c), jnp.cos(ang_c)], axis=-1)
    sin64 = jnp.concatenate([-jnp.sin(ang_r), jnp.sin(ang_r), -jnp.sin(ang_c), jnp.sin(ang_c)], axis=-1)
    assert cos64.shape[1] == 4 * half == DIFF_QK_DIM
    return jnp.tile(cos64, (1, 2)), jnp.tile(sin64, (1, 2))


def _ffn(x, norm_g, sh, sc, gate, wg, wu, wd):
    hidden = _ffn_up(x, norm_g, sh, sc, wg, wu)
    return _proj_res([hidden], wd, x, gate)


def _even_layer(x, xc, mods, cmods, n1g, n2g, w_in, w_out, lam_p, subln, sgu_ng, sgu_nb, sgu_w, sgu_b,
                wg, wu, wd, layer_idx):
    b, s, d = x.shape
    lam_init = 0.8 - 0.6 * math.exp(-0.3 * layer_idx)
    sh1, sc1, g1, sh2, sc2, g2 = mods
    csh1, csc1, cg1, csh2, csc2, cg2 = cmods
    o_k, o_v, o_u = Q_COLS, 2 * Q_COLS, 2 * Q_COLS + A_WIDTH
    w_qkug = jnp.concatenate([w_in[:, :o_v], w_in[:, o_u:]], axis=1).astype(BF16)
    w_vt = w_in[:, o_v:o_u].T.astype(BF16)
    wo = w_out.astype(BF16)
    n1g2 = n1g.reshape(1, d)
    cos_t, sin_t = _rope_tables(s)
    sc_len = xc.shape[1]

    qkug = _inproj(x, n1g2, sh1, sc1, w_qkug, cos_t, sin_t, rope=True)
    cqkug = _inproj(xc, n1g2, csh1, csc1, w_qkug, cos_t[:sc_len], sin_t[:sc_len], rope=False)
    tk = _tile(s, 512)
    vt4 = _inproj_nt(x, n1g2, sh1, sc1, w_vt, tk)
    cvt4 = _inproj_nt(xc, n1g2, csh1, csc1, w_vt, sc_len)
    k4 = qkug.reshape(b, s // tk, tk, qkug.shape[2])
    subln2 = subln.reshape(1, DIFF_V_DIM)

    a_l = _attention(lam_p, subln2, qkug, cqkug, cvt4, k4, vt4, lam_init=lam_init)
    s_l = _sgu(qkug, sgu_ng, sgu_nb, sgu_w, sgu_b)
    x = _proj_res([a_l, s_l], wo, x, g1, tn_pref=d)
    x = _ffn(x, n2g.reshape(1, d), sh2, sc2, g2, wg, wu, wd)

    a_c = _attention(lam_p, subln2, cqkug, cqkug, cvt4, lam_init=lam_init)
    s_c = _sgu(cqkug, sgu_ng, sgu_nb, sgu_w, sgu_b)
    xc = _proj_res([a_c, s_c], wo, xc, cg1, tn_pref=d)
    xc = _ffn(xc, n2g.reshape(1, d), csh2, csc2, cg2, wg, wu, wd)
    return x, xc


def _odd_layer(x, mods, n1g, n2g, w_in, conv_w, conv_b, fw1, fb1, fw2, fb2, fw3, fb3, ffreq, fw4, fbias,
               w_out, wg, wu, wd):
    b, s, d = x.shape
    sh1, sc1, g1, sh2, sc2, g2 = mods
    width = w_out.shape[0]
    n1 = 2 * s // DFT_MINOR
    tabs = _dft_tables(n1)
    kf = _hyena_spectra(s, fw1, fb1, fw2, fb2, fw3, fb3, ffreq, fw4, tabs)

    z4 = _inproj_nt(x, n1g.reshape(1, d), sh1, sc1, w_in.T.astype(BF16), DFT_MINOR)
    z4 = z4.transpose(0, 2, 1, 3)
    cw = conv_w.reshape(SHORT_CONV, 3, width)
    cb = conv_b.reshape(3, width)
    cols = []
    for part in range(3):
        cols += [cw[0, part], cw[1, part], cw[2, part], cb[part]]
    cols += [fbias[0], fbias[1], jnp.zeros_like(fbias[0]), jnp.zeros_like(fbias[0])]
    pc = jnp.stack(cols, axis=1)
    y4 = _hyena_core(z4, kf, pc, tabs["f1_half"], tabs["twr"], tabs["twi"], tabs["f2e"], tabs["f2c"],
                     tabs["g1e"])
    yt = y4.reshape(b, width, s)
    x = _proj_res([yt], w_out.astype(BF16), x, g1, transposed=True, tn_pref=d)
    return _ffn(x, n2g.reshape(1, d), sh2, sc2, g2, wg, wu, wd)


def kernel(x, c, ctx, c_ctx, ada_w, ada_b, norm1, norm2, ffn_w_gate, ffn_w_up, ffn_w_down, e_w_in, e_w_out, e_lambda, e_subln, e_sgu_norm_g, e_sgu_norm_b, e_sgu_w, e_sgu_b, o_w_in, o_conv_w, o_conv_b, o_filt_w1, o_filt_b1, o_filt_w2, o_filt_b2, o_filt_w3, o_filt_b3, o_filt_freq, o_filt_w4, o_filt_bias, o_w_out, final_norm):
    b, s, d = x.shape
    depth = ada_w.shape[0]
    assert b == 2, "the long convolution packs exactly two batches into one complex signal"
    assert depth == 2, "odd layers here never carry the context stream"
    cond = jnp.zeros((8, d), F32).at[:b].set(c).at[b].set(c_ctx)
    mod_all = _adaln_all(cond, ada_w, ada_b)
    xc = ctx
    for i in range(depth):
        j = i // 2
        parts = jnp.split(mod_all[i], 6, axis=-1)
        mods = [p[:b, None, :] for p in parts]
        cmods = [jnp.broadcast_to(p[b:b + 1, None, :], (b, 1, d)) for p in parts]
        wg, wu, wd = (ffn_w_gate[i].astype(BF16), ffn_w_up[i].astype(BF16), ffn_w_down[i].astype(BF16))
        if i % 2 == 0:
            x, xc = _even_layer(x, xc, mods, cmods, norm1[i], norm2[i], e_w_in[j], e_w_out[j], e_lambda[j],
                                e_subln[j], e_sgu_norm_g[j], e_sgu_norm_b[j], e_sgu_w[j], e_sgu_b[j],
                                wg, wu, wd, i)
        else:
            x = _odd_layer(x, mods, norm1[i], norm2[i], o_w_in[j], o_conv_w[j], o_conv_b[j], o_filt_w1[j],
                           o_filt_b1[j], o_filt_w2[j], o_filt_b2[j], o_filt_w3[j], o_filt_b3[j],
                           o_filt_freq[j], o_filt_w4[j], o_filt_bias[j], o_w_out[j], wg, wu, wd)
    return _final_norm(x, final_norm)
```

```python
import functools
import math

import numpy as np
import jax
import jax.numpy as jnp
from jax import lax
from jax.experimental import pallas as pl
from jax.experimental.pallas import tpu as pltpu

F32 = jnp.float32
BF16 = jnp.bfloat16
HIGHEST = lax.Precision.HIGHEST

GRID_W = 64
NORM_EPS = 1e-6
DIFF_HEADS = 8
DIFF_QK_DIM = 64
DIFF_V_DIM = 2 * DIFF_QK_DIM
DIFF_SCALE = DIFF_QK_DIM ** -0.5
A_WIDTH = DIFF_HEADS * DIFF_V_DIM
Q_COLS = DIFF_HEADS * 2 * DIFF_QK_DIM
ROPE_THETA = 10000.0
ROPE_AXIS_DIM = DIFF_QK_DIM // 2
SUBLN_EPS = 1e-5
SGU_GROUPS = 8
SGU_CHUNK = 128
SGU_CH = 128
B_WIDTH = SGU_GROUPS * SGU_CH
LN_EPS = 1e-5
HYENA_ORDER = 2
SHORT_CONV = 3
FILTER_EMB = 33
FILTER_BANDS = (FILTER_EMB - 1) // 2
FILTER_HIDDEN = 64
DECAY_TARGET = 1e-2
MAX_DECAY = math.log(DECAY_TARGET) / 0.3
MIN_DECAY = math.log(DECAY_TARGET) / 1.5

LANES = 128
ONES_ROWS = 16
DFT_MINOR = 256
VMEM_LIMIT = 48 * 1024 * 1024

NT_DIMS = (((1,), (1,)), ((), ()))
TN_DIMS = (((0,), (0,)), ((), ()))


def _params(*sem):
    return pltpu.CompilerParams(dimension_semantics=sem, vmem_limit_bytes=VMEM_LIMIT)


def _tile(n, pref):
    return pref if n % pref == 0 else n


def _adaln_kernel(c_ref, w_ref, b_ref, o_ref):
    a = c_ref[...]
    a = a * jax.nn.sigmoid(a)
    o_ref[...] = jnp.dot(a, w_ref[...], preferred_element_type=F32, precision=HIGHEST) + b_ref[...]


def _adaln_all(cond, ada_w, ada_b):
    depth, d, n6 = ada_w.shape
    tn = _tile(n6, 1024)
    return pl.pallas_call(
        _adaln_kernel,
        grid=(depth, n6 // tn),
        in_specs=[pl.BlockSpec((8, d), lambda l, j: (0, 0)),
                  pl.BlockSpec((None, d, tn), lambda l, j: (l, 0, j)),
                  pl.BlockSpec((None, 1, tn), lambda l, j: (l, 0, j))],
        out_specs=pl.BlockSpec((None, 8, tn), lambda l, j: (l, 0, j)),
        out_shape=jax.ShapeDtypeStruct((depth, 8, n6), F32),
        compiler_params=_params("parallel", "parallel"),
        name="adaln",
    )(cond, ada_w, ada_b.reshape(depth, 1, n6))


def _norm_mod(x_ref, g_ref, sh_ref, sc_ref):
    x = x_ref[...]
    ms = jnp.mean(x * x, axis=-1, keepdims=True)
    y = x * lax.rsqrt(ms + NORM_EPS) * g_ref[...]
    return (y * (1.0 + sc_ref[...]) + sh_ref[...]).astype(BF16)


def _inproj_kernel(x_ref, g_ref, sh_ref, sc_ref, w_ref, cos_ref, sin_ref, o_ref, hs_ref, *,
                   n_q, n_qk, rope):
    j = pl.program_id(2)

    @pl.when(j == 0)
    def _():
        hs_ref[...] = _norm_mod(x_ref, g_ref, sh_ref, sc_ref)

    acc = jnp.dot(hs_ref[...], w_ref[...], preferred_element_type=F32)

    @pl.when(j < n_qk)
    def _():
        a = acc
        if rope:
            tn = a.shape[1]
            lane = lax.broadcasted_iota(jnp.int32, a.shape, 1)
            first = (lane & 31) < 16
            partner = jnp.where(first, pltpu.roll(a, tn - 16, 1), pltpu.roll(a, 16, 1))
            reps = tn // LANES
            a = a * jnp.tile(cos_ref[...], (1, reps)) + partner * jnp.tile(sin_ref[...], (1, reps))
        a = jnp.where(j < n_q, a * DIFF_SCALE, a)
        o_ref[...] = a.astype(o_ref.dtype)

    @pl.when(j >= n_qk)
    def _():
        o_ref[...] = jax.nn.gelu(acc).astype(o_ref.dtype)


def _inproj(x, g, sh, sc, w, cos_t, sin_t, rope):
    b, s, d = x.shape
    n = w.shape[1]
    tm = _tile(s, 1024)
    tn = 512
    kern = functools.partial(_inproj_kernel, n_q=Q_COLS // tn, n_qk=2 * Q_COLS // tn, rope=rope)
    return pl.pallas_call(
        kern,
        grid=(b, s // tm, n // tn),
        in_specs=[pl.BlockSpec((None, tm, d), lambda bb, i, j: (bb, i, 0)),
                  pl.BlockSpec((1, d), lambda bb, i, j: (0, 0)),
                  pl.BlockSpec((None, 1, d), lambda bb, i, j: (bb, 0, 0)),
                  pl.BlockSpec((None, 1, d), lambda bb, i, j: (bb, 0, 0)),
                  pl.BlockSpec((d, tn), lambda bb, i, j: (0, j)),
                  pl.BlockSpec((tm, LANES), lambda bb, i, j: (i, 0)),
                  pl.BlockSpec((tm, LANES), lambda bb, i, j: (i, 0))],
        out_specs=pl.BlockSpec((None, tm, tn), lambda bb, i, j: (bb, i, j)),
        out_shape=jax.ShapeDtypeStruct((b, s, n), BF16),
        scratch_shapes=[pltpu.VMEM((tm, d), BF16)],
        compiler_params=_params("parallel", "parallel", "arbitrary"),
        name="inproj",
    )(x, g, sh, sc, w, cos_t, sin_t)


def _inproj_nt_kernel(x_ref, g_ref, sh_ref, sc_ref, wt_ref, o_ref, hs_ref, *, tl):
    j = pl.program_id(2)

    @pl.when(j == 0)
    def _():
        hs_ref[...] = _norm_mod(x_ref, g_ref, sh_ref, sc_ref)

    acc = lax.dot_general(wt_ref[...], hs_ref[...], NT_DIMS, preferred_element_type=F32)
    for c in range(o_ref.shape[0]):
        o_ref[c] = acc[:, c * tl:(c + 1) * tl].astype(o_ref.dtype)


def _inproj_nt(x, g, sh, sc, wt, tl):
    b, s, d = x.shape
    n = wt.shape[0]
    tm = _tile(s, 1024)
    tn = 512
    kern = functools.partial(_inproj_nt_kernel, tl=tl)
    return pl.pallas_call(
        kern,
        grid=(b, s // tm, n // tn),
        in_specs=[pl.BlockSpec((None, tm, d), lambda bb, i, j: (bb, i, 0)),
                  pl.BlockSpec((1, d), lambda bb, i, j: (0, 0)),
                  pl.BlockSpec((None, 1, d), lambda bb, i, j: (bb, 0, 0)),
                  pl.BlockSpec((None, 1, d), lambda bb, i, j: (bb, 0, 0)),
                  pl.BlockSpec((tn, d), lambda bb, i, j: (j, 0))],
        out_specs=pl.BlockSpec((None, tm // tl, tn, tl), lambda bb, i, j: (bb, i, j, 0)),
        out_shape=jax.ShapeDtypeStruct((b, s // tl, n, tl), BF16),
        scratch_shapes=[pltpu.VMEM((tm, d), BF16)],
        compiler_params=_params("parallel", "parallel", "arbitrary"),
        name="inproj_nt",
    )(x, g, sh, sc, wt)


def _ffn_up_kernel(x_ref, g_ref, sh_ref, sc_ref, wg_ref, wu_ref, o_ref, hs_ref):
    j = pl.program_id(2)

    @pl.when(j == 0)
    def _():
        hs_ref[...] = _norm_mod(x_ref, g_ref, sh_ref, sc_ref)

    hs = hs_ref[...]
    gate = jnp.dot(hs, wg_ref[...], preferred_element_type=F32)
    up = jnp.dot(hs, wu_ref[...], preferred_element_type=F32)
    o_ref[...] = (gate * jax.nn.sigmoid(gate) * up).astype(o_ref.dtype)


def _ffn_up(x, g, sh, sc, wg, wu):
    b, s, d = x.shape
    n = wg.shape[1]
    tm = _tile(s, 1024)
    tn = 512
    return pl.pallas_call(
        _ffn_up_kernel,
        grid=(b, s // tm, n // tn),
        in_specs=[pl.BlockSpec((None, tm, d), lambda bb, i, j: (bb, i, 0)),
                  pl.BlockSpec((1, d), lambda bb, i, j: (0, 0)),
                  pl.BlockSpec((None, 1, d), lambda bb, i, j: (bb, 0, 0)),
                  pl.BlockSpec((None, 1, d), lambda bb, i, j: (bb, 0, 0)),
                  pl.BlockSpec((d, tn), lambda bb, i, j: (0, j)),
                  pl.BlockSpec((d, tn), lambda bb, i, j: (0, j))],
        out_specs=pl.BlockSpec((None, tm, tn), lambda bb, i, j: (bb, i, j)),
        out_shape=jax.ShapeDtypeStruct((b, s, n), BF16),
        scratch_shapes=[pltpu.VMEM((tm, d), BF16)],
        compiler_params=_params("parallel", "parallel", "arbitrary"),
        name="ffn_up",
    )(x, g, sh, sc, wg, wu)


def _proj_res_kernel(*refs, ksizes, transposed):
    n = len(ksizes)
    a_refs = refs[:n]
    w_ref, x_ref, gate_ref, o_ref = refs[n:]
    acc = None
    off = 0
    for a_ref, ks in zip(a_refs, ksizes):
        w = w_ref[off:off + ks, :]
        if transposed:
            part = lax.dot_general(a_ref[...], w, TN_DIMS, preferred_element_type=F32)
        else:
            part = jnp.dot(a_ref[...], w, preferred_element_type=F32)
        acc = part if acc is None else acc + part
        off += ks
    o_ref[...] = x_ref[...] + gate_ref[...] * acc


def _proj_res(a_list, w, x, gate, transposed=False, tm_pref=512, tn_pref=512):
    b, s, d = x.shape
    ksizes = tuple(a.shape[1] if transposed else a.shape[2] for a in a_list)
    ktot = sum(ksizes)
    tm = _tile(s, tm_pref)
    tn = _tile(d, tn_pref)
    if transposed:
        a_specs = [pl.BlockSpec((None, ks, tm), lambda bb, i, j: (bb, 0, i)) for ks in ksizes]
    else:
        a_specs = [pl.BlockSpec((None, tm, ks), lambda bb, i, j: (bb, i, 0)) for ks in ksizes]
    kern = functools.partial(_proj_res_kernel, ksizes=ksizes, transposed=transposed)
    return pl.pallas_call(
        kern,
        grid=(b, s // tm, d // tn),
        in_specs=a_specs + [pl.BlockSpec((ktot, tn), lambda bb, i, j: (0, j)),
                            pl.BlockSpec((None, tm, tn), lambda bb, i, j: (bb, i, j)),
                            pl.BlockSpec((None, 1, tn), lambda bb, i, j: (bb, 0, j))],
        out_specs=pl.BlockSpec((None, tm, tn), lambda bb, i, j: (bb, i, j)),
        out_shape=jax.ShapeDtypeStruct((b, s, d), F32),
        compiler_params=_params("parallel", "parallel", "parallel"),
        name="proj_res",
    )(*a_list, w, x, gate)


def _attn_kernel(*refs, n_chunks, lam_init):
    if n_chunks:
        lam_ref, q_ref, kc_ref, vct_ref, k_ref, vt_ref, g_ref, o_ref, acc_ref = refs[:9]
        s_refs = refs[9:]
    else:
        lam_ref, q_ref, kc_ref, vct_ref, g_ref, o_ref, acc_ref = refs
    q = q_ref[...]
    tq = q.shape[0]
    dv = DIFF_V_DIM
    qm = (q[:, :DIFF_QK_DIM], q[:, DIFF_QK_DIM:])

    def scores(kblk):
        return tuple(lax.dot_general(kblk[:, m * DIFF_QK_DIM:(m + 1) * DIFF_QK_DIM], qm[m], NT_DIMS,
                                     preferred_element_type=F32) for m in range(2))

    def absorb(s_pair, vtblk, m_pair):
        vext = jnp.concatenate([vtblk, jnp.ones((ONES_ROWS, vtblk.shape[1]), BF16)], axis=0)
        out = []
        for m in range(2):
            m_old = m_pair[m]
            m_new = jnp.maximum(m_old, jnp.max(s_pair[m], axis=0, keepdims=True))
            alpha = jnp.exp(m_old - m_new)
            p = jnp.exp((s_pair[m] - m_new).astype(BF16))
            pv = jnp.dot(vext, p, preferred_element_type=F32)
            acc_ref[m] = alpha * acc_ref[m] + pv
            out.append(m_new)
        return tuple(out)

    def store(ref, s_pair):
        ref[0] = s_pair[0]
        ref[1] = s_pair[1]

    acc_ref[...] = jnp.zeros_like(acc_ref)
    init = jnp.full((1, tq), -1e30, F32)
    m_pair = absorb(scores(kc_ref[...]), vct_ref[0], (init, init))
    if n_chunks:
        group = len(s_refs) // 2
        sets = (s_refs[:group], s_refs[group:])
        assert n_chunks % (2 * group) == 0

        def half_trip(cur, nxt, base, mp, lookahead):
            for k in range(group):
                if lookahead:
                    store(nxt[k], scores(k_ref[base + group + k]))
                mp = absorb((cur[k][0], cur[k][1]), vt_ref[base + k], mp)
            return mp

        def trip(j, mp, lookahead):
            base = 2 * group * j
            mp = half_trip(sets[0], sets[1], base, mp, True)
            return half_trip(sets[1], sets[0], base + group, mp, lookahead)

        for k in range(group):
            store(sets[0][k], scores(k_ref[k]))
        n_trips = n_chunks // (2 * group)
        m_pair = lax.fori_loop(0, n_trips - 1, lambda j, mp: trip(j, mp, True), m_pair)
        m_pair = trip(n_trips - 1, m_pair, False)

    lp = lam_ref[...]
    lam = (jnp.exp(jnp.sum(lp[0:1] * lp[1:2], axis=-1, keepdims=True))
           - jnp.exp(jnp.sum(lp[2:3] * lp[3:4], axis=-1, keepdims=True)) + lam_init)
    acc0, acc1 = acc_ref[0], acc_ref[1]
    o = acc0[:dv] / acc0[dv:dv + 1] - lam * (acc1[:dv] / acc1[dv:dv + 1])
    ot = o.T
    ms = jnp.mean(ot * ot, axis=-1, keepdims=True)
    on = ot * lax.rsqrt(ms + SUBLN_EPS) * g_ref[...] * (1.0 - lam_init)
    o_ref[...] = on.astype(o_ref.dtype)


def _attention(lam_p, subln, q_arr, kc_arr, vct_arr, k4=None, vt4=None, *, lam_init):
    b, sq = q_arr.shape[0], q_arr.shape[1]
    sc = kc_arr.shape[1]
    h = DIFF_HEADS
    tq = _tile(sq, 256)
    dv = DIFF_V_DIM
    n_chunks = 0 if k4 is None else k4.shape[1]
    in_specs = [pl.BlockSpec((4, DIFF_QK_DIM), lambda bb, hh, i: (0, 0)),
                pl.BlockSpec((None, tq, dv), lambda bb, hh, i: (bb, i, hh)),
                pl.BlockSpec((None, sc, dv), lambda bb, hh, i: (bb, 0, h + hh)),
                pl.BlockSpec((None, 1, dv, sc), lambda bb, hh, i: (bb, 0, hh, 0))]
    args = [lam_p, q_arr, kc_arr, vct_arr]
    if n_chunks:
        tk = k4.shape[2]
        in_specs += [pl.BlockSpec((None, n_chunks, tk, dv), lambda bb, hh, i: (bb, 0, 0, h + hh)),
                     pl.BlockSpec((None, n_chunks, dv, tk), lambda bb, hh, i: (bb, 0, hh, 0))]
        args += [k4, vt4]
    in_specs.append(pl.BlockSpec((1, dv), lambda bb, hh, i: (0, 0)))
    args.append(subln)
    kern = functools.partial(_attn_kernel, n_chunks=n_chunks, lam_init=lam_init)
    scratch = [pltpu.VMEM((2, dv + ONES_ROWS, tq), F32)]
    if n_chunks:
        group = 4 if n_chunks % 16 == 0 else 2
        scratch += [pltpu.VMEM((2, tk, tq), F32) for _ in range(2 * group)]
    return pl.pallas_call(
        kern,
        grid=(b, h, sq // tq),
        in_specs=in_specs,
        out_specs=pl.BlockSpec((None, tq, dv), lambda bb, hh, i: (bb, i, hh)),
        out_shape=jax.ShapeDtypeStruct((b, sq, A_WIDTH), BF16),
        scratch_shapes=scratch,
        compiler_params=_params("parallel", "parallel", "parallel"),
        name="diff_attn",
    )(*args)


def _sgu_kernel(u_ref, g_ref, ng_ref, nb_ref, w_ref, bs_ref, o_ref):
    w = w_ref[...]
    for c in range(u_ref.shape[0] // SGU_CHUNK):
        sl = slice(c * SGU_CHUNK, (c + 1) * SGU_CHUNK)
        gg = g_ref[sl, :].astype(F32)
        mu = jnp.mean(gg, axis=-1, keepdims=True)
        dev = gg - mu
        var = jnp.mean(dev * dev, axis=-1, keepdims=True)
        vv = dev * lax.rsqrt(var + LN_EPS) * ng_ref[...] + nb_ref[...]
        mixed = jnp.dot(w, vv.astype(BF16), preferred_element_type=F32) + bs_ref[...]
        o_ref[sl, :] = (u_ref[sl, :].astype(F32) * mixed).astype(o_ref.dtype)


def _sgu(qkug, norm_g, norm_b, w_s, b_s):
    b, s = qkug.shape[0], qkug.shape[1]
    tm = _tile(s, 1024)
    gcount = SGU_GROUPS
    ublk = 2 * Q_COLS // SGU_CH
    return pl.pallas_call(
        _sgu_kernel,
        grid=(b, s // tm, gcount),
        in_specs=[pl.BlockSpec((None, tm, SGU_CH), lambda bb, i, gi: (bb, i, ublk + gi)),
                  pl.BlockSpec((None, tm, SGU_CH), lambda bb, i, gi: (bb, i, ublk + gcount + gi)),
                  pl.BlockSpec((None, 1, SGU_CH), lambda bb, i, gi: (gi, 0, 0)),
                  pl.BlockSpec((None, 1, SGU_CH), lambda bb, i, gi: (gi, 0, 0)),
                  pl.BlockSpec((None, SGU_CHUNK, SGU_CHUNK), lambda bb, i, gi: (gi, 0, 0)),
                  pl.BlockSpec((None, SGU_CHUNK, 1), lambda bb, i, gi: (gi, 0, 0))],
        out_specs=pl.BlockSpec((None, tm, SGU_CH), lambda bb, i, gi: (bb, i, gi)),
        out_shape=jax.ShapeDtypeStruct((b, s, B_WIDTH), BF16),
        compiler_params=_params("parallel", "parallel", "parallel"),
        name="sgu",
    )(qkug, qkug, norm_g.reshape(gcount, 1, SGU_CH), norm_b.reshape(gcount, 1, SGU_CH),
      w_s.astype(BF16), b_s.reshape(gcount, SGU_CHUNK, 1))


def _filter_mlp_kernel(f_ref, w1_ref, b1_ref, w2_ref, b2_ref, w3_ref, b3_ref, fr_ref, o_ref):
    def lin(a, w_ref, b_ref):
        return jnp.dot(a, w_ref[...], preferred_element_type=F32, precision=HIGHEST) + b_ref[...]
    fr = fr_ref[...]
    hcur = jnp.sin(fr[0:1] * lin(f_ref[...], w1_ref, b1_ref))
    hcur = jnp.sin(fr[1:2] * lin(hcur, w2_ref, b2_ref))
    o_ref[...] = jnp.sin(fr[2:3] * lin(hcur, w3_ref, b3_ref))


def _filter_mlp(feats, w1, b1, w2, b2, w3, b3, freq):
    rows, emb = feats.shape
    hid = FILTER_HIDDEN
    tr = _tile(rows, 2048)
    full = lambda shape: pl.BlockSpec(shape, lambda i: (0,) * len(shape))
    return pl.pallas_call(
        _filter_mlp_kernel,
        grid=(rows // tr,),
        in_specs=[pl.BlockSpec((tr, emb), lambda i: (i, 0)),
                  full((emb, hid)), full((1, hid)), full((hid, hid)), full((1, hid)),
                  full((hid, hid)), full((1, hid)), full((3, hid))],
        out_specs=pl.BlockSpec((tr, hid), lambda i: (i, 0)),
        out_shape=jax.ShapeDtypeStruct((rows, hid), F32),
        compiler_params=_params("parallel"),
        name="filter_mlp",
    )(feats, w1, b1.reshape(1, hid), w2, b2.reshape(1, hid), w3, b3.reshape(1, hid), freq)


def _filter_raw_kernel(w4t_ref, h_ref, t_ref, delta_ref, o_ref, *, zero_tile):
    rt = pl.program_id(2)
    raw = lax.dot_general(w4t_ref[...], h_ref[...], NT_DIMS, preferred_element_type=F32,
                          precision=HIGHEST)
    raw = raw * jnp.exp(-(delta_ref[...] * t_ref[...]))
    for c in range(o_ref.shape[0]):
        o_ref[c] = raw[:, c * DFT_MINOR:(c + 1) * DFT_MINOR]

    @pl.when(rt == zero_tile)
    def _():
        col = lax.broadcasted_iota(jnp.int32, (raw.shape[0], DFT_MINOR), 1)
        o_ref[0] = jnp.where(col == 0, 0.0, raw[:, :DFT_MINOR])


def _filter_raw(w4t, h3, t_row, deltas, seq):
    c = w4t.shape[2]
    rows = h3.shape[0]
    tr = _tile(rows // 2, 2048)
    tc = _tile(c, 512)
    half_tiles = seq // tr
    kern = functools.partial(_filter_raw_kernel, zero_tile=half_tiles)
    return pl.pallas_call(
        kern,
        grid=(HYENA_ORDER, c // tc, rows // tr),
        in_specs=[pl.BlockSpec((None, None, tc, FILTER_HIDDEN),
                               lambda n, ci, rt: (n, rt // half_tiles, ci, 0)),
                  pl.BlockSpec((tr, FILTER_HIDDEN), lambda n, ci, rt: (rt, 0)),
                  pl.BlockSpec((1, tr), lambda n, ci, rt: (0, rt)),
                  pl.BlockSpec((tc, 1), lambda n, ci, rt: (ci, 0))],
        out_specs=pl.BlockSpec((None, tr // DFT_MINOR, tc, DFT_MINOR), lambda n, ci, rt: (n, rt, ci, 0)),
        out_shape=jax.ShapeDtypeStruct((HYENA_ORDER, rows // DFT_MINOR, c, DFT_MINOR), F32),
        compiler_params=_params("parallel", "parallel", "parallel"),
        name="filter_raw",
    )(w4t, h3, t_row, deltas)


def _cmul(ar, ai, br, bi):
    return ar * br - ai * bi, ar * bi + ai * br


def _store_complex(ref, c, n1, re, im):
    r0 = pl.multiple_of(c * n1, n1)
    ref[pl.ds(r0, n1), :DFT_MINOR] = re.astype(ref.dtype)
    ref[pl.ds(r0, n1), DFT_MINOR:] = im.astype(ref.dtype)


def _load_complex(ref, c, n1):
    r0 = pl.multiple_of(c * n1, n1)
    tile = ref[pl.ds(r0, n1), :]
    return tile[:, :DFT_MINOR], tile[:, DFT_MINOR:]


def _filter_spec_kernel(k_ref, f1_ref, twr_ref, twi_ref, f2_ref, o_ref, a2_ref, *, inv_n):
    tc, n1, n2 = k_ref.shape

    def left(c, carry):
        k = k_ref[c]
        nrm = jnp.sum(jnp.sum(jnp.abs(k), axis=1, keepdims=True), axis=0, keepdims=True)
        kn = (k * (inv_n / nrm)).astype(BF16)
        a = jnp.dot(f1_ref[...], kn, preferred_element_type=F32)
        ar, ai = _cmul(a[:n1], a[n1:], twr_ref[...], twi_ref[...])
        _store_complex(a2_ref, c, n1, ar, ai)
        return carry

    lax.fori_loop(0, tc, left, 0, unroll=4)
    z = jnp.dot(a2_ref[...], f2_ref[...], preferred_element_type=F32).reshape(tc, n1, 2 * n2)
    o_ref[:, 0] = z[:, :, :n2].astype(o_ref.dtype)
    o_ref[:, 1] = z[:, :, n2:].astype(o_ref.dtype)


def _filter_spec(k2, f1_full, twr, twi, f2e):
    nc, n1, n2 = k2.shape
    tc = 16
    kern = functools.partial(_filter_spec_kernel, inv_n=1.0 / (n1 * n2))
    full = lambda shape: pl.BlockSpec(shape, lambda i: (0,) * len(shape))
    return pl.pallas_call(
        kern,
        grid=(nc // tc,),
        in_specs=[pl.BlockSpec((tc, n1, n2), lambda i: (i, 0, 0)),
                  full((2 * n1, n1)), full((n1, n2)), full((n1, n2)), full((2 * n2, 2 * n2))],
        out_specs=pl.BlockSpec((tc, 2, n1, n2), lambda i: (i, 0, 0, 0)),
        out_shape=jax.ShapeDtypeStruct((nc, 2, n1, n2), BF16),
        scratch_shapes=[pltpu.VMEM((tc * n1, 2 * n2), BF16)],
        compiler_params=_params("parallel"),
        name="filter_spec",
    )(k2, f1_full, twr, twi, f2e)


def _hyena_kernel(zv_ref, z1_ref, z2_ref, kf_ref, pc_ref, f1_ref, twr_ref, twi_ref, f2_ref, f2c_ref,
                  g1_ref, o_ref, u_ref, gate_ref, a2_ref, z_ref):
    nb, tc, hr, n2 = zv_ref.shape
    n1 = 2 * hr
    row = lax.broadcasted_iota(jnp.int32, (hr, n2), 0)
    lane = lax.broadcasted_iota(jnp.int32, (hr, n2), 1)
    first_lane, last_lane = lane == 0, lane == n2 - 1
    first_elem = first_lane & (row == 0)
    last_elem = last_lane & (row == hr - 1)

    def short_conv(z, w0, w1, w2, bias):
        prev = pltpu.roll(z, 1, 1)
        prev = jnp.where(first_lane, pltpu.roll(prev, 1, 0), prev)
        prev = jnp.where(first_elem, 0.0, prev)
        nxt = pltpu.roll(z, n2 - 1, 1)
        nxt = jnp.where(last_lane, pltpu.roll(nxt, hr - 1, 0), nxt)
        nxt = jnp.where(last_elem, 0.0, nxt)
        return bias + prev * w0 + z * w1 + nxt * w2

    def params(c):
        pc = pc_ref[pl.ds(c, 1), :]
        return lambda idx: pc[:, idx:idx + 1]

    def convs(c, carry):
        par = params(c)
        for bb in range(nb):
            u_ref[bb, c] = short_conv(zv_ref[bb, c].astype(F32), par(0), par(1), par(2), par(3))
            gate_ref[0, bb, c] = short_conv(z1_ref[bb, c].astype(F32), par(4), par(5), par(6), par(7))
            gate_ref[1, bb, c] = short_conv(z2_ref[bb, c].astype(F32), par(8), par(9), par(10), par(11))
        return carry

    lax.fori_loop(0, tc, convs, 0, unroll=2)

    for n in range(HYENA_ORDER):
        last = n == HYENA_ORDER - 1

        def fwd_left(c, carry):
            xs = jnp.concatenate([u_ref[0, c], u_ref[1, c]], axis=0).astype(BF16)
            a = jnp.dot(f1_ref[...], xs, preferred_element_type=F32)
            ar, ai = _cmul(a[:n1], a[n1:], twr_ref[...], twi_ref[...])
            _store_complex(a2_ref, c, n1, ar, ai)
            return carry

        lax.fori_loop(0, tc, fwd_left, 0, unroll=4)
        z_ref[...] = jnp.dot(a2_ref[...], f2_ref[...], preferred_element_type=F32)

        def spectrum(c, carry):
            zr, zi = _load_complex(z_ref, c, n1)
            wr, wi = _cmul(zr, zi, kf_ref[n, c, 0].astype(F32), kf_ref[n, c, 1].astype(F32))
            _store_complex(a2_ref, c, n1, wr, wi)
            return carry

        lax.fori_loop(0, tc, spectrum, 0, unroll=4)
        z_ref[...] = jnp.dot(a2_ref[...], f2c_ref[...], preferred_element_type=F32)

        def inv_left(c, carry):
            br, bi = _load_complex(z_ref, c, n1)
            br, bi = _cmul(br, bi, twr_ref[...], -twi_ref[...])
            bs = jnp.concatenate([br, bi], axis=0).astype(BF16)
            y = jnp.dot(g1_ref[...], bs, preferred_element_type=F32)
            fb = params(c)(12 + n)
            for bb in range(nb):
                new = gate_ref[n, bb, c] * (y[bb * hr:(bb + 1) * hr] + u_ref[bb, c] * fb)
                if last:
                    o_ref[bb, c] = new.astype(o_ref.dtype)
                else:
                    u_ref[bb, c] = new
            return carry

        lax.fori_loop(0, tc, inv_left, 0, unroll=4)


def _hyena_core(z4, kf, pc, f1h, twr, twi, f2e, f2c, g1e):
    nb, c3, hr, n2 = z4.shape
    c = c3 // 3
    n1 = 2 * hr
    tc = 16
    nct = c // tc
    full = lambda shape: pl.BlockSpec(shape, lambda i: (0,) * len(shape))
    zspec = lambda part: pl.BlockSpec((nb, tc, hr, n2), lambda i: (0, i + part * nct, 0, 0))
    return pl.pallas_call(
        _hyena_kernel,
        grid=(nct,),
        in_specs=[zspec(0), zspec(1), zspec(2),
                  pl.BlockSpec((HYENA_ORDER, tc, 2, n1, n2), lambda i: (0, i, 0, 0, 0)),
                  pl.BlockSpec((tc, 16), lambda i: (i, 0)),
                  full((2 * n1, n1)), full((n1, n2)), full((n1, n2)),
                  full((2 * n2, 2 * n2)), full((2 * n2, 2 * n2)), full((n1, 2 * n1))],
        out_specs=pl.BlockSpec((nb, tc, hr, n2), lambda i: (0, i, 0, 0)),
        out_shape=jax.ShapeDtypeStruct((nb, c, hr, n2), BF16),
        scratch_shapes=[pltpu.VMEM((nb, tc, hr, n2), F32),
                        pltpu.VMEM((HYENA_ORDER, nb, tc, hr, n2), F32),
                        pltpu.VMEM((tc * n1, 2 * n2), BF16),
                        pltpu.VMEM((tc * n1, 2 * n2), F32)],
        compiler_params=_params("parallel"),
        name="hyena_core",
    )(z4, z4, z4, kf, pc, f1h, twr, twi, f2e, f2c, g1e)


def _dft_tables(n1):
    n2 = DFT_MINOR
    n = n1 * n2
    a1 = 2.0 * np.pi * np.outer(np.arange(n1), np.arange(n1)) / n1
    f1r, f1i = np.cos(a1), -np.sin(a1)
    a2 = 2.0 * np.pi * np.outer(np.arange(n2), np.arange(n2)) / n2
    f2r, f2i = np.cos(a2), -np.sin(a2)
    at = 2.0 * np.pi * np.outer(np.arange(n1), np.arange(n2)) / n
    twr, twi = np.cos(at), -np.sin(at)
    hr = n1 // 2
    f1_full = np.concatenate([f1r, f1i], axis=0)
    f1_half = np.block([[f1r[:, :hr], -f1i[:, :hr]], [f1i[:, :hr], f1r[:, :hr]]])
    f2e = np.block([[f2r, f2i], [-f2i, f2r]])
    f2c = np.block([[f2r, -f2i], [f2i, f2r]])
    gr, gi = f1r[:hr, :], -f1i[:hr, :]
    g1e = np.block([[gr, -gi], [gi, gr]])
    bf = lambda m: jnp.asarray(m, F32).astype(BF16)
    return dict(f1_full=bf(f1_full), f1_half=bf(f1_half), f2e=bf(f2e), f2c=bf(f2c), g1e=bf(g1e),
                twr=jnp.asarray(twr, F32), twi=jnp.asarray(twi, F32))


def _hyena_spectra(seq, fw1, fb1, fw2, fb2, fw3, fb3, ffreq, fw4, tabs):
    l = seq
    width = fw4.shape[1] // (2 * HYENA_ORDER)
    idx = jnp.concatenate([jnp.arange(l), l - jnp.arange(l)]) % l
    t_lin = jnp.linspace(0.0, 1.0, l, dtype=F32)[idx]
    w = 2.0 * math.pi * idx.astype(F32)[:, None] / l
    f = jnp.linspace(1e-4, FILTER_BANDS - 1, FILTER_BANDS, dtype=F32)[None, :]
    feats = jnp.concatenate([t_lin[:, None], jnp.cos(f * w), -jnp.sin(f * w)], axis=-1)
    emb_pad = FILTER_HIDDEN
    feats = jnp.pad(feats, ((0, 0), (0, emb_pad - FILTER_EMB)))
    w1p = jnp.pad(fw1, ((0, emb_pad - FILTER_EMB), (0, 0)))
    h3 = _filter_mlp(feats, w1p, fb1, fw2, fb2, fw3, fb3, ffreq)
    w4t = fw4.reshape(FILTER_HIDDEN, HYENA_ORDER, 2, width).transpose(1, 2, 3, 0)
    deltas = jnp.abs(jnp.linspace(MIN_DECAY, MAX_DECAY, width, dtype=F32))[:, None]
    raw = _filter_raw(w4t, h3, t_lin[None, :], deltas, l)
    n1 = raw.shape[1]
    k2 = raw.transpose(0, 2, 1, 3).reshape(HYENA_ORDER * width, n1, DFT_MINOR)
    kf = _filter_spec(k2, tabs["f1_full"], tabs["twr"], tabs["twi"], tabs["f2e"])
    return kf.reshape(HYENA_ORDER, width, 2, n1, DFT_MINOR)


def _rms_kernel(x_ref, g_ref, o_ref):
    x = x_ref[...]
    ms = jnp.mean(x * x, axis=-1, keepdims=True)
    o_ref[...] = x * lax.rsqrt(ms + NORM_EPS) * g_ref[...]


def _final_norm(x, g):
    b, s, d = x.shape
    tm = _tile(s, 1024)
    return pl.pallas_call(
        _rms_kernel,
        grid=(b, s // tm),
        in_specs=[pl.BlockSpec((None, tm, d), lambda bb, i: (bb, i, 0)),
                  pl.BlockSpec((1, d), lambda bb, i: (0, 0))],
        out_specs=pl.BlockSpec((None, tm, d), lambda bb, i: (bb, i, 0)),
        out_shape=jax.ShapeDtypeStruct((b, s, d), F32),
        compiler_params=_params("parallel", "parallel"),
        name="final_norm",
    )(x, g.reshape(1, d))


def _rope_tables(n_tokens):
    tok = jnp.arange(n_tokens)
    row = (tok // GRID_W).astype(F32)
    col = (tok % GRID_W).astype(F32)
    half = ROPE_AXIS_DIM // 2
    inv = 1.0 / (ROPE_THETA ** (jnp.arange(0, ROPE_AXIS_DIM, 2, dtype=F32) / ROPE_AXIS_DIM))
    ang_r = row[:, None] * inv
    ang_c = col[:, None] * inv
    cos64 = jnp.concatenate([jnp.cos(ang_r), jnp.cos(ang_r), jnp.cos(ang_c), jnp.cos(ang_c)], axis=-1)
    sin64 = jnp.concatenate([-jnp.sin(ang_r), jnp.sin(ang_r), -jnp.sin(ang_c), jnp.sin(ang_c)], axis=-1)
    assert cos64.shape[1] == 4 * half == DIFF_QK_DIM
    return jnp.tile(cos64, (1, 2)), jnp.tile(sin64, (1, 2))


def _ffn(x, norm_g, sh, sc, gate, wg, wu, wd):
    hidden = _ffn_up(x, norm_g, sh, sc, wg, wu)
    return _proj_res([hidden], wd, x, gate)


def _even_layer(x, xc, mods, cmods, n1g, n2g, w_in, w_out, lam_p, subln, sgu_ng, sgu_nb, sgu_w, sgu_b,
                wg, wu, wd, layer_idx):
    b, s, d = x.shape
    lam_init = 0.8 - 0.6 * math.exp(-0.3 * layer_idx)
    sh1, sc1, g1, sh2, sc2, g2 = mods
    csh1, csc1, cg1, csh2, csc2, cg2 = cmods
    o_k, o_v, o_u = Q_COLS, 2 * Q_COLS, 2 * Q_COLS + A_WIDTH
    w_qkug = jnp.concatenate([w_in[:, :o_v], w_in[:, o_u:]], axis=1).astype(BF16)
    w_vt = w_in[:, o_v:o_u].T.astype(BF16)
    wo = w_out.astype(BF16)
    n1g2 = n1g.reshape(1, d)
    cos_t, sin_t = _rope_tables(s)
    sc_len = xc.shape[1]

    qkug = _inproj(x, n1g2, sh1, sc1, w_qkug, cos_t, sin_t, rope=True)
    cqkug = _inproj(xc, n1g2, csh1, csc1, w_qkug, cos_t[:sc_len], sin_t[:sc_len], rope=False)
    tk = _tile(s, 512)
    vt4 = _inproj_nt(x, n1g2, sh1, sc1, w_vt, tk)
    cvt4 = _inproj_nt(xc, n1g2, csh1, csc1, w_vt, sc_len)
    k4 = qkug.reshape(b, s // tk, tk, qkug.shape[2])
    subln2 = subln.reshape(1, DIFF_V_DIM)

    a_l = _attention(lam_p, subln2, qkug, cqkug, cvt4, k4, vt4, lam_init=lam_init)
    s_l = _sgu(qkug, sgu_ng, sgu_nb, sgu_w, sgu_b)
    x = _proj_res([a_l, s_l], wo, x, g1, tn_pref=d)
    x = _ffn(x, n2g.reshape(1, d), sh2, sc2, g2, wg, wu, wd)

    a_c = _attention(lam_p, subln2, cqkug, cqkug, cvt4, lam_init=lam_init)
    s_c = _sgu(cqkug, sgu_ng, sgu_nb, sgu_w, sgu_b)
    xc = _proj_res([a_c, s_c], wo, xc, cg1, tn_pref=d)
    xc = _ffn(xc, n2g.reshape(1, d), csh2, csc2, cg2, wg, wu, wd)
    return x, xc


def _odd_layer(x, mods, n1g, n2g, w_in, conv_w, conv_b, fw1, fb1, fw2, fb2, fw3, fb3, ffreq, fw4, fbias,
               w_out, wg, wu, wd):
    b, s, d = x.shape
    sh1, sc1, g1, sh2, sc2, g2 = mods
    width = w_out.shape[0]
    n1 = 2 * s // DFT_MINOR
    tabs = _dft_tables(n1)
    kf = _hyena_spectra(s, fw1, fb1, fw2, fb2, fw3, fb3, ffreq, fw4, tabs)

    z4 = _inproj_nt(x, n1g.reshape(1, d), sh1, sc1, w_in.T.astype(BF16), DFT_MINOR)
    z4 = z4.transpose(0, 2, 1, 3)
    cw = conv_w.reshape(SHORT_CONV, 3, width)
    cb = conv_b.reshape(3, width)
    cols = []
    for part in range(3):
        cols += [cw[0, part], cw[1, part], cw[2, part], cb[part]]
    cols += [fbias[0], fbias[1], jnp.zeros_like(fbias[0]), jnp.zeros_like(fbias[0])]
    pc = jnp.stack(cols, axis=1)
    y4 = _hyena_core(z4, kf, pc, tabs["f1_half"], tabs["twr"], tabs["twi"], tabs["f2e"], tabs["f2c"],
                     tabs["g1e"])
    yt = y4.reshape(b, width, s)
    x = _proj_res([yt], w_out.astype(BF16), x, g1, transposed=True, tn_pref=d)
    return _ffn(x, n2g.reshape(1, d), sh2, sc2, g2, wg, wu, wd)


def kernel(x, c, ctx, c_ctx, ada_w, ada_b, norm1, norm2, ffn_w_gate, ffn_w_up, ffn_w_down, e_w_in, e_w_out, e_lambda, e_subln, e_sgu_norm_g, e_sgu_norm_b, e_sgu_w, e_sgu_b, o_w_in, o_conv_w, o_conv_b, o_filt_w1, o_filt_b1, o_filt_w2, o_filt_b2, o_filt_w3, o_filt_b3, o_filt_freq, o_filt_w4, o_filt_bias, o_w_out, final_norm):
    b, s, d = x.shape
    depth = ada_w.shape[0]
    assert b == 2, "the long convolution packs exactly two batches into one complex signal"
    assert depth == 2, "odd layers here never carry the context stream"
    cond = jnp.zeros((8, d), F32).at[:b].set(c).at[b].set(c_ctx)
    mod_all = _adaln_all(cond, ada_w, ada_b)
    xc = ctx
    for i in range(depth):
        j = i // 2
        parts = jnp.split(mod_all[i], 6, axis=-1)
        mods = [p[:b, None, :] for p in parts]
        cmods = [jnp.broadcast_to(p[b:b + 1, None, :], (b, 1, d)) for p in parts]
        wg, wu, wd = (ffn_w_gate[i].astype(BF16), ffn_w_up[i].astype(BF16), ffn_w_down[i].astype(BF16))
        if i % 2 == 0:
            x, xc = _even_layer(x, xc, mods, cmods, norm1[i], norm2[i], e_w_in[j], e_w_out[j], e_lambda[j],
                                e_subln[j], e_sgu_norm_g[j], e_sgu_norm_b[j], e_sgu_w[j], e_sgu_b[j],
                                wg, wu, wd, i)
        else:
            x = _odd_layer(x, mods, norm1[i], norm2[i], o_w_in[j], o_conv_w[j], o_conv_b[j], o_filt_w1[j],
                           o_filt_b1[j], o_filt_w2[j], o_filt_b2[j], o_filt_w3[j], o_filt_b3[j],
                           o_filt_freq[j], o_filt_w4[j], o_filt_bias[j], o_w_out[j], wg, wu, wd)
    return _final_norm(x, final_norm)
```

```python
import functools
import math

import numpy as np
import jax
import jax.numpy as jnp
from jax import lax
from jax.experimental import pallas as pl
from jax.experimental.pallas import tpu as pltpu

F32 = jnp.float32
BF16 = jnp.bfloat16
HIGHEST = lax.Precision.HIGHEST

GRID_W = 64
NORM_EPS = 1e-6
DIFF_HEADS = 8
DIFF_QK_DIM = 64
DIFF_V_DIM = 2 * DIFF_QK_DIM
DIFF_SCALE = DIFF_QK_DIM ** -0.5
A_WIDTH = DIFF_HEADS * DIFF_V_DIM
Q_COLS = DIFF_HEADS * 2 * DIFF_QK_DIM
ROPE_THETA = 10000.0
ROPE_AXIS_DIM = DIFF_QK_DIM // 2
SUBLN_EPS = 1e-5
SGU_GROUPS = 8
SGU_CHUNK = 128
SGU_CH = 128
B_WIDTH = SGU_GROUPS * SGU_CH
LN_EPS = 1e-5
HYENA_ORDER = 2
SHORT_CONV = 3
FILTER_EMB = 33
FILTER_BANDS = (FILTER_EMB - 1) // 2
FILTER_HIDDEN = 64
DECAY_TARGET = 1e-2
MAX_DECAY = math.log(DECAY_TARGET) / 0.3
MIN_DECAY = math.log(DECAY_TARGET) / 1.5

LANES = 128
ONES_ROWS = 16
DFT_MINOR = 256
VMEM_LIMIT = 48 * 1024 * 1024
VMEM_LIMIT_WIDE = 56 * 1024 * 1024

NT_DIMS = (((1,), (1,)), ((), ()))
TN_DIMS = (((0,), (0,)), ((), ()))


def _params(*sem, vmem=VMEM_LIMIT):
    return pltpu.CompilerParams(dimension_semantics=sem, vmem_limit_bytes=vmem)


def _tile(n, pref):
    return pref if n % pref == 0 else n


def _adaln_kernel(c_ref, w_ref, b_ref, o_ref):
    a = c_ref[...]
    a = a * jax.nn.sigmoid(a)
    o_ref[...] = jnp.dot(a, w_ref[...], preferred_element_type=F32, precision=HIGHEST) + b_ref[...]


def _adaln_all(cond, ada_w, ada_b):
    depth, d, n6 = ada_w.shape
    tn = _tile(n6, 1024)
    return pl.pallas_call(
        _adaln_kernel,
        grid=(depth, n6 // tn),
        in_specs=[pl.BlockSpec((8, d), lambda l, j: (0, 0)),
                  pl.BlockSpec((None, d, tn), lambda l, j: (l, 0, j)),
                  pl.BlockSpec((None, 1, tn), lambda l, j: (l, 0, j))],
        out_specs=pl.BlockSpec((None, 8, tn), lambda l, j: (l, 0, j)),
        out_shape=jax.ShapeDtypeStruct((depth, 8, n6), F32),
        compiler_params=_params("parallel", "parallel"),
        name="adaln",
    )(cond, ada_w, ada_b.reshape(depth, 1, n6))


def _norm_mod(x_ref, g_ref, sh_ref, sc_ref):
    x = x_ref[...]
    ms = jnp.mean(x * x, axis=-1, keepdims=True)
    y = x * lax.rsqrt(ms + NORM_EPS) * g_ref[...]
    return (y * (1.0 + sc_ref[...]) + sh_ref[...]).astype(BF16)


def _inproj_kernel(x_ref, g_ref, sh_ref, sc_ref, w_ref, cos_ref, sin_ref, o_ref, hs_ref, *,
                   n_q, n_qk, rope):
    j = pl.program_id(2)

    @pl.when(j == 0)
    def _():
        hs_ref[...] = _norm_mod(x_ref, g_ref, sh_ref, sc_ref)

    tm, tn = o_ref.shape
    rc = min(tm, 256)

    def by_row_chunks(epilogue):
        for r in range(tm // rc):
            rows = pl.ds(r * rc, rc)
            acc = jnp.dot(hs_ref[rows, :], w_ref[...], preferred_element_type=F32)
            o_ref[rows, :] = epilogue(acc, rows).astype(o_ref.dtype)

    def qk_epilogue(a, rows):
        if rope:
            lane = lax.broadcasted_iota(jnp.int32, a.shape, 1)
            first = (lane & 31) < 16
            partner = jnp.where(first, pltpu.roll(a, tn - 16, 1), pltpu.roll(a, 16, 1))
            reps = tn // LANES
            a = (a * jnp.tile(cos_ref[rows, :], (1, reps))
                 + partner * jnp.tile(sin_ref[rows, :], (1, reps)))
        return jnp.where(j < n_q, a * DIFF_SCALE, a)

    @pl.when(j < n_qk)
    def _():
        by_row_chunks(qk_epilogue)

    @pl.when(j >= n_qk)
    def _():
        by_row_chunks(lambda a, rows: jax.nn.gelu(a))


def _inproj(x, g, sh, sc, w, cos_t, sin_t, rope):
    b, s, d = x.shape
    n = w.shape[1]
    tm = _tile(s, 1024)
    tn = 512
    kern = functools.partial(_inproj_kernel, n_q=Q_COLS // tn, n_qk=2 * Q_COLS // tn, rope=rope)
    return pl.pallas_call(
        kern,
        grid=(b, s // tm, n // tn),
        in_specs=[pl.BlockSpec((None, tm, d), lambda bb, i, j: (bb, i, 0)),
                  pl.BlockSpec((1, d), lambda bb, i, j: (0, 0)),
                  pl.BlockSpec((None, 1, d), lambda bb, i, j: (bb, 0, 0)),
                  pl.BlockSpec((None, 1, d), lambda bb, i, j: (bb, 0, 0)),
                  pl.BlockSpec((d, tn), lambda bb, i, j: (0, j)),
                  pl.BlockSpec((tm, LANES), lambda bb, i, j: (i, 0)),
                  pl.BlockSpec((tm, LANES), lambda bb, i, j: (i, 0))],
        out_specs=pl.BlockSpec((None, tm, tn), lambda bb, i, j: (bb, i, j)),
        out_shape=jax.ShapeDtypeStruct((b, s, n), BF16),
        scratch_shapes=[pltpu.VMEM((tm, d), BF16)],
        compiler_params=_params("parallel", "parallel", "arbitrary"),
        name="inproj",
    )(x, g, sh, sc, w, cos_t, sin_t)


def _inproj_nt_kernel(x_ref, g_ref, sh_ref, sc_ref, wt_ref, o_ref, hs_ref, *, tl):
    j = pl.program_id(2)

    @pl.when(j == 0)
    def _():
        hs_ref[...] = _norm_mod(x_ref, g_ref, sh_ref, sc_ref)

    acc = lax.dot_general(wt_ref[...], hs_ref[...], NT_DIMS, preferred_element_type=F32)
    for c in range(o_ref.shape[0]):
        o_ref[c] = acc[:, c * tl:(c + 1) * tl].astype(o_ref.dtype)


def _inproj_nt(x, g, sh, sc, wt, tl):
    b, s, d = x.shape
    n = wt.shape[0]
    tm = _tile(s, 1024)
    tn = 512
    kern = functools.partial(_inproj_nt_kernel, tl=tl)
    return pl.pallas_call(
        kern,
        grid=(b, s // tm, n // tn),
        in_specs=[pl.BlockSpec((None, tm, d), lambda bb, i, j: (bb, i, 0)),
                  pl.BlockSpec((1, d), lambda bb, i, j: (0, 0)),
                  pl.BlockSpec((None, 1, d), lambda bb, i, j: (bb, 0, 0)),
                  pl.BlockSpec((None, 1, d), lambda bb, i, j: (bb, 0, 0)),
                  pl.BlockSpec((tn, d), lambda bb, i, j: (j, 0))],
        out_specs=pl.BlockSpec((None, tm // tl, tn, tl), lambda bb, i, j: (bb, i, j, 0)),
        out_shape=jax.ShapeDtypeStruct((b, s // tl, n, tl), BF16),
        scratch_shapes=[pltpu.VMEM((tm, d), BF16)],
        compiler_params=_params("parallel", "parallel", "arbitrary"),
        name="inproj_nt",
    )(x, g, sh, sc, wt)


def _ffn_up_kernel(x_ref, g_ref, sh_ref, sc_ref, wg_ref, wu_ref, o_ref, hs_ref):
    j = pl.program_id(2)

    @pl.when(j == 0)
    def _():
        hs_ref[...] = _norm_mod(x_ref, g_ref, sh_ref, sc_ref)

    hs = hs_ref[...]
    gate = jnp.dot(hs, wg_ref[...], preferred_element_type=F32)
    up = jnp.dot(hs, wu_ref[...], preferred_element_type=F32)
    o_ref[...] = (gate * jax.nn.sigmoid(gate) * up).astype(o_ref.dtype)


def _ffn_up(x, g, sh, sc, wg, wu):
    b, s, d = x.shape
    n = wg.shape[1]
    tm = _tile(s, 1024)
    tn = 512
    return pl.pallas_call(
        _ffn_up_kernel,
        grid=(b, s // tm, n // tn),
        in_specs=[pl.BlockSpec((None, tm, d), lambda bb, i, j: (bb, i, 0)),
                  pl.BlockSpec((1, d), lambda bb, i, j: (0, 0)),
                  pl.BlockSpec((None, 1, d), lambda bb, i, j: (bb, 0, 0)),
                  pl.BlockSpec((None, 1, d), lambda bb, i, j: (bb, 0, 0)),
                  pl.BlockSpec((d, tn), lambda bb, i, j: (0, j)),
                  pl.BlockSpec((d, tn), lambda bb, i, j: (0, j))],
        out_specs=pl.BlockSpec((None, tm, tn), lambda bb, i, j: (bb, i, j)),
        out_shape=jax.ShapeDtypeStruct((b, s, n), BF16),
        scratch_shapes=[pltpu.VMEM((tm, d), BF16)],
        compiler_params=_params("parallel", "parallel", "arbitrary"),
        name="ffn_up",
    )(x, g, sh, sc, wg, wu)


def _proj_res_kernel(*refs, ksizes, transposed):
    n = len(ksizes)
    a_refs = refs[:n]
    w_ref, x_ref, gate_ref, o_ref = refs[n:]
    acc = None
    off = 0
    for a_ref, ks in zip(a_refs, ksizes):
        w = w_ref[off:off + ks, :]
        if transposed:
            part = lax.dot_general(a_ref[...], w, TN_DIMS, preferred_element_type=F32)
        else:
            part = jnp.dot(a_ref[...], w, preferred_element_type=F32)
        acc = part if acc is None else acc + part
        off += ks
    o_ref[...] = x_ref[...] + gate_ref[...] * acc


def _proj_res(a_list, w, x, gate, transposed=False, tm_pref=512, tn_pref=512, vmem=VMEM_LIMIT):
    b, s, d = x.shape
    ksizes = tuple(a.shape[1] if transposed else a.shape[2] for a in a_list)
    ktot = sum(ksizes)
    tm = _tile(s, tm_pref)
    tn = _tile(d, tn_pref)
    if transposed:
        a_specs = [pl.BlockSpec((None, ks, tm), lambda bb, i, j: (bb, 0, i)) for ks in ksizes]
    else:
        a_specs = [pl.BlockSpec((None, tm, ks), lambda bb, i, j: (bb, i, 0)) for ks in ksizes]
    kern = functools.partial(_proj_res_kernel, ksizes=ksizes, transposed=transposed)
    return pl.pallas_call(
        kern,
        grid=(b, s // tm, d // tn),
        in_specs=a_specs + [pl.BlockSpec((ktot, tn), lambda bb, i, j: (0, j)),
                            pl.BlockSpec((None, tm, tn), lambda bb, i, j: (bb, i, j)),
                            pl.BlockSpec((None, 1, tn), lambda bb, i, j: (bb, 0, j))],
        out_specs=pl.BlockSpec((None, tm, tn), lambda bb, i, j: (bb, i, j)),
        out_shape=jax.ShapeDtypeStruct((b, s, d), F32),
        compiler_params=_params("parallel", "parallel", "parallel", vmem=vmem),
        name="proj_res",
    )(*a_list, w, x, gate)


def _attn_kernel(*refs, n_chunks, lam_init):
    if n_chunks:
        lam_ref, q_ref, kc_ref, vct_ref, k_ref, vt_ref, g_ref, o_ref, acc_ref = refs[:9]
        s_refs = refs[9:]
    else:
        lam_ref, q_ref, kc_ref, vct_ref, g_ref, o_ref, acc_ref = refs
    q = q_ref[...]
    tq = q.shape[0]
    dv = DIFF_V_DIM
    qm = (q[:, :DIFF_QK_DIM], q[:, DIFF_QK_DIM:])

    def scores(kblk):
        return tuple(lax.dot_general(kblk[:, m * DIFF_QK_DIM:(m + 1) * DIFF_QK_DIM], qm[m], NT_DIMS,
                                     preferred_element_type=F32) for m in range(2))

    def absorb(s_pair, vtblk, m_pair):
        vext = jnp.concatenate([vtblk, jnp.ones((ONES_ROWS, vtblk.shape[1]), BF16)], axis=0)
        out = []
        for m in range(2):
            m_old = m_pair[m]
            m_new = jnp.maximum(m_old, jnp.max(s_pair[m], axis=0, keepdims=True))
            alpha = jnp.exp(m_old - m_new)
            p = jnp.exp((s_pair[m] - m_new).astype(BF16))
            pv = jnp.dot(vext, p, preferred_element_type=F32)
            acc_ref[m] = alpha * acc_ref[m] + pv
            out.append(m_new)
        return tuple(out)

    def store(ref, s_pair):
        ref[0] = s_pair[0]
        ref[1] = s_pair[1]

    acc_ref[...] = jnp.zeros_like(acc_ref)
    init = jnp.full((1, tq), -1e30, F32)
    m_pair = absorb(scores(kc_ref[...]), vct_ref[0], (init, init))
    if n_chunks:
        group = len(s_refs) // 2
        sets = (s_refs[:group], s_refs[group:])
        assert n_chunks % (2 * group) == 0

        def half_trip(cur, nxt, base, mp, lookahead):
            for k in range(group):
                if lookahead:
                    store(nxt[k], scores(k_ref[base + group + k]))
                mp = absorb((cur[k][0], cur[k][1]), vt_ref[base + k], mp)
            return mp

        def trip(j, mp, lookahead):
            base = 2 * group * j
            mp = half_trip(sets[0], sets[1], base, mp, True)
            return half_trip(sets[1], sets[0], base + group, mp, lookahead)

        for k in range(group):
            store(sets[0][k], scores(k_ref[k]))
        n_trips = n_chunks // (2 * group)
        m_pair = lax.fori_loop(0, n_trips - 1, lambda j, mp: trip(j, mp, True), m_pair)
        m_pair = trip(n_trips - 1, m_pair, False)

    lp = lam_ref[...]
    lam = (jnp.exp(jnp.sum(lp[0:1] * lp[1:2], axis=-1, keepdims=True))
           - jnp.exp(jnp.sum(lp[2:3] * lp[3:4], axis=-1, keepdims=True)) + lam_init)
    acc0, acc1 = acc_ref[0], acc_ref[1]
    o = acc0[:dv] / acc0[dv:dv + 1] - lam * (acc1[:dv] / acc1[dv:dv + 1])
    ot = o.T
    ms = jnp.mean(ot * ot, axis=-1, keepdims=True)
    on = ot * lax.rsqrt(ms + SUBLN_EPS) * g_ref[...] * (1.0 - lam_init)
    o_ref[...] = on.astype(o_ref.dtype)


def _attention(lam_p, subln, q_arr, kc_arr, vct_arr, k4=None, vt4=None, *, lam_init):
    b, sq = q_arr.shape[0], q_arr.shape[1]
    sc = kc_arr.shape[1]
    h = DIFF_HEADS
    tq = _tile(sq, 256)
    dv = DIFF_V_DIM
    n_chunks = 0 if k4 is None else k4.shape[1]
    in_specs = [pl.BlockSpec((4, DIFF_QK_DIM), lambda bb, hh, i: (0, 0)),
                pl.BlockSpec((None, tq, dv), lambda bb, hh, i: (bb, i, hh)),
                pl.BlockSpec((None, sc, dv), lambda bb, hh, i: (bb, 0, h + hh)),
                pl.BlockSpec((None, 1, dv, sc), lambda bb, hh, i: (bb, 0, hh, 0))]
    args = [lam_p, q_arr, kc_arr, vct_arr]
    if n_chunks:
        tk = k4.shape[2]
        in_specs += [pl.BlockSpec((None, n_chunks, tk, dv), lambda bb, hh, i: (bb, 0, 0, h + hh)),
                     pl.BlockSpec((None, n_chunks, dv, tk), lambda bb, hh, i: (bb, 0, hh, 0))]
        args += [k4, vt4]
    in_specs.append(pl.BlockSpec((1, dv), lambda bb, hh, i: (0, 0)))
    args.append(subln)
    kern = functools.partial(_attn_kernel, n_chunks=n_chunks, lam_init=lam_init)
    scratch = [pltpu.VMEM((2, dv + ONES_ROWS, tq), F32)]
    if n_chunks:
        group = 4 if n_chunks % 16 == 0 else 2
        scratch += [pltpu.VMEM((2, tk, tq), F32) for _ in range(2 * group)]
    return pl.pallas_call(
        kern,
        grid=(b, h, sq // tq),
        in_specs=in_specs,
        out_specs=pl.BlockSpec((None, tq, dv), lambda bb, hh, i: (bb, i, hh)),
        out_shape=jax.ShapeDtypeStruct((b, sq, A_WIDTH), BF16),
        scratch_shapes=scratch,
        compiler_params=_params("parallel", "parallel", "parallel"),
        name="diff_attn",
    )(*args)


def _sgu_kernel(u_ref, g_ref, ng_ref, nb_ref, w_ref, bs_ref, o_ref):
    w = w_ref[...]
    for c in range(u_ref.shape[0] // SGU_CHUNK):
        sl = slice(c * SGU_CHUNK, (c + 1) * SGU_CHUNK)
        gg = g_ref[sl, :].astype(F32)
        mu = jnp.mean(gg, axis=-1, keepdims=True)
        dev = gg - mu
        var = jnp.mean(dev * dev, axis=-1, keepdims=True)
        vv = dev * lax.rsqrt(var + LN_EPS) * ng_ref[...] + nb_ref[...]
        mixed = jnp.dot(w, vv.astype(BF16), preferred_element_type=F32) + bs_ref[...]
        o_ref[sl, :] = (u_ref[sl, :].astype(F32) * mixed).astype(o_ref.dtype)


def _sgu(qkug, norm_g, norm_b, w_s, b_s):
    b, s = qkug.shape[0], qkug.shape[1]
    tm = _tile(s, 1024)
    gcount = SGU_GROUPS
    ublk = 2 * Q_COLS // SGU_CH
    return pl.pallas_call(
        _sgu_kernel,
        grid=(b, s // tm, gcount),
        in_specs=[pl.BlockSpec((None, tm, SGU_CH), lambda bb, i, gi: (bb, i, ublk + gi)),
                  pl.BlockSpec((None, tm, SGU_CH), lambda bb, i, gi: (bb, i, ublk + gcount + gi)),
                  pl.BlockSpec((None, 1, SGU_CH), lambda bb, i, gi: (gi, 0, 0)),
                  pl.BlockSpec((None, 1, SGU_CH), lambda bb, i, gi: (gi, 0, 0)),
                  pl.BlockSpec((None, SGU_CHUNK, SGU_CHUNK), lambda bb, i, gi: (gi, 0, 0)),
                  pl.BlockSpec((None, SGU_CHUNK, 1), lambda bb, i, gi: (gi, 0, 0))],
        out_specs=pl.BlockSpec((None, tm, SGU_CH), lambda bb, i, gi: (bb, i, gi)),
        out_shape=jax.ShapeDtypeStruct((b, s, B_WIDTH), BF16),
        compiler_params=_params("parallel", "parallel", "parallel"),
        name="sgu",
    )(qkug, qkug, norm_g.reshape(gcount, 1, SGU_CH), norm_b.reshape(gcount, 1, SGU_CH),
      w_s.astype(BF16), b_s.reshape(gcount, SGU_CHUNK, 1))


def _filter_mlp_kernel(f_ref, w1_ref, b1_ref, w2_ref, b2_ref, w3_ref, b3_ref, fr_ref, o_ref):
    def lin(a, w_ref, b_ref):
        return jnp.dot(a, w_ref[...], preferred_element_type=F32, precision=HIGHEST) + b_ref[...]
    fr = fr_ref[...]
    hcur = jnp.sin(fr[0:1] * lin(f_ref[...], w1_ref, b1_ref))
    hcur = jnp.sin(fr[1:2] * lin(hcur, w2_ref, b2_ref))
    o_ref[...] = jnp.sin(fr[2:3] * lin(hcur, w3_ref, b3_ref))


def _filter_mlp(feats, w1, b1, w2, b2, w3, b3, freq):
    rows, emb = feats.shape
    hid = FILTER_HIDDEN
    tr = _tile(rows, 2048)
    full = lambda shape: pl.BlockSpec(shape, lambda i: (0,) * len(shape))
    return pl.pallas_call(
        _filter_mlp_kernel,
        grid=(rows // tr,),
        in_specs=[pl.BlockSpec((tr, emb), lambda i: (i, 0)),
                  full((emb, hid)), full((1, hid)), full((hid, hid)), full((1, hid)),
                  full((hid, hid)), full((1, hid)), full((3, hid))],
        out_specs=pl.BlockSpec((tr, hid), lambda i: (i, 0)),
        out_shape=jax.ShapeDtypeStruct((rows, hid), F32),
        compiler_params=_params("parallel"),
        name="filter_mlp",
    )(feats, w1, b1.reshape(1, hid), w2, b2.reshape(1, hid), w3, b3.reshape(1, hid), freq)


def _filter_raw_kernel(w4t_ref, h_ref, t_ref, delta_ref, o_ref, *, zero_tile):
    rt = pl.program_id(2)
    raw = lax.dot_general(w4t_ref[...], h_ref[...], NT_DIMS, preferred_element_type=F32,
                          precision=HIGHEST)
    raw = raw * jnp.exp(-(delta_ref[...] * t_ref[...]))
    for c in range(o_ref.shape[0]):
        o_ref[c] = raw[:, c * DFT_MINOR:(c + 1) * DFT_MINOR]

    @pl.when(rt == zero_tile)
    def _():
        col = lax.broadcasted_iota(jnp.int32, (raw.shape[0], DFT_MINOR), 1)
        o_ref[0] = jnp.where(col == 0, 0.0, raw[:, :DFT_MINOR])


def _filter_raw(w4t, h3, t_row, deltas, seq):
    c = w4t.shape[2]
    rows = h3.shape[0]
    tr = _tile(rows // 2, 2048)
    tc = _tile(c, 512)
    half_tiles = seq // tr
    kern = functools.partial(_filter_raw_kernel, zero_tile=half_tiles)
    return pl.pallas_call(
        kern,
        grid=(HYENA_ORDER, c // tc, rows // tr),
        in_specs=[pl.BlockSpec((None, None, tc, FILTER_HIDDEN),
                               lambda n, ci, rt: (n, rt // half_tiles, ci, 0)),
                  pl.BlockSpec((tr, FILTER_HIDDEN), lambda n, ci, rt: (rt, 0)),
                  pl.BlockSpec((1, tr), lambda n, ci, rt: (0, rt)),
                  pl.BlockSpec((tc, 1), lambda n, ci, rt: (ci, 0))],
        out_specs=pl.BlockSpec((None, tr // DFT_MINOR, tc, DFT_MINOR), lambda n, ci, rt: (n, rt, ci, 0)),
        out_shape=jax.ShapeDtypeStruct((HYENA_ORDER, rows // DFT_MINOR, c, DFT_MINOR), F32),
        compiler_params=_params("parallel", "parallel", "parallel"),
        name="filter_raw",
    )(w4t, h3, t_row, deltas)


def _cmul(ar, ai, br, bi):
    return ar * br - ai * bi, ar * bi + ai * br


def _store_complex(ref, c, n1, re, im):
    r0 = pl.multiple_of(c * n1, n1)
    ref[pl.ds(r0, n1), :DFT_MINOR] = re.astype(ref.dtype)
    ref[pl.ds(r0, n1), DFT_MINOR:] = im.astype(ref.dtype)


def _load_complex(ref, c, n1):
    r0 = pl.multiple_of(c * n1, n1)
    tile = ref[pl.ds(r0, n1), :]
    return tile[:, :DFT_MINOR], tile[:, DFT_MINOR:]


def _filter_spec_kernel(k_ref, f1_ref, twr_ref, twi_ref, f2_ref, o_ref, a2_ref, *, inv_n):
    tc, n1, n2 = k_ref.shape

    def left(c, carry):
        k = k_ref[c]
        nrm = jnp.sum(jnp.sum(jnp.abs(k), axis=1, keepdims=True), axis=0, keepdims=True)
        kn = (k * (inv_n / nrm)).astype(BF16)
        a = jnp.dot(f1_ref[...], kn, preferred_element_type=F32)
        ar, ai = _cmul(a[:n1], a[n1:], twr_ref[...], twi_ref[...])
        _store_complex(a2_ref, c, n1, ar, ai)
        return carry

    lax.fori_loop(0, tc, left, 0, unroll=4)
    z = jnp.dot(a2_ref[...], f2_ref[...], preferred_element_type=F32).reshape(tc, n1, 2 * n2)
    o_ref[:, 0] = z[:, :, :n2].astype(o_ref.dtype)
    o_ref[:, 1] = z[:, :, n2:].astype(o_ref.dtype)


def _filter_spec(k2, f1_full, twr, twi, f2e):
    nc, n1, n2 = k2.shape
    tc = 16
    kern = functools.partial(_filter_spec_kernel, inv_n=1.0 / (n1 * n2))
    full = lambda shape: pl.BlockSpec(shape, lambda i: (0,) * len(shape))
    return pl.pallas_call(
        kern,
        grid=(nc // tc,),
        in_specs=[pl.BlockSpec((tc, n1, n2), lambda i: (i, 0, 0)),
                  full((2 * n1, n1)), full((n1, n2)), full((n1, n2)), full((2 * n2, 2 * n2))],
        out_specs=pl.BlockSpec((tc, 2, n1, n2), lambda i: (i, 0, 0, 0)),
        out_shape=jax.ShapeDtypeStruct((nc, 2, n1, n2), BF16),
        scratch_shapes=[pltpu.VMEM((tc * n1, 2 * n2), BF16)],
        compiler_params=_params("parallel"),
        name="filter_spec",
    )(k2, f1_full, twr, twi, f2e)


def _hyena_kernel(zv_ref, z1_ref, z2_ref, kf_ref, pc_ref, f1_ref, twr_ref, twi_ref, f2_ref, f2c_ref,
                  g1_ref, o_ref, u_ref, gate_ref, a2_ref, z_ref):
    nb, tc, hr, n2 = zv_ref.shape
    n1 = 2 * hr
    row = lax.broadcasted_iota(jnp.int32, (hr, n2), 0)
    lane = lax.broadcasted_iota(jnp.int32, (hr, n2), 1)
    first_lane, last_lane = lane == 0, lane == n2 - 1
    first_elem = first_lane & (row == 0)
    last_elem = last_lane & (row == hr - 1)

    def short_conv(z, w0, w1, w2, bias):
        prev = pltpu.roll(z, 1, 1)
        prev = jnp.where(first_lane, pltpu.roll(prev, 1, 0), prev)
        prev = jnp.where(first_elem, 0.0, prev)
        nxt = pltpu.roll(z, n2 - 1, 1)
        nxt = jnp.where(last_lane, pltpu.roll(nxt, hr - 1, 0), nxt)
        nxt = jnp.where(last_elem, 0.0, nxt)
        return bias + prev * w0 + z * w1 + nxt * w2

    def params(c):
        pc = pc_ref[pl.ds(c, 1), :]
        return lambda idx: pc[:, idx:idx + 1]

    def convs(c, carry):
        par = params(c)
        for bb in range(nb):
            u_ref[bb, c] = short_conv(zv_ref[bb, c].astype(F32), par(0), par(1), par(2), par(3))
            gate_ref[0, bb, c] = short_conv(z1_ref[bb, c].astype(F32), par(4), par(5), par(6), par(7))
            gate_ref[1, bb, c] = short_conv(z2_ref[bb, c].astype(F32), par(8), par(9), par(10), par(11))
        return carry

    lax.fori_loop(0, tc, convs, 0, unroll=2)

    for n in range(HYENA_ORDER):
        last = n == HYENA_ORDER - 1

        def fwd_left(c, carry):
            xs = jnp.concatenate([u_ref[0, c], u_ref[1, c]], axis=0).astype(BF16)
            a = jnp.dot(f1_ref[...], xs, preferred_element_type=F32)
            ar, ai = _cmul(a[:n1], a[n1:], twr_ref[...], twi_ref[...])
            _store_complex(a2_ref, c, n1, ar, ai)
            return carry

        lax.fori_loop(0, tc, fwd_left, 0, unroll=4)
        z_ref[...] = jnp.dot(a2_ref[...], f2_ref[...], preferred_element_type=F32)

        def spectrum(c, carry):
            zr, zi = _load_complex(z_ref, c, n1)
            wr, wi = _cmul(zr, zi, kf_ref[n, c, 0].astype(F32), kf_ref[n, c, 1].astype(F32))
            _store_complex(a2_ref, c, n1, wr, wi)
            return carry

        lax.fori_loop(0, tc, spectrum, 0, unroll=4)
        z_ref[...] = jnp.dot(a2_ref[...], f2c_ref[...], preferred_element_type=F32)

        def inv_left(c, carry):
            br, bi = _load_complex(z_ref, c, n1)
            br, bi = _cmul(br, bi, twr_ref[...], -twi_ref[...])
            bs = jnp.concatenate([br, bi], axis=0).astype(BF16)
            y = jnp.dot(g1_ref[...], bs, preferred_element_type=F32)
            fb = params(c)(12 + n)
            for bb in range(nb):
                new = gate_ref[n, bb, c] * (y[bb * hr:(bb + 1) * hr] + u_ref[bb, c] * fb)
                if last:
                    o_ref[bb, c] = new.astype(o_ref.dtype)
                else:
                    u_ref[bb, c] = new
            return carry

        lax.fori_loop(0, tc, inv_left, 0, unroll=4)


def _hyena_core(z4, kf, pc, f1h, twr, twi, f2e, f2c, g1e):
    nb, c3, hr, n2 = z4.shape
    c = c3 // 3
    n1 = 2 * hr
    tc = 16
    nct = c // tc
    full = lambda shape: pl.BlockSpec(shape, lambda i: (0,) * len(shape))
    zspec = lambda part: pl.BlockSpec((nb, tc, hr, n2), lambda i: (0, i + part * nct, 0, 0))
    return pl.pallas_call(
        _hyena_kernel,
        grid=(nct,),
        in_specs=[zspec(0), zspec(1), zspec(2),
                  pl.BlockSpec((HYENA_ORDER, tc, 2, n1, n2), lambda i: (0, i, 0, 0, 0)),
                  pl.BlockSpec((tc, 16), lambda i: (i, 0)),
                  full((2 * n1, n1)), full((n1, n2)), full((n1, n2)),
                  full((2 * n2, 2 * n2)), full((2 * n2, 2 * n2)), full((n1, 2 * n1))],
        out_specs=pl.BlockSpec((nb, tc, hr, n2), lambda i: (0, i, 0, 0)),
        out_shape=jax.ShapeDtypeStruct((nb, c, hr, n2), BF16),
        scratch_shapes=[pltpu.VMEM((nb, tc, hr, n2), F32),
                        pltpu.VMEM((HYENA_ORDER, nb, tc, hr, n2), F32),
                        pltpu.VMEM((tc * n1, 2 * n2), BF16),
                        pltpu.VMEM((tc * n1, 2 * n2), F32)],
        compiler_params=_params("parallel"),
        name="hyena_core",
    )(z4, z4, z4, kf, pc, f1h, twr, twi, f2e, f2c, g1e)


def _dft_tables(n1):
    n2 = DFT_MINOR
    n = n1 * n2
    a1 = 2.0 * np.pi * np.outer(np.arange(n1), np.arange(n1)) / n1
    f1r, f1i = np.cos(a1), -np.sin(a1)
    a2 = 2.0 * np.pi * np.outer(np.arange(n2), np.arange(n2)) / n2
    f2r, f2i = np.cos(a2), -np.sin(a2)
    at = 2.0 * np.pi * np.outer(np.arange(n1), np.arange(n2)) / n
    twr, twi = np.cos(at), -np.sin(at)
    hr = n1 // 2
    f1_full = np.concatenate([f1r, f1i], axis=0)
    f1_half = np.block([[f1r[:, :hr], -f1i[:, :hr]], [f1i[:, :hr], f1r[:, :hr]]])
    f2e = np.block([[f2r, f2i], [-f2i, f2r]])
    f2c = np.block([[f2r, -f2i], [f2i, f2r]])
    gr, gi = f1r[:hr, :], -f1i[:hr, :]
    g1e = np.block([[gr, -gi], [gi, gr]])
    bf = lambda m: jnp.asarray(m, F32).astype(BF16)
    return dict(f1_full=bf(f1_full), f1_half=bf(f1_half), f2e=bf(f2e), f2c=bf(f2c), g1e=bf(g1e),
                twr=jnp.asarray(twr, F32), twi=jnp.asarray(twi, F32))


def _hyena_spectra(seq, fw1, fb1, fw2, fb2, fw3, fb3, ffreq, fw4, tabs):
    l = seq
    width = fw4.shape[1] // (2 * HYENA_ORDER)
    idx = jnp.concatenate([jnp.arange(l), l - jnp.arange(l)]) % l
    t_lin = jnp.linspace(0.0, 1.0, l, dtype=F32)[idx]
    w = 2.0 * math.pi * idx.astype(F32)[:, None] / l
    f = jnp.linspace(1e-4, FILTER_BANDS - 1, FILTER_BANDS, dtype=F32)[None, :]
    feats = jnp.concatenate([t_lin[:, None], jnp.cos(f * w), -jnp.sin(f * w)], axis=-1)
    emb_pad = FILTER_HIDDEN
    feats = jnp.pad(feats, ((0, 0), (0, emb_pad - FILTER_EMB)))
    w1p = jnp.pad(fw1, ((0, emb_pad - FILTER_EMB), (0, 0)))
    h3 = _filter_mlp(feats, w1p, fb1, fw2, fb2, fw3, fb3, ffreq)
    w4t = fw4.reshape(FILTER_HIDDEN, HYENA_ORDER, 2, width).transpose(1, 2, 3, 0)
    deltas = jnp.abs(jnp.linspace(MIN_DECAY, MAX_DECAY, width, dtype=F32))[:, None]
    raw = _filter_raw(w4t, h3, t_lin[None, :], deltas, l)
    n1 = raw.shape[1]
    k2 = raw.transpose(0, 2, 1, 3).reshape(HYENA_ORDER * width, n1, DFT_MINOR)
    kf = _filter_spec(k2, tabs["f1_full"], tabs["twr"], tabs["twi"], tabs["f2e"])
    return kf.reshape(HYENA_ORDER, width, 2, n1, DFT_MINOR)


def _rms_kernel(x_ref, g_ref, o_ref):
    x = x_ref[...]
    ms = jnp.mean(x * x, axis=-1, keepdims=True)
    o_ref[...] = x * lax.rsqrt(ms + NORM_EPS) * g_ref[...]


def _final_norm(x, g):
    b, s, d = x.shape
    tm = _tile(s, 1024)
    return pl.pallas_call(
        _rms_kernel,
        grid=(b, s // tm),
        in_specs=[pl.BlockSpec((None, tm, d), lambda bb, i: (bb, i, 0)),
                  pl.BlockSpec((1, d), lambda bb, i: (0, 0))],
        out_specs=pl.BlockSpec((None, tm, d), lambda bb, i: (bb, i, 0)),
        out_shape=jax.ShapeDtypeStruct((b, s, d), F32),
        compiler_params=_params("parallel", "parallel"),
        name="final_norm",
    )(x, g.reshape(1, d))


def _rope_tables(n_tokens):
    tok = jnp.arange(n_tokens)
    row = (tok // GRID_W).astype(F32)
    col = (tok % GRID_W).astype(F32)
    half = ROPE_AXIS_DIM // 2
    inv = 1.0 / (ROPE_THETA ** (jnp.arange(0, ROPE_AXIS_DIM, 2, dtype=F32) / ROPE_AXIS_DIM))
    ang_r = row[:, None] * inv
    ang_c = col[:, None] * inv
    cos64 = jnp.concatenate([jnp.cos(ang_r), jnp.cos(ang_r), jnp.cos(ang_c), jnp.cos(ang_c)], axis=-1)
    sin64 = jnp.concatenate([-jnp.sin(ang_r), jnp.sin(ang_r), -jnp.sin(ang_c), jnp.sin(ang_c)], axis=-1)
    assert cos64.shape[1] == 4 * half == DIFF_QK_DIM
    return jnp.tile(cos64, (1, 2)), jnp.tile(sin64, (1, 2))


def _ffn(x, norm_g, sh, sc, gate, wg, wu, wd):
    hidden = _ffn_up(x, norm_g, sh, sc, wg, wu)
    return _proj_res([hidden], wd, x, gate, tm_pref=1024, vmem=VMEM_LIMIT_WIDE)


def _even_layer(x, xc, mods, cmods, n1g, n2g, w_in, w_out, lam_p, subln, sgu_ng, sgu_nb, sgu_w, sgu_b,
                wg, wu, wd, layer_idx):
    b, s, d = x.shape
    lam_init = 0.8 - 0.6 * math.exp(-0.3 * layer_idx)
    sh1, sc1, g1, sh2, sc2, g2 = mods
    csh1, csc1, cg1, csh2, csc2, cg2 = cmods
    o_k, o_v, o_u = Q_COLS, 2 * Q_COLS, 2 * Q_COLS + A_WIDTH
    w_qkug = jnp.concatenate([w_in[:, :o_v], w_in[:, o_u:]], axis=1).astype(BF16)
    w_vt = w_in[:, o_v:o_u].T.astype(BF16)
    wo = w_out.astype(BF16)
    n1g2 = n1g.reshape(1, d)
    cos_t, sin_t = _rope_tables(s)
    sc_len = xc.shape[1]

    qkug = _inproj(x, n1g2, sh1, sc1, w_qkug, cos_t, sin_t, rope=True)
    cqkug = _inproj(xc, n1g2, csh1, csc1, w_qkug, cos_t[:sc_len], sin_t[:sc_len], rope=False)
    tk = _tile(s, 512)
    vt4 = _inproj_nt(x, n1g2, sh1, sc1, w_vt, tk)
    cvt4 = _inproj_nt(xc, n1g2, csh1, csc1, w_vt, sc_len)
    k4 = qkug.reshape(b, s // tk, tk, qkug.shape[2])
    subln2 = subln.reshape(1, DIFF_V_DIM)

    a_l = _attention(lam_p, subln2, qkug, cqkug, cvt4, k4, vt4, lam_init=lam_init)
    s_l = _sgu(qkug, sgu_ng, sgu_nb, sgu_w, sgu_b)
    x = _proj_res([a_l, s_l], wo, x, g1, tn_pref=d)
    x = _ffn(x, n2g.reshape(1, d), sh2, sc2, g2, wg, wu, wd)

    a_c = _attention(lam_p, subln2, cqkug, cqkug, cvt4, lam_init=lam_init)
    s_c = _sgu(cqkug, sgu_ng, sgu_nb, sgu_w, sgu_b)
    xc = _proj_res([a_c, s_c], wo, xc, cg1, tn_pref=d)
    xc = _ffn(xc, n2g.reshape(1, d), csh2, csc2, cg2, wg, wu, wd)
    return x, xc


def _odd_layer(x, mods, n1g, n2g, w_in, conv_w, conv_b, fw1, fb1, fw2, fb2, fw3, fb3, ffreq, fw4, fbias,
               w_out, wg, wu, wd):
    b, s, d = x.shape
    sh1, sc1, g1, sh2, sc2, g2 = mods
    width = w_out.shape[0]
    n1 = 2 * s // DFT_MINOR
    tabs = _dft_tables(n1)
    kf = _hyena_spectra(s, fw1, fb1, fw2, fb2, fw3, fb3, ffreq, fw4, tabs)

    z4 = _inproj_nt(x, n1g.reshape(1, d), sh1, sc1, w_in.T.astype(BF16), DFT_MINOR)
    z4 = z4.transpose(0, 2, 1, 3)
    cw = conv_w.reshape(SHORT_CONV, 3, width)
    cb = conv_b.reshape(3, width)
    cols = []
    for part in range(3):
        cols += [cw[0, part], cw[1, part], cw[2, part], cb[part]]
    cols += [fbias[0], fbias[1], jnp.zeros_like(fbias[0]), jnp.zeros_like(fbias[0])]
    pc = jnp.stack(cols, axis=1)
    y4 = _hyena_core(z4, kf, pc, tabs["f1_half"], tabs["twr"], tabs["twi"], tabs["f2e"], tabs["f2c"],
                     tabs["g1e"])
    yt = y4.reshape(b, width, s)
    x = _proj_res([yt], w_out.astype(BF16), x, g1, transposed=True, tn_pref=d)
    return _ffn(x, n2g.reshape(1, d), sh2, sc2, g2, wg, wu, wd)


def kernel(x, c, ctx, c_ctx, ada_w, ada_b, norm1, norm2, ffn_w_gate, ffn_w_up, ffn_w_down, e_w_in, e_w_out, e_lambda, e_subln, e_sgu_norm_g, e_sgu_norm_b, e_sgu_w, e_sgu_b, o_w_in, o_conv_w, o_conv_b, o_filt_w1, o_filt_b1, o_filt_w2, o_filt_b2, o_filt_w3, o_filt_b3, o_filt_freq, o_filt_w4, o_filt_bias, o_w_out, final_norm):
    b, s, d = x.shape
    depth = ada_w.shape[0]
    assert b == 2, "the long convolution packs exactly two batches into one complex signal"
    assert depth == 2, "odd layers here never carry the context stream"
    cond = jnp.zeros((8, d), F32).at[:b].set(c).at[b].set(c_ctx)
    mod_all = _adaln_all(cond, ada_w, ada_b)
    xc = ctx
    for i in range(depth):
        j = i // 2
        parts = jnp.split(mod_all[i], 6, axis=-1)
        mods = [p[:b, None, :] for p in parts]
        cmods = [jnp.broadcast_to(p[b:b + 1, None, :], (b, 1, d)) for p in parts]
        wg, wu, wd = (ffn_w_gate[i].astype(BF16), ffn_w_up[i].astype(BF16), ffn_w_down[i].astype(BF16))
        if i % 2 == 0:
            x, xc = _even_layer(x, xc, mods, cmods, norm1[i], norm2[i], e_w_in[j], e_w_out[j], e_lambda[j],
                                e_subln[j], e_sgu_norm_g[j], e_sgu_norm_b[j], e_sgu_w[j], e_sgu_b[j],
                                wg, wu, wd, i)
        else:
            x = _odd_layer(x, mods, norm1[i], norm2[i], o_w_in[j], o_conv_w[j], o_conv_b[j], o_filt_w1[j],
                           o_filt_b1[j], o_filt_w2[j], o_filt_b2[j], o_filt_w3[j], o_filt_b3[j],
                           o_filt_freq[j], o_filt_w4[j], o_filt_bias[j], o_w_out[j], wg, wu, wd)
    return _final_norm(x, final_norm)
```

```python
import functools
import math

import numpy as np
import jax
import jax.numpy as jnp
from jax import lax
from jax.experimental import pallas as pl
from jax.experimental.pallas import tpu as pltpu

F32 = jnp.float32
BF16 = jnp.bfloat16
HIGHEST = lax.Precision.HIGHEST

GRID_W = 64
NORM_EPS = 1e-6
DIFF_HEADS = 8
DIFF_QK_DIM = 64
DIFF_V_DIM = 2 * DIFF_QK_DIM
DIFF_SCALE = DIFF_QK_DIM ** -0.5
A_WIDTH = DIFF_HEADS * DIFF_V_DIM
Q_COLS = DIFF_HEADS * 2 * DIFF_QK_DIM
ROPE_THETA = 10000.0
ROPE_AXIS_DIM = DIFF_QK_DIM // 2
SUBLN_EPS = 1e-5
SGU_GROUPS = 8
SGU_CHUNK = 128
SGU_CH = 128
B_WIDTH = SGU_GROUPS * SGU_CH
LN_EPS = 1e-5
HYENA_ORDER = 2
SHORT_CONV = 3
FILTER_EMB = 33
FILTER_BANDS = (FILTER_EMB - 1) // 2
FILTER_HIDDEN = 64
DECAY_TARGET = 1e-2
MAX_DECAY = math.log(DECAY_TARGET) / 0.3
MIN_DECAY = math.log(DECAY_TARGET) / 1.5

LANES = 128
ONES_ROWS = 16
DFT_MINOR = 256
VMEM_LIMIT = 48 * 1024 * 1024
VMEM_LIMIT_WIDE = 56 * 1024 * 1024

NT_DIMS = (((1,), (1,)), ((), ()))
TN_DIMS = (((0,), (0,)), ((), ()))


def _params(*sem, vmem=VMEM_LIMIT):
    return pltpu.CompilerParams(dimension_semantics=sem, vmem_limit_bytes=vmem)


def _tile(n, pref):
    return pref if n % pref == 0 else n


def _adaln_kernel(c_ref, w_ref, b_ref, o_ref):
    a = c_ref[...]
    a = a * jax.nn.sigmoid(a)
    o_ref[...] = jnp.dot(a, w_ref[...], preferred_element_type=F32, precision=HIGHEST) + b_ref[...]


def _adaln_all(cond, ada_w, ada_b):
    depth, d, n6 = ada_w.shape
    tn = _tile(n6, 1024)
    return pl.pallas_call(
        _adaln_kernel,
        grid=(depth, n6 // tn),
        in_specs=[pl.BlockSpec((8, d), lambda l, j: (0, 0)),
                  pl.BlockSpec((None, d, tn), lambda l, j: (l, 0, j)),
                  pl.BlockSpec((None, 1, tn), lambda l, j: (l, 0, j))],
        out_specs=pl.BlockSpec((None, 8, tn), lambda l, j: (l, 0, j)),
        out_shape=jax.ShapeDtypeStruct((depth, 8, n6), F32),
        compiler_params=_params("parallel", "parallel"),
        name="adaln",
    )(cond, ada_w, ada_b.reshape(depth, 1, n6))


def _norm_mod(x_ref, g_ref, sh_ref, sc_ref):
    x = x_ref[...]
    ms = jnp.mean(x * x, axis=-1, keepdims=True)
    y = x * lax.rsqrt(ms + NORM_EPS) * g_ref[...]
    return (y * (1.0 + sc_ref[...]) + sh_ref[...]).astype(BF16)


def _inproj_kernel(x_ref, g_ref, sh_ref, sc_ref, w_ref, cos_ref, sin_ref, o_ref, hs_ref, *,
                   n_q, n_qk, rope):
    j = pl.program_id(2)

    @pl.when(j == 0)
    def _():
        hs_ref[...] = _norm_mod(x_ref, g_ref, sh_ref, sc_ref)

    tm, tn = o_ref.shape
    rc = min(tm, 256)

    def by_row_chunks(epilogue):
        for r in range(tm // rc):
            rows = pl.ds(r * rc, rc)
            acc = jnp.dot(hs_ref[rows, :], w_ref[...], preferred_element_type=F32)
            o_ref[rows, :] = epilogue(acc, rows).astype(o_ref.dtype)

    def qk_epilogue(a, rows):
        if rope:
            lane = lax.broadcasted_iota(jnp.int32, a.shape, 1)
            first = (lane & 31) < 16
            partner = jnp.where(first, pltpu.roll(a, tn - 16, 1), pltpu.roll(a, 16, 1))
            reps = tn // LANES
            a = (a * jnp.tile(cos_ref[rows, :], (1, reps))
                 + partner * jnp.tile(sin_ref[rows, :], (1, reps)))
        return jnp.where(j < n_q, a * DIFF_SCALE, a)

    @pl.when(j < n_qk)
    def _():
        by_row_chunks(qk_epilogue)

    @pl.when(j >= n_qk)
    def _():
        by_row_chunks(lambda a, rows: jax.nn.gelu(a))


def _inproj(x, g, sh, sc, w, cos_t, sin_t, rope):
    b, s, d = x.shape
    n = w.shape[1]
    tm = _tile(s, 1024)
    tn = 512
    kern = functools.partial(_inproj_kernel, n_q=Q_COLS // tn, n_qk=2 * Q_COLS // tn, rope=rope)
    return pl.pallas_call(
        kern,
        grid=(b, s // tm, n // tn),
        in_specs=[pl.BlockSpec((None, tm, d), lambda bb, i, j: (bb, i, 0)),
                  pl.BlockSpec((1, d), lambda bb, i, j: (0, 0)),
                  pl.BlockSpec((None, 1, d), lambda bb, i, j: (bb, 0, 0)),
                  pl.BlockSpec((None, 1, d), lambda bb, i, j: (bb, 0, 0)),
                  pl.BlockSpec((d, tn), lambda bb, i, j: (0, j)),
                  pl.BlockSpec((tm, LANES), lambda bb, i, j: (i, 0)),
                  pl.BlockSpec((tm, LANES), lambda bb, i, j: (i, 0))],
        out_specs=pl.BlockSpec((None, tm, tn), lambda bb, i, j: (bb, i, j)),
        out_shape=jax.ShapeDtypeStruct((b, s, n), BF16),
        scratch_shapes=[pltpu.VMEM((tm, d), BF16)],
        compiler_params=_params("parallel", "parallel", "arbitrary"),
        name="inproj",
    )(x, g, sh, sc, w, cos_t, sin_t)


def _inproj_nt_kernel(x_ref, g_ref, sh_ref, sc_ref, wt_ref, *rest, tl, conv):
    if conv:
        cw_ref, halo_ref, o_ref, hs_ref = rest
    else:
        o_ref, hs_ref = rest
    j = pl.program_id(2)

    @pl.when(j == 0)
    def _():
        hs_ref[...] = _norm_mod(x_ref, g_ref, sh_ref, sc_ref)

    n_chunks = o_ref.shape[0]
    if not conv:
        acc = lax.dot_general(wt_ref[...], hs_ref[...], NT_DIMS, preferred_element_type=F32)
        for c in range(n_chunks):
            o_ref[c] = acc[:, c * tl:(c + 1) * tl].astype(o_ref.dtype)
        return

    acc = jnp.concatenate(
        [lax.dot_general(wt_ref[...], hs_ref[c * tl:(c + 1) * tl, :], NT_DIMS, preferred_element_type=F32)
         for c in range(n_chunks)], axis=1)
    tn, tm = acc.shape
    rc = 32
    lane = lax.broadcasted_iota(jnp.int32, (rc, LANES), 1)
    for r in range(tn // rc):
        rows = slice(r * rc, (r + 1) * rc)
        a = acc[rows]
        halo = halo_ref[rows, :]
        prev = pltpu.roll(a, 1, 1)
        prev = jnp.concatenate([jnp.where(lane == 0, halo[:, 0:1], prev[:, :LANES]), prev[:, LANES:]], axis=1)
        nxt = pltpu.roll(a, tm - 1, 1)
        nxt = jnp.concatenate([nxt[:, :tm - LANES],
                               jnp.where(lane == LANES - 1, halo[:, 1:2], nxt[:, tm - LANES:])], axis=1)

        def tap(k):
            return jnp.tile(cw_ref[rows, k * LANES:(k + 1) * LANES], (1, tm // LANES))

        out = tap(3) + prev * tap(0) + a * tap(1) + nxt * tap(2)
        for c in range(n_chunks):
            o_ref[c, rows, :] = out[:, c * tl:(c + 1) * tl].astype(o_ref.dtype)


def _inproj_nt(x, g, sh, sc, wt, tl, conv_params=None, halo=None):
    b, s, d = x.shape
    n = wt.shape[0]
    tm = _tile(s, 1024)
    tn = 512
    conv = conv_params is not None
    kern = functools.partial(_inproj_nt_kernel, tl=tl, conv=conv)
    in_specs = [pl.BlockSpec((None, tm, d), lambda bb, i, j: (bb, i, 0)),
                pl.BlockSpec((1, d), lambda bb, i, j: (0, 0)),
                pl.BlockSpec((None, 1, d), lambda bb, i, j: (bb, 0, 0)),
                pl.BlockSpec((None, 1, d), lambda bb, i, j: (bb, 0, 0)),
                pl.BlockSpec((tn, d), lambda bb, i, j: (j, 0))]
    args = [x, g, sh, sc, wt]
    if conv:
        in_specs += [pl.BlockSpec((tn, 4 * LANES), lambda bb, i, j: (j, 0)),
                     pl.BlockSpec((None, None, tn, 2), lambda bb, i, j: (bb, i, j, 0))]
        args += [conv_params, halo]
    return pl.pallas_call(
        kern,
        grid=(b, s // tm, n // tn),
        in_specs=in_specs,
        out_specs=pl.BlockSpec((None, tm // tl, tn, tl), lambda bb, i, j: (bb, i, j, 0)),
        out_shape=jax.ShapeDtypeStruct((b, s // tl, n, tl), BF16),
        scratch_shapes=[pltpu.VMEM((tm, d), BF16)],
        compiler_params=_params("parallel", "parallel", "arbitrary"),
        name="inproj_nt",
    )(*args)


def _ffn_up_kernel(x_ref, g_ref, sh_ref, sc_ref, wg_ref, wu_ref, o_ref, hs_ref):
    j = pl.program_id(2)

    @pl.when(j == 0)
    def _():
        hs_ref[...] = _norm_mod(x_ref, g_ref, sh_ref, sc_ref)

    hs = hs_ref[...]
    gate = jnp.dot(hs, wg_ref[...], preferred_element_type=F32)
    up = jnp.dot(hs, wu_ref[...], preferred_element_type=F32)
    o_ref[...] = (gate * jax.nn.sigmoid(gate) * up).astype(o_ref.dtype)


def _ffn_up(x, g, sh, sc, wg, wu):
    b, s, d = x.shape
    n = wg.shape[1]
    tm = _tile(s, 1024)
    tn = 512
    return pl.pallas_call(
        _ffn_up_kernel,
        grid=(b, s // tm, n // tn),
        in_specs=[pl.BlockSpec((None, tm, d), lambda bb, i, j: (bb, i, 0)),
                  pl.BlockSpec((1, d), lambda bb, i, j: (0, 0)),
                  pl.BlockSpec((None, 1, d), lambda bb, i, j: (bb, 0, 0)),
                  pl.BlockSpec((None, 1, d), lambda bb, i, j: (bb, 0, 0)),
                  pl.BlockSpec((d, tn), lambda bb, i, j: (0, j)),
                  pl.BlockSpec((d, tn), lambda bb, i, j: (0, j))],
        out_specs=pl.BlockSpec((None, tm, tn), lambda bb, i, j: (bb, i, j)),
        out_shape=jax.ShapeDtypeStruct((b, s, n), BF16),
        scratch_shapes=[pltpu.VMEM((tm, d), BF16)],
        compiler_params=_params("parallel", "parallel", "arbitrary"),
        name="ffn_up",
    )(x, g, sh, sc, wg, wu)


def _proj_res_kernel(*refs, ksizes, transposed):
    n = len(ksizes)
    a_refs = refs[:n]
    w_ref, x_ref, gate_ref, o_ref = refs[n:]
    acc = None
    off = 0
    for a_ref, ks in zip(a_refs, ksizes):
        w = w_ref[off:off + ks, :]
        if transposed:
            part = lax.dot_general(a_ref[...], w, TN_DIMS, preferred_element_type=F32)
        else:
            part = jnp.dot(a_ref[...], w, preferred_element_type=F32)
        acc = part if acc is None else acc + part
        off += ks
    o_ref[...] = x_ref[...] + gate_ref[...] * acc


def _proj_res(a_list, w, x, gate, transposed=False, tm_pref=512, tn_pref=512, vmem=VMEM_LIMIT):
    b, s, d = x.shape
    ksizes = tuple(a.shape[1] if transposed else a.shape[2] for a in a_list)
    ktot = sum(ksizes)
    tm = _tile(s, tm_pref)
    tn = _tile(d, tn_pref)
    if transposed:
        a_specs = [pl.BlockSpec((None, ks, tm), lambda bb, i, j: (bb, 0, i)) for ks in ksizes]
    else:
        a_specs = [pl.BlockSpec((None, tm, ks), lambda bb, i, j: (bb, i, 0)) for ks in ksizes]
    kern = functools.partial(_proj_res_kernel, ksizes=ksizes, transposed=transposed)
    return pl.pallas_call(
        kern,
        grid=(b, s // tm, d // tn),
        in_specs=a_specs + [pl.BlockSpec((ktot, tn), lambda bb, i, j: (0, j)),
                            pl.BlockSpec((None, tm, tn), lambda bb, i, j: (bb, i, j)),
                            pl.BlockSpec((None, 1, tn), lambda bb, i, j: (bb, 0, j))],
        out_specs=pl.BlockSpec((None, tm, tn), lambda bb, i, j: (bb, i, j)),
        out_shape=jax.ShapeDtypeStruct((b, s, d), F32),
        compiler_params=_params("parallel", "parallel", "parallel", vmem=vmem),
        name="proj_res",
    )(*a_list, w, x, gate)


def _attn_kernel(*refs, n_chunks, lam_init):
    if n_chunks:
        lam_ref, q_ref, kc_ref, vct_ref, k_ref, vt_ref, g_ref, o_ref, acc_ref = refs[:9]
        s_refs = refs[9:]
    else:
        lam_ref, q_ref, kc_ref, vct_ref, g_ref, o_ref, acc_ref = refs
    q = q_ref[...]
    tq = q.shape[0]
    dv = DIFF_V_DIM
    qm = (q[:, :DIFF_QK_DIM], q[:, DIFF_QK_DIM:])

    def scores(kblk):
        return tuple(lax.dot_general(kblk[:, m * DIFF_QK_DIM:(m + 1) * DIFF_QK_DIM], qm[m], NT_DIMS,
                                     preferred_element_type=F32) for m in range(2))

    def absorb(s_pair, vtblk, m_pair):
        vext = jnp.concatenate([vtblk, jnp.ones((ONES_ROWS, vtblk.shape[1]), BF16)], axis=0)
        out = []
        for m in range(2):
            m_old = m_pair[m]
            m_new = jnp.maximum(m_old, jnp.max(s_pair[m], axis=0, keepdims=True))
            alpha = jnp.exp(m_old - m_new)
            p = jnp.exp((s_pair[m] - m_new).astype(BF16))
            pv = jnp.dot(vext, p, preferred_element_type=F32)
            acc_ref[m] = alpha * acc_ref[m] + pv
            out.append(m_new)
        return tuple(out)

    def store(ref, s_pair):
        ref[0] = s_pair[0]
        ref[1] = s_pair[1]

    acc_ref[...] = jnp.zeros_like(acc_ref)
    init = jnp.full((1, tq), -1e30, F32)
    m_pair = absorb(scores(kc_ref[...]), vct_ref[0], (init, init))
    if n_chunks:
        group = len(s_refs) // 2
        sets = (s_refs[:group], s_refs[group:])
        assert n_chunks % (2 * group) == 0

        def half_trip(cur, nxt, base, mp, lookahead):
            for k in range(group):
                if lookahead:
                    store(nxt[k], scores(k_ref[base + group + k]))
                mp = absorb((cur[k][0], cur[k][1]), vt_ref[base + k], mp)
            return mp

        def trip(j, mp, lookahead):
            base = 2 * group * j
            mp = half_trip(sets[0], sets[1], base, mp, True)
            return half_trip(sets[1], sets[0], base + group, mp, lookahead)

        for k in range(group):
            store(sets[0][k], scores(k_ref[k]))
        n_trips = n_chunks // (2 * group)
        m_pair = lax.fori_loop(0, n_trips - 1, lambda j, mp: trip(j, mp, True), m_pair)
        m_pair = trip(n_trips - 1, m_pair, False)

    lp = lam_ref[...]
    lam = (jnp.exp(jnp.sum(lp[0:1] * lp[1:2], axis=-1, keepdims=True))
           - jnp.exp(jnp.sum(lp[2:3] * lp[3:4], axis=-1, keepdims=True)) + lam_init)
    acc0, acc1 = acc_ref[0], acc_ref[1]
    o = acc0[:dv] / acc0[dv:dv + 1] - lam * (acc1[:dv] / acc1[dv:dv + 1])
    ot = o.T
    ms = jnp.mean(ot * ot, axis=-1, keepdims=True)
    on = ot * lax.rsqrt(ms + SUBLN_EPS) * g_ref[...] * (1.0 - lam_init)
    o_ref[...] = on.astype(o_ref.dtype)


def _attention(lam_p, subln, q_arr, kc_arr, vct_arr, k4=None, vt4=None, *, lam_init):
    b, sq = q_arr.shape[0], q_arr.shape[1]
    sc = kc_arr.shape[1]
    h = DIFF_HEADS
    tq = _tile(sq, 256)
    dv = DIFF_V_DIM
    n_chunks = 0 if k4 is None else k4.shape[1]
    in_specs = [pl.BlockSpec((4, DIFF_QK_DIM), lambda bb, hh, i: (0, 0)),
                pl.BlockSpec((None, tq, dv), lambda bb, hh, i: (bb, i, hh)),
                pl.BlockSpec((None, sc, dv), lambda bb, hh, i: (bb, 0, h + hh)),
                pl.BlockSpec((None, 1, dv, sc), lambda bb, hh, i: (bb, 0, hh, 0))]
    args = [lam_p, q_arr, kc_arr, vct_arr]
    if n_chunks:
        tk = k4.shape[2]
        in_specs += [pl.BlockSpec((None, n_chunks, tk, dv), lambda bb, hh, i: (bb, 0, 0, h + hh)),
                     pl.BlockSpec((None, n_chunks, dv, tk), lambda bb, hh, i: (bb, 0, hh, 0))]
        args += [k4, vt4]
    in_specs.append(pl.BlockSpec((1, dv), lambda bb, hh, i: (0, 0)))
    args.append(subln)
    kern = functools.partial(_attn_kernel, n_chunks=n_chunks, lam_init=lam_init)
    scratch = [pltpu.VMEM((2, dv + ONES_ROWS, tq), F32)]
    if n_chunks:
        group = 4 if n_chunks % 16 == 0 else 2
        scratch += [pltpu.VMEM((2, tk, tq), F32) for _ in range(2 * group)]
    return pl.pallas_call(
        kern,
        grid=(b, h, sq // tq),
        in_specs=in_specs,
        out_specs=pl.BlockSpec((None, tq, dv), lambda bb, hh, i: (bb, i, hh)),
        out_shape=jax.ShapeDtypeStruct((b, sq, A_WIDTH), BF16),
        scratch_shapes=scratch,
        compiler_params=_params("parallel", "parallel", "parallel"),
        name="diff_attn",
    )(*args)


def _sgu_kernel(u_ref, g_ref, ng_ref, nb_ref, w_ref, bs_ref, o_ref):
    w = w_ref[...]
    for c in range(u_ref.shape[0] // SGU_CHUNK):
        sl = slice(c * SGU_CHUNK, (c + 1) * SGU_CHUNK)
        gg = g_ref[sl, :].astype(F32)
        mu = jnp.mean(gg, axis=-1, keepdims=True)
        dev = gg - mu
        var = jnp.mean(dev * dev, axis=-1, keepdims=True)
        vv = dev * lax.rsqrt(var + LN_EPS) * ng_ref[...] + nb_ref[...]
        mixed = jnp.dot(w, vv.astype(BF16), preferred_element_type=F32) + bs_ref[...]
        o_ref[sl, :] = (u_ref[sl, :].astype(F32) * mixed).astype(o_ref.dtype)


def _sgu(qkug, norm_g, norm_b, w_s, b_s):
    b, s = qkug.shape[0], qkug.shape[1]
    tm = _tile(s, 1024)
    gcount = SGU_GROUPS
    ublk = 2 * Q_COLS // SGU_CH
    return pl.pallas_call(
        _sgu_kernel,
        grid=(b, s // tm, gcount),
        in_specs=[pl.BlockSpec((None, tm, SGU_CH), lambda bb, i, gi: (bb, i, ublk + gi)),
                  pl.BlockSpec((None, tm, SGU_CH), lambda bb, i, gi: (bb, i, ublk + gcount + gi)),
                  pl.BlockSpec((None, 1, SGU_CH), lambda bb, i, gi: (gi, 0, 0)),
                  pl.BlockSpec((None, 1, SGU_CH), lambda bb, i, gi: (gi, 0, 0)),
                  pl.BlockSpec((None, SGU_CHUNK, SGU_CHUNK), lambda bb, i, gi: (gi, 0, 0)),
                  pl.BlockSpec((None, SGU_CHUNK, 1), lambda bb, i, gi: (gi, 0, 0))],
        out_specs=pl.BlockSpec((None, tm, SGU_CH), lambda bb, i, gi: (bb, i, gi)),
        out_shape=jax.ShapeDtypeStruct((b, s, B_WIDTH), BF16),
        compiler_params=_params("parallel", "parallel", "parallel"),
        name="sgu",
    )(qkug, qkug, norm_g.reshape(gcount, 1, SGU_CH), norm_b.reshape(gcount, 1, SGU_CH),
      w_s.astype(BF16), b_s.reshape(gcount, SGU_CHUNK, 1))


def _filter_mlp_kernel(f_ref, w1_ref, b1_ref, w2_ref, b2_ref, w3_ref, b3_ref, fr_ref, o_ref):
    def lin(a, w_ref, b_ref):
        return jnp.dot(a, w_ref[...], preferred_element_type=F32, precision=HIGHEST) + b_ref[...]
    fr = fr_ref[...]
    hcur = jnp.sin(fr[0:1] * lin(f_ref[...], w1_ref, b1_ref))
    hcur = jnp.sin(fr[1:2] * lin(hcur, w2_ref, b2_ref))
    o_ref[...] = jnp.sin(fr[2:3] * lin(hcur, w3_ref, b3_ref))


def _filter_mlp(feats, w1, b1, w2, b2, w3, b3, freq):
    rows, emb = feats.shape
    hid = FILTER_HIDDEN
    tr = _tile(rows, 2048)
    full = lambda shape: pl.BlockSpec(shape, lambda i: (0,) * len(shape))
    return pl.pallas_call(
        _filter_mlp_kernel,
        grid=(rows // tr,),
        in_specs=[pl.BlockSpec((tr, emb), lambda i: (i, 0)),
                  full((emb, hid)), full((1, hid)), full((hid, hid)), full((1, hid)),
                  full((hid, hid)), full((1, hid)), full((3, hid))],
        out_specs=pl.BlockSpec((tr, hid), lambda i: (i, 0)),
        out_shape=jax.ShapeDtypeStruct((rows, hid), F32),
        compiler_params=_params("parallel"),
        name="filter_mlp",
    )(feats, w1, b1.reshape(1, hid), w2, b2.reshape(1, hid), w3, b3.reshape(1, hid), freq)


def _filter_raw_kernel(w4t_ref, h_ref, t_ref, delta_ref, o_ref, *, zero_tile):
    rt = pl.program_id(2)

    def split(a):
        hi = a.astype(BF16)
        return hi, (a - hi.astype(F32)).astype(BF16)

    def nt(a, bm):
        return lax.dot_general(a, bm, NT_DIMS, preferred_element_type=F32)

    w_hi, w_lo = split(w4t_ref[...])
    h_hi, h_lo = split(h_ref[...])
    raw = nt(w_hi, h_hi) + (nt(w_hi, h_lo) + nt(w_lo, h_hi))
    raw = raw * jnp.exp(-(delta_ref[...] * t_ref[...]))
    for c in range(o_ref.shape[0]):
        o_ref[c] = raw[:, c * DFT_MINOR:(c + 1) * DFT_MINOR].astype(o_ref.dtype)

    @pl.when(rt == zero_tile)
    def _():
        col = lax.broadcasted_iota(jnp.int32, (raw.shape[0], DFT_MINOR), 1)
        o_ref[0] = jnp.where(col == 0, 0.0, raw[:, :DFT_MINOR]).astype(o_ref.dtype)


def _filter_raw(w4t, h3, t_row, deltas, seq):
    c = w4t.shape[2]
    rows = h3.shape[0]
    tr = _tile(rows // 2, 2048)
    tc = _tile(c, 512)
    half_tiles = seq // tr
    kern = functools.partial(_filter_raw_kernel, zero_tile=half_tiles)
    return pl.pallas_call(
        kern,
        grid=(HYENA_ORDER, c // tc, rows // tr),
        in_specs=[pl.BlockSpec((None, None, tc, FILTER_HIDDEN),
                               lambda n, ci, rt: (n, rt // half_tiles, ci, 0)),
                  pl.BlockSpec((tr, FILTER_HIDDEN), lambda n, ci, rt: (rt, 0)),
                  pl.BlockSpec((1, tr), lambda n, ci, rt: (0, rt)),
                  pl.BlockSpec((tc, 1), lambda n, ci, rt: (ci, 0))],
        out_specs=pl.BlockSpec((None, tr // DFT_MINOR, tc, DFT_MINOR), lambda n, ci, rt: (n, rt, ci, 0)),
        out_shape=jax.ShapeDtypeStruct((HYENA_ORDER, rows // DFT_MINOR, c, DFT_MINOR), BF16),
        compiler_params=_params("parallel", "parallel", "parallel"),
        name="filter_raw",
    )(w4t, h3, t_row, deltas)


def _cmul(ar, ai, br, bi):
    return ar * br - ai * bi, ar * bi + ai * br


def _store_complex(ref, c, n1, re, im):
    r0 = pl.multiple_of(c * n1, n1)
    ref[pl.ds(r0, n1), :DFT_MINOR] = re.astype(ref.dtype)
    ref[pl.ds(r0, n1), DFT_MINOR:] = im.astype(ref.dtype)


def _load_complex(ref, c, n1):
    r0 = pl.multiple_of(c * n1, n1)
    tile = ref[pl.ds(r0, n1), :]
    return tile[:, :DFT_MINOR], tile[:, DFT_MINOR:]


def _filter_spec_kernel(k_ref, f1_ref, twr_ref, twi_ref, f2_ref, o_ref, a2_ref, *, inv_n):
    tc, n1, n2 = k_ref.shape

    def left(c, carry):
        k = k_ref[c].astype(F32)
        nrm = jnp.sum(jnp.sum(jnp.abs(k), axis=1, keepdims=True), axis=0, keepdims=True)
        kn = (k * (inv_n / nrm)).astype(BF16)
        a = jnp.dot(f1_ref[...], kn, preferred_element_type=F32)
        ar, ai = _cmul(a[:n1], a[n1:], twr_ref[...], twi_ref[...])
        _store_complex(a2_ref, c, n1, ar, ai)
        return carry

    lax.fori_loop(0, tc, left, 0, unroll=4)
    z = jnp.dot(a2_ref[...], f2_ref[...], preferred_element_type=F32).reshape(tc, n1, 2 * n2)
    o_ref[:, 0] = z[:, :, :n2].astype(o_ref.dtype)
    o_ref[:, 1] = z[:, :, n2:].astype(o_ref.dtype)


def _filter_spec(k2, f1_full, twr, twi, f2e):
    nc, n1, n2 = k2.shape
    tc = 16
    kern = functools.partial(_filter_spec_kernel, inv_n=1.0 / (n1 * n2))
    full = lambda shape: pl.BlockSpec(shape, lambda i: (0,) * len(shape))
    return pl.pallas_call(
        kern,
        grid=(nc // tc,),
        in_specs=[pl.BlockSpec((tc, n1, n2), lambda i: (i, 0, 0)),
                  full((2 * n1, n1)), full((n1, n2)), full((n1, n2)), full((2 * n2, 2 * n2))],
        out_specs=pl.BlockSpec((tc, 2, n1, n2), lambda i: (i, 0, 0, 0)),
        out_shape=jax.ShapeDtypeStruct((nc, 2, n1, n2), BF16),
        scratch_shapes=[pltpu.VMEM((tc * n1, 2 * n2), BF16)],
        compiler_params=_params("parallel"),
        name="filter_spec",
    )(k2, f1_full, twr, twi, f2e)


def _hyena_kernel(zv_ref, z1_ref, z2_ref, kf_ref, fb_ref, f1_ref, twr_ref, twi_ref, f2_ref, f2c_ref,
                  g1_ref, o_ref, u_ref, a2_ref, z_ref):
    nb, tc, hr, n2 = zv_ref.shape
    n1 = 2 * hr
    gate_refs = (z1_ref, z2_ref)

    for n in range(HYENA_ORDER):
        last = n == HYENA_ORDER - 1
        sig_ref = zv_ref if n == 0 else u_ref

        def fwd_left(c, carry):
            xs = jnp.concatenate([sig_ref[0, c], sig_ref[1, c]], axis=0).astype(BF16)
            a = jnp.dot(f1_ref[...], xs, preferred_element_type=F32)
            ar, ai = _cmul(a[:n1], a[n1:], twr_ref[...], twi_ref[...])
            _store_complex(a2_ref, c, n1, ar, ai)
            return carry

        lax.fori_loop(0, tc, fwd_left, 0, unroll=4)
        z_ref[...] = jnp.dot(a2_ref[...], f2_ref[...], preferred_element_type=F32)

        def spectrum(c, carry):
            zr, zi = _load_complex(z_ref, c, n1)
            wr, wi = _cmul(zr, zi, kf_ref[n, c, 0].astype(F32), kf_ref[n, c, 1].astype(F32))
            _store_complex(a2_ref, c, n1, wr, wi)
            return carry

        lax.fori_loop(0, tc, spectrum, 0, unroll=4)
        z_ref[...] = jnp.dot(a2_ref[...], f2c_ref[...], preferred_element_type=F32)

        def inv_left(c, carry):
            br, bi = _load_complex(z_ref, c, n1)
            br, bi = _cmul(br, bi, twr_ref[...], -twi_ref[...])
            bs = jnp.concatenate([br, bi], axis=0).astype(BF16)
            y = jnp.dot(g1_ref[...], bs, preferred_element_type=F32)
            fb = fb_ref[pl.ds(c, 1), n:n + 1]
            for bb in range(nb):
                new = gate_refs[n][bb, c].astype(F32) * (y[bb * hr:(bb + 1) * hr]
                                                         + sig_ref[bb, c].astype(F32) * fb)
                if last:
                    o_ref[bb, c] = new.astype(o_ref.dtype)
                else:
                    u_ref[bb, c] = new
            return carry

        lax.fori_loop(0, tc, inv_left, 0, unroll=4)


def _hyena_core(z4, kf, fbias, f1h, twr, twi, f2e, f2c, g1e):
    nb, c3, hr, n2 = z4.shape
    c = c3 // 3
    n1 = 2 * hr
    tc = 16
    nct = c // tc
    full = lambda shape: pl.BlockSpec(shape, lambda i: (0,) * len(shape))
    zspec = lambda part: pl.BlockSpec((nb, tc, hr, n2), lambda i: (0, i + part * nct, 0, 0))
    return pl.pallas_call(
        _hyena_kernel,
        grid=(nct,),
        in_specs=[zspec(0), zspec(1), zspec(2),
                  pl.BlockSpec((HYENA_ORDER, tc, 2, n1, n2), lambda i: (0, i, 0, 0, 0)),
                  pl.BlockSpec((tc, HYENA_ORDER), lambda i: (i, 0)),
                  full((2 * n1, n1)), full((n1, n2)), full((n1, n2)),
                  full((2 * n2, 2 * n2)), full((2 * n2, 2 * n2)), full((n1, 2 * n1))],
        out_specs=pl.BlockSpec((nb, tc, hr, n2), lambda i: (0, i, 0, 0)),
        out_shape=jax.ShapeDtypeStruct((nb, c, hr, n2), BF16),
        scratch_shapes=[pltpu.VMEM((nb, tc, hr, n2), F32),
                        pltpu.VMEM((tc * n1, 2 * n2), BF16),
                        pltpu.VMEM((tc * n1, 2 * n2), F32)],
        compiler_params=_params("parallel"),
        name="hyena_core",
    )(z4, z4, z4, kf, fbias, f1h, twr, twi, f2e, f2c, g1e)


def _dft_tables(n1):
    n2 = DFT_MINOR
    n = n1 * n2
    a1 = 2.0 * np.pi * np.outer(np.arange(n1), np.arange(n1)) / n1
    f1r, f1i = np.cos(a1), -np.sin(a1)
    a2 = 2.0 * np.pi * np.outer(np.arange(n2), np.arange(n2)) / n2
    f2r, f2i = np.cos(a2), -np.sin(a2)
    at = 2.0 * np.pi * np.outer(np.arange(n1), np.arange(n2)) / n
    twr, twi = np.cos(at), -np.sin(at)
    hr = n1 // 2
    f1_full = np.concatenate([f1r, f1i], axis=0)
    f1_half = np.block([[f1r[:, :hr], -f1i[:, :hr]], [f1i[:, :hr], f1r[:, :hr]]])
    f2e = np.block([[f2r, f2i], [-f2i, f2r]])
    f2c = np.block([[f2r, -f2i], [f2i, f2r]])
    gr, gi = f1r[:hr, :], -f1i[:hr, :]
    g1e = np.block([[gr, -gi], [gi, gr]])
    bf = lambda m: jnp.asarray(m, F32).astype(BF16)
    return dict(f1_full=bf(f1_full), f1_half=bf(f1_half), f2e=bf(f2e), f2c=bf(f2c), g1e=bf(g1e),
                twr=jnp.asarray(twr, F32), twi=jnp.asarray(twi, F32))


def _hyena_spectra(seq, fw1, fb1, fw2, fb2, fw3, fb3, ffreq, fw4, tabs):
    l = seq
    width = fw4.shape[1] // (2 * HYENA_ORDER)
    t_fwd = jnp.linspace(0.0, 1.0, l, dtype=F32)
    w = 2.0 * math.pi * jnp.arange(l, dtype=F32)[:, None] / l
    f = jnp.linspace(1e-4, FILTER_BANDS - 1, FILTER_BANDS, dtype=F32)[None, :]
    feats = jnp.concatenate([t_fwd[:, None], jnp.cos(f * w), -jnp.sin(f * w)], axis=-1)

    def two_sided(a):
        return jnp.concatenate([a, a[:1], a[1:][::-1]], axis=0)

    feats = two_sided(feats)
    t_lin = two_sided(t_fwd)
    emb_pad = FILTER_HIDDEN
    feats = jnp.pad(feats, ((0, 0), (0, emb_pad - FILTER_EMB)))
    w1p = jnp.pad(fw1, ((0, emb_pad - FILTER_EMB), (0, 0)))
    h3 = _filter_mlp(feats, w1p, fb1, fw2, fb2, fw3, fb3, ffreq)
    w4t = fw4.reshape(FILTER_HIDDEN, HYENA_ORDER, 2, width).transpose(1, 2, 3, 0)
    deltas = jnp.abs(jnp.linspace(MIN_DECAY, MAX_DECAY, width, dtype=F32))[:, None]
    raw = _filter_raw(w4t, h3, t_lin[None, :], deltas, l)
    n1 = raw.shape[1]
    k2 = raw.transpose(0, 2, 1, 3).reshape(HYENA_ORDER * width, n1, DFT_MINOR)
    kf = _filter_spec(k2, tabs["f1_full"], tabs["twr"], tabs["twi"], tabs["f2e"])
    return kf.reshape(HYENA_ORDER, width, 2, n1, DFT_MINOR)


def _rms_kernel(x_ref, g_ref, o_ref):
    x = x_ref[...]
    ms = jnp.mean(x * x, axis=-1, keepdims=True)
    o_ref[...] = x * lax.rsqrt(ms + NORM_EPS) * g_ref[...]


def _final_norm(x, g):
    b, s, d = x.shape
    tm = _tile(s, 1024)
    return pl.pallas_call(
        _rms_kernel,
        grid=(b, s // tm),
        in_specs=[pl.BlockSpec((None, tm, d), lambda bb, i: (bb, i, 0)),
                  pl.BlockSpec((1, d), lambda bb, i: (0, 0))],
        out_specs=pl.BlockSpec((None, tm, d), lambda bb, i: (bb, i, 0)),
        out_shape=jax.ShapeDtypeStruct((b, s, d), F32),
        compiler_params=_params("parallel", "parallel"),
        name="final_norm",
    )(x, g.reshape(1, d))


def _rope_tables(n_tokens):
    tok = jnp.arange(n_tokens)
    row = (tok // GRID_W).astype(F32)
    col = (tok % GRID_W).astype(F32)
    half = ROPE_AXIS_DIM // 2
    inv = 1.0 / (ROPE_THETA ** (jnp.arange(0, ROPE_AXIS_DIM, 2, dtype=F32) / ROPE_AXIS_DIM))
    ang_r = row[:, None] * inv
    ang_c = col[:, None] * inv
    cos64 = jnp.concatenate([jnp.cos(ang_r), jnp.cos(ang_r), jnp.cos(ang_c), jnp.cos(ang_c)], axis=-1)
    sin64 = jnp.concatenate([-jnp.sin(ang_r), jnp.sin(ang_r), -jnp.sin(ang_c), jnp.sin(ang_c)], axis=-1)
    assert cos64.shape[1] == 4 * half == DIFF_QK_DIM
    return jnp.tile(cos64, (1, 2)), jnp.tile(sin64, (1, 2))


def _ffn(x, norm_g, sh, sc, gate, wg, wu, wd):
    hidden = _ffn_up(x, norm_g, sh, sc, wg, wu)
    return _proj_res([hidden], wd, x, gate, tm_pref=1024, vmem=VMEM_LIMIT_WIDE)


def _even_layer(x, xc, mods, cmods, n1g, n2g, w_in, w_out, lam_p, subln, sgu_ng, sgu_nb, sgu_w, sgu_b,
                wg, wu, wd, layer_idx):
    b, s, d = x.shape
    lam_init = 0.8 - 0.6 * math.exp(-0.3 * layer_idx)
    sh1, sc1, g1, sh2, sc2, g2 = mods
    csh1, csc1, cg1, csh2, csc2, cg2 = cmods
    o_k, o_v, o_u = Q_COLS, 2 * Q_COLS, 2 * Q_COLS + A_WIDTH
    w_qkug = jnp.concatenate([w_in[:, :o_v], w_in[:, o_u:]], axis=1).astype(BF16)
    w_vt = w_in[:, o_v:o_u].T.astype(BF16)
    wo = w_out.astype(BF16)
    n1g2 = n1g.reshape(1, d)
    cos_t, sin_t = _rope_tables(s)
    sc_len = xc.shape[1]

    qkug = _inproj(x, n1g2, sh1, sc1, w_qkug, cos_t, sin_t, rope=True)
    cqkug = _inproj(xc, n1g2, csh1, csc1, w_qkug, cos_t[:sc_len], sin_t[:sc_len], rope=False)
    tk = _tile(s, 512)
    vt4 = _inproj_nt(x, n1g2, sh1, sc1, w_vt, tk)
    cvt4 = _inproj_nt(xc, n1g2, csh1, csc1, w_vt, sc_len)
    k4 = qkug.reshape(b, s // tk, tk, qkug.shape[2])
    subln2 = subln.reshape(1, DIFF_V_DIM)

    a_l = _attention(lam_p, subln2, qkug, cqkug, cvt4, k4, vt4, lam_init=lam_init)
    s_l = _sgu(qkug, sgu_ng, sgu_nb, sgu_w, sgu_b)
    x = _proj_res([a_l, s_l], wo, x, g1, tn_pref=d)
    x = _ffn(x, n2g.reshape(1, d), sh2, sc2, g2, wg, wu, wd)

    a_c = _attention(lam_p, subln2, cqkug, cqkug, cvt4, lam_init=lam_init)
    s_c = _sgu(cqkug, sgu_ng, sgu_nb, sgu_w, sgu_b)
    xc = _proj_res([a_c, s_c], wo, xc, cg1, tn_pref=d)
    xc = _ffn(xc, n2g.reshape(1, d), csh2, csc2, cg2, wg, wu, wd)
    return x, xc


def _odd_layer(x, mods, n1g, n2g, w_in, conv_w, conv_b, fw1, fb1, fw2, fb2, fw3, fb3, ffreq, fw4, fbias,
               w_out, wg, wu, wd):
    b, s, d = x.shape
    sh1, sc1, g1, sh2, sc2, g2 = mods
    width = w_out.shape[0]
    n1 = 2 * s // DFT_MINOR
    tabs = _dft_tables(n1)
    kf = _hyena_spectra(s, fw1, fb1, fw2, fb2, fw3, fb3, ffreq, fw4, tabs)

    n1g2 = n1g.reshape(1, d)
    wt = w_in.T.astype(BF16)
    tm = _tile(s, 1024)
    n_tiles = s // tm
    first = jnp.arange(n_tiles) * tm
    edge_tokens = jnp.stack([jnp.maximum(first - 1, 0), jnp.minimum(first + tm, s - 1)], axis=1).reshape(-1)
    z_edge = _inproj_nt(x[:, edge_tokens, :], n1g2, sh1, sc1, wt, 2 * n_tiles)
    inside = jnp.ones((n_tiles, 2), F32).at[0, 0].set(0.0).at[n_tiles - 1, 1].set(0.0)
    halo = (z_edge[:, 0].astype(F32).reshape(b, 3 * width, n_tiles, 2).transpose(0, 2, 1, 3)
            * inside[None, :, None, :])
    conv_params = jnp.repeat(jnp.concatenate([conv_w.T, conv_b[:, None]], axis=1), LANES, axis=1)

    z4 = _inproj_nt(x, n1g2, sh1, sc1, wt, DFT_MINOR, conv_params, halo)
    z4 = z4.transpose(0, 2, 1, 3)
    y4 = _hyena_core(z4, kf, fbias.T, tabs["f1_half"], tabs["twr"], tabs["twi"], tabs["f2e"], tabs["f2c"],
                     tabs["g1e"])
    yt = y4.reshape(b, width, s)
    x = _proj_res([yt], w_out.astype(BF16), x, g1, transposed=True, tn_pref=d)
    return _ffn(x, n2g.reshape(1, d), sh2, sc2, g2, wg, wu, wd)


def kernel(x, c, ctx, c_ctx, ada_w, ada_b, norm1, norm2, ffn_w_gate, ffn_w_up, ffn_w_down, e_w_in, e_w_out, e_lambda, e_subln, e_sgu_norm_g, e_sgu_norm_b, e_sgu_w, e_sgu_b, o_w_in, o_conv_w, o_conv_b, o_filt_w1, o_filt_b1, o_filt_w2, o_filt_b2, o_filt_w3, o_filt_b3, o_filt_freq, o_filt_w4, o_filt_bias, o_w_out, final_norm):
    b, s, d = x.shape
    depth = ada_w.shape[0]
    assert b == 2, "the long convolution packs exactly two batches into one complex signal"
    assert depth == 2, "odd layers here never carry the context stream"
    cond = jnp.zeros((8, d), F32).at[:b].set(c).at[b].set(c_ctx)
    mod_all = _adaln_all(cond, ada_w, ada_b)
    xc = ctx
    for i in range(depth):
        j = i // 2
        parts = jnp.split(mod_all[i], 6, axis=-1)
        mods = [p[:b, None, :] for p in parts]
        cmods = [jnp.broadcast_to(p[b:b + 1, None, :], (b, 1, d)) for p in parts]
        wg, wu, wd = (ffn_w_gate[i].astype(BF16), ffn_w_up[i].astype(BF16), ffn_w_down[i].astype(BF16))
        if i % 2 == 0:
            x, xc = _even_layer(x, xc, mods, cmods, norm1[i], norm2[i], e_w_in[j], e_w_out[j], e_lambda[j],
                                e_subln[j], e_sgu_norm_g[j], e_sgu_norm_b[j], e_sgu_w[j], e_sgu_b[j],
                                wg, wu, wd, i)
        else:
            x = _odd_layer(x, mods, norm1[i], norm2[i], o_w_in[j], o_conv_w[j], o_conv_b[j], o_filt_w1[j],
                           o_filt_b1[j], o_filt_w2[j], o_filt_b2[j], o_filt_w3[j], o_filt_b3[j],
                           o_filt_freq[j], o_filt_w4[j], o_filt_bias[j], o_w_out[j], wg, wu, wd)
    return _final_norm(x, final_norm)
```

```python
import functools
import math

import numpy as np
import jax
import jax.numpy as jnp
from jax import lax
from jax.experimental import pallas as pl
from jax.experimental.pallas import tpu as pltpu

F32 = jnp.float32
BF16 = jnp.bfloat16
HIGHEST = lax.Precision.HIGHEST

GRID_W = 64
NORM_EPS = 1e-6
DIFF_HEADS = 8
DIFF_QK_DIM = 64
DIFF_V_DIM = 2 * DIFF_QK_DIM
DIFF_SCALE = DIFF_QK_DIM ** -0.5
A_WIDTH = DIFF_HEADS * DIFF_V_DIM
Q_COLS = DIFF_HEADS * 2 * DIFF_QK_DIM
ROPE_THETA = 10000.0
ROPE_AXIS_DIM = DIFF_QK_DIM // 2
SUBLN_EPS = 1e-5
SGU_GROUPS = 8
SGU_CHUNK = 128
SGU_CH = 128
B_WIDTH = SGU_GROUPS * SGU_CH
LN_EPS = 1e-5
HYENA_ORDER = 2
SHORT_CONV = 3
FILTER_EMB = 33
FILTER_BANDS = (FILTER_EMB - 1) // 2
FILTER_HIDDEN = 64
DECAY_TARGET = 1e-2
MAX_DECAY = math.log(DECAY_TARGET) / 0.3
MIN_DECAY = math.log(DECAY_TARGET) / 1.5

LANES = 128
ONES_ROWS = 16
DFT_MINOR = 256
VMEM_LIMIT = 48 * 1024 * 1024
VMEM_LIMIT_WIDE = 56 * 1024 * 1024

NT_DIMS = (((1,), (1,)), ((), ()))
TN_DIMS = (((0,), (0,)), ((), ()))


def _params(*sem, vmem=VMEM_LIMIT):
    return pltpu.CompilerParams(dimension_semantics=sem, vmem_limit_bytes=vmem)


def _tile(n, pref):
    return pref if n % pref == 0 else n


def _adaln_kernel(ct_ref, w_ref, b_ref, o_ref, *, n_rows):
    a = ct_ref[...]
    a = a * jax.nn.sigmoid(a)
    w = w_ref[...]
    rows = [jnp.sum(w * a[:, r:r + 1], axis=0, keepdims=True) + b_ref[...] for r in range(n_rows)]
    rows.append(jnp.zeros((o_ref.shape[0] - n_rows, w.shape[1]), F32))
    o_ref[...] = jnp.concatenate(rows, axis=0)


def _adaln_all(cond_t, n_rows, ada_w, ada_b):
    depth, d, n6 = ada_w.shape
    tn = _tile(n6, 1024)
    return pl.pallas_call(
        functools.partial(_adaln_kernel, n_rows=n_rows),
        grid=(depth, n6 // tn),
        in_specs=[pl.BlockSpec((d, 8), lambda l, j: (0, 0)),
                  pl.BlockSpec((None, d, tn), lambda l, j: (l, 0, j)),
                  pl.BlockSpec((None, 1, tn), lambda l, j: (l, 0, j))],
        out_specs=pl.BlockSpec((None, 8, tn), lambda l, j: (l, 0, j)),
        out_shape=jax.ShapeDtypeStruct((depth, 8, n6), F32),
        compiler_params=_params("parallel", "parallel"),
        name="adaln",
    )(cond_t, ada_w, ada_b.reshape(depth, 1, n6))


def _norm_mod(x_ref, g_ref, sh_ref, sc_ref):
    x = x_ref[...]
    ms = jnp.mean(x * x, axis=-1, keepdims=True)
    y = x * lax.rsqrt(ms + NORM_EPS) * g_ref[...]
    return (y * (1.0 + sc_ref[...]) + sh_ref[...]).astype(BF16)


def _inproj_kernel(x_ref, g_ref, sh_ref, sc_ref, w_ref, cos_ref, sin_ref, o_ref, hs_ref, *,
                   n_q, n_qk, rope):
    j = pl.program_id(2)

    @pl.when(j == 0)
    def _():
        hs_ref[...] = _norm_mod(x_ref, g_ref, sh_ref, sc_ref)

    tm, tn = o_ref.shape
    rc = min(tm, 256)

    def by_row_chunks(epilogue):
        for r in range(tm // rc):
            rows = pl.ds(r * rc, rc)
            acc = jnp.dot(hs_ref[rows, :], w_ref[...], preferred_element_type=F32)
            o_ref[rows, :] = epilogue(acc, rows).astype(o_ref.dtype)

    def qk_epilogue(a, rows):
        if rope:
            lane = lax.broadcasted_iota(jnp.int32, a.shape, 1)
            first = (lane & 31) < 16
            partner = jnp.where(first, pltpu.roll(a, tn - 16, 1), pltpu.roll(a, 16, 1))
            reps = tn // LANES
            a = (a * jnp.tile(cos_ref[rows, :], (1, reps))
                 + partner * jnp.tile(sin_ref[rows, :], (1, reps)))
        return jnp.where(j < n_q, a * DIFF_SCALE, a)

    @pl.when(j < n_qk)
    def _():
        by_row_chunks(qk_epilogue)

    @pl.when(j >= n_qk)
    def _():
        by_row_chunks(lambda a, rows: jax.nn.gelu(a))


def _inproj(x, g, sh, sc, w, cos_t, sin_t, rope):
    b, s, d = x.shape
    n = w.shape[1]
    tm = _tile(s, 1024)
    tn = 512
    kern = functools.partial(_inproj_kernel, n_q=Q_COLS // tn, n_qk=2 * Q_COLS // tn, rope=rope)
    return pl.pallas_call(
        kern,
        grid=(b, s // tm, n // tn),
        in_specs=[pl.BlockSpec((None, tm, d), lambda bb, i, j: (bb, i, 0)),
                  pl.BlockSpec((1, d), lambda bb, i, j: (0, 0)),
                  pl.BlockSpec((None, 1, d), lambda bb, i, j: (bb, 0, 0)),
                  pl.BlockSpec((None, 1, d), lambda bb, i, j: (bb, 0, 0)),
                  pl.BlockSpec((d, tn), lambda bb, i, j: (0, j)),
                  pl.BlockSpec((tm, LANES), lambda bb, i, j: (i, 0)),
                  pl.BlockSpec((tm, LANES), lambda bb, i, j: (i, 0))],
        out_specs=pl.BlockSpec((None, tm, tn), lambda bb, i, j: (bb, i, j)),
        out_shape=jax.ShapeDtypeStruct((b, s, n), BF16),
        scratch_shapes=[pltpu.VMEM((tm, d), BF16)],
        compiler_params=_params("parallel", "parallel", "arbitrary"),
        name="inproj",
    )(x, g, sh, sc, w, cos_t, sin_t)


def _inproj_nt_kernel(x_ref, g_ref, sh_ref, sc_ref, wt_ref, *rest, tl, conv):
    if conv:
        cw_ref, halo_ref, o_ref, hs_ref = rest
    else:
        o_ref, hs_ref = rest
    j = pl.program_id(2)

    @pl.when(j == 0)
    def _():
        hs_ref[...] = _norm_mod(x_ref, g_ref, sh_ref, sc_ref)

    n_chunks = o_ref.shape[0]
    if not conv:
        acc = lax.dot_general(wt_ref[...], hs_ref[...], NT_DIMS, preferred_element_type=F32)
        for c in range(n_chunks):
            o_ref[c] = acc[:, c * tl:(c + 1) * tl].astype(o_ref.dtype)
        return

    acc = jnp.concatenate(
        [lax.dot_general(wt_ref[...], hs_ref[c * tl:(c + 1) * tl, :], NT_DIMS, preferred_element_type=F32)
         for c in range(n_chunks)], axis=1)
    tn, tm = acc.shape
    halo = halo_ref[...]
    col = lax.broadcasted_iota(jnp.int32, halo.shape, 1)
    tile = pl.program_id(1)
    before = jnp.sum(jnp.where(col == 2 * tile, halo, 0.0), axis=1, keepdims=True)
    after = jnp.sum(jnp.where(col == 2 * tile + 1, halo, 0.0), axis=1, keepdims=True)
    lane = lax.broadcasted_iota(jnp.int32, (tn, LANES), 1)
    prev = pltpu.roll(acc, 1, 1)
    prev = jnp.concatenate([jnp.where(lane == 0, before, prev[:, :LANES]), prev[:, LANES:]], axis=1)
    nxt = pltpu.roll(acc, tm - 1, 1)
    nxt = jnp.concatenate([nxt[:, :tm - LANES],
                           jnp.where(lane == LANES - 1, after, nxt[:, tm - LANES:])], axis=1)

    def tap(k):
        return jnp.tile(cw_ref[:, k * LANES:(k + 1) * LANES], (1, tm // LANES))

    out = tap(3) + prev * tap(0) + acc * tap(1) + nxt * tap(2)
    for c in range(n_chunks):
        o_ref[c] = out[:, c * tl:(c + 1) * tl].astype(o_ref.dtype)


def _inproj_nt(x, g, sh, sc, wt, tl, conv_params=None, halo=None):
    b, s, d = x.shape
    n = wt.shape[0]
    tm = _tile(s, 1024)
    tn = 512
    conv = conv_params is not None
    kern = functools.partial(_inproj_nt_kernel, tl=tl, conv=conv)
    in_specs = [pl.BlockSpec((None, tm, d), lambda bb, i, j: (bb, i, 0)),
                pl.BlockSpec((1, d), lambda bb, i, j: (0, 0)),
                pl.BlockSpec((None, 1, d), lambda bb, i, j: (bb, 0, 0)),
                pl.BlockSpec((None, 1, d), lambda bb, i, j: (bb, 0, 0)),
                pl.BlockSpec((tn, d), lambda bb, i, j: (j, 0))]
    args = [x, g, sh, sc, wt]
    if conv:
        in_specs += [pl.BlockSpec((tn, 4 * LANES), lambda bb, i, j: (j, 0)),
                     pl.BlockSpec((None, tn, halo.shape[2]), lambda bb, i, j: (bb, j, 0))]
        args += [conv_params, halo]
    return pl.pallas_call(
        kern,
        grid=(b, s // tm, n // tn),
        in_specs=in_specs,
        out_specs=pl.BlockSpec((None, tm // tl, tn, tl), lambda bb, i, j: (bb, i, j, 0)),
        out_shape=jax.ShapeDtypeStruct((b, s // tl, n, tl), BF16),
        scratch_shapes=[pltpu.VMEM((tm, d), BF16)],
        compiler_params=_params("parallel", "parallel", "arbitrary"),
        name="inproj_nt",
    )(*args)


def _ffn_up_kernel(x_ref, g_ref, sh_ref, sc_ref, wg_ref, wu_ref, o_ref, hs_ref):
    j = pl.program_id(2)

    @pl.when(j == 0)
    def _():
        hs_ref[...] = _norm_mod(x_ref, g_ref, sh_ref, sc_ref)

    hs = hs_ref[...]
    gate = jnp.dot(hs, wg_ref[...], preferred_element_type=F32)
    up = jnp.dot(hs, wu_ref[...], preferred_element_type=F32)
    o_ref[...] = (gate * jax.nn.sigmoid(gate) * up).astype(o_ref.dtype)


def _ffn_up(x, g, sh, sc, wg, wu):
    b, s, d = x.shape
    n = wg.shape[1]
    tm = _tile(s, 1024)
    tn = 512
    return pl.pallas_call(
        _ffn_up_kernel,
        grid=(b, s // tm, n // tn),
        in_specs=[pl.BlockSpec((None, tm, d), lambda bb, i, j: (bb, i, 0)),
                  pl.BlockSpec((1, d), lambda bb, i, j: (0, 0)),
                  pl.BlockSpec((None, 1, d), lambda bb, i, j: (bb, 0, 0)),
                  pl.BlockSpec((None, 1, d), lambda bb, i, j: (bb, 0, 0)),
                  pl.BlockSpec((d, tn), lambda bb, i, j: (0, j)),
                  pl.BlockSpec((d, tn), lambda bb, i, j: (0, j))],
        out_specs=pl.BlockSpec((None, tm, tn), lambda bb, i, j: (bb, i, j)),
        out_shape=jax.ShapeDtypeStruct((b, s, n), BF16),
        scratch_shapes=[pltpu.VMEM((tm, d), BF16)],
        compiler_params=_params("parallel", "parallel", "arbitrary"),
        name="ffn_up",
    )(x, g, sh, sc, wg, wu)


def _proj_res_kernel(*refs, ksizes, transposed):
    n = len(ksizes)
    a_refs = refs[:n]
    w_ref, x_ref, gate_ref, o_ref = refs[n:]
    acc = None
    off = 0
    for a_ref, ks in zip(a_refs, ksizes):
        w = w_ref[off:off + ks, :]
        if transposed:
            part = lax.dot_general(a_ref[...], w, TN_DIMS, preferred_element_type=F32)
        else:
            part = jnp.dot(a_ref[...], w, preferred_element_type=F32)
        acc = part if acc is None else acc + part
        off += ks
    o_ref[...] = x_ref[...] + gate_ref[...] * acc


def _proj_res(a_list, w, x, gate, transposed=False, tm_pref=512, tn_pref=512, vmem=VMEM_LIMIT):
    b, s, d = x.shape
    ksizes = tuple(a.shape[1] if transposed else a.shape[2] for a in a_list)
    ktot = sum(ksizes)
    tm = _tile(s, tm_pref)
    tn = _tile(d, tn_pref)
    if transposed:
        a_specs = [pl.BlockSpec((None, ks, tm), lambda bb, i, j: (bb, 0, i)) for ks in ksizes]
    else:
        a_specs = [pl.BlockSpec((None, tm, ks), lambda bb, i, j: (bb, i, 0)) for ks in ksizes]
    kern = functools.partial(_proj_res_kernel, ksizes=ksizes, transposed=transposed)
    return pl.pallas_call(
        kern,
        grid=(b, s // tm, d // tn),
        in_specs=a_specs + [pl.BlockSpec((ktot, tn), lambda bb, i, j: (0, j)),
                            pl.BlockSpec((None, tm, tn), lambda bb, i, j: (bb, i, j)),
                            pl.BlockSpec((None, 1, tn), lambda bb, i, j: (bb, 0, j))],
        out_specs=pl.BlockSpec((None, tm, tn), lambda bb, i, j: (bb, i, j)),
        out_shape=jax.ShapeDtypeStruct((b, s, d), F32),
        compiler_params=_params("parallel", "parallel", "parallel", vmem=vmem),
        name="proj_res",
    )(*a_list, w, x, gate)


def _attn_kernel(*refs, n_chunks, lam_init):
    if n_chunks:
        lam_ref, q_ref, kc_ref, vct_ref, k_ref, vt_ref, g_ref, o_ref, acc_ref = refs[:9]
        s_refs = refs[9:]
    else:
        lam_ref, q_ref, kc_ref, vct_ref, g_ref, o_ref, acc_ref = refs
    q = q_ref[...]
    tq = q.shape[0]
    dv = DIFF_V_DIM
    qm = (q[:, :DIFF_QK_DIM], q[:, DIFF_QK_DIM:])

    def scores(kblk):
        return tuple(lax.dot_general(kblk[:, m * DIFF_QK_DIM:(m + 1) * DIFF_QK_DIM], qm[m], NT_DIMS,
                                     preferred_element_type=F32) for m in range(2))

    def absorb(s_pair, vtblk, m_pair):
        vext = jnp.concatenate([vtblk, jnp.ones((ONES_ROWS, vtblk.shape[1]), BF16)], axis=0)
        out = []
        for m in range(2):
            m_old = m_pair[m]
            m_new = jnp.maximum(m_old, jnp.max(s_pair[m], axis=0, keepdims=True))
            alpha = jnp.exp(m_old - m_new)
            p = jnp.exp((s_pair[m] - m_new).astype(BF16))
            pv = jnp.dot(vext, p, preferred_element_type=F32)
            acc_ref[m] = alpha * acc_ref[m] + pv
            out.append(m_new)
        return tuple(out)

    def store(ref, s_pair):
        ref[0] = s_pair[0]
        ref[1] = s_pair[1]

    acc_ref[...] = jnp.zeros_like(acc_ref)
    init = jnp.full((1, tq), -1e30, F32)
    m_pair = absorb(scores(kc_ref[...]), vct_ref[0], (init, init))
    if n_chunks:
        group = len(s_refs) // 2
        sets = (s_refs[:group], s_refs[group:])
        assert n_chunks % (2 * group) == 0

        def half_trip(cur, nxt, base, mp, lookahead):
            for k in range(group):
                if lookahead:
                    store(nxt[k], scores(k_ref[base + group + k]))
                mp = absorb((cur[k][0], cur[k][1]), vt_ref[base + k], mp)
            return mp

        def trip(j, mp, lookahead):
            base = 2 * group * j
            mp = half_trip(sets[0], sets[1], base, mp, True)
            return half_trip(sets[1], sets[0], base + group, mp, lookahead)

        for k in range(group):
            store(sets[0][k], scores(k_ref[k]))
        n_trips = n_chunks // (2 * group)
        m_pair = lax.fori_loop(0, n_trips - 1, lambda j, mp: trip(j, mp, True), m_pair)
        m_pair = trip(n_trips - 1, m_pair, False)

    lp = lam_ref[...]
    lam = (jnp.exp(jnp.sum(lp[0:1] * lp[1:2], axis=-1, keepdims=True))
           - jnp.exp(jnp.sum(lp[2:3] * lp[3:4], axis=-1, keepdims=True)) + lam_init)
    acc0, acc1 = acc_ref[0], acc_ref[1]
    o = acc0[:dv] / acc0[dv:dv + 1] - lam * (acc1[:dv] / acc1[dv:dv + 1])
    ot = o.T
    ms = jnp.mean(ot * ot, axis=-1, keepdims=True)
    on = ot * lax.rsqrt(ms + SUBLN_EPS) * g_ref[...] * (1.0 - lam_init)
    o_ref[...] = on.astype(o_ref.dtype)


def _attention(lam_p, subln, q_arr, kc_arr, vct_arr, k4=None, vt4=None, *, lam_init):
    b, sq = q_arr.shape[0], q_arr.shape[1]
    sc = kc_arr.shape[1]
    h = DIFF_HEADS
    tq = _tile(sq, 256)
    dv = DIFF_V_DIM
    n_chunks = 0 if k4 is None else k4.shape[1]
    in_specs = [pl.BlockSpec((4, DIFF_QK_DIM), lambda bb, hh, i: (0, 0)),
                pl.BlockSpec((None, tq, dv), lambda bb, hh, i: (bb, i, hh)),
                pl.BlockSpec((None, sc, dv), lambda bb, hh, i: (bb, 0, h + hh)),
                pl.BlockSpec((None, 1, dv, sc), lambda bb, hh, i: (bb, 0, hh, 0))]
    args = [lam_p, q_arr, kc_arr, vct_arr]
    if n_chunks:
        tk = k4.shape[2]
        in_specs += [pl.BlockSpec((None, n_chunks, tk, dv), lambda bb, hh, i: (bb, 0, 0, h + hh)),
                     pl.BlockSpec((None, n_chunks, dv, tk), lambda bb, hh, i: (bb, 0, hh, 0))]
        args += [k4, vt4]
    in_specs.append(pl.BlockSpec((1, dv), lambda bb, hh, i: (0, 0)))
    args.append(subln)
    kern = functools.partial(_attn_kernel, n_chunks=n_chunks, lam_init=lam_init)
    scratch = [pltpu.VMEM((2, dv + ONES_ROWS, tq), F32)]
    if n_chunks:
        group = 4 if n_chunks % 16 == 0 else 2
        scratch += [pltpu.VMEM((2, tk, tq), F32) for _ in range(2 * group)]
    return pl.pallas_call(
        kern,
        grid=(b, h, sq // tq),
        in_specs=in_specs,
        out_specs=pl.BlockSpec((None, tq, dv), lambda bb, hh, i: (bb, i, hh)),
        out_shape=jax.ShapeDtypeStruct((b, sq, A_WIDTH), BF16),
        scratch_shapes=scratch,
        compiler_params=_params("parallel", "parallel", "parallel"),
        name="diff_attn",
    )(*args)


def _sgu_kernel(u_ref, g_ref, ng_ref, nb_ref, w_ref, bs_ref, o_ref):
    for gi in range(SGU_GROUPS):
        cols = slice(gi * SGU_CH, (gi + 1) * SGU_CH)
        w = w_ref[gi]
        for c in range(u_ref.shape[0] // SGU_CHUNK):
            sl = slice(c * SGU_CHUNK, (c + 1) * SGU_CHUNK)
            gg = g_ref[sl, cols].astype(F32)
            mu = jnp.mean(gg, axis=-1, keepdims=True)
            dev = gg - mu
            var = jnp.mean(dev * dev, axis=-1, keepdims=True)
            vv = dev * lax.rsqrt(var + LN_EPS) * ng_ref[gi] + nb_ref[gi]
            mixed = jnp.dot(w, vv.astype(BF16), preferred_element_type=F32) + bs_ref[gi]
            o_ref[sl, cols] = (u_ref[sl, cols].astype(F32) * mixed).astype(o_ref.dtype)


def _sgu(qkug, norm_g, norm_b, w_s, b_s):
    b, s = qkug.shape[0], qkug.shape[1]
    tm = _tile(s, 1024)
    gcount = SGU_GROUPS
    ublk = 2 * Q_COLS // B_WIDTH
    full = lambda shape: pl.BlockSpec(shape, lambda bb, i: (0,) * len(shape))
    return pl.pallas_call(
        _sgu_kernel,
        grid=(b, s // tm),
        in_specs=[pl.BlockSpec((None, tm, B_WIDTH), lambda bb, i: (bb, i, ublk)),
                  pl.BlockSpec((None, tm, B_WIDTH), lambda bb, i: (bb, i, ublk + 1)),
                  full((gcount, 1, SGU_CH)), full((gcount, 1, SGU_CH)),
                  full((gcount, SGU_CHUNK, SGU_CHUNK)), full((gcount, SGU_CHUNK, 1))],
        out_specs=pl.BlockSpec((None, tm, B_WIDTH), lambda bb, i: (bb, i, 0)),
        out_shape=jax.ShapeDtypeStruct((b, s, B_WIDTH), BF16),
        compiler_params=_params("parallel", "parallel"),
        name="sgu",
    )(qkug, qkug, norm_g.reshape(gcount, 1, SGU_CH), norm_b.reshape(gcount, 1, SGU_CH),
      w_s.astype(BF16), b_s.reshape(gcount, SGU_CHUNK, 1))


def _filter_mlp_kernel(f_ref, w1_ref, b1_ref, w2_ref, b2_ref, w3_ref, b3_ref, fr_ref, o_ref):
    def lin(a, w_ref, b_ref):
        return jnp.dot(a, w_ref[...], preferred_element_type=F32, precision=HIGHEST) + b_ref[...]
    fr = fr_ref[...]
    hcur = jnp.sin(fr[0:1] * lin(f_ref[...], w1_ref, b1_ref))
    hcur = jnp.sin(fr[1:2] * lin(hcur, w2_ref, b2_ref))
    o_ref[...] = jnp.sin(fr[2:3] * lin(hcur, w3_ref, b3_ref))


def _filter_mlp(feats, w1, b1, w2, b2, w3, b3, freq):
    rows, emb = feats.shape
    hid = FILTER_HIDDEN
    tr = _tile(rows, 2048)
    full = lambda shape: pl.BlockSpec(shape, lambda i: (0,) * len(shape))
    return pl.pallas_call(
        _filter_mlp_kernel,
        grid=(rows // tr,),
        in_specs=[pl.BlockSpec((tr, emb), lambda i: (i, 0)),
                  full((emb, hid)), full((1, hid)), full((hid, hid)), full((1, hid)),
                  full((hid, hid)), full((1, hid)), full((3, hid))],
        out_specs=pl.BlockSpec((tr, hid), lambda i: (i, 0)),
        out_shape=jax.ShapeDtypeStruct((rows, hid), F32),
        compiler_params=_params("parallel"),
        name="filter_mlp",
    )(feats, w1, b1.reshape(1, hid), w2, b2.reshape(1, hid), w3, b3.reshape(1, hid), freq)


def _filter_raw_kernel(w4t_ref, h_ref, t_ref, delta_ref, o_ref, *, zero_tile):
    rt = pl.program_id(2)

    def split(a):
        hi = a.astype(BF16)
        return hi, (a - hi.astype(F32)).astype(BF16)

    def nt(a, bm):
        return lax.dot_general(a, bm, NT_DIMS, preferred_element_type=F32)

    w_hi, w_lo = split(w4t_ref[...])
    h_hi, h_lo = split(h_ref[...])
    raw = nt(w_hi, h_hi) + (nt(w_hi, h_lo) + nt(w_lo, h_hi))
    raw = raw * jnp.exp(-(delta_ref[...] * t_ref[...]))
    for c in range(o_ref.shape[0]):
        o_ref[c] = raw[:, c * DFT_MINOR:(c + 1) * DFT_MINOR].astype(o_ref.dtype)

    @pl.when(rt == zero_tile)
    def _():
        col = lax.broadcasted_iota(jnp.int32, (raw.shape[0], DFT_MINOR), 1)
        o_ref[0] = jnp.where(col == 0, 0.0, raw[:, :DFT_MINOR]).astype(o_ref.dtype)


def _filter_raw(w4t, h3, t_row, deltas, seq):
    c = w4t.shape[2]
    rows = h3.shape[0]
    tr = _tile(rows // 2, 2048)
    tc = _tile(c, 512)
    half_tiles = seq // tr
    kern = functools.partial(_filter_raw_kernel, zero_tile=half_tiles)
    return pl.pallas_call(
        kern,
        grid=(HYENA_ORDER, c // tc, rows // tr),
        in_specs=[pl.BlockSpec((None, None, tc, FILTER_HIDDEN),
                               lambda n, ci, rt: (n, rt // half_tiles, ci, 0)),
                  pl.BlockSpec((tr, FILTER_HIDDEN), lambda n, ci, rt: (rt, 0)),
                  pl.BlockSpec((1, tr), lambda n, ci, rt: (0, rt)),
                  pl.BlockSpec((tc, 1), lambda n, ci, rt: (ci, 0))],
        out_specs=pl.BlockSpec((None, tr // DFT_MINOR, tc, DFT_MINOR), lambda n, ci, rt: (n, rt, ci, 0)),
        out_shape=jax.ShapeDtypeStruct((HYENA_ORDER, rows // DFT_MINOR, c, DFT_MINOR), BF16),
        compiler_params=_params("parallel", "parallel", "parallel"),
        name="filter_raw",
    )(w4t, h3, t_row, deltas)


def _cmul(ar, ai, br, bi):
    return ar * br - ai * bi, ar * bi + ai * br


def _store_complex(ref, c, n1, re, im):
    r0 = pl.multiple_of(c * n1, n1)
    ref[pl.ds(r0, n1), :DFT_MINOR] = re.astype(ref.dtype)
    ref[pl.ds(r0, n1), DFT_MINOR:] = im.astype(ref.dtype)


def _load_complex(ref, c, n1):
    r0 = pl.multiple_of(c * n1, n1)
    tile = ref[pl.ds(r0, n1), :]
    return tile[:, :DFT_MINOR], tile[:, DFT_MINOR:]


def _filter_spec_kernel(k_ref, f1_ref, twr_ref, twi_ref, f2_ref, o_ref, a2_ref, *, inv_n):
    tc, n1, n2 = k_ref.shape

    def left(c, carry):
        k = k_ref[c].astype(F32)
        nrm = jnp.sum(jnp.sum(jnp.abs(k), axis=1, keepdims=True), axis=0, keepdims=True)
        kn = (k * (inv_n / nrm)).astype(BF16)
        a = jnp.dot(f1_ref[...], kn, preferred_element_type=F32)
        ar, ai = _cmul(a[:n1], a[n1:], twr_ref[...], twi_ref[...])
        _store_complex(a2_ref, c, n1, ar, ai)
        return carry

    lax.fori_loop(0, tc, left, 0, unroll=4)
    z = jnp.dot(a2_ref[...], f2_ref[...], preferred_element_type=F32).reshape(tc, n1, 2 * n2)
    o_ref[:, 0] = z[:, :, :n2].astype(o_ref.dtype)
    o_ref[:, 1] = z[:, :, n2:].astype(o_ref.dtype)


def _filter_spec(k2, f1_full, twr, twi, f2e):
    nc, n1, n2 = k2.shape
    tc = 16
    kern = functools.partial(_filter_spec_kernel, inv_n=1.0 / (n1 * n2))
    full = lambda shape: pl.BlockSpec(shape, lambda i: (0,) * len(shape))
    return pl.pallas_call(
        kern,
        grid=(nc // tc,),
        in_specs=[pl.BlockSpec((tc, n1, n2), lambda i: (i, 0, 0)),
                  full((2 * n1, n1)), full((n1, n2)), full((n1, n2)), full((2 * n2, 2 * n2))],
        out_specs=pl.BlockSpec((tc, 2, n1, n2), lambda i: (i, 0, 0, 0)),
        out_shape=jax.ShapeDtypeStruct((nc, 2, n1, n2), BF16),
        scratch_shapes=[pltpu.VMEM((tc * n1, 2 * n2), BF16)],
        compiler_params=_params("parallel"),
        name="filter_spec",
    )(k2, f1_full, twr, twi, f2e)


def _hyena_kernel(zv_ref, z1_ref, z2_ref, kf_ref, fb_ref, f1_ref, twr_ref, twi_ref, f2_ref, f2c_ref,
                  g1_ref, o_ref, u_ref, a2_ref, z_ref):
    nb, tc, hr, n2 = zv_ref.shape
    n1 = 2 * hr
    gate_refs = (z1_ref, z2_ref)

    for n in range(HYENA_ORDER):
        last = n == HYENA_ORDER - 1
        sig_ref = zv_ref if n == 0 else u_ref

        def fwd_left(c, carry):
            xs = jnp.concatenate([sig_ref[0, c], sig_ref[1, c]], axis=0).astype(BF16)
            a = jnp.dot(f1_ref[...], xs, preferred_element_type=F32)
            ar, ai = _cmul(a[:n1], a[n1:], twr_ref[...], twi_ref[...])
            _store_complex(a2_ref, c, n1, ar, ai)
            return carry

        lax.fori_loop(0, tc, fwd_left, 0, unroll=4)
        z_ref[...] = jnp.dot(a2_ref[...], f2_ref[...], preferred_element_type=F32)

        def spectrum(c, carry):
            zr, zi = _load_complex(z_ref, c, n1)
            wr, wi = _cmul(zr, zi, kf_ref[n, c, 0].astype(F32), kf_ref[n, c, 1].astype(F32))
            _store_complex(a2_ref, c, n1, wr, wi)
            return carry

        lax.fori_loop(0, tc, spectrum, 0, unroll=4)
        z_ref[...] = jnp.dot(a2_ref[...], f2c_ref[...], preferred_element_type=F32)

        def inv_left(c, carry):
            br, bi = _load_complex(z_ref, c, n1)
            br, bi = _cmul(br, bi, twr_ref[...], -twi_ref[...])
            bs = jnp.concatenate([br, bi], axis=0).astype(BF16)
            y = jnp.dot(g1_ref[...], bs, preferred_element_type=F32)
            fb = fb_ref[pl.ds(c, 1), n:n + 1]
            for bb in range(nb):
                new = gate_refs[n][bb, c].astype(F32) * (y[bb * hr:(bb + 1) * hr]
                                                         + sig_ref[bb, c].astype(F32) * fb)
                if last:
                    o_ref[bb, c] = new.astype(o_ref.dtype)
                else:
                    u_ref[bb, c] = new
            return carry

        lax.fori_loop(0, tc, inv_left, 0, unroll=4)


def _hyena_core(z4, kf, fbias, f1h, twr, twi, f2e, f2c, g1e):
    nb, c3, hr, n2 = z4.shape
    c = c3 // 3
    n1 = 2 * hr
    tc = 16
    nct = c // tc
    full = lambda shape: pl.BlockSpec(shape, lambda i: (0,) * len(shape))
    zspec = lambda part: pl.BlockSpec((nb, tc, hr, n2), lambda i: (0, i + part * nct, 0, 0))
    return pl.pallas_call(
        _hyena_kernel,
        grid=(nct,),
        in_specs=[zspec(0), zspec(1), zspec(2),
                  pl.BlockSpec((HYENA_ORDER, tc, 2, n1, n2), lambda i: (0, i, 0, 0, 0)),
                  pl.BlockSpec((tc, HYENA_ORDER), lambda i: (i, 0)),
                  full((2 * n1, n1)), full((n1, n2)), full((n1, n2)),
                  full((2 * n2, 2 * n2)), full((2 * n2, 2 * n2)), full((n1, 2 * n1))],
        out_specs=pl.BlockSpec((nb, tc, hr, n2), lambda i: (0, i, 0, 0)),
        out_shape=jax.ShapeDtypeStruct((nb, c, hr, n2), BF16),
        scratch_shapes=[pltpu.VMEM((nb, tc, hr, n2), F32),
                        pltpu.VMEM((tc * n1, 2 * n2), BF16),
                        pltpu.VMEM((tc * n1, 2 * n2), F32)],
        compiler_params=_params("parallel"),
        name="hyena_core",
    )(z4, z4, z4, kf, fbias, f1h, twr, twi, f2e, f2c, g1e)


def _dft_tables(n1):
    n2 = DFT_MINOR
    n = n1 * n2
    a1 = 2.0 * np.pi * np.outer(np.arange(n1), np.arange(n1)) / n1
    f1r, f1i = np.cos(a1), -np.sin(a1)
    a2 = 2.0 * np.pi * np.outer(np.arange(n2), np.arange(n2)) / n2
    f2r, f2i = np.cos(a2), -np.sin(a2)
    at = 2.0 * np.pi * np.outer(np.arange(n1), np.arange(n2)) / n
    twr, twi = np.cos(at), -np.sin(at)
    hr = n1 // 2
    f1_full = np.concatenate([f1r, f1i], axis=0)
    f1_half = np.block([[f1r[:, :hr], -f1i[:, :hr]], [f1i[:, :hr], f1r[:, :hr]]])
    f2e = np.block([[f2r, f2i], [-f2i, f2r]])
    f2c = np.block([[f2r, -f2i], [f2i, f2r]])
    gr, gi = f1r[:hr, :], -f1i[:hr, :]
    g1e = np.block([[gr, -gi], [gi, gr]])
    bf = lambda m: jnp.asarray(m, F32).astype(BF16)
    return dict(f1_full=bf(f1_full), f1_half=bf(f1_half), f2e=bf(f2e), f2c=bf(f2c), g1e=bf(g1e),
                twr=jnp.asarray(twr, F32), twi=jnp.asarray(twi, F32))


def _hyena_spectra(seq, fw1, fb1, fw2, fb2, fw3, fb3, ffreq, fw4, tabs):
    l = seq
    width = fw4.shape[1] // (2 * HYENA_ORDER)
    t_fwd = jnp.linspace(0.0, 1.0, l, dtype=F32)
    w = 2.0 * math.pi * jnp.arange(l, dtype=F32)[:, None] / l
    f = jnp.linspace(1e-4, FILTER_BANDS - 1, FILTER_BANDS, dtype=F32)[None, :]
    feats = jnp.concatenate([t_fwd[:, None], jnp.cos(f * w), -jnp.sin(f * w)], axis=-1)

    def two_sided(a):
        return jnp.concatenate([a, a[:1], a[1:][::-1]], axis=0)

    feats = two_sided(feats)
    t_lin = two_sided(t_fwd)
    emb_pad = FILTER_HIDDEN
    feats = jnp.pad(feats, ((0, 0), (0, emb_pad - FILTER_EMB)))
    w1p = jnp.pad(fw1, ((0, emb_pad - FILTER_EMB), (0, 0)))
    h3 = _filter_mlp(feats, w1p, fb1, fw2, fb2, fw3, fb3, ffreq)
    w4t = fw4.reshape(FILTER_HIDDEN, HYENA_ORDER, 2, width).transpose(1, 2, 3, 0)
    deltas = jnp.abs(jnp.linspace(MIN_DECAY, MAX_DECAY, width, dtype=F32))[:, None]
    raw = _filter_raw(w4t, h3, t_lin[None, :], deltas, l)
    n1 = raw.shape[1]
    k2 = raw.transpose(0, 2, 1, 3).reshape(HYENA_ORDER * width, n1, DFT_MINOR)
    kf = _filter_spec(k2, tabs["f1_full"], tabs["twr"], tabs["twi"], tabs["f2e"])
    return kf.reshape(HYENA_ORDER, width, 2, n1, DFT_MINOR)


def _rms_kernel(x_ref, g_ref, o_ref):
    x = x_ref[...]
    ms = jnp.mean(x * x, axis=-1, keepdims=True)
    o_ref[...] = x * lax.rsqrt(ms + NORM_EPS) * g_ref[...]


def _final_norm(x, g):
    b, s, d = x.shape
    tm = _tile(s, 1024)
    return pl.pallas_call(
        _rms_kernel,
        grid=(b, s // tm),
        in_specs=[pl.BlockSpec((None, tm, d), lambda bb, i: (bb, i, 0)),
                  pl.BlockSpec((1, d), lambda bb, i: (0, 0))],
        out_specs=pl.BlockSpec((None, tm, d), lambda bb, i: (bb, i, 0)),
        out_shape=jax.ShapeDtypeStruct((b, s, d), F32),
        compiler_params=_params("parallel", "parallel"),
        name="final_norm",
    )(x, g.reshape(1, d))


def _rope_tables(n_tokens):
    tok = jnp.arange(n_tokens)
    row = (tok // GRID_W).astype(F32)
    col = (tok % GRID_W).astype(F32)
    half = ROPE_AXIS_DIM // 2
    inv = 1.0 / (ROPE_THETA ** (jnp.arange(0, ROPE_AXIS_DIM, 2, dtype=F32) / ROPE_AXIS_DIM))
    ang_r = row[:, None] * inv
    ang_c = col[:, None] * inv
    cos64 = jnp.concatenate([jnp.cos(ang_r), jnp.cos(ang_r), jnp.cos(ang_c), jnp.cos(ang_c)], axis=-1)
    sin64 = jnp.concatenate([-jnp.sin(ang_r), jnp.sin(ang_r), -jnp.sin(ang_c), jnp.sin(ang_c)], axis=-1)
    assert cos64.shape[1] == 4 * half == DIFF_QK_DIM
    return jnp.tile(cos64, (1, 2)), jnp.tile(sin64, (1, 2))


def _ffn(x, norm_g, sh, sc, gate, wg, wu, wd):
    hidden = _ffn_up(x, norm_g, sh, sc, wg, wu)
    return _proj_res([hidden], wd, x, gate, tm_pref=1024, vmem=VMEM_LIMIT_WIDE)


def _even_layer(x, xc, mods, cmods, n1g, n2g, w_in, w_out, lam_p, subln, sgu_ng, sgu_nb, sgu_w, sgu_b,
                wg, wu, wd, layer_idx):
    b, s, d = x.shape
    lam_init = 0.8 - 0.6 * math.exp(-0.3 * layer_idx)
    sh1, sc1, g1, sh2, sc2, g2 = mods
    csh1, csc1, cg1, csh2, csc2, cg2 = cmods
    o_k, o_v, o_u = Q_COLS, 2 * Q_COLS, 2 * Q_COLS + A_WIDTH
    w_qkug = jnp.concatenate([w_in[:, :o_v], w_in[:, o_u:]], axis=1).astype(BF16)
    w_vt = w_in[:, o_v:o_u].T.astype(BF16)
    wo = w_out.astype(BF16)
    n1g2 = n1g.reshape(1, d)
    cos_t, sin_t = _rope_tables(s)
    sc_len = xc.shape[1]

    qkug = _inproj(x, n1g2, sh1, sc1, w_qkug, cos_t, sin_t, rope=True)
    cqkug = _inproj(xc, n1g2, csh1, csc1, w_qkug, cos_t[:sc_len], sin_t[:sc_len], rope=False)
    tk = _tile(s, 512)
    vt4 = _inproj_nt(x, n1g2, sh1, sc1, w_vt, tk)
    cvt4 = _inproj_nt(xc, n1g2, csh1, csc1, w_vt, sc_len)
    k4 = qkug.reshape(b, s // tk, tk, qkug.shape[2])
    subln2 = subln.reshape(1, DIFF_V_DIM)

    a_l = _attention(lam_p, subln2, qkug, cqkug, cvt4, k4, vt4, lam_init=lam_init)
    s_l = _sgu(qkug, sgu_ng, sgu_nb, sgu_w, sgu_b)
    x = _proj_res([a_l, s_l], wo, x, g1, tn_pref=d)
    x = _ffn(x, n2g.reshape(1, d), sh2, sc2, g2, wg, wu, wd)

    a_c = _attention(lam_p, subln2, cqkug, cqkug, cvt4, lam_init=lam_init)
    s_c = _sgu(cqkug, sgu_ng, sgu_nb, sgu_w, sgu_b)
    xc = _proj_res([a_c, s_c], wo, xc, cg1, tn_pref=d)
    xc = _ffn(xc, n2g.reshape(1, d), csh2, csc2, cg2, wg, wu, wd)
    return x, xc


def _odd_layer(x, mods, n1g, n2g, w_in, conv_w, conv_b, fw1, fb1, fw2, fb2, fw3, fb3, ffreq, fw4, fbias,
               w_out, wg, wu, wd):
    b, s, d = x.shape
    sh1, sc1, g1, sh2, sc2, g2 = mods
    width = w_out.shape[0]
    n1 = 2 * s // DFT_MINOR
    tabs = _dft_tables(n1)
    kf = _hyena_spectra(s, fw1, fb1, fw2, fb2, fw3, fb3, ffreq, fw4, tabs)

    n1g2 = n1g.reshape(1, d)
    wt = w_in.T.astype(BF16)
    tm = _tile(s, 1024)
    n_tiles = s // tm
    first = jnp.arange(n_tiles) * tm
    edge_tokens = jnp.stack([jnp.maximum(first - 1, 0), jnp.minimum(first + tm, s - 1)], axis=1).reshape(-1)
    z_edge = _inproj_nt(x[:, edge_tokens, :], n1g2, sh1, sc1, wt, 2 * n_tiles)
    inside = jnp.ones((2 * n_tiles,), F32).at[0].set(0.0).at[2 * n_tiles - 1].set(0.0)
    halo = z_edge[:, 0].astype(F32) * inside
    conv_params = jnp.repeat(jnp.concatenate([conv_w.T, conv_b[:, None]], axis=1), LANES, axis=1)

    z4 = _inproj_nt(x, n1g2, sh1, sc1, wt, DFT_MINOR, conv_params, halo)
    z4 = z4.transpose(0, 2, 1, 3)
    y4 = _hyena_core(z4, kf, fbias.T, tabs["f1_half"], tabs["twr"], tabs["twi"], tabs["f2e"], tabs["f2c"],
                     tabs["g1e"])
    yt = y4.reshape(b, width, s)
    x = _proj_res([yt], w_out.astype(BF16), x, g1, transposed=True, tn_pref=d)
    return _ffn(x, n2g.reshape(1, d), sh2, sc2, g2, wg, wu, wd)


def kernel(x, c, ctx, c_ctx, ada_w, ada_b, norm1, norm2, ffn_w_gate, ffn_w_up, ffn_w_down, e_w_in, e_w_out, e_lambda, e_subln, e_sgu_norm_g, e_sgu_norm_b, e_sgu_w, e_sgu_b, o_w_in, o_conv_w, o_conv_b, o_filt_w1, o_filt_b1, o_filt_w2, o_filt_b2, o_filt_w3, o_filt_b3, o_filt_freq, o_filt_w4, o_filt_bias, o_w_out, final_norm):
    b, s, d = x.shape
    depth = ada_w.shape[0]
    assert b == 2, "the long convolution packs exactly two batches into one complex signal"
    assert depth == 2, "odd layers here never carry the context stream"
    cond_t = jnp.zeros((d, 8), F32).at[:, :b].set(c.T).at[:, b].set(c_ctx)
    mod_all = _adaln_all(cond_t, b + 1, ada_w, ada_b)
    xc = ctx
    for i in range(depth):
        j = i // 2
        parts = jnp.split(mod_all[i], 6, axis=-1)
        mods = [p[:b, None, :] for p in parts]
        cmods = [jnp.broadcast_to(p[b:b + 1, None, :], (b, 1, d)) for p in parts]
        wg, wu, wd = (ffn_w_gate[i].astype(BF16), ffn_w_up[i].astype(BF16), ffn_w_down[i].astype(BF16))
        if i % 2 == 0:
            x, xc = _even_layer(x, xc, mods, cmods, norm1[i], norm2[i], e_w_in[j], e_w_out[j], e_lambda[j],
                                e_subln[j], e_sgu_norm_g[j], e_sgu_norm_b[j], e_sgu_w[j], e_sgu_b[j],
                                wg, wu, wd, i)
        else:
            x = _odd_layer(x, mods, norm1[i], norm2[i], o_w_in[j], o_conv_w[j], o_conv_b[j], o_filt_w1[j],
                           o_filt_b1[j], o_filt_w2[j], o_filt_b2[j], o_filt_w3[j], o_filt_b3[j],
                           o_filt_freq[j], o_filt_w4[j], o_filt_bias[j], o_w_out[j], wg, wu, wd)
    return _final_norm(x, final_norm)
```

```python
import functools
import math

import numpy as np
import jax
import jax.numpy as jnp
from jax import lax
from jax.experimental import pallas as pl
from jax.experimental.pallas import tpu as pltpu

F32 = jnp.float32
BF16 = jnp.bfloat16
HIGHEST = lax.Precision.HIGHEST

GRID_W = 64
NORM_EPS = 1e-6
DIFF_HEADS = 8
DIFF_QK_DIM = 64
DIFF_V_DIM = 2 * DIFF_QK_DIM
DIFF_SCALE = DIFF_QK_DIM ** -0.5
A_WIDTH = DIFF_HEADS * DIFF_V_DIM
Q_COLS = DIFF_HEADS * 2 * DIFF_QK_DIM
ROPE_THETA = 10000.0
ROPE_AXIS_DIM = DIFF_QK_DIM // 2
SUBLN_EPS = 1e-5
SGU_GROUPS = 8
SGU_CHUNK = 128
SGU_CH = 128
B_WIDTH = SGU_GROUPS * SGU_CH
LN_EPS = 1e-5
HYENA_ORDER = 2
SHORT_CONV = 3
FILTER_EMB = 33
FILTER_BANDS = (FILTER_EMB - 1) // 2
FILTER_HIDDEN = 64
DECAY_TARGET = 1e-2
MAX_DECAY = math.log(DECAY_TARGET) / 0.3
MIN_DECAY = math.log(DECAY_TARGET) / 1.5

LANES = 128
ONES_ROWS = 16
DFT_MINOR = 256
VMEM_LIMIT = 48 * 1024 * 1024
VMEM_LIMIT_WIDE = 56 * 1024 * 1024

NT_DIMS = (((1,), (1,)), ((), ()))
TN_DIMS = (((0,), (0,)), ((), ()))


def _params(*sem, vmem=VMEM_LIMIT):
    return pltpu.CompilerParams(dimension_semantics=sem, vmem_limit_bytes=vmem)


def _tile(n, pref):
    return pref if n % pref == 0 else n


def _adaln_kernel(ct_ref, w_ref, b_ref, o_ref, *, n_rows):
    k = pl.program_id(1)
    a = ct_ref[...]
    a = a * jax.nn.sigmoid(a)
    w = w_ref[...]
    rows = [jnp.sum(w * a[:, r:r + 1], axis=0, keepdims=True) for r in range(n_rows)]
    rows.append(jnp.zeros((o_ref.shape[0] - n_rows, w.shape[1]), F32))
    part = jnp.concatenate(rows, axis=0)

    @pl.when(k == 0)
    def _():
        valid = lax.broadcasted_iota(jnp.int32, part.shape, 0) < n_rows
        o_ref[...] = part + jnp.where(valid, b_ref[...], 0.0)

    @pl.when(k > 0)
    def _():
        o_ref[...] += part


def _adaln_all(cond_t, n_rows, ada_w, ada_b):
    depth, d, n6 = ada_w.shape
    tk = _tile(d, 256)
    return pl.pallas_call(
        functools.partial(_adaln_kernel, n_rows=n_rows),
        grid=(depth, d // tk),
        in_specs=[pl.BlockSpec((tk, 8), lambda l, k: (k, 0)),
                  pl.BlockSpec((None, tk, n6), lambda l, k: (l, k, 0)),
                  pl.BlockSpec((None, 1, n6), lambda l, k: (l, 0, 0))],
        out_specs=pl.BlockSpec((None, 8, n6), lambda l, k: (l, 0, 0)),
        out_shape=jax.ShapeDtypeStruct((depth, 8, n6), F32),
        compiler_params=_params("parallel", "arbitrary"),
        name="adaln",
    )(cond_t, ada_w, ada_b.reshape(depth, 1, n6))


def _norm_mod(x_ref, g_ref, sh_ref, sc_ref):
    x = x_ref[...]
    ms = jnp.mean(x * x, axis=-1, keepdims=True)
    y = x * lax.rsqrt(ms + NORM_EPS) * g_ref[...]
    return (y * (1.0 + sc_ref[...]) + sh_ref[...]).astype(BF16)


def _inproj_kernel(x_ref, g_ref, sh_ref, sc_ref, w_ref, cos_ref, sin_ref, o_ref, hs_ref, *,
                   n_q, n_qk, rope):
    j = pl.program_id(2)

    @pl.when(j == 0)
    def _():
        hs_ref[...] = _norm_mod(x_ref, g_ref, sh_ref, sc_ref)

    tm, tn = o_ref.shape
    rc = min(tm, 256)

    def by_row_chunks(epilogue):
        for r in range(tm // rc):
            rows = pl.ds(r * rc, rc)
            acc = jnp.dot(hs_ref[rows, :], w_ref[...], preferred_element_type=F32)
            o_ref[rows, :] = epilogue(acc, rows).astype(o_ref.dtype)

    def qk_epilogue(a, rows):
        if rope:
            lane = lax.broadcasted_iota(jnp.int32, a.shape, 1)
            first = (lane & 31) < 16
            partner = jnp.where(first, pltpu.roll(a, tn - 16, 1), pltpu.roll(a, 16, 1))
            reps = tn // LANES
            a = (a * jnp.tile(cos_ref[rows, :], (1, reps))
                 + partner * jnp.tile(sin_ref[rows, :], (1, reps)))
        return jnp.where(j < n_q, a * DIFF_SCALE, a)

    @pl.when(j < n_qk)
    def _():
        by_row_chunks(qk_epilogue)

    @pl.when(j >= n_qk)
    def _():
        by_row_chunks(lambda a, rows: jax.nn.gelu(a))


def _inproj(x, g, sh, sc, w, cos_t, sin_t, rope):
    b, s, d = x.shape
    n = w.shape[1]
    tm = _tile(s, 1024)
    tn = 512
    kern = functools.partial(_inproj_kernel, n_q=Q_COLS // tn, n_qk=2 * Q_COLS // tn, rope=rope)
    return pl.pallas_call(
        kern,
        grid=(b, s // tm, n // tn),
        in_specs=[pl.BlockSpec((None, tm, d), lambda bb, i, j: (bb, i, 0)),
                  pl.BlockSpec((1, d), lambda bb, i, j: (0, 0)),
                  pl.BlockSpec((None, 1, d), lambda bb, i, j: (bb, 0, 0)),
                  pl.BlockSpec((None, 1, d), lambda bb, i, j: (bb, 0, 0)),
                  pl.BlockSpec((d, tn), lambda bb, i, j: (0, j)),
                  pl.BlockSpec((tm, LANES), lambda bb, i, j: (i, 0)),
                  pl.BlockSpec((tm, LANES), lambda bb, i, j: (i, 0))],
        out_specs=pl.BlockSpec((None, tm, tn), lambda bb, i, j: (bb, i, j)),
        out_shape=jax.ShapeDtypeStruct((b, s, n), BF16),
        scratch_shapes=[pltpu.VMEM((tm, d), BF16)],
        compiler_params=_params("parallel", "parallel", "arbitrary"),
        name="inproj",
    )(x, g, sh, sc, w, cos_t, sin_t)


def _inproj_nt_kernel(x_ref, g_ref, sh_ref, sc_ref, wt_ref, *rest, tl, conv):
    if conv:
        cw_ref, halo_ref, o_ref, hs_ref = rest
    else:
        o_ref, hs_ref = rest
    j = pl.program_id(2)

    @pl.when(j == 0)
    def _():
        hs_ref[...] = _norm_mod(x_ref, g_ref, sh_ref, sc_ref)

    n_chunks = o_ref.shape[0]
    if not conv:
        acc = lax.dot_general(wt_ref[...], hs_ref[...], NT_DIMS, preferred_element_type=F32)
        for c in range(n_chunks):
            o_ref[c] = acc[:, c * tl:(c + 1) * tl].astype(o_ref.dtype)
        return

    acc = jnp.concatenate(
        [lax.dot_general(wt_ref[...], hs_ref[c * tl:(c + 1) * tl, :], NT_DIMS, preferred_element_type=F32)
         for c in range(n_chunks)], axis=1)
    tn, tm = acc.shape
    halo = halo_ref[...]
    col = lax.broadcasted_iota(jnp.int32, halo.shape, 1)
    tile = pl.program_id(1)
    before = jnp.sum(jnp.where(col == 2 * tile, halo, 0.0), axis=1, keepdims=True)
    after = jnp.sum(jnp.where(col == 2 * tile + 1, halo, 0.0), axis=1, keepdims=True)
    lane = lax.broadcasted_iota(jnp.int32, (tn, LANES), 1)
    prev = pltpu.roll(acc, 1, 1)
    prev = jnp.concatenate([jnp.where(lane == 0, before, prev[:, :LANES]), prev[:, LANES:]], axis=1)
    nxt = pltpu.roll(acc, tm - 1, 1)
    nxt = jnp.concatenate([nxt[:, :tm - LANES],
                           jnp.where(lane == LANES - 1, after, nxt[:, tm - LANES:])], axis=1)

    def tap(k):
        return jnp.tile(cw_ref[:, k * LANES:(k + 1) * LANES], (1, tm // LANES))

    out = tap(3) + prev * tap(0) + acc * tap(1) + nxt * tap(2)
    for c in range(n_chunks):
        o_ref[c] = out[:, c * tl:(c + 1) * tl].astype(o_ref.dtype)


def _inproj_nt(x, g, sh, sc, wt, tl, conv_params=None, halo=None):
    b, s, d = x.shape
    n = wt.shape[0]
    tm = _tile(s, 1024)
    tn = 512
    conv = conv_params is not None
    kern = functools.partial(_inproj_nt_kernel, tl=tl, conv=conv)
    in_specs = [pl.BlockSpec((None, tm, d), lambda bb, i, j: (bb, i, 0)),
                pl.BlockSpec((1, d), lambda bb, i, j: (0, 0)),
                pl.BlockSpec((None, 1, d), lambda bb, i, j: (bb, 0, 0)),
                pl.BlockSpec((None, 1, d), lambda bb, i, j: (bb, 0, 0)),
                pl.BlockSpec((tn, d), lambda bb, i, j: (j, 0))]
    args = [x, g, sh, sc, wt]
    if conv:
        in_specs += [pl.BlockSpec((tn, 4 * LANES), lambda bb, i, j: (j, 0)),
                     pl.BlockSpec((None, tn, halo.shape[2]), lambda bb, i, j: (bb, j, 0))]
        args += [conv_params, halo]
    return pl.pallas_call(
        kern,
        grid=(b, s // tm, n // tn),
        in_specs=in_specs,
        out_specs=pl.BlockSpec((None, tm // tl, tn, tl), lambda bb, i, j: (bb, i, j, 0)),
        out_shape=jax.ShapeDtypeStruct((b, s // tl, n, tl), BF16),
        scratch_shapes=[pltpu.VMEM((tm, d), BF16)],
        compiler_params=_params("parallel", "parallel", "arbitrary"),
        name="inproj_nt",
    )(*args)


def _ffn_up_kernel(x_ref, g_ref, sh_ref, sc_ref, wg_ref, wu_ref, o_ref, hs_ref):
    j = pl.program_id(2)

    @pl.when(j == 0)
    def _():
        hs_ref[...] = _norm_mod(x_ref, g_ref, sh_ref, sc_ref)

    hs = hs_ref[...]
    gate = jnp.dot(hs, wg_ref[...], preferred_element_type=F32)
    up = jnp.dot(hs, wu_ref[...], preferred_element_type=F32)
    o_ref[...] = (gate * jax.nn.sigmoid(gate) * up).astype(o_ref.dtype)


def _ffn_up(x, g, sh, sc, wg, wu):
    b, s, d = x.shape
    n = wg.shape[1]
    tm = _tile(s, 1024)
    tn = 512
    return pl.pallas_call(
        _ffn_up_kernel,
        grid=(b, s // tm, n // tn),
        in_specs=[pl.BlockSpec((None, tm, d), lambda bb, i, j: (bb, i, 0)),
                  pl.BlockSpec((1, d), lambda bb, i, j: (0, 0)),
                  pl.BlockSpec((None, 1, d), lambda bb, i, j: (bb, 0, 0)),
                  pl.BlockSpec((None, 1, d), lambda bb, i, j: (bb, 0, 0)),
                  pl.BlockSpec((d, tn), lambda bb, i, j: (0, j)),
                  pl.BlockSpec((d, tn), lambda bb, i, j: (0, j))],
        out_specs=pl.BlockSpec((None, tm, tn), lambda bb, i, j: (bb, i, j)),
        out_shape=jax.ShapeDtypeStruct((b, s, n), BF16),
        scratch_shapes=[pltpu.VMEM((tm, d), BF16)],
        compiler_params=_params("parallel", "parallel", "arbitrary"),
        name="ffn_up",
    )(x, g, sh, sc, wg, wu)


def _proj_res_kernel(*refs, ksizes, transposed):
    n = len(ksizes)
    a_refs = refs[:n]
    w_ref, x_ref, gate_ref, o_ref = refs[n:]
    acc = None
    off = 0
    for a_ref, ks in zip(a_refs, ksizes):
        w = w_ref[off:off + ks, :]
        if transposed:
            part = lax.dot_general(a_ref[...], w, TN_DIMS, preferred_element_type=F32)
        else:
            part = jnp.dot(a_ref[...], w, preferred_element_type=F32)
        acc = part if acc is None else acc + part
        off += ks
    o_ref[...] = x_ref[...] + gate_ref[...] * acc


def _proj_res(a_list, w, x, gate, transposed=False, tm_pref=512, tn_pref=512, vmem=VMEM_LIMIT):
    b, s, d = x.shape
    ksizes = tuple(a.shape[1] if transposed else a.shape[2] for a in a_list)
    ktot = sum(ksizes)
    tm = _tile(s, tm_pref)
    tn = _tile(d, tn_pref)
    if transposed:
        a_specs = [pl.BlockSpec((None, ks, tm), lambda bb, i, j: (bb, 0, i)) for ks in ksizes]
    else:
        a_specs = [pl.BlockSpec((None, tm, ks), lambda bb, i, j: (bb, i, 0)) for ks in ksizes]
    kern = functools.partial(_proj_res_kernel, ksizes=ksizes, transposed=transposed)
    return pl.pallas_call(
        kern,
        grid=(b, s // tm, d // tn),
        in_specs=a_specs + [pl.BlockSpec((ktot, tn), lambda bb, i, j: (0, j)),
                            pl.BlockSpec((None, tm, tn), lambda bb, i, j: (bb, i, j)),
                            pl.BlockSpec((None, 1, tn), lambda bb, i, j: (bb, 0, j))],
        out_specs=pl.BlockSpec((None, tm, tn), lambda bb, i, j: (bb, i, j)),
        out_shape=jax.ShapeDtypeStruct((b, s, d), F32),
        compiler_params=_params("parallel", "parallel", "parallel", vmem=vmem),
        name="proj_res",
    )(*a_list, w, x, gate)


def _attn_kernel(*refs, n_chunks, lam_init):
    if n_chunks:
        lam_ref, q_ref, kc_ref, vct_ref, k_ref, vt_ref, g_ref, o_ref, acc_ref = refs[:9]
        s_refs = refs[9:]
    else:
        lam_ref, q_ref, kc_ref, vct_ref, g_ref, o_ref, acc_ref = refs
    q = q_ref[...]
    tq = q.shape[0]
    dv = DIFF_V_DIM
    qm = (q[:, :DIFF_QK_DIM], q[:, DIFF_QK_DIM:])

    def scores(kblk):
        return tuple(lax.dot_general(kblk[:, m * DIFF_QK_DIM:(m + 1) * DIFF_QK_DIM], qm[m], NT_DIMS,
                                     preferred_element_type=F32) for m in range(2))

    def absorb(s_pair, vtblk, m_pair):
        vext = jnp.concatenate([vtblk, jnp.ones((ONES_ROWS, vtblk.shape[1]), BF16)], axis=0)
        out = []
        for m in range(2):
            m_old = m_pair[m]
            m_new = jnp.maximum(m_old, jnp.max(s_pair[m], axis=0, keepdims=True))
            alpha = jnp.exp(m_old - m_new)
            p = jnp.exp((s_pair[m] - m_new).astype(BF16))
            pv = jnp.dot(vext, p, preferred_element_type=F32)
            acc_ref[m] = alpha * acc_ref[m] + pv
            out.append(m_new)
        return tuple(out)

    def store(ref, s_pair):
        ref[0] = s_pair[0]
        ref[1] = s_pair[1]

    acc_ref[...] = jnp.zeros_like(acc_ref)
    init = jnp.full((1, tq), -1e30, F32)
    m_pair = absorb(scores(kc_ref[...]), vct_ref[0], (init, init))
    if n_chunks:
        group = len(s_refs) // 2
        sets = (s_refs[:group], s_refs[group:])
        assert n_chunks % (2 * group) == 0

        def half_trip(cur, nxt, base, mp, lookahead):
            for k in range(group):
                if lookahead:
                    store(nxt[k], scores(k_ref[base + group + k]))
                mp = absorb((cur[k][0], cur[k][1]), vt_ref[base + k], mp)
            return mp

        def trip(j, mp, lookahead):
            base = 2 * group * j
            mp = half_trip(sets[0], sets[1], base, mp, True)
            return half_trip(sets[1], sets[0], base + group, mp, lookahead)

        for k in range(group):
            store(sets[0][k], scores(k_ref[k]))
        n_trips = n_chunks // (2 * group)
        m_pair = lax.fori_loop(0, n_trips - 1, lambda j, mp: trip(j, mp, True), m_pair)
        m_pair = trip(n_trips - 1, m_pair, False)

    lp = lam_ref[...]
    lam = (jnp.exp(jnp.sum(lp[0:1] * lp[1:2], axis=-1, keepdims=True))
           - jnp.exp(jnp.sum(lp[2:3] * lp[3:4], axis=-1, keepdims=True)) + lam_init)
    acc0, acc1 = acc_ref[0], acc_ref[1]
    o = acc0[:dv] / acc0[dv:dv + 1] - lam * (acc1[:dv] / acc1[dv:dv + 1])
    ot = o.T
    ms = jnp.mean(ot * ot, axis=-1, keepdims=True)
    on = ot * lax.rsqrt(ms + SUBLN_EPS) * g_ref[...] * (1.0 - lam_init)
    o_ref[...] = on.astype(o_ref.dtype)


def _attention(lam_p, subln, q_arr, kc_arr, vct_arr, k4=None, vt4=None, *, lam_init):
    b, sq = q_arr.shape[0], q_arr.shape[1]
    sc = kc_arr.shape[1]
    h = DIFF_HEADS
    tq = _tile(sq, 256)
    dv = DIFF_V_DIM
    n_chunks = 0 if k4 is None else k4.shape[1]
    in_specs = [pl.BlockSpec((4, DIFF_QK_DIM), lambda bb, hh, i: (0, 0)),
                pl.BlockSpec((None, tq, dv), lambda bb, hh, i: (bb, i, hh)),
                pl.BlockSpec((None, sc, dv), lambda bb, hh, i: (bb, 0, h + hh)),
                pl.BlockSpec((None, 1, dv, sc), lambda bb, hh, i: (bb, 0, hh, 0))]
    args = [lam_p, q_arr, kc_arr, vct_arr]
    if n_chunks:
        tk = k4.shape[2]
        in_specs += [pl.BlockSpec((None, n_chunks, tk, dv), lambda bb, hh, i: (bb, 0, 0, h + hh)),
                     pl.BlockSpec((None, n_chunks, dv, tk), lambda bb, hh, i: (bb, 0, hh, 0))]
        args += [k4, vt4]
    in_specs.append(pl.BlockSpec((1, dv), lambda bb, hh, i: (0, 0)))
    args.append(subln)
    kern = functools.partial(_attn_kernel, n_chunks=n_chunks, lam_init=lam_init)
    scratch = [pltpu.VMEM((2, dv + ONES_ROWS, tq), F32)]
    if n_chunks:
        group = 4 if n_chunks % 16 == 0 else 2
        scratch += [pltpu.VMEM((2, tk, tq), F32) for _ in range(2 * group)]
    return pl.pallas_call(
        kern,
        grid=(b, h, sq // tq),
        in_specs=in_specs,
        out_specs=pl.BlockSpec((None, tq, dv), lambda bb, hh, i: (bb, i, hh)),
        out_shape=jax.ShapeDtypeStruct((b, sq, A_WIDTH), BF16),
        scratch_shapes=scratch,
        compiler_params=_params("parallel", "parallel", "parallel"),
        name="diff_attn",
    )(*args)


def _sgu_kernel(u_ref, g_ref, ng_ref, nb_ref, w_ref, bs_ref, o_ref):
    for gi in range(SGU_GROUPS):
        cols = slice(gi * SGU_CH, (gi + 1) * SGU_CH)
        w = w_ref[gi]
        for c in range(u_ref.shape[0] // SGU_CHUNK):
            sl = slice(c * SGU_CHUNK, (c + 1) * SGU_CHUNK)
            gg = g_ref[sl, cols].astype(F32)
            mu = jnp.mean(gg, axis=-1, keepdims=True)
            dev = gg - mu
            var = jnp.mean(dev * dev, axis=-1, keepdims=True)
            vv = dev * lax.rsqrt(var + LN_EPS) * ng_ref[gi] + nb_ref[gi]
            mixed = jnp.dot(w, vv.astype(BF16), preferred_element_type=F32) + bs_ref[gi]
            o_ref[sl, cols] = (u_ref[sl, cols].astype(F32) * mixed).astype(o_ref.dtype)


def _sgu(qkug, norm_g, norm_b, w_s, b_s):
    b, s = qkug.shape[0], qkug.shape[1]
    tm = _tile(s, 1024)
    gcount = SGU_GROUPS
    ublk = 2 * Q_COLS // B_WIDTH
    full = lambda shape: pl.BlockSpec(shape, lambda bb, i: (0,) * len(shape))
    return pl.pallas_call(
        _sgu_kernel,
        grid=(b, s // tm),
        in_specs=[pl.BlockSpec((None, tm, B_WIDTH), lambda bb, i: (bb, i, ublk)),
                  pl.BlockSpec((None, tm, B_WIDTH), lambda bb, i: (bb, i, ublk + 1)),
                  full((gcount, 1, SGU_CH)), full((gcount, 1, SGU_CH)),
                  full((gcount, SGU_CHUNK, SGU_CHUNK)), full((gcount, SGU_CHUNK, 1))],
        out_specs=pl.BlockSpec((None, tm, B_WIDTH), lambda bb, i: (bb, i, 0)),
        out_shape=jax.ShapeDtypeStruct((b, s, B_WIDTH), BF16),
        compiler_params=_params("parallel", "parallel"),
        name="sgu",
    )(qkug, qkug, norm_g.reshape(gcount, 1, SGU_CH), norm_b.reshape(gcount, 1, SGU_CH),
      w_s.astype(BF16), b_s.reshape(gcount, SGU_CHUNK, 1))


def _filter_mlp_kernel(f_ref, w1_ref, b1_ref, w2_ref, b2_ref, w3_ref, b3_ref, fr_ref, o_ref):
    def lin(a, w_ref, b_ref):
        return jnp.dot(a, w_ref[...], preferred_element_type=F32, precision=HIGHEST) + b_ref[...]
    fr = fr_ref[...]
    hcur = jnp.sin(fr[0:1] * lin(f_ref[...], w1_ref, b1_ref))
    hcur = jnp.sin(fr[1:2] * lin(hcur, w2_ref, b2_ref))
    o_ref[...] = jnp.sin(fr[2:3] * lin(hcur, w3_ref, b3_ref))


def _filter_mlp(feats, w1, b1, w2, b2, w3, b3, freq):
    rows, emb = feats.shape
    hid = FILTER_HIDDEN
    tr = _tile(rows, 2048)
    full = lambda shape: pl.BlockSpec(shape, lambda i: (0,) * len(shape))
    return pl.pallas_call(
        _filter_mlp_kernel,
        grid=(rows // tr,),
        in_specs=[pl.BlockSpec((tr, emb), lambda i: (i, 0)),
                  full((emb, hid)), full((1, hid)), full((hid, hid)), full((1, hid)),
                  full((hid, hid)), full((1, hid)), full((3, hid))],
        out_specs=pl.BlockSpec((tr, hid), lambda i: (i, 0)),
        out_shape=jax.ShapeDtypeStruct((rows, hid), F32),
        compiler_params=_params("parallel"),
        name="filter_mlp",
    )(feats, w1, b1.reshape(1, hid), w2, b2.reshape(1, hid), w3, b3.reshape(1, hid), freq)


def _filter_raw_kernel(w4t_ref, h_ref, t_ref, delta_ref, o_ref, *, zero_tile):
    rt = pl.program_id(2)

    def split(a):
        hi = a.astype(BF16)
        return hi, (a - hi.astype(F32)).astype(BF16)

    def nt(a, bm):
        return lax.dot_general(a, bm, NT_DIMS, preferred_element_type=F32)

    w_hi, w_lo = split(w4t_ref[...])
    h_hi, h_lo = split(h_ref[...])
    raw = nt(w_hi, h_hi) + (nt(w_hi, h_lo) + nt(w_lo, h_hi))
    raw = raw * jnp.exp(-(delta_ref[...] * t_ref[...]))
    for c in range(o_ref.shape[0]):
        o_ref[c] = raw[:, c * DFT_MINOR:(c + 1) * DFT_MINOR].astype(o_ref.dtype)

    @pl.when(rt == zero_tile)
    def _():
        col = lax.broadcasted_iota(jnp.int32, (raw.shape[0], DFT_MINOR), 1)
        o_ref[0] = jnp.where(col == 0, 0.0, raw[:, :DFT_MINOR]).astype(o_ref.dtype)


def _filter_raw(w4t, h3, t_row, deltas, seq):
    c = w4t.shape[2]
    rows = h3.shape[0]
    tr = _tile(rows // 2, 2048)
    tc = _tile(c, 512)
    half_tiles = seq // tr
    kern = functools.partial(_filter_raw_kernel, zero_tile=half_tiles)
    return pl.pallas_call(
        kern,
        grid=(HYENA_ORDER, c // tc, rows // tr),
        in_specs=[pl.BlockSpec((None, None, tc, FILTER_HIDDEN),
                               lambda n, ci, rt: (n, rt // half_tiles, ci, 0)),
                  pl.BlockSpec((tr, FILTER_HIDDEN), lambda n, ci, rt: (rt, 0)),
                  pl.BlockSpec((1, tr), lambda n, ci, rt: (0, rt)),
                  pl.BlockSpec((tc, 1), lambda n, ci, rt: (ci, 0))],
        out_specs=pl.BlockSpec((None, tr // DFT_MINOR, tc, DFT_MINOR), lambda n, ci, rt: (n, rt, ci, 0)),
        out_shape=jax.ShapeDtypeStruct((HYENA_ORDER, rows // DFT_MINOR, c, DFT_MINOR), BF16),
        compiler_params=_params("parallel", "parallel", "parallel"),
        name="filter_raw",
    )(w4t, h3, t_row, deltas)


def _cmul(ar, ai, br, bi):
    return ar * br - ai * bi, ar * bi + ai * br


def _store_complex(ref, c, n1, re, im):
    r0 = pl.multiple_of(c * n1, n1)
    ref[pl.ds(r0, n1), :DFT_MINOR] = re.astype(ref.dtype)
    ref[pl.ds(r0, n1), DFT_MINOR:] = im.astype(ref.dtype)


def _load_complex(ref, c, n1):
    r0 = pl.multiple_of(c * n1, n1)
    tile = ref[pl.ds(r0, n1), :]
    return tile[:, :DFT_MINOR], tile[:, DFT_MINOR:]


def _filter_spec_kernel(k_ref, f1_ref, twr_ref, twi_ref, f2_ref, o_ref, a2_ref, *, inv_n):
    tc, n1, n2 = k_ref.shape

    def left(c, carry):
        k = k_ref[c].astype(F32)
        nrm = jnp.sum(jnp.sum(jnp.abs(k), axis=1, keepdims=True), axis=0, keepdims=True)
        kn = (k * (inv_n / nrm)).astype(BF16)
        a = jnp.dot(f1_ref[...], kn, preferred_element_type=F32)
        ar, ai = _cmul(a[:n1], a[n1:], twr_ref[...], twi_ref[...])
        _store_complex(a2_ref, c, n1, ar, ai)
        return carry

    lax.fori_loop(0, tc, left, 0, unroll=4)
    z = jnp.dot(a2_ref[...], f2_ref[...], preferred_element_type=F32).reshape(tc, n1, 2 * n2)
    o_ref[:, 0] = z[:, :, :n2].astype(o_ref.dtype)
    o_ref[:, 1] = z[:, :, n2:].astype(o_ref.dtype)


def _filter_spec(k2, f1_full, twr, twi, f2e):
    nc, n1, n2 = k2.shape
    tc = 16
    kern = functools.partial(_filter_spec_kernel, inv_n=1.0 / (n1 * n2))
    full = lambda shape: pl.BlockSpec(shape, lambda i: (0,) * len(shape))
    return pl.pallas_call(
        kern,
        grid=(nc // tc,),
        in_specs=[pl.BlockSpec((tc, n1, n2), lambda i: (i, 0, 0)),
                  full((2 * n1, n1)), full((n1, n2)), full((n1, n2)), full((2 * n2, 2 * n2))],
        out_specs=pl.BlockSpec((tc, 2, n1, n2), lambda i: (i, 0, 0, 0)),
        out_shape=jax.ShapeDtypeStruct((nc, 2, n1, n2), BF16),
        scratch_shapes=[pltpu.VMEM((tc * n1, 2 * n2), BF16)],
        compiler_params=_params("parallel"),
        name="filter_spec",
    )(k2, f1_full, twr, twi, f2e)


def _hyena_kernel(zv_ref, z1_ref, z2_ref, kf_ref, fb_ref, f1_ref, twr_ref, twi_ref, f2_ref, f2c_ref,
                  g1_ref, o_ref, u_ref, a2_ref, z_ref):
    nb, tc, hr, n2 = zv_ref.shape
    n1 = 2 * hr
    gate_refs = (z1_ref, z2_ref)

    for n in range(HYENA_ORDER):
        last = n == HYENA_ORDER - 1
        sig_ref = zv_ref if n == 0 else u_ref

        def fwd_left(c, carry):
            xs = jnp.concatenate([sig_ref[0, c], sig_ref[1, c]], axis=0).astype(BF16)
            a = jnp.dot(f1_ref[...], xs, preferred_element_type=F32)
            ar, ai = _cmul(a[:n1], a[n1:], twr_ref[...], twi_ref[...])
            _store_complex(a2_ref, c, n1, ar, ai)
            return carry

        lax.fori_loop(0, tc, fwd_left, 0, unroll=4)
        z_ref[...] = jnp.dot(a2_ref[...], f2_ref[...], preferred_element_type=F32)

        def spectrum(c, carry):
            zr, zi = _load_complex(z_ref, c, n1)
            wr, wi = _cmul(zr, zi, kf_ref[n, c, 0].astype(F32), kf_ref[n, c, 1].astype(F32))
            _store_complex(a2_ref, c, n1, wr, wi)
            return carry

        lax.fori_loop(0, tc, spectrum, 0, unroll=4)
        z_ref[...] = jnp.dot(a2_ref[...], f2c_ref[...], preferred_element_type=F32)

        def inv_left(c, carry):
            br, bi = _load_complex(z_ref, c, n1)
            br, bi = _cmul(br, bi, twr_ref[...], -twi_ref[...])
            bs = jnp.concatenate([br, bi], axis=0).astype(BF16)
            y = jnp.dot(g1_ref[...], bs, preferred_element_type=F32)
            fb = fb_ref[pl.ds(c, 1), n:n + 1]
            for bb in range(nb):
                new = gate_refs[n][bb, c].astype(F32) * (y[bb * hr:(bb + 1) * hr]
                                                         + sig_ref[bb, c].astype(F32) * fb)
                if last:
                    o_ref[bb, c] = new.astype(o_ref.dtype)
                else:
                    u_ref[bb, c] = new
            return carry

        lax.fori_loop(0, tc, inv_left, 0, unroll=4)


def _hyena_core(z4, kf, fbias, f1h, twr, twi, f2e, f2c, g1e):
    nb, c3, hr, n2 = z4.shape
    c = c3 // 3
    n1 = 2 * hr
    tc = 16
    nct = c // tc
    full = lambda shape: pl.BlockSpec(shape, lambda i: (0,) * len(shape))
    zspec = lambda part: pl.BlockSpec((nb, tc, hr, n2), lambda i: (0, i + part * nct, 0, 0))
    return pl.pallas_call(
        _hyena_kernel,
        grid=(nct,),
        in_specs=[zspec(0), zspec(1), zspec(2),
                  pl.BlockSpec((HYENA_ORDER, tc, 2, n1, n2), lambda i: (0, i, 0, 0, 0)),
                  pl.BlockSpec((tc, HYENA_ORDER), lambda i: (i, 0)),
                  full((2 * n1, n1)), full((n1, n2)), full((n1, n2)),
                  full((2 * n2, 2 * n2)), full((2 * n2, 2 * n2)), full((n1, 2 * n1))],
        out_specs=pl.BlockSpec((nb, tc, hr, n2), lambda i: (0, i, 0, 0)),
        out_shape=jax.ShapeDtypeStruct((nb, c, hr, n2), BF16),
        scratch_shapes=[pltpu.VMEM((nb, tc, hr, n2), F32),
                        pltpu.VMEM((tc * n1, 2 * n2), BF16),
                        pltpu.VMEM((tc * n1, 2 * n2), F32)],
        compiler_params=_params("parallel"),
        name="hyena_core",
    )(z4, z4, z4, kf, fbias, f1h, twr, twi, f2e, f2c, g1e)


def _dft_tables(n1):
    n2 = DFT_MINOR
    n = n1 * n2
    a1 = 2.0 * np.pi * np.outer(np.arange(n1), np.arange(n1)) / n1
    f1r, f1i = np.cos(a1), -np.sin(a1)
    a2 = 2.0 * np.pi * np.outer(np.arange(n2), np.arange(n2)) / n2
    f2r, f2i = np.cos(a2), -np.sin(a2)
    at = 2.0 * np.pi * np.outer(np.arange(n1), np.arange(n2)) / n
    twr, twi = np.cos(at), -np.sin(at)
    hr = n1 // 2
    f1_full = np.concatenate([f1r, f1i], axis=0)
    f1_half = np.block([[f1r[:, :hr], -f1i[:, :hr]], [f1i[:, :hr], f1r[:, :hr]]])
    f2e = np.block([[f2r, f2i], [-f2i, f2r]])
    f2c = np.block([[f2r, -f2i], [f2i, f2r]])
    gr, gi = f1r[:hr, :], -f1i[:hr, :]
    g1e = np.block([[gr, -gi], [gi, gr]])
    bf = lambda m: jnp.asarray(m, F32).astype(BF16)
    return dict(f1_full=bf(f1_full), f1_half=bf(f1_half), f2e=bf(f2e), f2c=bf(f2c), g1e=bf(g1e),
                twr=jnp.asarray(twr, F32), twi=jnp.asarray(twi, F32))


def _hyena_spectra(seq, fw1, fb1, fw2, fb2, fw3, fb3, ffreq, fw4, tabs):
    l = seq
    width = fw4.shape[1] // (2 * HYENA_ORDER)
    t_fwd = jnp.linspace(0.0, 1.0, l, dtype=F32)
    w = 2.0 * math.pi * jnp.arange(l, dtype=F32)[:, None] / l
    f = jnp.linspace(1e-4, FILTER_BANDS - 1, FILTER_BANDS, dtype=F32)[None, :]
    feats = jnp.concatenate([t_fwd[:, None], jnp.cos(f * w), -jnp.sin(f * w)], axis=-1)

    def two_sided(a):
        return jnp.concatenate([a, a[:1], a[1:][::-1]], axis=0)

    feats = two_sided(feats)
    t_lin = two_sided(t_fwd)
    emb_pad = FILTER_HIDDEN
    feats = jnp.pad(feats, ((0, 0), (0, emb_pad - FILTER_EMB)))
    w1p = jnp.pad(fw1, ((0, emb_pad - FILTER_EMB), (0, 0)))
    h3 = _filter_mlp(feats, w1p, fb1, fw2, fb2, fw3, fb3, ffreq)
    w4t = fw4.reshape(FILTER_HIDDEN, HYENA_ORDER, 2, width).transpose(1, 2, 3, 0)
    deltas = jnp.abs(jnp.linspace(MIN_DECAY, MAX_DECAY, width, dtype=F32))[:, None]
    raw = _filter_raw(w4t, h3, t_lin[None, :], deltas, l)
    n1 = raw.shape[1]
    k2 = raw.transpose(0, 2, 1, 3).reshape(HYENA_ORDER * width, n1, DFT_MINOR)
    kf = _filter_spec(k2, tabs["f1_full"], tabs["twr"], tabs["twi"], tabs["f2e"])
    return kf.reshape(HYENA_ORDER, width, 2, n1, DFT_MINOR)


def _rms_kernel(x_ref, g_ref, o_ref):
    x = x_ref[...]
    ms = jnp.mean(x * x, axis=-1, keepdims=True)
    o_ref[...] = x * lax.rsqrt(ms + NORM_EPS) * g_ref[...]


def _final_norm(x, g):
    b, s, d = x.shape
    tm = _tile(s, 1024)
    return pl.pallas_call(
        _rms_kernel,
        grid=(b, s // tm),
        in_specs=[pl.BlockSpec((None, tm, d), lambda bb, i: (bb, i, 0)),
                  pl.BlockSpec((1, d), lambda bb, i: (0, 0))],
        out_specs=pl.BlockSpec((None, tm, d), lambda bb, i: (bb, i, 0)),
        out_shape=jax.ShapeDtypeStruct((b, s, d), F32),
        compiler_params=_params("parallel", "parallel"),
        name="final_norm",
    )(x, g.reshape(1, d))


def _rope_tables(n_tokens):
    tok = jnp.arange(n_tokens)
    row = (tok // GRID_W).astype(F32)
    col = (tok % GRID_W).astype(F32)
    half = ROPE_AXIS_DIM // 2
    inv = 1.0 / (ROPE_THETA ** (jnp.arange(0, ROPE_AXIS_DIM, 2, dtype=F32) / ROPE_AXIS_DIM))
    ang_r = row[:, None] * inv
    ang_c = col[:, None] * inv
    cos64 = jnp.concatenate([jnp.cos(ang_r), jnp.cos(ang_r), jnp.cos(ang_c), jnp.cos(ang_c)], axis=-1)
    sin64 = jnp.concatenate([-jnp.sin(ang_r), jnp.sin(ang_r), -jnp.sin(ang_c), jnp.sin(ang_c)], axis=-1)
    assert cos64.shape[1] == 4 * half == DIFF_QK_DIM
    return jnp.tile(cos64, (1, 2)), jnp.tile(sin64, (1, 2))


def _ffn(x, norm_g, sh, sc, gate, wg, wu, wd):
    hidden = _ffn_up(x, norm_g, sh, sc, wg, wu)
    return _proj_res([hidden], wd, x, gate, tm_pref=1024, vmem=VMEM_LIMIT_WIDE)


def _even_layer(x, xc, mods, cmods, n1g, n2g, w_in, w_out, lam_p, subln, sgu_ng, sgu_nb, sgu_w, sgu_b,
                wg, wu, wd, layer_idx):
    b, s, d = x.shape
    lam_init = 0.8 - 0.6 * math.exp(-0.3 * layer_idx)
    sh1, sc1, g1, sh2, sc2, g2 = mods
    csh1, csc1, cg1, csh2, csc2, cg2 = cmods
    o_k, o_v, o_u = Q_COLS, 2 * Q_COLS, 2 * Q_COLS + A_WIDTH
    w_qkug = jnp.concatenate([w_in[:, :o_v], w_in[:, o_u:]], axis=1).astype(BF16)
    w_vt = w_in[:, o_v:o_u].T.astype(BF16)
    wo = w_out.astype(BF16)
    n1g2 = n1g.reshape(1, d)
    cos_t, sin_t = _rope_tables(s)
    sc_len = xc.shape[1]

    qkug = _inproj(x, n1g2, sh1, sc1, w_qkug, cos_t, sin_t, rope=True)
    cqkug = _inproj(xc, n1g2, csh1, csc1, w_qkug, cos_t[:sc_len], sin_t[:sc_len], rope=False)
    tk = _tile(s, 512)
    vt4 = _inproj_nt(x, n1g2, sh1, sc1, w_vt, tk)
    cvt4 = _inproj_nt(xc, n1g2, csh1, csc1, w_vt, sc_len)
    k4 = qkug.reshape(b, s // tk, tk, qkug.shape[2])
    subln2 = subln.reshape(1, DIFF_V_DIM)

    a_l = _attention(lam_p, subln2, qkug, cqkug, cvt4, k4, vt4, lam_init=lam_init)
    s_l = _sgu(qkug, sgu_ng, sgu_nb, sgu_w, sgu_b)
    x = _proj_res([a_l, s_l], wo, x, g1, tn_pref=d)
    x = _ffn(x, n2g.reshape(1, d), sh2, sc2, g2, wg, wu, wd)

    a_c = _attention(lam_p, subln2, cqkug, cqkug, cvt4, lam_init=lam_init)
    s_c = _sgu(cqkug, sgu_ng, sgu_nb, sgu_w, sgu_b)
    xc = _proj_res([a_c, s_c], wo, xc, cg1, tn_pref=d)
    xc = _ffn(xc, n2g.reshape(1, d), csh2, csc2, cg2, wg, wu, wd)
    return x, xc


def _odd_layer(x, mods, n1g, n2g, w_in, conv_w, conv_b, fw1, fb1, fw2, fb2, fw3, fb3, ffreq, fw4, fbias,
               w_out, wg, wu, wd):
    b, s, d = x.shape
    sh1, sc1, g1, sh2, sc2, g2 = mods
    width = w_out.shape[0]
    n1 = 2 * s // DFT_MINOR
    tabs = _dft_tables(n1)
    kf = _hyena_spectra(s, fw1, fb1, fw2, fb2, fw3, fb3, ffreq, fw4, tabs)

    n1g2 = n1g.reshape(1, d)
    wt = w_in.T.astype(BF16)
    tm = _tile(s, 1024)
    n_tiles = s // tm
    x_tiles = x.reshape(b, n_tiles, tm, d)
    firsts, lasts = x_tiles[:, :, 0, :], x_tiles[:, :, tm - 1, :]
    before = jnp.concatenate([lasts[:, :1], lasts[:, :-1]], axis=1)
    after = jnp.concatenate([firsts[:, 1:], firsts[:, -1:]], axis=1)
    x_edge = jnp.stack([before, after], axis=2).reshape(b, 2 * n_tiles, d)
    z_edge = _inproj_nt(x_edge, n1g2, sh1, sc1, wt, 2 * n_tiles)
    inside = jnp.ones((2 * n_tiles,), F32).at[0].set(0.0).at[2 * n_tiles - 1].set(0.0)
    halo = z_edge[:, 0].astype(F32) * inside
    conv_params = jnp.repeat(jnp.concatenate([conv_w.T, conv_b[:, None]], axis=1), LANES, axis=1)

    z4 = _inproj_nt(x, n1g2, sh1, sc1, wt, DFT_MINOR, conv_params, halo)
    z4 = z4.transpose(0, 2, 1, 3)
    y4 = _hyena_core(z4, kf, fbias.T, tabs["f1_half"], tabs["twr"], tabs["twi"], tabs["f2e"], tabs["f2c"],
                     tabs["g1e"])
    yt = y4.reshape(b, width, s)
    x = _proj_res([yt], w_out.astype(BF16), x, g1, transposed=True, tn_pref=d)
    return _ffn(x, n2g.reshape(1, d), sh2, sc2, g2, wg, wu, wd)


def kernel(x, c, ctx, c_ctx, ada_w, ada_b, norm1, norm2, ffn_w_gate, ffn_w_up, ffn_w_down, e_w_in, e_w_out, e_lambda, e_subln, e_sgu_norm_g, e_sgu_norm_b, e_sgu_w, e_sgu_b, o_w_in, o_conv_w, o_conv_b, o_filt_w1, o_filt_b1, o_filt_w2, o_filt_b2, o_filt_w3, o_filt_b3, o_filt_freq, o_filt_w4, o_filt_bias, o_w_out, final_norm):
    b, s, d = x.shape
    depth = ada_w.shape[0]
    assert b == 2, "the long convolution packs exactly two batches into one complex signal"
    assert depth == 2, "odd layers here never carry the context stream"
    cond_t = jnp.zeros((d, 8), F32).at[:, :b].set(c.T).at[:, b].set(c_ctx)
    mod_all = _adaln_all(cond_t, b + 1, ada_w, ada_b)
    xc = ctx
    for i in range(depth):
        j = i // 2
        parts = jnp.split(mod_all[i], 6, axis=-1)
        mods = [p[:b, None, :] for p in parts]
        cmods = [jnp.broadcast_to(p[b:b + 1, None, :], (b, 1, d)) for p in parts]
        wg, wu, wd = (ffn_w_gate[i].astype(BF16), ffn_w_up[i].astype(BF16), ffn_w_down[i].astype(BF16))
        if i % 2 == 0:
            x, xc = _even_layer(x, xc, mods, cmods, norm1[i], norm2[i], e_w_in[j], e_w_out[j], e_lambda[j],
                                e_subln[j], e_sgu_norm_g[j], e_sgu_norm_b[j], e_sgu_w[j], e_sgu_b[j],
                                wg, wu, wd, i)
        else:
            x = _odd_layer(x, mods, norm1[i], norm2[i], o_w_in[j], o_conv_w[j], o_conv_b[j], o_filt_w1[j],
                           o_filt_b1[j], o_filt_w2[j], o_filt_b2[j], o_filt_w3[j], o_filt_b3[j],
                           o_filt_freq[j], o_filt_w4[j], o_filt_bias[j], o_w_out[j], wg, wu, wd)
    return _final_norm(x, final_norm)
```

```python
import functools
import math

import numpy as np
import jax
import jax.numpy as jnp
from jax import lax
from jax.experimental import pallas as pl
from jax.experimental.pallas import tpu as pltpu

F32 = jnp.float32
BF16 = jnp.bfloat16
HIGHEST = lax.Precision.HIGHEST

GRID_W = 64
NORM_EPS = 1e-6
DIFF_HEADS = 8
DIFF_QK_DIM = 64
DIFF_V_DIM = 2 * DIFF_QK_DIM
DIFF_SCALE = DIFF_QK_DIM ** -0.5
A_WIDTH = DIFF_HEADS * DIFF_V_DIM
Q_COLS = DIFF_HEADS * 2 * DIFF_QK_DIM
ROPE_THETA = 10000.0
ROPE_AXIS_DIM = DIFF_QK_DIM // 2
SUBLN_EPS = 1e-5
SGU_GROUPS = 8
SGU_CHUNK = 128
SGU_CH = 128
B_WIDTH = SGU_GROUPS * SGU_CH
LN_EPS = 1e-5
HYENA_ORDER = 2
SHORT_CONV = 3
FILTER_EMB = 33
FILTER_BANDS = (FILTER_EMB - 1) // 2
FILTER_HIDDEN = 64
DECAY_TARGET = 1e-2
MAX_DECAY = math.log(DECAY_TARGET) / 0.3
MIN_DECAY = math.log(DECAY_TARGET) / 1.5

LANES = 128
ONES_ROWS = 16
DFT_MINOR = 256
VMEM_LIMIT = 48 * 1024 * 1024
VMEM_LIMIT_WIDE = 56 * 1024 * 1024

NT_DIMS = (((1,), (1,)), ((), ()))
TN_DIMS = (((0,), (0,)), ((), ()))


def _params(*sem, vmem=VMEM_LIMIT):
    return pltpu.CompilerParams(dimension_semantics=sem, vmem_limit_bytes=vmem)


def _tile(n, pref):
    return pref if n % pref == 0 else n


def _adaln_kernel(ct_ref, w_ref, b_ref, o_ref, *, n_rows):
    k = pl.program_id(1)
    a = ct_ref[...]
    a = a * jax.nn.sigmoid(a)
    w = w_ref[...]
    rows = [jnp.sum(w * a[:, r:r + 1], axis=0, keepdims=True) for r in range(n_rows)]
    rows.append(jnp.zeros((o_ref.shape[0] - n_rows, w.shape[1]), F32))
    part = jnp.concatenate(rows, axis=0)

    @pl.when(k == 0)
    def _():
        valid = lax.broadcasted_iota(jnp.int32, part.shape, 0) < n_rows
        o_ref[...] = part + jnp.where(valid, b_ref[...], 0.0)

    @pl.when(k > 0)
    def _():
        o_ref[...] += part


def _adaln_all(cond_t, n_rows, ada_w, ada_b):
    depth, d, n6 = ada_w.shape
    tk = _tile(d, 256)
    return pl.pallas_call(
        functools.partial(_adaln_kernel, n_rows=n_rows),
        grid=(depth, d // tk),
        in_specs=[pl.BlockSpec((tk, 8), lambda l, k: (k, 0)),
                  pl.BlockSpec((None, tk, n6), lambda l, k: (l, k, 0)),
                  pl.BlockSpec((None, 1, n6), lambda l, k: (l, 0, 0))],
        out_specs=pl.BlockSpec((None, 8, n6), lambda l, k: (l, 0, 0)),
        out_shape=jax.ShapeDtypeStruct((depth, 8, n6), F32),
        compiler_params=_params("parallel", "arbitrary"),
        name="adaln",
    )(cond_t, ada_w, ada_b.reshape(depth, 1, n6))


def _norm_mod(x_ref, g_ref, sh_ref, sc_ref):
    x = x_ref[...]
    ms = jnp.mean(x * x, axis=-1, keepdims=True)
    y = x * lax.rsqrt(ms + NORM_EPS) * g_ref[...]
    return (y * (1.0 + sc_ref[...]) + sh_ref[...]).astype(BF16)


def _inproj_kernel(x_ref, g_ref, sh_ref, sc_ref, w_ref, cos_ref, sin_ref, o_ref, hs_ref, *,
                   n_q, n_qk, rope):
    j = pl.program_id(2)

    @pl.when(j == 0)
    def _():
        hs_ref[...] = _norm_mod(x_ref, g_ref, sh_ref, sc_ref)

    tm, tn = o_ref.shape
    rc = min(tm, 256)

    def by_row_chunks(epilogue):
        for r in range(tm // rc):
            rows = pl.ds(r * rc, rc)
            acc = jnp.dot(hs_ref[rows, :], w_ref[...], preferred_element_type=F32)
            o_ref[rows, :] = epilogue(acc, rows).astype(o_ref.dtype)

    def qk_epilogue(a, rows):
        if rope:
            lane = lax.broadcasted_iota(jnp.int32, a.shape, 1)
            first = (lane & 31) < 16
            partner = jnp.where(first, pltpu.roll(a, tn - 16, 1), pltpu.roll(a, 16, 1))
            reps = tn // LANES
            a = (a * jnp.tile(cos_ref[rows, :], (1, reps))
                 + partner * jnp.tile(sin_ref[rows, :], (1, reps)))
        return jnp.where(j < n_q, a * DIFF_SCALE, a)

    @pl.when(j < n_qk)
    def _():
        by_row_chunks(qk_epilogue)

    @pl.when(j >= n_qk)
    def _():
        by_row_chunks(lambda a, rows: jax.nn.gelu(a))


def _inproj(x, g, sh, sc, w, cos_t, sin_t, rope):
    b, s, d = x.shape
    n = w.shape[1]
    tm = _tile(s, 1024)
    tn = 512
    kern = functools.partial(_inproj_kernel, n_q=Q_COLS // tn, n_qk=2 * Q_COLS // tn, rope=rope)
    return pl.pallas_call(
        kern,
        grid=(b, s // tm, n // tn),
        in_specs=[pl.BlockSpec((None, tm, d), lambda bb, i, j: (bb, i, 0)),
                  pl.BlockSpec((1, d), lambda bb, i, j: (0, 0)),
                  pl.BlockSpec((None, 1, d), lambda bb, i, j: (bb, 0, 0)),
                  pl.BlockSpec((None, 1, d), lambda bb, i, j: (bb, 0, 0)),
                  pl.BlockSpec((d, tn), lambda bb, i, j: (0, j)),
                  pl.BlockSpec((tm, LANES), lambda bb, i, j: (i, 0)),
                  pl.BlockSpec((tm, LANES), lambda bb, i, j: (i, 0))],
        out_specs=pl.BlockSpec((None, tm, tn), lambda bb, i, j: (bb, i, j)),
        out_shape=jax.ShapeDtypeStruct((b, s, n), BF16),
        scratch_shapes=[pltpu.VMEM((tm, d), BF16)],
        compiler_params=_params("parallel", "parallel", "arbitrary"),
        name="inproj",
    )(x, g, sh, sc, w, cos_t, sin_t)


def _inproj_nt_kernel(x_ref, g_ref, sh_ref, sc_ref, wt_ref, *rest, tl, conv):
    if conv:
        cw_ref, halo_ref, o_ref, hs_ref = rest
    else:
        o_ref, hs_ref = rest
    j = pl.program_id(2)

    @pl.when(j == 0)
    def _():
        hs_ref[...] = _norm_mod(x_ref, g_ref, sh_ref, sc_ref)

    n_chunks = o_ref.shape[0]
    if not conv:
        acc = lax.dot_general(wt_ref[...], hs_ref[...], NT_DIMS, preferred_element_type=F32)
        for c in range(n_chunks):
            o_ref[c] = acc[:, c * tl:(c + 1) * tl].astype(o_ref.dtype)
        return

    acc = jnp.concatenate(
        [lax.dot_general(wt_ref[...], hs_ref[c * tl:(c + 1) * tl, :], NT_DIMS, preferred_element_type=F32)
         for c in range(n_chunks)], axis=1)
    tn, tm = acc.shape
    halo = halo_ref[...]
    col = lax.broadcasted_iota(jnp.int32, halo.shape, 1)
    tile = pl.program_id(1)
    before = jnp.sum(jnp.where(col == 2 * tile, halo, 0.0), axis=1, keepdims=True)
    after = jnp.sum(jnp.where(col == 2 * tile + 1, halo, 0.0), axis=1, keepdims=True)
    lane = lax.broadcasted_iota(jnp.int32, (tn, LANES), 1)
    prev = pltpu.roll(acc, 1, 1)
    prev = jnp.concatenate([jnp.where(lane == 0, before, prev[:, :LANES]), prev[:, LANES:]], axis=1)
    nxt = pltpu.roll(acc, tm - 1, 1)
    nxt = jnp.concatenate([nxt[:, :tm - LANES],
                           jnp.where(lane == LANES - 1, after, nxt[:, tm - LANES:])], axis=1)

    def tap(k):
        return jnp.tile(cw_ref[:, k * LANES:(k + 1) * LANES], (1, tm // LANES))

    out = tap(3) + prev * tap(0) + acc * tap(1) + nxt * tap(2)
    for c in range(n_chunks):
        o_ref[c] = out[:, c * tl:(c + 1) * tl].astype(o_ref.dtype)


def _inproj_nt(x, g, sh, sc, wt, tl, conv_params=None, halo=None):
    b, s, d = x.shape
    n = wt.shape[0]
    tm = _tile(s, 1024)
    tn = 512
    conv = conv_params is not None
    kern = functools.partial(_inproj_nt_kernel, tl=tl, conv=conv)
    in_specs = [pl.BlockSpec((None, tm, d), lambda bb, i, j: (bb, i, 0)),
                pl.BlockSpec((1, d), lambda bb, i, j: (0, 0)),
                pl.BlockSpec((None, 1, d), lambda bb, i, j: (bb, 0, 0)),
                pl.BlockSpec((None, 1, d), lambda bb, i, j: (bb, 0, 0)),
                pl.BlockSpec((tn, d), lambda bb, i, j: (j, 0))]
    args = [x, g, sh, sc, wt]
    if conv:
        in_specs += [pl.BlockSpec((tn, 4 * LANES), lambda bb, i, j: (j, 0)),
                     pl.BlockSpec((None, tn, halo.shape[2]), lambda bb, i, j: (bb, j, 0))]
        args += [conv_params, halo]
    return pl.pallas_call(
        kern,
        grid=(b, s // tm, n // tn),
        in_specs=in_specs,
        out_specs=pl.BlockSpec((None, tm // tl, tn, tl), lambda bb, i, j: (bb, i, j, 0)),
        out_shape=jax.ShapeDtypeStruct((b, s // tl, n, tl), BF16),
        scratch_shapes=[pltpu.VMEM((tm, d), BF16)],
        compiler_params=_params("parallel", "parallel", "arbitrary"),
        name="inproj_nt",
    )(*args)


def _ffn_up_kernel(x_ref, g_ref, sh_ref, sc_ref, wg_ref, wu_ref, o_ref, hs_ref):
    j = pl.program_id(2)

    @pl.when(j == 0)
    def _():
        hs_ref[...] = _norm_mod(x_ref, g_ref, sh_ref, sc_ref)

    hs = hs_ref[...]
    gate = jnp.dot(hs, wg_ref[...], preferred_element_type=F32)
    up = jnp.dot(hs, wu_ref[...], preferred_element_type=F32)
    o_ref[...] = (gate * jax.nn.sigmoid(gate) * up).astype(o_ref.dtype)


def _ffn_up(x, g, sh, sc, wg, wu):
    b, s, d = x.shape
    n = wg.shape[1]
    tm = _tile(s, 1024)
    tn = 512
    return pl.pallas_call(
        _ffn_up_kernel,
        grid=(b, s // tm, n // tn),
        in_specs=[pl.BlockSpec((None, tm, d), lambda bb, i, j: (bb, i, 0)),
                  pl.BlockSpec((1, d), lambda bb, i, j: (0, 0)),
                  pl.BlockSpec((None, 1, d), lambda bb, i, j: (bb, 0, 0)),
                  pl.BlockSpec((None, 1, d), lambda bb, i, j: (bb, 0, 0)),
                  pl.BlockSpec((d, tn), lambda bb, i, j: (0, j)),
                  pl.BlockSpec((d, tn), lambda bb, i, j: (0, j))],
        out_specs=pl.BlockSpec((None, tm, tn), lambda bb, i, j: (bb, i, j)),
        out_shape=jax.ShapeDtypeStruct((b, s, n), BF16),
        scratch_shapes=[pltpu.VMEM((tm, d), BF16)],
        compiler_params=_params("parallel", "parallel", "arbitrary"),
        name="ffn_up",
    )(x, g, sh, sc, wg, wu)


def _proj_res_kernel(*refs, ksizes, transposed):
    n = len(ksizes)
    a_refs = refs[:n]
    w_ref, x_ref, gate_ref, o_ref = refs[n:]
    if transposed:
        (a_ref,) = a_refs
        tl = a_ref.shape[2]
        for c in range(a_ref.shape[0]):
            rows = slice(c * tl, (c + 1) * tl)
            acc = lax.dot_general(a_ref[c], w_ref[...], TN_DIMS, preferred_element_type=F32)
            o_ref[rows, :] = x_ref[rows, :] + gate_ref[...] * acc
        return
    acc = None
    off = 0
    for a_ref, ks in zip(a_refs, ksizes):
        part = jnp.dot(a_ref[...], w_ref[off:off + ks, :], preferred_element_type=F32)
        acc = part if acc is None else acc + part
        off += ks
    o_ref[...] = x_ref[...] + gate_ref[...] * acc


def _proj_res(a_list, w, x, gate, transposed=False, tm_pref=512, tn_pref=512, vmem=VMEM_LIMIT):
    b, s, d = x.shape
    ksizes = tuple(a.shape[2] for a in a_list)
    ktot = sum(ksizes)
    tm = _tile(s, tm_pref)
    tn = _tile(d, tn_pref)
    if transposed:
        tl = a_list[0].shape[3]
        a_specs = [pl.BlockSpec((None, tm // tl, ktot, tl), lambda bb, i, j: (bb, i, 0, 0))]
    else:
        a_specs = [pl.BlockSpec((None, tm, ks), lambda bb, i, j: (bb, i, 0)) for ks in ksizes]
    kern = functools.partial(_proj_res_kernel, ksizes=ksizes, transposed=transposed)
    return pl.pallas_call(
        kern,
        grid=(b, s // tm, d // tn),
        in_specs=a_specs + [pl.BlockSpec((ktot, tn), lambda bb, i, j: (0, j)),
                            pl.BlockSpec((None, tm, tn), lambda bb, i, j: (bb, i, j)),
                            pl.BlockSpec((None, 1, tn), lambda bb, i, j: (bb, 0, j))],
        out_specs=pl.BlockSpec((None, tm, tn), lambda bb, i, j: (bb, i, j)),
        out_shape=jax.ShapeDtypeStruct((b, s, d), F32),
        compiler_params=_params("parallel", "parallel", "parallel", vmem=vmem),
        name="proj_res",
    )(*a_list, w, x, gate)


def _attn_kernel(*refs, n_chunks, lam_init):
    if n_chunks:
        lam_ref, q_ref, kc_ref, vct_ref, k_ref, vt_ref, g_ref, o_ref, acc_ref = refs[:9]
        s_refs = refs[9:]
    else:
        lam_ref, q_ref, kc_ref, vct_ref, g_ref, o_ref, acc_ref = refs
    q = q_ref[...]
    tq = q.shape[0]
    dv = DIFF_V_DIM
    qm = (q[:, :DIFF_QK_DIM], q[:, DIFF_QK_DIM:])

    def scores(kblk):
        return tuple(lax.dot_general(kblk[:, m * DIFF_QK_DIM:(m + 1) * DIFF_QK_DIM], qm[m], NT_DIMS,
                                     preferred_element_type=F32) for m in range(2))

    def absorb(s_pair, vtblk, m_pair):
        vext = jnp.concatenate([vtblk, jnp.ones((ONES_ROWS, vtblk.shape[1]), BF16)], axis=0)
        out = []
        for m in range(2):
            m_old = m_pair[m]
            m_new = jnp.maximum(m_old, jnp.max(s_pair[m], axis=0, keepdims=True))
            alpha = jnp.exp(m_old - m_new)
            p = jnp.exp((s_pair[m] - m_new).astype(BF16))
            pv = jnp.dot(vext, p, preferred_element_type=F32)
            acc_ref[m] = alpha * acc_ref[m] + pv
            out.append(m_new)
        return tuple(out)

    def store(ref, s_pair):
        ref[0] = s_pair[0]
        ref[1] = s_pair[1]

    acc_ref[...] = jnp.zeros_like(acc_ref)
    init = jnp.full((1, tq), -1e30, F32)
    m_pair = absorb(scores(kc_ref[...]), vct_ref[0], (init, init))
    if n_chunks:
        group = len(s_refs) // 2
        sets = (s_refs[:group], s_refs[group:])
        assert n_chunks % (2 * group) == 0

        def half_trip(cur, nxt, base, mp, lookahead):
            for k in range(group):
                if lookahead:
                    store(nxt[k], scores(k_ref[base + group + k]))
                mp = absorb((cur[k][0], cur[k][1]), vt_ref[base + k], mp)
            return mp

        def trip(j, mp, lookahead):
            base = 2 * group * j
            mp = half_trip(sets[0], sets[1], base, mp, True)
            return half_trip(sets[1], sets[0], base + group, mp, lookahead)

        for k in range(group):
            store(sets[0][k], scores(k_ref[k]))
        n_trips = n_chunks // (2 * group)
        m_pair = lax.fori_loop(0, n_trips - 1, lambda j, mp: trip(j, mp, True), m_pair)
        m_pair = trip(n_trips - 1, m_pair, False)

    lp = lam_ref[...]
    lam = (jnp.exp(jnp.sum(lp[0:1] * lp[1:2], axis=-1, keepdims=True))
           - jnp.exp(jnp.sum(lp[2:3] * lp[3:4], axis=-1, keepdims=True)) + lam_init)
    acc0, acc1 = acc_ref[0], acc_ref[1]
    o = acc0[:dv] / acc0[dv:dv + 1] - lam * (acc1[:dv] / acc1[dv:dv + 1])
    ot = o.T
    ms = jnp.mean(ot * ot, axis=-1, keepdims=True)
    on = ot * lax.rsqrt(ms + SUBLN_EPS) * g_ref[...] * (1.0 - lam_init)
    o_ref[...] = on.astype(o_ref.dtype)


def _attention(lam_p, subln, q_arr, kc_arr, vct_arr, k4=None, vt4=None, *, lam_init):
    b, sq = q_arr.shape[0], q_arr.shape[1]
    sc = kc_arr.shape[1]
    h = DIFF_HEADS
    tq = _tile(sq, 256)
    dv = DIFF_V_DIM
    n_chunks = 0 if k4 is None else k4.shape[1]
    in_specs = [pl.BlockSpec((4, DIFF_QK_DIM), lambda bb, hh, i: (0, 0)),
                pl.BlockSpec((None, tq, dv), lambda bb, hh, i: (bb, i, hh)),
                pl.BlockSpec((None, sc, dv), lambda bb, hh, i: (bb, 0, h + hh)),
                pl.BlockSpec((None, 1, dv, sc), lambda bb, hh, i: (bb, 0, hh, 0))]
    args = [lam_p, q_arr, kc_arr, vct_arr]
    if n_chunks:
        tk = k4.shape[2]
        in_specs += [pl.BlockSpec((None, n_chunks, tk, dv), lambda bb, hh, i: (bb, 0, 0, h + hh)),
                     pl.BlockSpec((None, n_chunks, dv, tk), lambda bb, hh, i: (bb, 0, hh, 0))]
        args += [k4, vt4]
    in_specs.append(pl.BlockSpec((1, dv), lambda bb, hh, i: (0, 0)))
    args.append(subln)
    kern = functools.partial(_attn_kernel, n_chunks=n_chunks, lam_init=lam_init)
    scratch = [pltpu.VMEM((2, dv + ONES_ROWS, tq), F32)]
    if n_chunks:
        group = 4 if n_chunks % 16 == 0 else 2
        scratch += [pltpu.VMEM((2, tk, tq), F32) for _ in range(2 * group)]
    return pl.pallas_call(
        kern,
        grid=(b, h, sq // tq),
        in_specs=in_specs,
        out_specs=pl.BlockSpec((None, tq, dv), lambda bb, hh, i: (bb, i, hh)),
        out_shape=jax.ShapeDtypeStruct((b, sq, A_WIDTH), BF16),
        scratch_shapes=scratch,
        compiler_params=_params("parallel", "parallel", "parallel"),
        name="diff_attn",
    )(*args)


def _sgu_kernel(u_ref, g_ref, ng_ref, nb_ref, w_ref, bs_ref, o_ref):
    for gi in range(SGU_GROUPS):
        cols = slice(gi * SGU_CH, (gi + 1) * SGU_CH)
        w = w_ref[gi]
        for c in range(u_ref.shape[0] // SGU_CHUNK):
            sl = slice(c * SGU_CHUNK, (c + 1) * SGU_CHUNK)
            gg = g_ref[sl, cols].astype(F32)
            mu = jnp.mean(gg, axis=-1, keepdims=True)
            dev = gg - mu
            var = jnp.mean(dev * dev, axis=-1, keepdims=True)
            vv = dev * lax.rsqrt(var + LN_EPS) * ng_ref[gi] + nb_ref[gi]
            mixed = jnp.dot(w, vv.astype(BF16), preferred_element_type=F32) + bs_ref[gi]
            o_ref[sl, cols] = (u_ref[sl, cols].astype(F32) * mixed).astype(o_ref.dtype)


def _sgu(qkug, norm_g, norm_b, w_s, b_s):
    b, s = qkug.shape[0], qkug.shape[1]
    tm = _tile(s, 1024)
    gcount = SGU_GROUPS
    ublk = 2 * Q_COLS // B_WIDTH
    full = lambda shape: pl.BlockSpec(shape, lambda bb, i: (0,) * len(shape))
    return pl.pallas_call(
        _sgu_kernel,
        grid=(b, s // tm),
        in_specs=[pl.BlockSpec((None, tm, B_WIDTH), lambda bb, i: (bb, i, ublk)),
                  pl.BlockSpec((None, tm, B_WIDTH), lambda bb, i: (bb, i, ublk + 1)),
                  full((gcount, 1, SGU_CH)), full((gcount, 1, SGU_CH)),
                  full((gcount, SGU_CHUNK, SGU_CHUNK)), full((gcount, SGU_CHUNK, 1))],
        out_specs=pl.BlockSpec((None, tm, B_WIDTH), lambda bb, i: (bb, i, 0)),
        out_shape=jax.ShapeDtypeStruct((b, s, B_WIDTH), BF16),
        compiler_params=_params("parallel", "parallel"),
        name="sgu",
    )(qkug, qkug, norm_g.reshape(gcount, 1, SGU_CH), norm_b.reshape(gcount, 1, SGU_CH),
      w_s.astype(BF16), b_s.reshape(gcount, SGU_CHUNK, 1))


def _filter_mlp_kernel(f_ref, w1_ref, b1_ref, w2_ref, b2_ref, w3_ref, b3_ref, fr_ref, o_ref):
    def lin(a, w_ref, b_ref):
        return jnp.dot(a, w_ref[...], preferred_element_type=F32, precision=HIGHEST) + b_ref[...]
    fr = fr_ref[...]
    hcur = jnp.sin(fr[0:1] * lin(f_ref[...], w1_ref, b1_ref))
    hcur = jnp.sin(fr[1:2] * lin(hcur, w2_ref, b2_ref))
    o_ref[...] = jnp.sin(fr[2:3] * lin(hcur, w3_ref, b3_ref))


def _filter_mlp(feats, w1, b1, w2, b2, w3, b3, freq):
    rows, emb = feats.shape
    hid = FILTER_HIDDEN
    tr = _tile(rows, 2048)
    full = lambda shape: pl.BlockSpec(shape, lambda i: (0,) * len(shape))
    return pl.pallas_call(
        _filter_mlp_kernel,
        grid=(rows // tr,),
        in_specs=[pl.BlockSpec((tr, emb), lambda i: (i, 0)),
                  full((emb, hid)), full((1, hid)), full((hid, hid)), full((1, hid)),
                  full((hid, hid)), full((1, hid)), full((3, hid))],
        out_specs=pl.BlockSpec((tr, hid), lambda i: (i, 0)),
        out_shape=jax.ShapeDtypeStruct((rows, hid), F32),
        compiler_params=_params("parallel"),
        name="filter_mlp",
    )(feats, w1, b1.reshape(1, hid), w2, b2.reshape(1, hid), w3, b3.reshape(1, hid), freq)


def _filter_raw_kernel(w4t_ref, h_ref, t_ref, delta_ref, o_ref, *, zero_tile):
    rt = pl.program_id(2)

    def split(a):
        hi = a.astype(BF16)
        return hi, (a - hi.astype(F32)).astype(BF16)

    def nt(a, bm):
        return lax.dot_general(a, bm, NT_DIMS, preferred_element_type=F32)

    w_hi, w_lo = split(w4t_ref[...])
    h_hi, h_lo = split(h_ref[...])
    raw = nt(w_hi, h_hi) + (nt(w_hi, h_lo) + nt(w_lo, h_hi))
    raw = raw * jnp.exp(-(delta_ref[...] * t_ref[...]))
    for c in range(o_ref.shape[0]):
        o_ref[c] = raw[:, c * DFT_MINOR:(c + 1) * DFT_MINOR].astype(o_ref.dtype)

    @pl.when(rt == zero_tile)
    def _():
        col = lax.broadcasted_iota(jnp.int32, (raw.shape[0], DFT_MINOR), 1)
        o_ref[0] = jnp.where(col == 0, 0.0, raw[:, :DFT_MINOR]).astype(o_ref.dtype)


def _filter_raw(w4t, h3, t_row, deltas, seq):
    c = w4t.shape[2]
    rows = h3.shape[0]
    tr = _tile(rows // 2, 2048)
    tc = _tile(c, 512)
    half_tiles = seq // tr
    kern = functools.partial(_filter_raw_kernel, zero_tile=half_tiles)
    return pl.pallas_call(
        kern,
        grid=(HYENA_ORDER, c // tc, rows // tr),
        in_specs=[pl.BlockSpec((None, None, tc, FILTER_HIDDEN),
                               lambda n, ci, rt: (n, rt // half_tiles, ci, 0)),
                  pl.BlockSpec((tr, FILTER_HIDDEN), lambda n, ci, rt: (rt, 0)),
                  pl.BlockSpec((1, tr), lambda n, ci, rt: (0, rt)),
                  pl.BlockSpec((tc, 1), lambda n, ci, rt: (ci, 0))],
        out_specs=pl.BlockSpec((None, tr // DFT_MINOR, tc, DFT_MINOR), lambda n, ci, rt: (n, rt, ci, 0)),
        out_shape=jax.ShapeDtypeStruct((HYENA_ORDER, rows // DFT_MINOR, c, DFT_MINOR), BF16),
        compiler_params=_params("parallel", "parallel", "parallel"),
        name="filter_raw",
    )(w4t, h3, t_row, deltas)


def _cmul(ar, ai, br, bi):
    return ar * br - ai * bi, ar * bi + ai * br


def _store_complex(ref, c, n1, re, im):
    r0 = pl.multiple_of(c * n1, n1)
    ref[pl.ds(r0, n1), :DFT_MINOR] = re.astype(ref.dtype)
    ref[pl.ds(r0, n1), DFT_MINOR:] = im.astype(ref.dtype)


def _load_complex(ref, c, n1):
    r0 = pl.multiple_of(c * n1, n1)
    tile = ref[pl.ds(r0, n1), :]
    return tile[:, :DFT_MINOR], tile[:, DFT_MINOR:]


def _filter_spec_kernel(k_ref, f1_ref, twr_ref, twi_ref, f2_ref, o_ref, a2_ref, *, inv_n):
    tc, n1, n2 = k_ref.shape

    def left(c, carry):
        k = k_ref[c].astype(F32)
        nrm = jnp.sum(jnp.sum(jnp.abs(k), axis=1, keepdims=True), axis=0, keepdims=True)
        kn = (k * (inv_n / nrm)).astype(BF16)
        a = jnp.dot(f1_ref[...], kn, preferred_element_type=F32)
        ar, ai = _cmul(a[:n1], a[n1:], twr_ref[...], twi_ref[...])
        _store_complex(a2_ref, c, n1, ar, ai)
        return carry

    lax.fori_loop(0, tc, left, 0, unroll=4)
    z = jnp.dot(a2_ref[...], f2_ref[...], preferred_element_type=F32).reshape(tc, n1, 2 * n2)
    o_ref[:, 0] = z[:, :, :n2].astype(o_ref.dtype)
    o_ref[:, 1] = z[:, :, n2:].astype(o_ref.dtype)


def _filter_spec(k2, f1_full, twr, twi, f2e):
    nc, n1, n2 = k2.shape
    tc = 16
    kern = functools.partial(_filter_spec_kernel, inv_n=1.0 / (n1 * n2))
    full = lambda shape: pl.BlockSpec(shape, lambda i: (0,) * len(shape))
    return pl.pallas_call(
        kern,
        grid=(nc // tc,),
        in_specs=[pl.BlockSpec((tc, n1, n2), lambda i: (i, 0, 0)),
                  full((2 * n1, n1)), full((n1, n2)), full((n1, n2)), full((2 * n2, 2 * n2))],
        out_specs=pl.BlockSpec((tc, 2, n1, n2), lambda i: (i, 0, 0, 0)),
        out_shape=jax.ShapeDtypeStruct((nc, 2, n1, n2), BF16),
        scratch_shapes=[pltpu.VMEM((tc * n1, 2 * n2), BF16)],
        compiler_params=_params("parallel"),
        name="filter_spec",
    )(k2, f1_full, twr, twi, f2e)


def _hyena_kernel(zv_ref, z1_ref, z2_ref, kf_ref, fb_ref, f1_ref, twr_ref, twi_ref, f2_ref, f2c_ref,
                  g1_ref, o_ref, sig_ref, a2_ref, z_ref):
    nb, hr, tc, n2 = zv_ref.shape
    n1 = 2 * hr
    for part, ref in enumerate((zv_ref, z1_ref, z2_ref)):
        for bb in range(nb):
            sig_ref[part, bb] = pltpu.einshape("tcl->ctl", ref[bb].astype(F32))

    for n in range(HYENA_ORDER):
        def fwd_left(c, carry):
            xs = jnp.concatenate([sig_ref[0, 0, c], sig_ref[0, 1, c]], axis=0).astype(BF16)
            a = jnp.dot(f1_ref[...], xs, preferred_element_type=F32)
            ar, ai = _cmul(a[:n1], a[n1:], twr_ref[...], twi_ref[...])
            _store_complex(a2_ref, c, n1, ar, ai)
            return carry

        lax.fori_loop(0, tc, fwd_left, 0, unroll=4)
        z_ref[...] = jnp.dot(a2_ref[...], f2_ref[...], preferred_element_type=F32)

        def spectrum(c, carry):
            zr, zi = _load_complex(z_ref, c, n1)
            wr, wi = _cmul(zr, zi, kf_ref[n, c, 0].astype(F32), kf_ref[n, c, 1].astype(F32))
            _store_complex(a2_ref, c, n1, wr, wi)
            return carry

        lax.fori_loop(0, tc, spectrum, 0, unroll=4)
        z_ref[...] = jnp.dot(a2_ref[...], f2c_ref[...], preferred_element_type=F32)

        def inv_left(c, carry):
            br, bi = _load_complex(z_ref, c, n1)
            br, bi = _cmul(br, bi, twr_ref[...], -twi_ref[...])
            bs = jnp.concatenate([br, bi], axis=0).astype(BF16)
            y = jnp.dot(g1_ref[...], bs, preferred_element_type=F32)
            fb = fb_ref[pl.ds(c, 1), n:n + 1]
            for bb in range(nb):
                sig_ref[0, bb, c] = sig_ref[1 + n, bb, c] * (y[bb * hr:(bb + 1) * hr] + sig_ref[0, bb, c] * fb)
            return carry

        lax.fori_loop(0, tc, inv_left, 0, unroll=4)

    for bb in range(nb):
        o_ref[bb] = pltpu.einshape("ctl->tcl", sig_ref[0, bb]).astype(o_ref.dtype)


def _hyena_core(z4, kf, fbias, f1h, twr, twi, f2e, f2c, g1e):
    nb, hr, c3, n2 = z4.shape
    c = c3 // 3
    n1 = 2 * hr
    tc = 16
    nct = c // tc
    full = lambda shape: pl.BlockSpec(shape, lambda i: (0,) * len(shape))
    zspec = lambda part: pl.BlockSpec((nb, hr, tc, n2), lambda i: (0, 0, i + part * nct, 0))
    return pl.pallas_call(
        _hyena_kernel,
        grid=(nct,),
        in_specs=[zspec(0), zspec(1), zspec(2),
                  pl.BlockSpec((HYENA_ORDER, tc, 2, n1, n2), lambda i: (0, i, 0, 0, 0)),
                  pl.BlockSpec((tc, HYENA_ORDER), lambda i: (i, 0)),
                  full((2 * n1, n1)), full((n1, n2)), full((n1, n2)),
                  full((2 * n2, 2 * n2)), full((2 * n2, 2 * n2)), full((n1, 2 * n1))],
        out_specs=pl.BlockSpec((nb, hr, tc, n2), lambda i: (0, 0, i, 0)),
        out_shape=jax.ShapeDtypeStruct((nb, hr, c, n2), BF16),
        scratch_shapes=[pltpu.VMEM((3, nb, tc, hr, n2), F32),
                        pltpu.VMEM((tc * n1, 2 * n2), BF16),
                        pltpu.VMEM((tc * n1, 2 * n2), F32)],
        compiler_params=_params("parallel"),
        name="hyena_core",
    )(z4, z4, z4, kf, fbias, f1h, twr, twi, f2e, f2c, g1e)


def _dft_tables(n1):
    n2 = DFT_MINOR
    n = n1 * n2
    a1 = 2.0 * np.pi * np.outer(np.arange(n1), np.arange(n1)) / n1
    f1r, f1i = np.cos(a1), -np.sin(a1)
    a2 = 2.0 * np.pi * np.outer(np.arange(n2), np.arange(n2)) / n2
    f2r, f2i = np.cos(a2), -np.sin(a2)
    at = 2.0 * np.pi * np.outer(np.arange(n1), np.arange(n2)) / n
    twr, twi = np.cos(at), -np.sin(at)
    hr = n1 // 2
    f1_full = np.concatenate([f1r, f1i], axis=0)
    f1_half = np.block([[f1r[:, :hr], -f1i[:, :hr]], [f1i[:, :hr], f1r[:, :hr]]])
    f2e = np.block([[f2r, f2i], [-f2i, f2r]])
    f2c = np.block([[f2r, -f2i], [f2i, f2r]])
    gr, gi = f1r[:hr, :], -f1i[:hr, :]
    g1e = np.block([[gr, -gi], [gi, gr]])
    bf = lambda m: jnp.asarray(m, F32).astype(BF16)
    return dict(f1_full=bf(f1_full), f1_half=bf(f1_half), f2e=bf(f2e), f2c=bf(f2c), g1e=bf(g1e),
                twr=jnp.asarray(twr, F32), twi=jnp.asarray(twi, F32))


def _hyena_spectra(seq, fw1, fb1, fw2, fb2, fw3, fb3, ffreq, fw4, tabs):
    l = seq
    width = fw4.shape[1] // (2 * HYENA_ORDER)
    t_fwd = jnp.linspace(0.0, 1.0, l, dtype=F32)
    w = 2.0 * math.pi * jnp.arange(l, dtype=F32)[:, None] / l
    f = jnp.linspace(1e-4, FILTER_BANDS - 1, FILTER_BANDS, dtype=F32)[None, :]
    feats = jnp.concatenate([t_fwd[:, None], jnp.cos(f * w), -jnp.sin(f * w)], axis=-1)

    def two_sided(a):
        return jnp.concatenate([a, a[:1], a[1:][::-1]], axis=0)

    feats = two_sided(feats)
    t_lin = two_sided(t_fwd)
    emb_pad = FILTER_HIDDEN
    feats = jnp.pad(feats, ((0, 0), (0, emb_pad - FILTER_EMB)))
    w1p = jnp.pad(fw1, ((0, emb_pad - FILTER_EMB), (0, 0)))
    h3 = _filter_mlp(feats, w1p, fb1, fw2, fb2, fw3, fb3, ffreq)
    w4t = fw4.reshape(FILTER_HIDDEN, HYENA_ORDER, 2, width).transpose(1, 2, 3, 0)
    deltas = jnp.abs(jnp.linspace(MIN_DECAY, MAX_DECAY, width, dtype=F32))[:, None]
    raw = _filter_raw(w4t, h3, t_lin[None, :], deltas, l)
    n1 = raw.shape[1]
    k2 = raw.transpose(0, 2, 1, 3).reshape(HYENA_ORDER * width, n1, DFT_MINOR)
    kf = _filter_spec(k2, tabs["f1_full"], tabs["twr"], tabs["twi"], tabs["f2e"])
    return kf.reshape(HYENA_ORDER, width, 2, n1, DFT_MINOR)


def _rms_kernel(x_ref, g_ref, o_ref):
    x = x_ref[...]
    ms = jnp.mean(x * x, axis=-1, keepdims=True)
    o_ref[...] = x * lax.rsqrt(ms + NORM_EPS) * g_ref[...]


def _final_norm(x, g):
    b, s, d = x.shape
    tm = _tile(s, 1024)
    return pl.pallas_call(
        _rms_kernel,
        grid=(b, s // tm),
        in_specs=[pl.BlockSpec((None, tm, d), lambda bb, i: (bb, i, 0)),
                  pl.BlockSpec((1, d), lambda bb, i: (0, 0))],
        out_specs=pl.BlockSpec((None, tm, d), lambda bb, i: (bb, i, 0)),
        out_shape=jax.ShapeDtypeStruct((b, s, d), F32),
        compiler_params=_params("parallel", "parallel"),
        name="final_norm",
    )(x, g.reshape(1, d))


def _rope_tables(n_tokens):
    tok = jnp.arange(n_tokens)
    row = (tok // GRID_W).astype(F32)
    col = (tok % GRID_W).astype(F32)
    half = ROPE_AXIS_DIM // 2
    inv = 1.0 / (ROPE_THETA ** (jnp.arange(0, ROPE_AXIS_DIM, 2, dtype=F32) / ROPE_AXIS_DIM))
    ang_r = row[:, None] * inv
    ang_c = col[:, None] * inv
    cos64 = jnp.concatenate([jnp.cos(ang_r), jnp.cos(ang_r), jnp.cos(ang_c), jnp.cos(ang_c)], axis=-1)
    sin64 = jnp.concatenate([-jnp.sin(ang_r), jnp.sin(ang_r), -jnp.sin(ang_c), jnp.sin(ang_c)], axis=-1)
    assert cos64.shape[1] == 4 * half == DIFF_QK_DIM
    return jnp.tile(cos64, (1, 2)), jnp.tile(sin64, (1, 2))


def _ffn(x, norm_g, sh, sc, gate, wg, wu, wd):
    hidden = _ffn_up(x, norm_g, sh, sc, wg, wu)
    return _proj_res([hidden], wd, x, gate, tm_pref=1024, vmem=VMEM_LIMIT_WIDE)


def _even_layer(x, xc, mods, cmods, n1g, n2g, w_in, w_out, lam_p, subln, sgu_ng, sgu_nb, sgu_w, sgu_b,
                wg, wu, wd, layer_idx):
    b, s, d = x.shape
    lam_init = 0.8 - 0.6 * math.exp(-0.3 * layer_idx)
    sh1, sc1, g1, sh2, sc2, g2 = mods
    csh1, csc1, cg1, csh2, csc2, cg2 = cmods
    o_k, o_v, o_u = Q_COLS, 2 * Q_COLS, 2 * Q_COLS + A_WIDTH
    w_qkug = jnp.concatenate([w_in[:, :o_v], w_in[:, o_u:]], axis=1).astype(BF16)
    w_vt = w_in[:, o_v:o_u].T.astype(BF16)
    wo = w_out.astype(BF16)
    n1g2 = n1g.reshape(1, d)
    cos_t, sin_t = _rope_tables(s)
    sc_len = xc.shape[1]

    qkug = _inproj(x, n1g2, sh1, sc1, w_qkug, cos_t, sin_t, rope=True)
    cqkug = _inproj(xc, n1g2, csh1, csc1, w_qkug, cos_t[:sc_len], sin_t[:sc_len], rope=False)
    tk = _tile(s, 512)
    vt4 = _inproj_nt(x, n1g2, sh1, sc1, w_vt, tk)
    cvt4 = _inproj_nt(xc, n1g2, csh1, csc1, w_vt, sc_len)
    k4 = qkug.reshape(b, s // tk, tk, qkug.shape[2])
    subln2 = subln.reshape(1, DIFF_V_DIM)

    a_l = _attention(lam_p, subln2, qkug, cqkug, cvt4, k4, vt4, lam_init=lam_init)
    s_l = _sgu(qkug, sgu_ng, sgu_nb, sgu_w, sgu_b)
    x = _proj_res([a_l, s_l], wo, x, g1, tn_pref=d)
    x = _ffn(x, n2g.reshape(1, d), sh2, sc2, g2, wg, wu, wd)

    a_c = _attention(lam_p, subln2, cqkug, cqkug, cvt4, lam_init=lam_init)
    s_c = _sgu(cqkug, sgu_ng, sgu_nb, sgu_w, sgu_b)
    xc = _proj_res([a_c, s_c], wo, xc, cg1, tn_pref=d)
    xc = _ffn(xc, n2g.reshape(1, d), csh2, csc2, cg2, wg, wu, wd)
    return x, xc


def _odd_layer(x, mods, n1g, n2g, w_in, conv_w, conv_b, fw1, fb1, fw2, fb2, fw3, fb3, ffreq, fw4, fbias,
               w_out, wg, wu, wd):
    b, s, d = x.shape
    sh1, sc1, g1, sh2, sc2, g2 = mods
    width = w_out.shape[0]
    n1 = 2 * s // DFT_MINOR
    tabs = _dft_tables(n1)
    kf = _hyena_spectra(s, fw1, fb1, fw2, fb2, fw3, fb3, ffreq, fw4, tabs)

    n1g2 = n1g.reshape(1, d)
    wt = w_in.T.astype(BF16)
    tm = _tile(s, 1024)
    n_tiles = s // tm
    x_tiles = x.reshape(b, n_tiles, tm, d)
    firsts, lasts = x_tiles[:, :, 0, :], x_tiles[:, :, tm - 1, :]
    before = jnp.concatenate([lasts[:, :1], lasts[:, :-1]], axis=1)
    after = jnp.concatenate([firsts[:, 1:], firsts[:, -1:]], axis=1)
    x_edge = jnp.stack([before, after], axis=2).reshape(b, 2 * n_tiles, d)
    z_edge = _inproj_nt(x_edge, n1g2, sh1, sc1, wt, 2 * n_tiles)
    inside = jnp.ones((2 * n_tiles,), F32).at[0].set(0.0).at[2 * n_tiles - 1].set(0.0)
    halo = z_edge[:, 0].astype(F32) * inside
    conv_params = jnp.repeat(jnp.concatenate([conv_w.T, conv_b[:, None]], axis=1), LANES, axis=1)

    z4 = _inproj_nt(x, n1g2, sh1, sc1, wt, DFT_MINOR, conv_params, halo)
    y4 = _hyena_core(z4, kf, fbias.T, tabs["f1_half"], tabs["twr"], tabs["twi"], tabs["f2e"], tabs["f2c"],
                     tabs["g1e"])
    x = _proj_res([y4], w_out.astype(BF16), x, g1, transposed=True, tn_pref=d)
    return _ffn(x, n2g.reshape(1, d), sh2, sc2, g2, wg, wu, wd)


def kernel(x, c, ctx, c_ctx, ada_w, ada_b, norm1, norm2, ffn_w_gate, ffn_w_up, ffn_w_down, e_w_in, e_w_out, e_lambda, e_subln, e_sgu_norm_g, e_sgu_norm_b, e_sgu_w, e_sgu_b, o_w_in, o_conv_w, o_conv_b, o_filt_w1, o_filt_b1, o_filt_w2, o_filt_b2, o_filt_w3, o_filt_b3, o_filt_freq, o_filt_w4, o_filt_bias, o_w_out, final_norm):
    b, s, d = x.shape
    depth = ada_w.shape[0]
    assert b == 2, "the long convolution packs exactly two batches into one complex signal"
    assert depth == 2, "odd layers here never carry the context stream"
    cond_t = jnp.zeros((d, 8), F32).at[:, :b].set(c.T).at[:, b].set(c_ctx)
    mod_all = _adaln_all(cond_t, b + 1, ada_w, ada_b)
    xc = ctx
    for i in range(depth):
        j = i // 2
        parts = jnp.split(mod_all[i], 6, axis=-1)
        mods = [p[:b, None, :] for p in parts]
        cmods = [jnp.broadcast_to(p[b:b + 1, None, :], (b, 1, d)) for p in parts]
        wg, wu, wd = (ffn_w_gate[i].astype(BF16), ffn_w_up[i].astype(BF16), ffn_w_down[i].astype(BF16))
        if i % 2 == 0:
            x, xc = _even_layer(x, xc, mods, cmods, norm1[i], norm2[i], e_w_in[j], e_w_out[j], e_lambda[j],
                                e_subln[j], e_sgu_norm_g[j], e_sgu_norm_b[j], e_sgu_w[j], e_sgu_b[j],
                                wg, wu, wd, i)
        else:
            x = _odd_layer(x, mods, norm1[i], norm2[i], o_w_in[j], o_conv_w[j], o_conv_b[j], o_filt_w1[j],
                           o_filt_b1[j], o_filt_w2[j], o_filt_b2[j], o_filt_w3[j], o_filt_b3[j],
                           o_filt_freq[j], o_filt_w4[j], o_filt_bias[j], o_w_out[j], wg, wu, wd)
    return _final_norm(x, final_norm)
```

```python
import functools
import math

import numpy as np
import jax
import jax.numpy as jnp
from jax import lax
from jax.experimental import pallas as pl
from jax.experimental.pallas import tpu as pltpu

F32 = jnp.float32
BF16 = jnp.bfloat16
HIGHEST = lax.Precision.HIGHEST

GRID_W = 64
NORM_EPS = 1e-6
DIFF_HEADS = 8
DIFF_QK_DIM = 64
DIFF_V_DIM = 2 * DIFF_QK_DIM
DIFF_SCALE = DIFF_QK_DIM ** -0.5
A_WIDTH = DIFF_HEADS * DIFF_V_DIM
Q_COLS = DIFF_HEADS * 2 * DIFF_QK_DIM
ROPE_THETA = 10000.0
ROPE_AXIS_DIM = DIFF_QK_DIM // 2
SUBLN_EPS = 1e-5
SGU_GROUPS = 8
SGU_CHUNK = 128
SGU_CH = 128
B_WIDTH = SGU_GROUPS * SGU_CH
LN_EPS = 1e-5
HYENA_ORDER = 2
SHORT_CONV = 3
FILTER_EMB = 33
FILTER_BANDS = (FILTER_EMB - 1) // 2
FILTER_HIDDEN = 64
DECAY_TARGET = 1e-2
MAX_DECAY = math.log(DECAY_TARGET) / 0.3
MIN_DECAY = math.log(DECAY_TARGET) / 1.5

LANES = 128
ONES_ROWS = 16
ATTN_SUB = 256
DFT_MINOR = 256
VMEM_LIMIT = 48 * 1024 * 1024
VMEM_LIMIT_WIDE = 56 * 1024 * 1024

NT_DIMS = (((1,), (1,)), ((), ()))
TN_DIMS = (((0,), (0,)), ((), ()))


def _params(*sem, vmem=VMEM_LIMIT):
    return pltpu.CompilerParams(dimension_semantics=sem, vmem_limit_bytes=vmem)


def _tile(n, pref):
    return pref if n % pref == 0 else n


def _adaln_kernel(ct_ref, w_ref, b_ref, o_ref, *, n_rows):
    k = pl.program_id(1)
    a = ct_ref[...]
    a = a * jax.nn.sigmoid(a)
    w = w_ref[...]
    rows = [jnp.sum(w * a[:, r:r + 1], axis=0, keepdims=True) for r in range(n_rows)]
    rows.append(jnp.zeros((o_ref.shape[0] - n_rows, w.shape[1]), F32))
    part = jnp.concatenate(rows, axis=0)

    @pl.when(k == 0)
    def _():
        valid = lax.broadcasted_iota(jnp.int32, part.shape, 0) < n_rows
        o_ref[...] = part + jnp.where(valid, b_ref[...], 0.0)

    @pl.when(k > 0)
    def _():
        o_ref[...] += part


def _adaln_all(cond_t, n_rows, ada_w, ada_b):
    depth, d, n6 = ada_w.shape
    tk = _tile(d, 256)
    return pl.pallas_call(
        functools.partial(_adaln_kernel, n_rows=n_rows),
        grid=(depth, d // tk),
        in_specs=[pl.BlockSpec((tk, 8), lambda l, k: (k, 0)),
                  pl.BlockSpec((None, tk, n6), lambda l, k: (l, k, 0)),
                  pl.BlockSpec((None, 1, n6), lambda l, k: (l, 0, 0))],
        out_specs=pl.BlockSpec((None, 8, n6), lambda l, k: (l, 0, 0)),
        out_shape=jax.ShapeDtypeStruct((depth, 8, n6), F32),
        compiler_params=_params("parallel", "arbitrary"),
        name="adaln",
    )(cond_t, ada_w, ada_b.reshape(depth, 1, n6))


def _norm_mod(x_ref, g_ref, sh_ref, sc_ref):
    x = x_ref[...]
    ms = jnp.mean(x * x, axis=-1, keepdims=True)
    y = x * lax.rsqrt(ms + NORM_EPS) * g_ref[...]
    return (y * (1.0 + sc_ref[...]) + sh_ref[...]).astype(BF16)


def _inproj_kernel(x_ref, g_ref, sh_ref, sc_ref, w_ref, cos_ref, sin_ref, o_ref, hs_ref, *,
                   n_q, n_qk, rope):
    j = pl.program_id(2)

    @pl.when(j == 0)
    def _():
        hs_ref[...] = _norm_mod(x_ref, g_ref, sh_ref, sc_ref)

    tm, tn = o_ref.shape
    rc = min(tm, 256)

    def by_row_chunks(epilogue):
        for r in range(tm // rc):
            rows = pl.ds(r * rc, rc)
            acc = jnp.dot(hs_ref[rows, :], w_ref[...], preferred_element_type=F32)
            o_ref[rows, :] = epilogue(acc, rows).astype(o_ref.dtype)

    def qk_epilogue(a, rows):
        if rope:
            lane = lax.broadcasted_iota(jnp.int32, a.shape, 1)
            first = (lane & 31) < 16
            partner = jnp.where(first, pltpu.roll(a, tn - 16, 1), pltpu.roll(a, 16, 1))
            reps = tn // LANES
            a = (a * jnp.tile(cos_ref[rows, :], (1, reps))
                 + partner * jnp.tile(sin_ref[rows, :], (1, reps)))
        return jnp.where(j < n_q, a * DIFF_SCALE, a)

    @pl.when(j < n_qk)
    def _():
        by_row_chunks(qk_epilogue)

    @pl.when(j >= n_qk)
    def _():
        by_row_chunks(lambda a, rows: jax.nn.gelu(a))


def _inproj(x, g, sh, sc, w, cos_t, sin_t, rope):
    b, s, d = x.shape
    n = w.shape[1]
    tm = _tile(s, 1024)
    tn = 512
    kern = functools.partial(_inproj_kernel, n_q=Q_COLS // tn, n_qk=2 * Q_COLS // tn, rope=rope)
    return pl.pallas_call(
        kern,
        grid=(b, s // tm, n // tn),
        in_specs=[pl.BlockSpec((None, tm, d), lambda bb, i, j: (bb, i, 0)),
                  pl.BlockSpec((1, d), lambda bb, i, j: (0, 0)),
                  pl.BlockSpec((None, 1, d), lambda bb, i, j: (bb, 0, 0)),
                  pl.BlockSpec((None, 1, d), lambda bb, i, j: (bb, 0, 0)),
                  pl.BlockSpec((d, tn), lambda bb, i, j: (0, j)),
                  pl.BlockSpec((tm, LANES), lambda bb, i, j: (i, 0)),
                  pl.BlockSpec((tm, LANES), lambda bb, i, j: (i, 0))],
        out_specs=pl.BlockSpec((None, tm, tn), lambda bb, i, j: (bb, i, j)),
        out_shape=jax.ShapeDtypeStruct((b, s, n), BF16),
        scratch_shapes=[pltpu.VMEM((tm, d), BF16)],
        compiler_params=_params("parallel", "parallel", "arbitrary"),
        name="inproj",
    )(x, g, sh, sc, w, cos_t, sin_t)


def _inproj_nt_kernel(x_ref, g_ref, sh_ref, sc_ref, wt_ref, *rest, tl, conv):
    if conv:
        cw_ref, halo_ref, o_ref, hs_ref = rest
    else:
        o_ref, hs_ref = rest
    j = pl.program_id(2)

    @pl.when(j == 0)
    def _():
        hs_ref[...] = _norm_mod(x_ref, g_ref, sh_ref, sc_ref)

    n_chunks = o_ref.shape[0]
    if not conv:
        acc = lax.dot_general(wt_ref[...], hs_ref[...], NT_DIMS, preferred_element_type=F32)
        for c in range(n_chunks):
            o_ref[c] = acc[:, c * tl:(c + 1) * tl].astype(o_ref.dtype)
        return

    acc = jnp.concatenate(
        [lax.dot_general(wt_ref[...], hs_ref[c * tl:(c + 1) * tl, :], NT_DIMS, preferred_element_type=F32)
         for c in range(n_chunks)], axis=1)
    tn, tm = acc.shape
    halo = halo_ref[...]
    col = lax.broadcasted_iota(jnp.int32, halo.shape, 1)
    tile = pl.program_id(1)
    before = jnp.sum(jnp.where(col == 2 * tile, halo, 0.0), axis=1, keepdims=True)
    after = jnp.sum(jnp.where(col == 2 * tile + 1, halo, 0.0), axis=1, keepdims=True)
    lane = lax.broadcasted_iota(jnp.int32, (tn, LANES), 1)
    prev = pltpu.roll(acc, 1, 1)
    prev = jnp.concatenate([jnp.where(lane == 0, before, prev[:, :LANES]), prev[:, LANES:]], axis=1)
    nxt = pltpu.roll(acc, tm - 1, 1)
    nxt = jnp.concatenate([nxt[:, :tm - LANES],
                           jnp.where(lane == LANES - 1, after, nxt[:, tm - LANES:])], axis=1)

    def tap(k):
        return jnp.tile(cw_ref[:, k * LANES:(k + 1) * LANES], (1, tm // LANES))

    out = tap(3) + prev * tap(0) + acc * tap(1) + nxt * tap(2)
    for c in range(n_chunks):
        o_ref[c] = out[:, c * tl:(c + 1) * tl].astype(o_ref.dtype)


def _inproj_nt(x, g, sh, sc, wt, tl, conv_params=None, halo=None):
    b, s, d = x.shape
    n = wt.shape[0]
    tm = _tile(s, 1024)
    tn = 512
    conv = conv_params is not None
    kern = functools.partial(_inproj_nt_kernel, tl=tl, conv=conv)
    in_specs = [pl.BlockSpec((None, tm, d), lambda bb, i, j: (bb, i, 0)),
                pl.BlockSpec((1, d), lambda bb, i, j: (0, 0)),
                pl.BlockSpec((None, 1, d), lambda bb, i, j: (bb, 0, 0)),
                pl.BlockSpec((None, 1, d), lambda bb, i, j: (bb, 0, 0)),
                pl.BlockSpec((tn, d), lambda bb, i, j: (j, 0))]
    args = [x, g, sh, sc, wt]
    if conv:
        in_specs += [pl.BlockSpec((tn, 4 * LANES), lambda bb, i, j: (j, 0)),
                     pl.BlockSpec((None, tn, halo.shape[2]), lambda bb, i, j: (bb, j, 0))]
        args += [conv_params, halo]
    return pl.pallas_call(
        kern,
        grid=(b, s // tm, n // tn),
        in_specs=in_specs,
        out_specs=pl.BlockSpec((None, tm // tl, tn, tl), lambda bb, i, j: (bb, i, j, 0)),
        out_shape=jax.ShapeDtypeStruct((b, s // tl, n, tl), BF16),
        scratch_shapes=[pltpu.VMEM((tm, d), BF16)],
        compiler_params=_params("parallel", "parallel", "arbitrary"),
        name="inproj_nt",
    )(*args)


def _ffn_up_kernel(x_ref, g_ref, sh_ref, sc_ref, wg_ref, wu_ref, o_ref, hs_ref):
    j = pl.program_id(2)

    @pl.when(j == 0)
    def _():
        hs_ref[...] = _norm_mod(x_ref, g_ref, sh_ref, sc_ref)

    hs = hs_ref[...]
    gate = jnp.dot(hs, wg_ref[...], preferred_element_type=F32)
    up = jnp.dot(hs, wu_ref[...], preferred_element_type=F32)
    o_ref[...] = (gate * jax.nn.sigmoid(gate) * up).astype(o_ref.dtype)


def _ffn_up(x, g, sh, sc, wg, wu):
    b, s, d = x.shape
    n = wg.shape[1]
    tm = _tile(s, 1024)
    tn = 512
    return pl.pallas_call(
        _ffn_up_kernel,
        grid=(b, s // tm, n // tn),
        in_specs=[pl.BlockSpec((None, tm, d), lambda bb, i, j: (bb, i, 0)),
                  pl.BlockSpec((1, d), lambda bb, i, j: (0, 0)),
                  pl.BlockSpec((None, 1, d), lambda bb, i, j: (bb, 0, 0)),
                  pl.BlockSpec((None, 1, d), lambda bb, i, j: (bb, 0, 0)),
                  pl.BlockSpec((d, tn), lambda bb, i, j: (0, j)),
                  pl.BlockSpec((d, tn), lambda bb, i, j: (0, j))],
        out_specs=pl.BlockSpec((None, tm, tn), lambda bb, i, j: (bb, i, j)),
        out_shape=jax.ShapeDtypeStruct((b, s, n), BF16),
        scratch_shapes=[pltpu.VMEM((tm, d), BF16)],
        compiler_params=_params("parallel", "parallel", "arbitrary"),
        name="ffn_up",
    )(x, g, sh, sc, wg, wu)


def _proj_res_kernel(*refs, ksizes, transposed):
    n = len(ksizes)
    a_refs = refs[:n]
    w_ref, x_ref, gate_ref, o_ref = refs[n:]
    if transposed:
        (a_ref,) = a_refs
        tl = a_ref.shape[2]
        for c in range(a_ref.shape[0]):
            rows = slice(c * tl, (c + 1) * tl)
            acc = lax.dot_general(a_ref[c], w_ref[...], TN_DIMS, preferred_element_type=F32)
            o_ref[rows, :] = x_ref[rows, :] + gate_ref[...] * acc
        return
    acc = None
    off = 0
    for a_ref, ks in zip(a_refs, ksizes):
        part = jnp.dot(a_ref[...], w_ref[off:off + ks, :], preferred_element_type=F32)
        acc = part if acc is None else acc + part
        off += ks
    o_ref[...] = x_ref[...] + gate_ref[...] * acc


def _proj_res(a_list, w, x, gate, transposed=False, tm_pref=512, tn_pref=512, vmem=VMEM_LIMIT):
    b, s, d = x.shape
    ksizes = tuple(a.shape[2] for a in a_list)
    ktot = sum(ksizes)
    tm = _tile(s, tm_pref)
    tn = _tile(d, tn_pref)
    if transposed:
        tl = a_list[0].shape[3]
        a_specs = [pl.BlockSpec((None, tm // tl, ktot, tl), lambda bb, i, j: (bb, i, 0, 0))]
    else:
        a_specs = [pl.BlockSpec((None, tm, ks), lambda bb, i, j: (bb, i, 0)) for ks in ksizes]
    kern = functools.partial(_proj_res_kernel, ksizes=ksizes, transposed=transposed)
    return pl.pallas_call(
        kern,
        grid=(b, s // tm, d // tn),
        in_specs=a_specs + [pl.BlockSpec((ktot, tn), lambda bb, i, j: (0, j)),
                            pl.BlockSpec((None, tm, tn), lambda bb, i, j: (bb, i, j)),
                            pl.BlockSpec((None, 1, tn), lambda bb, i, j: (bb, 0, j))],
        out_specs=pl.BlockSpec((None, tm, tn), lambda bb, i, j: (bb, i, j)),
        out_shape=jax.ShapeDtypeStruct((b, s, d), F32),
        compiler_params=_params("parallel", "parallel", "parallel", vmem=vmem),
        name="proj_res",
    )(*a_list, w, x, gate)


def _attn_kernel(*refs, n_chunks, lam_init):
    if n_chunks:
        lam_ref, q_ref, kc_ref, vct_ref, k_ref, vt_ref, g_ref, o_ref, acc_ref = refs[:9]
        n_slots = (len(refs) - 9) // 2
        slots = tuple(zip(refs[9:9 + n_slots], refs[9 + n_slots:]))
    else:
        lam_ref, q_ref, kc_ref, vct_ref, g_ref, o_ref, acc_ref = refs
    n_sub = acc_ref.shape[0]
    tq = acc_ref.shape[-1]
    dv = DIFF_V_DIM
    q = q_ref[...]
    qm = [(q[s * tq:(s + 1) * tq, :DIFF_QK_DIM], q[s * tq:(s + 1) * tq, DIFF_QK_DIM:]) for s in range(n_sub)]

    def scores(kblk, sub):
        s_pair = tuple(lax.dot_general(kblk[:, m * DIFF_QK_DIM:(m + 1) * DIFF_QK_DIM], qm[sub][m], NT_DIMS,
                                       preferred_element_type=F32) for m in range(2))
        return s_pair, tuple(jnp.max(s, axis=0, keepdims=True) for s in s_pair)

    def with_ones(vtblk):
        return jnp.concatenate([vtblk, jnp.ones((ONES_ROWS, vtblk.shape[1]), BF16)], axis=0)

    def absorb(scored, vext, m_pair, sub):
        s_pair, smax_pair = scored
        out = []
        for m in range(2):
            m_old = m_pair[m]
            m_new = jnp.maximum(m_old, smax_pair[m])
            alpha = jnp.exp(m_old - m_new)
            p = jnp.exp((s_pair[m] - m_new).astype(BF16))
            pv = jnp.dot(vext, p, preferred_element_type=F32)
            acc_ref[sub, m] = alpha * acc_ref[sub, m] + pv
            out.append(m_new)
        return tuple(out)

    def store(slot, scored):
        for m in range(2):
            slot[0][m] = scored[0][m]
            slot[1][m] = scored[1][m]

    def load(slot):
        return (slot[0][0], slot[0][1]), (slot[1][0], slot[1][1])

    acc_ref[...] = jnp.zeros_like(acc_ref)
    init = jnp.full((1, tq), -1e30, F32)
    kc, vc = kc_ref[...], with_ones(vct_ref[0])
    m_state = tuple(absorb(scores(kc, s), vc, (init, init), s) for s in range(n_sub))
    if n_chunks:
        group = len(slots) // (2 * n_sub)
        assert n_chunks % (2 * group) == 0

        def slot(half, k, sub):
            return slots[(half * group + k) * n_sub + sub]

        def half_trip(half, base, ms, lookahead):
            ms = list(ms)
            for k in range(group):
                vext = with_ones(vt_ref[base + k])
                for s in range(n_sub):
                    if lookahead:
                        store(slot(1 - half, k, s), scores(k_ref[base + group + k], s))
                    ms[s] = absorb(load(slot(half, k, s)), vext, ms[s], s)
            return tuple(ms)

        def trip(j, ms, lookahead):
            base = 2 * group * j
            ms = half_trip(0, base, ms, True)
            return half_trip(1, base + group, ms, lookahead)

        for k in range(group):
            for s in range(n_sub):
                store(slot(0, k, s), scores(k_ref[k], s))
        n_trips = n_chunks // (2 * group)
        m_state = lax.fori_loop(0, n_trips - 1, lambda j, ms: trip(j, ms, True), m_state)
        m_state = trip(n_trips - 1, m_state, False)

    lp = lam_ref[...]
    lam = (jnp.exp(jnp.sum(lp[0:1] * lp[1:2], axis=-1, keepdims=True))
           - jnp.exp(jnp.sum(lp[2:3] * lp[3:4], axis=-1, keepdims=True)) + lam_init)
    for s in range(n_sub):
        acc0, acc1 = acc_ref[s, 0], acc_ref[s, 1]
        o = acc0[:dv] / acc0[dv:dv + 1] - lam * (acc1[:dv] / acc1[dv:dv + 1])
        ot = o.T
        ms = jnp.mean(ot * ot, axis=-1, keepdims=True)
        on = ot * lax.rsqrt(ms + SUBLN_EPS) * g_ref[...] * (1.0 - lam_init)
        o_ref[s * tq:(s + 1) * tq, :] = on.astype(o_ref.dtype)


def _attention(lam_p, subln, q_arr, kc_arr, vct_arr, k4=None, vt4=None, *, lam_init):
    b, sq = q_arr.shape[0], q_arr.shape[1]
    sc = kc_arr.shape[1]
    h = DIFF_HEADS
    dv = DIFF_V_DIM
    n_sub = next(n for n in (4, 2, 1) if sq % (n * ATTN_SUB) == 0)
    tq = n_sub * ATTN_SUB
    n_chunks = 0 if k4 is None else k4.shape[1]
    in_specs = [pl.BlockSpec((4, DIFF_QK_DIM), lambda bb, hh, i: (0, 0)),
                pl.BlockSpec((None, tq, dv), lambda bb, hh, i: (bb, i, hh)),
                pl.BlockSpec((None, sc, dv), lambda bb, hh, i: (bb, 0, h + hh)),
                pl.BlockSpec((None, 1, dv, sc), lambda bb, hh, i: (bb, 0, hh, 0))]
    args = [lam_p, q_arr, kc_arr, vct_arr]
    if n_chunks:
        tk = k4.shape[2]
        in_specs += [pl.BlockSpec((None, n_chunks, tk, dv), lambda bb, hh, i: (bb, 0, 0, h + hh)),
                     pl.BlockSpec((None, n_chunks, dv, tk), lambda bb, hh, i: (bb, 0, hh, 0))]
        args += [k4, vt4]
    in_specs.append(pl.BlockSpec((1, dv), lambda bb, hh, i: (0, 0)))
    args.append(subln)
    kern = functools.partial(_attn_kernel, n_chunks=n_chunks, lam_init=lam_init)
    scratch = [pltpu.VMEM((n_sub, 2, dv + ONES_ROWS, ATTN_SUB), F32)]
    if n_chunks:
        n_slots = 8 if n_chunks % 16 == 0 else 2 * n_sub
        scratch += [pltpu.VMEM((2, tk, ATTN_SUB), F32) for _ in range(n_slots)]
        scratch += [pltpu.VMEM((2, 1, ATTN_SUB), F32) for _ in range(n_slots)]
    return pl.pallas_call(
        kern,
        grid=(b, h, sq // tq),
        in_specs=in_specs,
        out_specs=pl.BlockSpec((None, tq, dv), lambda bb, hh, i: (bb, i, hh)),
        out_shape=jax.ShapeDtypeStruct((b, sq, A_WIDTH), BF16),
        scratch_shapes=scratch,
        compiler_params=_params("parallel", "parallel", "parallel"),
        name="diff_attn",
    )(*args)


def _sgu_kernel(u_ref, g_ref, ng_ref, nb_ref, w_ref, bs_ref, o_ref):
    for gi in range(SGU_GROUPS):
        cols = slice(gi * SGU_CH, (gi + 1) * SGU_CH)
        w = w_ref[gi]
        for c in range(u_ref.shape[0] // SGU_CHUNK):
            sl = slice(c * SGU_CHUNK, (c + 1) * SGU_CHUNK)
            gg = g_ref[sl, cols].astype(F32)
            mu = jnp.mean(gg, axis=-1, keepdims=True)
            dev = gg - mu
            var = jnp.mean(dev * dev, axis=-1, keepdims=True)
            vv = dev * lax.rsqrt(var + LN_EPS) * ng_ref[gi] + nb_ref[gi]
            mixed = jnp.dot(w, vv.astype(BF16), preferred_element_type=F32) + bs_ref[gi]
            o_ref[sl, cols] = (u_ref[sl, cols].astype(F32) * mixed).astype(o_ref.dtype)


def _sgu(qkug, norm_g, norm_b, w_s, b_s):
    b, s = qkug.shape[0], qkug.shape[1]
    tm = _tile(s, 1024)
    gcount = SGU_GROUPS
    ublk = 2 * Q_COLS // B_WIDTH
    full = lambda shape: pl.BlockSpec(shape, lambda bb, i: (0,) * len(shape))
    return pl.pallas_call(
        _sgu_kernel,
        grid=(b, s // tm),
        in_specs=[pl.BlockSpec((None, tm, B_WIDTH), lambda bb, i: (bb, i, ublk)),
                  pl.BlockSpec((None, tm, B_WIDTH), lambda bb, i: (bb, i, ublk + 1)),
                  full((gcount, 1, SGU_CH)), full((gcount, 1, SGU_CH)),
                  full((gcount, SGU_CHUNK, SGU_CHUNK)), full((gcount, SGU_CHUNK, 1))],
        out_specs=pl.BlockSpec((None, tm, B_WIDTH), lambda bb, i: (bb, i, 0)),
        out_shape=jax.ShapeDtypeStruct((b, s, B_WIDTH), BF16),
        compiler_params=_params("parallel", "parallel"),
        name="sgu",
    )(qkug, qkug, norm_g.reshape(gcount, 1, SGU_CH), norm_b.reshape(gcount, 1, SGU_CH),
      w_s.astype(BF16), b_s.reshape(gcount, SGU_CHUNK, 1))


def _filter_mlp_kernel(f_ref, w1_ref, b1_ref, w2_ref, b2_ref, w3_ref, b3_ref, fr_ref, o_ref):
    def lin(a, w_ref, b_ref):
        return jnp.dot(a, w_ref[...], preferred_element_type=F32, precision=HIGHEST) + b_ref[...]
    fr = fr_ref[...]
    hcur = jnp.sin(fr[0:1] * lin(f_ref[...], w1_ref, b1_ref))
    hcur = jnp.sin(fr[1:2] * lin(hcur, w2_ref, b2_ref))
    o_ref[...] = jnp.sin(fr[2:3] * lin(hcur, w3_ref, b3_ref))


def _filter_mlp(feats, w1, b1, w2, b2, w3, b3, freq):
    rows, emb = feats.shape
    hid = FILTER_HIDDEN
    tr = _tile(rows, 2048)
    full = lambda shape: pl.BlockSpec(shape, lambda i: (0,) * len(shape))
    return pl.pallas_call(
        _filter_mlp_kernel,
        grid=(rows // tr,),
        in_specs=[pl.BlockSpec((tr, emb), lambda i: (i, 0)),
                  full((emb, hid)), full((1, hid)), full((hid, hid)), full((1, hid)),
                  full((hid, hid)), full((1, hid)), full((3, hid))],
        out_specs=pl.BlockSpec((tr, hid), lambda i: (i, 0)),
        out_shape=jax.ShapeDtypeStruct((rows, hid), F32),
        compiler_params=_params("parallel"),
        name="filter_mlp",
    )(feats, w1, b1.reshape(1, hid), w2, b2.reshape(1, hid), w3, b3.reshape(1, hid), freq)


def _filter_raw_kernel(w4t_ref, h_ref, t_ref, delta_ref, o_ref, *, zero_tile):
    rt = pl.program_id(2)

    def split(a):
        hi = a.astype(BF16)
        return hi, (a - hi.astype(F32)).astype(BF16)

    def nt(a, bm):
        return lax.dot_general(a, bm, NT_DIMS, preferred_element_type=F32)

    w_hi, w_lo = split(w4t_ref[...])
    h_hi, h_lo = split(h_ref[...])
    raw = nt(w_hi, h_hi) + (nt(w_hi, h_lo) + nt(w_lo, h_hi))
    raw = raw * jnp.exp(-(delta_ref[...] * t_ref[...]))
    for c in range(o_ref.shape[0]):
        o_ref[c] = raw[:, c * DFT_MINOR:(c + 1) * DFT_MINOR].astype(o_ref.dtype)

    @pl.when(rt == zero_tile)
    def _():
        col = lax.broadcasted_iota(jnp.int32, (raw.shape[0], DFT_MINOR), 1)
        o_ref[0] = jnp.where(col == 0, 0.0, raw[:, :DFT_MINOR]).astype(o_ref.dtype)


def _filter_raw(w4t, h3, t_row, deltas, seq):
    c = w4t.shape[2]
    rows = h3.shape[0]
    tr = _tile(rows // 2, 2048)
    tc = _tile(c, 512)
    half_tiles = seq // tr
    kern = functools.partial(_filter_raw_kernel, zero_tile=half_tiles)
    return pl.pallas_call(
        kern,
        grid=(HYENA_ORDER, c // tc, rows // tr),
        in_specs=[pl.BlockSpec((None, None, tc, FILTER_HIDDEN),
                               lambda n, ci, rt: (n, rt // half_tiles, ci, 0)),
                  pl.BlockSpec((tr, FILTER_HIDDEN), lambda n, ci, rt: (rt, 0)),
                  pl.BlockSpec((1, tr), lambda n, ci, rt: (0, rt)),
                  pl.BlockSpec((tc, 1), lambda n, ci, rt: (ci, 0))],
        out_specs=pl.BlockSpec((None, tr // DFT_MINOR, tc, DFT_MINOR), lambda n, ci, rt: (n, rt, ci, 0)),
        out_shape=jax.ShapeDtypeStruct((HYENA_ORDER, rows // DFT_MINOR, c, DFT_MINOR), BF16),
        compiler_params=_params("parallel", "parallel", "parallel"),
        name="filter_raw",
    )(w4t, h3, t_row, deltas)


def _cmul(ar, ai, br, bi):
    return ar * br - ai * bi, ar * bi + ai * br


def _store_complex(ref, c, n1, re, im):
    r0 = pl.multiple_of(c * n1, n1)
    ref[pl.ds(r0, n1), :DFT_MINOR] = re.astype(ref.dtype)
    ref[pl.ds(r0, n1), DFT_MINOR:] = im.astype(ref.dtype)


def _load_complex(ref, c, n1):
    r0 = pl.multiple_of(c * n1, n1)
    tile = ref[pl.ds(r0, n1), :]
    return tile[:, :DFT_MINOR], tile[:, DFT_MINOR:]


def _filter_spec_kernel(k_ref, f1_ref, twr_ref, twi_ref, f2_ref, o_ref, a2_ref, *, inv_n):
    tc, n1, n2 = k_ref.shape

    def left(c, carry):
        k = k_ref[c].astype(F32)
        nrm = jnp.sum(jnp.sum(jnp.abs(k), axis=1, keepdims=True), axis=0, keepdims=True)
        kn = (k * (inv_n / nrm)).astype(BF16)
        a = jnp.dot(f1_ref[...], kn, preferred_element_type=F32)
        ar, ai = _cmul(a[:n1], a[n1:], twr_ref[...], twi_ref[...])
        _store_complex(a2_ref, c, n1, ar, ai)
        return carry

    lax.fori_loop(0, tc, left, 0, unroll=4)
    z = jnp.dot(a2_ref[...], f2_ref[...], preferred_element_type=F32).reshape(tc, n1, 2 * n2)
    o_ref[:, 0] = z[:, :, :n2].astype(o_ref.dtype)
    o_ref[:, 1] = z[:, :, n2:].astype(o_ref.dtype)


def _filter_spec(k2, f1_full, twr, twi, f2e):
    nc, n1, n2 = k2.shape
    tc = 16
    kern = functools.partial(_filter_spec_kernel, inv_n=1.0 / (n1 * n2))
    full = lambda shape: pl.BlockSpec(shape, lambda i: (0,) * len(shape))
    return pl.pallas_call(
        kern,
        grid=(nc // tc,),
        in_specs=[pl.BlockSpec((tc, n1, n2), lambda i: (i, 0, 0)),
                  full((2 * n1, n1)), full((n1, n2)), full((n1, n2)), full((2 * n2, 2 * n2))],
        out_specs=pl.BlockSpec((tc, 2, n1, n2), lambda i: (i, 0, 0, 0)),
        out_shape=jax.ShapeDtypeStruct((nc, 2, n1, n2), BF16),
        scratch_shapes=[pltpu.VMEM((tc * n1, 2 * n2), BF16)],
        compiler_params=_params("parallel"),
        name="filter_spec",
    )(k2, f1_full, twr, twi, f2e)


def _hyena_kernel(zv_ref, z1_ref, z2_ref, kf_ref, fb_ref, f1_ref, twr_ref, twi_ref, f2_ref, f2c_ref,
                  g1_ref, o_ref, sig_ref, a2_ref, z_ref):
    nb, hr, tc, n2 = zv_ref.shape
    n1 = 2 * hr
    for part, ref in enumerate((zv_ref, z1_ref, z2_ref)):
        for bb in range(nb):
            sig_ref[part, bb] = pltpu.einshape("tcl->ctl", ref[bb].astype(F32))

    for n in range(HYENA_ORDER):
        def fwd_left(c, carry):
            xs = jnp.concatenate([sig_ref[0, 0, c], sig_ref[0, 1, c]], axis=0).astype(BF16)
            a = jnp.dot(f1_ref[...], xs, preferred_element_type=F32)
            ar, ai = _cmul(a[:n1], a[n1:], twr_ref[...], twi_ref[...])
            _store_complex(a2_ref, c, n1, ar, ai)
            return carry

        lax.fori_loop(0, tc, fwd_left, 0, unroll=4)
        z_ref[...] = jnp.dot(a2_ref[...], f2_ref[...], preferred_element_type=F32)

        def spectrum(c, carry):
            zr, zi = _load_complex(z_ref, c, n1)
            wr, wi = _cmul(zr, zi, kf_ref[n, c, 0].astype(F32), kf_ref[n, c, 1].astype(F32))
            _store_complex(a2_ref, c, n1, wr, wi)
            return carry

        lax.fori_loop(0, tc, spectrum, 0, unroll=4)
        z_ref[...] = jnp.dot(a2_ref[...], f2c_ref[...], preferred_element_type=F32)

        def inv_left(c, carry):
            br, bi = _load_complex(z_ref, c, n1)
            br, bi = _cmul(br, bi, twr_ref[...], -twi_ref[...])
            bs = jnp.concatenate([br, bi], axis=0).astype(BF16)
            y = jnp.dot(g1_ref[...], bs, preferred_element_type=F32)
            fb = fb_ref[pl.ds(c, 1), n:n + 1]
            for bb in range(nb):
                sig_ref[0, bb, c] = sig_ref[1 + n, bb, c] * (y[bb * hr:(bb + 1) * hr] + sig_ref[0, bb, c] * fb)
            return carry

        lax.fori_loop(0, tc, inv_left, 0, unroll=4)

    for bb in range(nb):
        o_ref[bb] = pltpu.einshape("ctl->tcl", sig_ref[0, bb]).astype(o_ref.dtype)


def _hyena_core(z4, kf, fbias, f1h, twr, twi, f2e, f2c, g1e):
    nb, hr, c3, n2 = z4.shape
    c = c3 // 3
    n1 = 2 * hr
    tc = 16
    nct = c // tc
    full = lambda shape: pl.BlockSpec(shape, lambda i: (0,) * len(shape))
    zspec = lambda part: pl.BlockSpec((nb, hr, tc, n2), lambda i: (0, 0, i + part * nct, 0))
    return pl.pallas_call(
        _hyena_kernel,
        grid=(nct,),
        in_specs=[zspec(0), zspec(1), zspec(2),
                  pl.BlockSpec((HYENA_ORDER, tc, 2, n1, n2), lambda i: (0, i, 0, 0, 0)),
                  pl.BlockSpec((tc, HYENA_ORDER), lambda i: (i, 0)),
                  full((2 * n1, n1)), full((n1, n2)), full((n1, n2)),
                  full((2 * n2, 2 * n2)), full((2 * n2, 2 * n2)), full((n1, 2 * n1))],
        out_specs=pl.BlockSpec((nb, hr, tc, n2), lambda i: (0, 0, i, 0)),
        out_shape=jax.ShapeDtypeStruct((nb, hr, c, n2), BF16),
        scratch_shapes=[pltpu.VMEM((3, nb, tc, hr, n2), F32),
                        pltpu.VMEM((tc * n1, 2 * n2), BF16),
                        pltpu.VMEM((tc * n1, 2 * n2), F32)],
        compiler_params=_params("parallel"),
        name="hyena_core",
    )(z4, z4, z4, kf, fbias, f1h, twr, twi, f2e, f2c, g1e)


def _dft_tables(n1):
    n2 = DFT_MINOR
    n = n1 * n2
    a1 = 2.0 * np.pi * np.outer(np.arange(n1), np.arange(n1)) / n1
    f1r, f1i = np.cos(a1), -np.sin(a1)
    a2 = 2.0 * np.pi * np.outer(np.arange(n2), np.arange(n2)) / n2
    f2r, f2i = np.cos(a2), -np.sin(a2)
    at = 2.0 * np.pi * np.outer(np.arange(n1), np.arange(n2)) / n
    twr, twi = np.cos(at), -np.sin(at)
    hr = n1 // 2
    f1_full = np.concatenate([f1r, f1i], axis=0)
    f1_half = np.block([[f1r[:, :hr], -f1i[:, :hr]], [f1i[:, :hr], f1r[:, :hr]]])
    f2e = np.block([[f2r, f2i], [-f2i, f2r]])
    f2c = np.block([[f2r, -f2i], [f2i, f2r]])
    gr, gi = f1r[:hr, :], -f1i[:hr, :]
    g1e = np.block([[gr, -gi], [gi, gr]])
    bf = lambda m: jnp.asarray(m, F32).astype(BF16)
    return dict(f1_full=bf(f1_full), f1_half=bf(f1_half), f2e=bf(f2e), f2c=bf(f2c), g1e=bf(g1e),
                twr=jnp.asarray(twr, F32), twi=jnp.asarray(twi, F32))


def _hyena_spectra(seq, fw1, fb1, fw2, fb2, fw3, fb3, ffreq, fw4, tabs):
    l = seq
    width = fw4.shape[1] // (2 * HYENA_ORDER)
    t_fwd = jnp.linspace(0.0, 1.0, l, dtype=F32)
    w = 2.0 * math.pi * jnp.arange(l, dtype=F32)[:, None] / l
    f = jnp.linspace(1e-4, FILTER_BANDS - 1, FILTER_BANDS, dtype=F32)[None, :]
    feats = jnp.concatenate([t_fwd[:, None], jnp.cos(f * w), -jnp.sin(f * w)], axis=-1)

    def two_sided(a):
        return jnp.concatenate([a, a[:1], a[1:][::-1]], axis=0)

    feats = two_sided(feats)
    t_lin = two_sided(t_fwd)
    emb_pad = FILTER_HIDDEN
    feats = jnp.pad(feats, ((0, 0), (0, emb_pad - FILTER_EMB)))
    w1p = jnp.pad(fw1, ((0, emb_pad - FILTER_EMB), (0, 0)))
    h3 = _filter_mlp(feats, w1p, fb1, fw2, fb2, fw3, fb3, ffreq)
    w4t = fw4.reshape(FILTER_HIDDEN, HYENA_ORDER, 2, width).transpose(1, 2, 3, 0)
    deltas = jnp.abs(jnp.linspace(MIN_DECAY, MAX_DECAY, width, dtype=F32))[:, None]
    raw = _filter_raw(w4t, h3, t_lin[None, :], deltas, l)
    n1 = raw.shape[1]
    k2 = raw.transpose(0, 2, 1, 3).reshape(HYENA_ORDER * width, n1, DFT_MINOR)
    kf = _filter_spec(k2, tabs["f1_full"], tabs["twr"], tabs["twi"], tabs["f2e"])
    return kf.reshape(HYENA_ORDER, width, 2, n1, DFT_MINOR)


def _rms_kernel(x_ref, g_ref, o_ref):
    x = x_ref[...]
    ms = jnp.mean(x * x, axis=-1, keepdims=True)
    o_ref[...] = x * lax.rsqrt(ms + NORM_EPS) * g_ref[...]


def _final_norm(x, g):
    b, s, d = x.shape
    tm = _tile(s, 1024)
    return pl.pallas_call(
        _rms_kernel,
        grid=(b, s // tm),
        in_specs=[pl.BlockSpec((None, tm, d), lambda bb, i: (bb, i, 0)),
                  pl.BlockSpec((1, d), lambda bb, i: (0, 0))],
        out_specs=pl.BlockSpec((None, tm, d), lambda bb, i: (bb, i, 0)),
        out_shape=jax.ShapeDtypeStruct((b, s, d), F32),
        compiler_params=_params("parallel", "parallel"),
        name="final_norm",
    )(x, g.reshape(1, d))


def _rope_tables(n_tokens):
    tok = jnp.arange(n_tokens)
    row = (tok // GRID_W).astype(F32)
    col = (tok % GRID_W).astype(F32)
    half = ROPE_AXIS_DIM // 2
    inv = 1.0 / (ROPE_THETA ** (jnp.arange(0, ROPE_AXIS_DIM, 2, dtype=F32) / ROPE_AXIS_DIM))
    ang_r = row[:, None] * inv
    ang_c = col[:, None] * inv
    cos64 = jnp.concatenate([jnp.cos(ang_r), jnp.cos(ang_r), jnp.cos(ang_c), jnp.cos(ang_c)], axis=-1)
    sin64 = jnp.concatenate([-jnp.sin(ang_r), jnp.sin(ang_r), -jnp.sin(ang_c), jnp.sin(ang_c)], axis=-1)
    assert cos64.shape[1] == 4 * half == DIFF_QK_DIM
    return jnp.tile(cos64, (1, 2)), jnp.tile(sin64, (1, 2))


def _ffn(x, norm_g, sh, sc, gate, wg, wu, wd):
    hidden = _ffn_up(x, norm_g, sh, sc, wg, wu)
    return _proj_res([hidden], wd, x, gate, tm_pref=1024, vmem=VMEM_LIMIT_WIDE)


def _even_layer(x, xc, mods, cmods, n1g, n2g, w_in, w_out, lam_p, subln, sgu_ng, sgu_nb, sgu_w, sgu_b,
                wg, wu, wd, layer_idx):
    b, s, d = x.shape
    lam_init = 0.8 - 0.6 * math.exp(-0.3 * layer_idx)
    sh1, sc1, g1, sh2, sc2, g2 = mods
    csh1, csc1, cg1, csh2, csc2, cg2 = cmods
    o_k, o_v, o_u = Q_COLS, 2 * Q_COLS, 2 * Q_COLS + A_WIDTH
    w_qkug = jnp.concatenate([w_in[:, :o_v], w_in[:, o_u:]], axis=1).astype(BF16)
    w_vt = w_in[:, o_v:o_u].T.astype(BF16)
    wo = w_out.astype(BF16)
    n1g2 = n1g.reshape(1, d)
    cos_t, sin_t = _rope_tables(s)
    sc_len = xc.shape[1]

    qkug = _inproj(x, n1g2, sh1, sc1, w_qkug, cos_t, sin_t, rope=True)
    cqkug = _inproj(xc, n1g2, csh1, csc1, w_qkug, cos_t[:sc_len], sin_t[:sc_len], rope=False)
    tk = _tile(s, 512)
    vt4 = _inproj_nt(x, n1g2, sh1, sc1, w_vt, tk)
    cvt4 = _inproj_nt(xc, n1g2, csh1, csc1, w_vt, sc_len)
    k4 = qkug.reshape(b, s // tk, tk, qkug.shape[2])
    subln2 = subln.reshape(1, DIFF_V_DIM)

    a_l = _attention(lam_p, subln2, qkug, cqkug, cvt4, k4, vt4, lam_init=lam_init)
    s_l = _sgu(qkug, sgu_ng, sgu_nb, sgu_w, sgu_b)
    x = _proj_res([a_l, s_l], wo, x, g1, tn_pref=d)
    x = _ffn(x, n2g.reshape(1, d), sh2, sc2, g2, wg, wu, wd)

    a_c = _attention(lam_p, subln2, cqkug, cqkug, cvt4, lam_init=lam_init)
    s_c = _sgu(cqkug, sgu_ng, sgu_nb, sgu_w, sgu_b)
    xc = _proj_res([a_c, s_c], wo, xc, cg1, tn_pref=d)
    xc = _ffn(xc, n2g.reshape(1, d), csh2, csc2, cg2, wg, wu, wd)
    return x, xc


def _odd_layer(x, mods, n1g, n2g, w_in, conv_w, conv_b, fw1, fb1, fw2, fb2, fw3, fb3, ffreq, fw4, fbias,
               w_out, wg, wu, wd):
    b, s, d = x.shape
    sh1, sc1, g1, sh2, sc2, g2 = mods
    width = w_out.shape[0]
    n1 = 2 * s // DFT_MINOR
    tabs = _dft_tables(n1)
    kf = _hyena_spectra(s, fw1, fb1, fw2, fb2, fw3, fb3, ffreq, fw4, tabs)

    n1g2 = n1g.reshape(1, d)
    wt = w_in.T.astype(BF16)
    tm = _tile(s, 1024)
    n_tiles = s // tm
    x_tiles = x.reshape(b, n_tiles, tm, d)
    firsts, lasts = x_tiles[:, :, 0, :], x_tiles[:, :, tm - 1, :]
    before = jnp.concatenate([lasts[:, :1], lasts[:, :-1]], axis=1)
    after = jnp.concatenate([firsts[:, 1:], firsts[:, -1:]], axis=1)
    x_edge = jnp.stack([before, after], axis=2).reshape(b, 2 * n_tiles, d)
    z_edge = _inproj_nt(x_edge, n1g2, sh1, sc1, wt, 2 * n_tiles)
    inside = jnp.ones((2 * n_tiles,), F32).at[0].set(0.0).at[2 * n_tiles - 1].set(0.0)
    halo = z_edge[:, 0].astype(F32) * inside
    conv_params = jnp.repeat(jnp.concatenate([conv_w.T, conv_b[:, None]], axis=1), LANES, axis=1)

    z4 = _inproj_nt(x, n1g2, sh1, sc1, wt, DFT_MINOR, conv_params, halo)
    y4 = _hyena_core(z4, kf, fbias.T, tabs["f1_half"], tabs["twr"], tabs["twi"], tabs["f2e"], tabs["f2c"],
                     tabs["g1e"])
    x = _proj_res([y4], w_out.astype(BF16), x, g1, transposed=True, tn_pref=d)
    return _ffn(x, n2g.reshape(1, d), sh2, sc2, g2, wg, wu, wd)


def kernel(x, c, ctx, c_ctx, ada_w, ada_b, norm1, norm2, ffn_w_gate, ffn_w_up, ffn_w_down, e_w_in, e_w_out, e_lambda, e_subln, e_sgu_norm_g, e_sgu_norm_b, e_sgu_w, e_sgu_b, o_w_in, o_conv_w, o_conv_b, o_filt_w1, o_filt_b1, o_filt_w2, o_filt_b2, o_filt_w3, o_filt_b3, o_filt_freq, o_filt_w4, o_filt_bias, o_w_out, final_norm):
    b, s, d = x.shape
    depth = ada_w.shape[0]
    assert b == 2, "the long convolution packs exactly two batches into one complex signal"
    assert depth == 2, "odd layers here never carry the context stream"
    cond_t = jnp.zeros((d, 8), F32).at[:, :b].set(c.T).at[:, b].set(c_ctx)
    mod_all = _adaln_all(cond_t, b + 1, ada_w, ada_b)
    xc = ctx
    for i in range(depth):
        j = i // 2
        parts = jnp.split(mod_all[i], 6, axis=-1)
        mods = [p[:b, None, :] for p in parts]
        cmods = [jnp.broadcast_to(p[b:b + 1, None, :], (b, 1, d)) for p in parts]
        wg, wu, wd = (ffn_w_gate[i].astype(BF16), ffn_w_up[i].astype(BF16), ffn_w_down[i].astype(BF16))
        if i % 2 == 0:
            x, xc = _even_layer(x, xc, mods, cmods, norm1[i], norm2[i], e_w_in[j], e_w_out[j], e_lambda[j],
                                e_subln[j], e_sgu_norm_g[j], e_sgu_norm_b[j], e_sgu_w[j], e_sgu_b[j],
                                wg, wu, wd, i)
        else:
            x = _odd_layer(x, mods, norm1[i], norm2[i], o_w_in[j], o_conv_w[j], o_conv_b[j], o_filt_w1[j],
                           o_filt_b1[j], o_filt_w2[j], o_filt_b2[j], o_filt_w3[j], o_filt_b3[j],
                           o_filt_freq[j], o_filt_w4[j], o_filt_bias[j], o_w_out[j], wg, wu, wd)
    return _final_norm(x, final_norm)
```

```python
import functools
import math

import numpy as np
import jax
import jax.numpy as jnp
from jax import lax
from jax.experimental import pallas as pl
from jax.experimental.pallas import tpu as pltpu

F32 = jnp.float32
BF16 = jnp.bfloat16
HIGHEST = lax.Precision.HIGHEST

GRID_W = 64
NORM_EPS = 1e-6
DIFF_HEADS = 8
DIFF_QK_DIM = 64
DIFF_V_DIM = 2 * DIFF_QK_DIM
DIFF_SCALE = DIFF_QK_DIM ** -0.5
A_WIDTH = DIFF_HEADS * DIFF_V_DIM
Q_COLS = DIFF_HEADS * 2 * DIFF_QK_DIM
ROPE_THETA = 10000.0
ROPE_AXIS_DIM = DIFF_QK_DIM // 2
SUBLN_EPS = 1e-5
SGU_GROUPS = 8
SGU_CHUNK = 128
SGU_CH = 128
B_WIDTH = SGU_GROUPS * SGU_CH
LN_EPS = 1e-5
HYENA_ORDER = 2
SHORT_CONV = 3
FILTER_EMB = 33
FILTER_BANDS = (FILTER_EMB - 1) // 2
FILTER_HIDDEN = 64
DECAY_TARGET = 1e-2
MAX_DECAY = math.log(DECAY_TARGET) / 0.3
MIN_DECAY = math.log(DECAY_TARGET) / 1.5

LANES = 128
ONES_ROWS = 16
ATTN_SUB = 256
DFT_MINOR = 256
VMEM_LIMIT = 48 * 1024 * 1024
VMEM_LIMIT_WIDE = 56 * 1024 * 1024

NT_DIMS = (((1,), (1,)), ((), ()))
TN_DIMS = (((0,), (0,)), ((), ()))


def _params(*sem, vmem=VMEM_LIMIT):
    return pltpu.CompilerParams(dimension_semantics=sem, vmem_limit_bytes=vmem)


def _tile(n, pref):
    return pref if n % pref == 0 else n


def _adaln_kernel(ct_ref, w_ref, b_ref, o_ref, *, n_rows):
    k = pl.program_id(1)
    a = ct_ref[...]
    a = a * jax.nn.sigmoid(a)
    w = w_ref[...]
    rows = [jnp.sum(w * a[:, r:r + 1], axis=0, keepdims=True) for r in range(n_rows)]
    rows.append(jnp.zeros((o_ref.shape[0] - n_rows, w.shape[1]), F32))
    part = jnp.concatenate(rows, axis=0)

    @pl.when(k == 0)
    def _():
        valid = lax.broadcasted_iota(jnp.int32, part.shape, 0) < n_rows
        o_ref[...] = part + jnp.where(valid, b_ref[...], 0.0)

    @pl.when(k > 0)
    def _():
        o_ref[...] += part


def _adaln_all(cond_t, n_rows, ada_w, ada_b):
    depth, d, n6 = ada_w.shape
    tk = _tile(d, 256)
    return pl.pallas_call(
        functools.partial(_adaln_kernel, n_rows=n_rows),
        grid=(depth, d // tk),
        in_specs=[pl.BlockSpec((tk, 8), lambda l, k: (k, 0)),
                  pl.BlockSpec((None, tk, n6), lambda l, k: (l, k, 0)),
                  pl.BlockSpec((None, 1, n6), lambda l, k: (l, 0, 0))],
        out_specs=pl.BlockSpec((None, 8, n6), lambda l, k: (l, 0, 0)),
        out_shape=jax.ShapeDtypeStruct((depth, 8, n6), F32),
        compiler_params=_params("parallel", "arbitrary"),
        name="adaln",
    )(cond_t, ada_w, ada_b.reshape(depth, 1, n6))


def _norm_mod(x_ref, g_ref, sh_ref, sc_ref):
    x = x_ref[...]
    ms = jnp.mean(x * x, axis=-1, keepdims=True)
    y = x * lax.rsqrt(ms + NORM_EPS) * g_ref[...]
    return (y * (1.0 + sc_ref[...]) + sh_ref[...]).astype(BF16)


def _inproj_kernel(x_ref, g_ref, sh_ref, sc_ref, w_ref, cos_ref, sin_ref, o_ref, hs_ref, *,
                   n_q, n_qk, rope):
    j = pl.program_id(2)

    @pl.when(j == 0)
    def _():
        hs_ref[...] = _norm_mod(x_ref, g_ref, sh_ref, sc_ref)

    tm, tn = o_ref.shape
    rc = min(tm, 256)

    def by_row_chunks(epilogue):
        for r in range(tm // rc):
            rows = pl.ds(r * rc, rc)
            acc = jnp.dot(hs_ref[rows, :], w_ref[...], preferred_element_type=F32)
            o_ref[rows, :] = epilogue(acc, rows).astype(o_ref.dtype)

    def qk_epilogue(a, rows):
        if rope:
            lane = lax.broadcasted_iota(jnp.int32, a.shape, 1)
            first = (lane & 31) < 16
            partner = jnp.where(first, pltpu.roll(a, tn - 16, 1), pltpu.roll(a, 16, 1))
            reps = tn // LANES
            a = (a * jnp.tile(cos_ref[rows, :], (1, reps))
                 + partner * jnp.tile(sin_ref[rows, :], (1, reps)))
        return jnp.where(j < n_q, a * DIFF_SCALE, a)

    @pl.when(j < n_qk)
    def _():
        by_row_chunks(qk_epilogue)

    @pl.when(j >= n_qk)
    def _():
        by_row_chunks(lambda a, rows: jax.nn.gelu(a))


def _inproj(x, g, sh, sc, w, cos_t, sin_t, rope):
    b, s, d = x.shape
    n = w.shape[1]
    tm = _tile(s, 1024)
    tn = 512
    kern = functools.partial(_inproj_kernel, n_q=Q_COLS // tn, n_qk=2 * Q_COLS // tn, rope=rope)
    return pl.pallas_call(
        kern,
        grid=(b, s // tm, n // tn),
        in_specs=[pl.BlockSpec((None, tm, d), lambda bb, i, j: (bb, i, 0)),
                  pl.BlockSpec((1, d), lambda bb, i, j: (0, 0)),
                  pl.BlockSpec((None, 1, d), lambda bb, i, j: (bb, 0, 0)),
                  pl.BlockSpec((None, 1, d), lambda bb, i, j: (bb, 0, 0)),
                  pl.BlockSpec((d, tn), lambda bb, i, j: (0, j)),
                  pl.BlockSpec((tm, LANES), lambda bb, i, j: (i, 0)),
                  pl.BlockSpec((tm, LANES), lambda bb, i, j: (i, 0))],
        out_specs=pl.BlockSpec((None, tm, tn), lambda bb, i, j: (bb, i, j)),
        out_shape=jax.ShapeDtypeStruct((b, s, n), BF16),
        scratch_shapes=[pltpu.VMEM((tm, d), BF16)],
        compiler_params=_params("parallel", "parallel", "arbitrary"),
        name="inproj",
    )(x, g, sh, sc, w, cos_t, sin_t)


def _inproj_nt_kernel(x_ref, g_ref, sh_ref, sc_ref, wt_ref, *rest, tl, conv):
    if conv:
        cw_ref, halo_ref, o_ref, hs_ref = rest
    else:
        o_ref, hs_ref = rest
    j = pl.program_id(2)

    @pl.when(j == 0)
    def _():
        hs_ref[...] = _norm_mod(x_ref, g_ref, sh_ref, sc_ref)

    n_chunks = o_ref.shape[0]
    if not conv:
        acc = lax.dot_general(wt_ref[...], hs_ref[...], NT_DIMS, preferred_element_type=F32)
        for c in range(n_chunks):
            o_ref[c] = acc[:, c * tl:(c + 1) * tl].astype(o_ref.dtype)
        return

    acc = jnp.concatenate(
        [lax.dot_general(wt_ref[...], hs_ref[c * tl:(c + 1) * tl, :], NT_DIMS, preferred_element_type=F32)
         for c in range(n_chunks)], axis=1)
    tn, tm = acc.shape
    halo = halo_ref[...]
    col = lax.broadcasted_iota(jnp.int32, halo.shape, 1)
    tile = pl.program_id(1)
    before = jnp.sum(jnp.where(col == 2 * tile, halo, 0.0), axis=1, keepdims=True)
    after = jnp.sum(jnp.where(col == 2 * tile + 1, halo, 0.0), axis=1, keepdims=True)
    lane = lax.broadcasted_iota(jnp.int32, (tn, LANES), 1)
    prev = pltpu.roll(acc, 1, 1)
    prev = jnp.concatenate([jnp.where(lane == 0, before, prev[:, :LANES]), prev[:, LANES:]], axis=1)
    nxt = pltpu.roll(acc, tm - 1, 1)
    nxt = jnp.concatenate([nxt[:, :tm - LANES],
                           jnp.where(lane == LANES - 1, after, nxt[:, tm - LANES:])], axis=1)

    def tap(k):
        return jnp.tile(cw_ref[:, k * LANES:(k + 1) * LANES], (1, tm // LANES))

    out = tap(3) + prev * tap(0) + acc * tap(1) + nxt * tap(2)
    for c in range(n_chunks):
        o_ref[c] = out[:, c * tl:(c + 1) * tl].astype(o_ref.dtype)


def _inproj_nt(x, g, sh, sc, wt, tl, conv_params=None, halo=None):
    b, s, d = x.shape
    n = wt.shape[0]
    tm = _tile(s, 1024)
    tn = 512
    conv = conv_params is not None
    kern = functools.partial(_inproj_nt_kernel, tl=tl, conv=conv)
    in_specs = [pl.BlockSpec((None, tm, d), lambda bb, i, j: (bb, i, 0)),
                pl.BlockSpec((1, d), lambda bb, i, j: (0, 0)),
                pl.BlockSpec((None, 1, d), lambda bb, i, j: (bb, 0, 0)),
                pl.BlockSpec((None, 1, d), lambda bb, i, j: (bb, 0, 0)),
                pl.BlockSpec((tn, d), lambda bb, i, j: (j, 0))]
    args = [x, g, sh, sc, wt]
    if conv:
        in_specs += [pl.BlockSpec((tn, 4 * LANES), lambda bb, i, j: (j, 0)),
                     pl.BlockSpec((None, tn, halo.shape[2]), lambda bb, i, j: (bb, j, 0))]
        args += [conv_params, halo]
    return pl.pallas_call(
        kern,
        grid=(b, s // tm, n // tn),
        in_specs=in_specs,
        out_specs=pl.BlockSpec((None, tm // tl, tn, tl), lambda bb, i, j: (bb, i, j, 0)),
        out_shape=jax.ShapeDtypeStruct((b, s // tl, n, tl), BF16),
        scratch_shapes=[pltpu.VMEM((tm, d), BF16)],
        compiler_params=_params("parallel", "parallel", "arbitrary"),
        name="inproj_nt",
    )(*args)


def _ffn_up_kernel(x_ref, g_ref, sh_ref, sc_ref, wg_ref, wu_ref, o_ref, hs_ref):
    j = pl.program_id(2)

    @pl.when(j == 0)
    def _():
        hs_ref[...] = _norm_mod(x_ref, g_ref, sh_ref, sc_ref)

    hs = hs_ref[...]
    gate = jnp.dot(hs, wg_ref[...], preferred_element_type=F32)
    up = jnp.dot(hs, wu_ref[...], preferred_element_type=F32)
    o_ref[...] = (gate * jax.nn.sigmoid(gate) * up).astype(o_ref.dtype)


def _ffn_up(x, g, sh, sc, wg, wu):
    b, s, d = x.shape
    n = wg.shape[1]
    tm = _tile(s, 1024)
    tn = 512
    return pl.pallas_call(
        _ffn_up_kernel,
        grid=(b, s // tm, n // tn),
        in_specs=[pl.BlockSpec((None, tm, d), lambda bb, i, j: (bb, i, 0)),
                  pl.BlockSpec((1, d), lambda bb, i, j: (0, 0)),
                  pl.BlockSpec((None, 1, d), lambda bb, i, j: (bb, 0, 0)),
                  pl.BlockSpec((None, 1, d), lambda bb, i, j: (bb, 0, 0)),
                  pl.BlockSpec((d, tn), lambda bb, i, j: (0, j)),
                  pl.BlockSpec((d, tn), lambda bb, i, j: (0, j))],
        out_specs=pl.BlockSpec((None, tm, tn), lambda bb, i, j: (bb, i, j)),
        out_shape=jax.ShapeDtypeStruct((b, s, n), BF16),
        scratch_shapes=[pltpu.VMEM((tm, d), BF16)],
        compiler_params=_params("parallel", "parallel", "arbitrary"),
        name="ffn_up",
    )(x, g, sh, sc, wg, wu)


def _proj_res_kernel(*refs, ksizes, transposed):
    n = len(ksizes)
    a_refs = refs[:n]
    w_ref, x_ref, gate_ref, o_ref = refs[n:]
    if transposed:
        (a_ref,) = a_refs
        tl = a_ref.shape[2]
        for c in range(a_ref.shape[0]):
            rows = slice(c * tl, (c + 1) * tl)
            acc = lax.dot_general(a_ref[c], w_ref[...], TN_DIMS, preferred_element_type=F32)
            o_ref[rows, :] = x_ref[rows, :] + gate_ref[...] * acc
        return
    acc = None
    off = 0
    for a_ref, ks in zip(a_refs, ksizes):
        part = jnp.dot(a_ref[...], w_ref[off:off + ks, :], preferred_element_type=F32)
        acc = part if acc is None else acc + part
        off += ks
    o_ref[...] = x_ref[...] + gate_ref[...] * acc


def _proj_res(a_list, w, x, gate, transposed=False, tm_pref=512, tn_pref=512, vmem=VMEM_LIMIT):
    b, s, d = x.shape
    ksizes = tuple(a.shape[2] for a in a_list)
    ktot = sum(ksizes)
    tm = _tile(s, tm_pref)
    tn = _tile(d, tn_pref)
    if transposed:
        tl = a_list[0].shape[3]
        a_specs = [pl.BlockSpec((None, tm // tl, ktot, tl), lambda bb, i, j: (bb, i, 0, 0))]
    else:
        a_specs = [pl.BlockSpec((None, tm, ks), lambda bb, i, j: (bb, i, 0)) for ks in ksizes]
    kern = functools.partial(_proj_res_kernel, ksizes=ksizes, transposed=transposed)
    return pl.pallas_call(
        kern,
        grid=(b, s // tm, d // tn),
        in_specs=a_specs + [pl.BlockSpec((ktot, tn), lambda bb, i, j: (0, j)),
                            pl.BlockSpec((None, tm, tn), lambda bb, i, j: (bb, i, j)),
                            pl.BlockSpec((None, 1, tn), lambda bb, i, j: (bb, 0, j))],
        out_specs=pl.BlockSpec((None, tm, tn), lambda bb, i, j: (bb, i, j)),
        out_shape=jax.ShapeDtypeStruct((b, s, d), F32),
        compiler_params=_params("parallel", "parallel", "parallel", vmem=vmem),
        name="proj_res",
    )(*a_list, w, x, gate)


def _attn_kernel(*refs, n_chunks, lam_init):
    if n_chunks:
        lam_ref, q_ref, kc_ref, vct_ref, k_ref, vt_ref, g_ref, o_ref, acc_ref = refs[:9]
        n_slots = (len(refs) - 9) // 2
        slots = tuple(zip(refs[9:9 + n_slots], refs[9 + n_slots:]))
    else:
        lam_ref, q_ref, kc_ref, vct_ref, g_ref, o_ref, acc_ref = refs
    n_sub = acc_ref.shape[0]
    tq = acc_ref.shape[-1]
    dv = DIFF_V_DIM
    q = q_ref[...]
    qm = [(q[s * tq:(s + 1) * tq, :DIFF_QK_DIM], q[s * tq:(s + 1) * tq, DIFF_QK_DIM:]) for s in range(n_sub)]

    def scores(kblk, sub):
        s_pair = tuple(lax.dot_general(kblk[:, m * DIFF_QK_DIM:(m + 1) * DIFF_QK_DIM], qm[sub][m], NT_DIMS,
                                       preferred_element_type=F32) for m in range(2))
        return s_pair, tuple(jnp.max(s, axis=0, keepdims=True) for s in s_pair)

    def with_ones(vtblk):
        return jnp.concatenate([vtblk, jnp.ones((ONES_ROWS, vtblk.shape[1]), BF16)], axis=0)

    def absorb(scored, vext, m_pair, sub):
        s_pair, smax_pair = scored
        out = []
        for m in range(2):
            m_old = m_pair[m]
            m_new = jnp.maximum(m_old, smax_pair[m])
            alpha = jnp.exp(m_old - m_new)
            p = jnp.exp((s_pair[m] - m_new).astype(BF16))
            pv = jnp.dot(vext, p, preferred_element_type=F32)
            acc_ref[sub, m] = alpha * acc_ref[sub, m] + pv
            out.append(m_new)
        return tuple(out)

    def store(slot, scored):
        for m in range(2):
            slot[0][m] = scored[0][m]
            slot[1][m] = scored[1][m]

    def load(slot):
        return (slot[0][0], slot[0][1]), (slot[1][0], slot[1][1])

    acc_ref[...] = jnp.zeros_like(acc_ref)
    init = jnp.full((1, tq), -1e30, F32)
    kc, vc = kc_ref[...], with_ones(vct_ref[0])
    m_state = tuple(absorb(scores(kc, s), vc, (init, init), s) for s in range(n_sub))
    if n_chunks:
        group = len(slots) // (2 * n_sub)
        assert n_chunks % (2 * group) == 0

        def slot(half, k, sub):
            return slots[(half * group + k) * n_sub + sub]

        def half_trip(half, base, ms, lookahead):
            ms = list(ms)
            for k in range(group):
                vext = with_ones(vt_ref[base + k])
                for s in range(n_sub):
                    if lookahead:
                        store(slot(1 - half, k, s), scores(k_ref[base + group + k], s))
                    ms[s] = absorb(load(slot(half, k, s)), vext, ms[s], s)
            return tuple(ms)

        def trip(j, ms, lookahead):
            base = 2 * group * j
            ms = half_trip(0, base, ms, True)
            return half_trip(1, base + group, ms, lookahead)

        for k in range(group):
            for s in range(n_sub):
                store(slot(0, k, s), scores(k_ref[k], s))
        n_trips = n_chunks // (2 * group)
        m_state = lax.fori_loop(0, n_trips - 1, lambda j, ms: trip(j, ms, True), m_state)
        m_state = trip(n_trips - 1, m_state, False)

    lp = lam_ref[...]
    lam = (jnp.exp(jnp.sum(lp[0:1] * lp[1:2], axis=-1, keepdims=True))
           - jnp.exp(jnp.sum(lp[2:3] * lp[3:4], axis=-1, keepdims=True)) + lam_init)
    for s in range(n_sub):
        acc0, acc1 = acc_ref[s, 0], acc_ref[s, 1]
        o = acc0[:dv] / acc0[dv:dv + 1] - lam * (acc1[:dv] / acc1[dv:dv + 1])
        ot = o.T
        ms = jnp.mean(ot * ot, axis=-1, keepdims=True)
        on = ot * lax.rsqrt(ms + SUBLN_EPS) * g_ref[...] * (1.0 - lam_init)
        o_ref[s * tq:(s + 1) * tq, :] = on.astype(o_ref.dtype)


def _attention(lam_p, subln, q_arr, kc_arr, vct_arr, k4=None, vt4=None, *, lam_init):
    b, sq = q_arr.shape[0], q_arr.shape[1]
    sc = kc_arr.shape[1]
    h = DIFF_HEADS
    dv = DIFF_V_DIM
    n_sub = next(n for n in (4, 2, 1) if sq % (n * ATTN_SUB) == 0)
    tq = n_sub * ATTN_SUB
    n_chunks = 0 if k4 is None else k4.shape[1]
    in_specs = [pl.BlockSpec((4, DIFF_QK_DIM), lambda bb, hh, i: (0, 0)),
                pl.BlockSpec((None, tq, dv), lambda bb, hh, i: (bb, i, hh)),
                pl.BlockSpec((None, sc, dv), lambda bb, hh, i: (bb, 0, h + hh)),
                pl.BlockSpec((None, 1, dv, sc), lambda bb, hh, i: (bb, 0, hh, 0))]
    args = [lam_p, q_arr, kc_arr, vct_arr]
    if n_chunks:
        tk = k4.shape[2]
        in_specs += [pl.BlockSpec((None, n_chunks, tk, dv), lambda bb, hh, i: (bb, 0, 0, h + hh)),
                     pl.BlockSpec((None, n_chunks, dv, tk), lambda bb, hh, i: (bb, 0, hh, 0))]
        args += [k4, vt4]
    in_specs.append(pl.BlockSpec((1, dv), lambda bb, hh, i: (0, 0)))
    args.append(subln)
    kern = functools.partial(_attn_kernel, n_chunks=n_chunks, lam_init=lam_init)
    scratch = [pltpu.VMEM((n_sub, 2, dv + ONES_ROWS, ATTN_SUB), F32)]
    if n_chunks:
        n_slots = 8 if n_chunks % 16 == 0 else 2 * n_sub
        scratch += [pltpu.VMEM((2, tk, ATTN_SUB), F32) for _ in range(n_slots)]
        scratch += [pltpu.VMEM((2, 1, ATTN_SUB), F32) for _ in range(n_slots)]
    return pl.pallas_call(
        kern,
        grid=(b, h, sq // tq),
        in_specs=in_specs,
        out_specs=pl.BlockSpec((None, tq, dv), lambda bb, hh, i: (bb, i, hh)),
        out_shape=jax.ShapeDtypeStruct((b, sq, A_WIDTH), BF16),
        scratch_shapes=scratch,
        compiler_params=_params("parallel", "parallel", "parallel"),
        name="diff_attn",
    )(*args)


def _sgu_kernel(u_ref, g_ref, ng_ref, nb_ref, w_ref, bs_ref, o_ref):
    for gi in range(SGU_GROUPS):
        cols = slice(gi * SGU_CH, (gi + 1) * SGU_CH)
        w = w_ref[gi]
        for c in range(u_ref.shape[0] // SGU_CHUNK):
            sl = slice(c * SGU_CHUNK, (c + 1) * SGU_CHUNK)
            gg = g_ref[sl, cols].astype(F32)
            mu = jnp.mean(gg, axis=-1, keepdims=True)
            dev = gg - mu
            var = jnp.mean(dev * dev, axis=-1, keepdims=True)
            vv = dev * lax.rsqrt(var + LN_EPS) * ng_ref[gi] + nb_ref[gi]
            mixed = jnp.dot(w, vv.astype(BF16), preferred_element_type=F32) + bs_ref[gi]
            o_ref[sl, cols] = (u_ref[sl, cols].astype(F32) * mixed).astype(o_ref.dtype)


def _sgu(qkug, norm_g, norm_b, w_s, b_s):
    b, s = qkug.shape[0], qkug.shape[1]
    tm = _tile(s, 1024)
    gcount = SGU_GROUPS
    ublk = 2 * Q_COLS // B_WIDTH
    full = lambda shape: pl.BlockSpec(shape, lambda bb, i: (0,) * len(shape))
    return pl.pallas_call(
        _sgu_kernel,
        grid=(b, s // tm),
        in_specs=[pl.BlockSpec((None, tm, B_WIDTH), lambda bb, i: (bb, i, ublk)),
                  pl.BlockSpec((None, tm, B_WIDTH), lambda bb, i: (bb, i, ublk + 1)),
                  full((gcount, 1, SGU_CH)), full((gcount, 1, SGU_CH)),
                  full((gcount, SGU_CHUNK, SGU_CHUNK)), full((gcount, SGU_CHUNK, 1))],
        out_specs=pl.BlockSpec((None, tm, B_WIDTH), lambda bb, i: (bb, i, 0)),
        out_shape=jax.ShapeDtypeStruct((b, s, B_WIDTH), BF16),
        compiler_params=_params("parallel", "parallel"),
        name="sgu",
    )(qkug, qkug, norm_g.reshape(gcount, 1, SGU_CH), norm_b.reshape(gcount, 1, SGU_CH),
      w_s.astype(BF16), b_s.reshape(gcount, SGU_CHUNK, 1))


def _filter_mlp_kernel(f_ref, w1_ref, b1_ref, w2_ref, b2_ref, w3_ref, b3_ref, fr_ref, o_ref):
    def lin(a, w_ref, b_ref):
        return jnp.dot(a, w_ref[...], preferred_element_type=F32, precision=HIGHEST) + b_ref[...]
    fr = fr_ref[...]
    hcur = jnp.sin(fr[0:1] * lin(f_ref[...], w1_ref, b1_ref))
    hcur = jnp.sin(fr[1:2] * lin(hcur, w2_ref, b2_ref))
    o_ref[...] = jnp.sin(fr[2:3] * lin(hcur, w3_ref, b3_ref))


def _filter_mlp(feats, w1, b1, w2, b2, w3, b3, freq):
    rows, emb = feats.shape
    hid = FILTER_HIDDEN
    tr = _tile(rows, 2048)
    full = lambda shape: pl.BlockSpec(shape, lambda i: (0,) * len(shape))
    return pl.pallas_call(
        _filter_mlp_kernel,
        grid=(rows // tr,),
        in_specs=[pl.BlockSpec((tr, emb), lambda i: (i, 0)),
                  full((emb, hid)), full((1, hid)), full((hid, hid)), full((1, hid)),
                  full((hid, hid)), full((1, hid)), full((3, hid))],
        out_specs=pl.BlockSpec((tr, hid), lambda i: (i, 0)),
        out_shape=jax.ShapeDtypeStruct((rows, hid), F32),
        compiler_params=_params("parallel"),
        name="filter_mlp",
    )(feats, w1, b1.reshape(1, hid), w2, b2.reshape(1, hid), w3, b3.reshape(1, hid), freq)


def _filter_raw_kernel(w4t_ref, h_ref, t_ref, delta_ref, o_ref, *, zero_tile):
    rt = pl.program_id(2)

    def split(a):
        hi = a.astype(BF16)
        return hi, (a - hi.astype(F32)).astype(BF16)

    def nt(a, bm):
        return lax.dot_general(a, bm, NT_DIMS, preferred_element_type=F32)

    w_hi, w_lo = split(w4t_ref[...])
    h_hi, h_lo = split(h_ref[...])
    raw = nt(w_hi, h_hi) + (nt(w_hi, h_lo) + nt(w_lo, h_hi))
    raw = raw * jnp.exp(-(delta_ref[...] * t_ref[...]))
    for c in range(o_ref.shape[0]):
        o_ref[c] = raw[:, c * DFT_MINOR:(c + 1) * DFT_MINOR].astype(o_ref.dtype)

    @pl.when(rt == zero_tile)
    def _():
        col = lax.broadcasted_iota(jnp.int32, (raw.shape[0], DFT_MINOR), 1)
        o_ref[0] = jnp.where(col == 0, 0.0, raw[:, :DFT_MINOR]).astype(o_ref.dtype)


def _filter_raw(w4t, h3, t_row, deltas, seq):
    c = w4t.shape[2]
    rows = h3.shape[0]
    tr = _tile(rows // 2, 2048)
    tc = _tile(c, 512)
    half_tiles = seq // tr
    kern = functools.partial(_filter_raw_kernel, zero_tile=half_tiles)
    return pl.pallas_call(
        kern,
        grid=(HYENA_ORDER, c // tc, rows // tr),
        in_specs=[pl.BlockSpec((None, None, tc, FILTER_HIDDEN),
                               lambda n, ci, rt: (n, rt // half_tiles, ci, 0)),
                  pl.BlockSpec((tr, FILTER_HIDDEN), lambda n, ci, rt: (rt, 0)),
                  pl.BlockSpec((1, tr), lambda n, ci, rt: (0, rt)),
                  pl.BlockSpec((tc, 1), lambda n, ci, rt: (ci, 0))],
        out_specs=pl.BlockSpec((None, tr // DFT_MINOR, tc, DFT_MINOR), lambda n, ci, rt: (n, rt, ci, 0)),
        out_shape=jax.ShapeDtypeStruct((HYENA_ORDER, rows // DFT_MINOR, c, DFT_MINOR), BF16),
        compiler_params=_params("parallel", "parallel", "parallel"),
        name="filter_raw",
    )(w4t, h3, t_row, deltas)


def _cmul(ar, ai, br, bi):
    return ar * br - ai * bi, ar * bi + ai * br


def _store_complex(ref, c, n1, re, im):
    r0 = pl.multiple_of(c * n1, n1)
    ref[pl.ds(r0, n1), :DFT_MINOR] = re.astype(ref.dtype)
    ref[pl.ds(r0, n1), DFT_MINOR:] = im.astype(ref.dtype)


def _load_complex(ref, c, n1):
    r0 = pl.multiple_of(c * n1, n1)
    tile = ref[pl.ds(r0, n1), :]
    return tile[:, :DFT_MINOR], tile[:, DFT_MINOR:]


def _filter_spec_kernel(k_ref, f1_ref, twr_ref, twi_ref, f2_ref, o_ref, a2_ref, *, inv_n):
    tc, n1, n2 = k_ref.shape

    def left(c, carry):
        k = k_ref[c].astype(F32)
        nrm = jnp.sum(jnp.sum(jnp.abs(k), axis=1, keepdims=True), axis=0, keepdims=True)
        kn = (k * (inv_n / nrm)).astype(BF16)
        a = jnp.dot(f1_ref[...], kn, preferred_element_type=F32)
        ar, ai = _cmul(a[:n1], a[n1:], twr_ref[...], twi_ref[...])
        _store_complex(a2_ref, c, n1, ar, ai)
        return carry

    lax.fori_loop(0, tc, left, 0, unroll=8)
    z = jnp.dot(a2_ref[...], f2_ref[...], preferred_element_type=F32).reshape(tc, n1, 2 * n2)
    o_ref[:, 0] = z[:, :, :n2].astype(o_ref.dtype)
    o_ref[:, 1] = z[:, :, n2:].astype(o_ref.dtype)


def _filter_spec(k2, f1_full, twr, twi, f2e):
    nc, n1, n2 = k2.shape
    tc = 16
    kern = functools.partial(_filter_spec_kernel, inv_n=1.0 / (n1 * n2))
    full = lambda shape: pl.BlockSpec(shape, lambda i: (0,) * len(shape))
    return pl.pallas_call(
        kern,
        grid=(nc // tc,),
        in_specs=[pl.BlockSpec((tc, n1, n2), lambda i: (i, 0, 0)),
                  full((2 * n1, n1)), full((n1, n2)), full((n1, n2)), full((2 * n2, 2 * n2))],
        out_specs=pl.BlockSpec((tc, 2, n1, n2), lambda i: (i, 0, 0, 0)),
        out_shape=jax.ShapeDtypeStruct((nc, 2, n1, n2), BF16),
        scratch_shapes=[pltpu.VMEM((tc * n1, 2 * n2), BF16)],
        compiler_params=_params("parallel"),
        name="filter_spec",
    )(k2, f1_full, twr, twi, f2e)


def _hyena_kernel(zv_ref, z1_ref, z2_ref, kf_ref, fb_ref, f1_ref, twr_ref, twi_ref, f2_ref, f2c_ref,
                  g1_ref, o_ref, sig_ref, a2_ref, z_ref):
    nb, hr, tc, n2 = zv_ref.shape
    n1 = 2 * hr
    for part, ref in enumerate((zv_ref, z1_ref, z2_ref)):
        for bb in range(nb):
            sig_ref[part, bb] = pltpu.einshape("tcl->ctl", ref[bb].astype(F32))

    for n in range(HYENA_ORDER):
        def fwd_left(c, carry):
            xs = jnp.concatenate([sig_ref[0, 0, c], sig_ref[0, 1, c]], axis=0).astype(BF16)
            a = jnp.dot(f1_ref[...], xs, preferred_element_type=F32)
            ar, ai = _cmul(a[:n1], a[n1:], twr_ref[...], twi_ref[...])
            _store_complex(a2_ref, c, n1, ar, ai)
            return carry

        lax.fori_loop(0, tc, fwd_left, 0, unroll=8)
        z_ref[...] = jnp.dot(a2_ref[...], f2_ref[...], preferred_element_type=F32)

        def spectrum(c, carry):
            zr, zi = _load_complex(z_ref, c, n1)
            wr, wi = _cmul(zr, zi, kf_ref[n, c, 0].astype(F32), kf_ref[n, c, 1].astype(F32))
            _store_complex(a2_ref, c, n1, wr, wi)
            return carry

        lax.fori_loop(0, tc, spectrum, 0, unroll=4)
        z_ref[...] = jnp.dot(a2_ref[...], f2c_ref[...], preferred_element_type=F32)

        def inv_left(c, carry):
            br, bi = _load_complex(z_ref, c, n1)
            br, bi = _cmul(br, bi, twr_ref[...], -twi_ref[...])
            bs = jnp.concatenate([br, bi], axis=0).astype(BF16)
            y = jnp.dot(g1_ref[...], bs, preferred_element_type=F32)
            fb = fb_ref[pl.ds(c, 1), n:n + 1]
            for bb in range(nb):
                sig_ref[0, bb, c] = sig_ref[1 + n, bb, c] * (y[bb * hr:(bb + 1) * hr] + sig_ref[0, bb, c] * fb)
            return carry

        lax.fori_loop(0, tc, inv_left, 0, unroll=8)

    for bb in range(nb):
        o_ref[bb] = pltpu.einshape("ctl->tcl", sig_ref[0, bb]).astype(o_ref.dtype)


def _hyena_core(z4, kf, fbias, f1h, twr, twi, f2e, f2c, g1e):
    nb, hr, c3, n2 = z4.shape
    c = c3 // 3
    n1 = 2 * hr
    tc = 16
    nct = c // tc
    full = lambda shape: pl.BlockSpec(shape, lambda i: (0,) * len(shape))
    zspec = lambda part: pl.BlockSpec((nb, hr, tc, n2), lambda i: (0, 0, i + part * nct, 0))
    return pl.pallas_call(
        _hyena_kernel,
        grid=(nct,),
        in_specs=[zspec(0), zspec(1), zspec(2),
                  pl.BlockSpec((HYENA_ORDER, tc, 2, n1, n2), lambda i: (0, i, 0, 0, 0)),
                  pl.BlockSpec((tc, HYENA_ORDER), lambda i: (i, 0)),
                  full((2 * n1, n1)), full((n1, n2)), full((n1, n2)),
                  full((2 * n2, 2 * n2)), full((2 * n2, 2 * n2)), full((n1, 2 * n1))],
        out_specs=pl.BlockSpec((nb, hr, tc, n2), lambda i: (0, 0, i, 0)),
        out_shape=jax.ShapeDtypeStruct((nb, hr, c, n2), BF16),
        scratch_shapes=[pltpu.VMEM((3, nb, tc, hr, n2), F32),
                        pltpu.VMEM((tc * n1, 2 * n2), BF16),
                        pltpu.VMEM((tc * n1, 2 * n2), F32)],
        compiler_params=_params("parallel"),
        name="hyena_core",
    )(z4, z4, z4, kf, fbias, f1h, twr, twi, f2e, f2c, g1e)


def _dft_tables(n1):
    n2 = DFT_MINOR
    n = n1 * n2
    a1 = 2.0 * np.pi * np.outer(np.arange(n1), np.arange(n1)) / n1
    f1r, f1i = np.cos(a1), -np.sin(a1)
    a2 = 2.0 * np.pi * np.outer(np.arange(n2), np.arange(n2)) / n2
    f2r, f2i = np.cos(a2), -np.sin(a2)
    at = 2.0 * np.pi * np.outer(np.arange(n1), np.arange(n2)) / n
    twr, twi = np.cos(at), -np.sin(at)
    hr = n1 // 2
    f1_full = np.concatenate([f1r, f1i], axis=0)
    f1_half = np.block([[f1r[:, :hr], -f1i[:, :hr]], [f1i[:, :hr], f1r[:, :hr]]])
    f2e = np.block([[f2r, f2i], [-f2i, f2r]])
    f2c = np.block([[f2r, -f2i], [f2i, f2r]])
    gr, gi = f1r[:hr, :], -f1i[:hr, :]
    g1e = np.block([[gr, -gi], [gi, gr]])
    bf = lambda m: jnp.asarray(m, F32).astype(BF16)
    return dict(f1_full=bf(f1_full), f1_half=bf(f1_half), f2e=bf(f2e), f2c=bf(f2c), g1e=bf(g1e),
                twr=jnp.asarray(twr, F32), twi=jnp.asarray(twi, F32))


def _hyena_spectra(seq, fw1, fb1, fw2, fb2, fw3, fb3, ffreq, fw4, tabs):
    l = seq
    width = fw4.shape[1] // (2 * HYENA_ORDER)
    t_fwd = jnp.linspace(0.0, 1.0, l, dtype=F32)
    w = 2.0 * math.pi * jnp.arange(l, dtype=F32)[:, None] / l
    f = jnp.linspace(1e-4, FILTER_BANDS - 1, FILTER_BANDS, dtype=F32)[None, :]
    feats = jnp.concatenate([t_fwd[:, None], jnp.cos(f * w), -jnp.sin(f * w)], axis=-1)

    def two_sided(a):
        return jnp.concatenate([a, a[:1], a[1:][::-1]], axis=0)

    feats = two_sided(feats)
    t_lin = two_sided(t_fwd)
    emb_pad = FILTER_HIDDEN
    feats = jnp.pad(feats, ((0, 0), (0, emb_pad - FILTER_EMB)))
    w1p = jnp.pad(fw1, ((0, emb_pad - FILTER_EMB), (0, 0)))
    h3 = _filter_mlp(feats, w1p, fb1, fw2, fb2, fw3, fb3, ffreq)
    w4t = fw4.reshape(FILTER_HIDDEN, HYENA_ORDER, 2, width).transpose(1, 2, 3, 0)
    deltas = jnp.abs(jnp.linspace(MIN_DECAY, MAX_DECAY, width, dtype=F32))[:, None]
    raw = _filter_raw(w4t, h3, t_lin[None, :], deltas, l)
    n1 = raw.shape[1]
    k2 = raw.transpose(0, 2, 1, 3).reshape(HYENA_ORDER * width, n1, DFT_MINOR)
    kf = _filter_spec(k2, tabs["f1_full"], tabs["twr"], tabs["twi"], tabs["f2e"])
    return kf.reshape(HYENA_ORDER, width, 2, n1, DFT_MINOR)


def _rope_tables(n_tokens):
    tok = jnp.arange(n_tokens)
    row = (tok // GRID_W).astype(F32)
    col = (tok % GRID_W).astype(F32)
    half = ROPE_AXIS_DIM // 2
    inv = 1.0 / (ROPE_THETA ** (jnp.arange(0, ROPE_AXIS_DIM, 2, dtype=F32) / ROPE_AXIS_DIM))
    ang_r = row[:, None] * inv
    ang_c = col[:, None] * inv
    cos64 = jnp.concatenate([jnp.cos(ang_r), jnp.cos(ang_r), jnp.cos(ang_c), jnp.cos(ang_c)], axis=-1)
    sin64 = jnp.concatenate([-jnp.sin(ang_r), jnp.sin(ang_r), -jnp.sin(ang_c), jnp.sin(ang_c)], axis=-1)
    assert cos64.shape[1] == 4 * half == DIFF_QK_DIM
    return jnp.tile(cos64, (1, 2)), jnp.tile(sin64, (1, 2))


def _ffn(x, norm_g, sh, sc, gate, wg, wu, wd):
    hidden = _ffn_up(x, norm_g, sh, sc, wg, wu)
    return _proj_res([hidden], wd, x, gate, tm_pref=1024, vmem=VMEM_LIMIT_WIDE)


def _even_layer(x, xc, mods, cmods, n1g, n2g, w_in, w_out, lam_p, subln, sgu_ng, sgu_nb, sgu_w, sgu_b,
                wg, wu, wd, layer_idx):
    b, s, d = x.shape
    lam_init = 0.8 - 0.6 * math.exp(-0.3 * layer_idx)
    sh1, sc1, g1, sh2, sc2, g2 = mods
    csh1, csc1, cg1, csh2, csc2, cg2 = cmods
    o_k, o_v, o_u = Q_COLS, 2 * Q_COLS, 2 * Q_COLS + A_WIDTH
    w_qkug = jnp.concatenate([w_in[:, :o_v], w_in[:, o_u:]], axis=1).astype(BF16)
    w_vt = w_in[:, o_v:o_u].T.astype(BF16)
    wo = w_out.astype(BF16)
    n1g2 = n1g.reshape(1, d)
    cos_t, sin_t = _rope_tables(s)
    sc_len = xc.shape[1]

    qkug = _inproj(x, n1g2, sh1, sc1, w_qkug, cos_t, sin_t, rope=True)
    cqkug = _inproj(xc, n1g2, csh1, csc1, w_qkug, cos_t[:sc_len], sin_t[:sc_len], rope=False)
    tk = _tile(s, 512)
    vt4 = _inproj_nt(x, n1g2, sh1, sc1, w_vt, tk)
    cvt4 = _inproj_nt(xc, n1g2, csh1, csc1, w_vt, sc_len)
    k4 = qkug.reshape(b, s // tk, tk, qkug.shape[2])
    subln2 = subln.reshape(1, DIFF_V_DIM)

    a_l = _attention(lam_p, subln2, qkug, cqkug, cvt4, k4, vt4, lam_init=lam_init)
    s_l = _sgu(qkug, sgu_ng, sgu_nb, sgu_w, sgu_b)
    x = _proj_res([a_l, s_l], wo, x, g1, tn_pref=d)
    x = _ffn(x, n2g.reshape(1, d), sh2, sc2, g2, wg, wu, wd)

    a_c = _attention(lam_p, subln2, cqkug, cqkug, cvt4, lam_init=lam_init)
    s_c = _sgu(cqkug, sgu_ng, sgu_nb, sgu_w, sgu_b)
    xc = _proj_res([a_c, s_c], wo, xc, cg1, tn_pref=d)
    xc = _ffn(xc, n2g.reshape(1, d), csh2, csc2, cg2, wg, wu, wd)
    return x, xc


def _odd_layer(x, mods, n1g, n2g, w_in, conv_w, conv_b, fw1, fb1, fw2, fb2, fw3, fb3, ffreq, fw4, fbias,
               w_out, wg, wu, wd):
    b, s, d = x.shape
    sh1, sc1, g1, sh2, sc2, g2 = mods
    width = w_out.shape[0]
    n1 = 2 * s // DFT_MINOR
    tabs = _dft_tables(n1)
    kf = _hyena_spectra(s, fw1, fb1, fw2, fb2, fw3, fb3, ffreq, fw4, tabs)

    n1g2 = n1g.reshape(1, d)
    wt = w_in.T.astype(BF16)
    tm = _tile(s, 1024)
    n_tiles = s // tm
    x_tiles = x.reshape(b, n_tiles, tm, d)
    firsts, lasts = x_tiles[:, :, 0, :], x_tiles[:, :, tm - 1, :]
    before = jnp.concatenate([lasts[:, :1], lasts[:, :-1]], axis=1)
    after = jnp.concatenate([firsts[:, 1:], firsts[:, -1:]], axis=1)
    x_edge = jnp.stack([before, after], axis=2).reshape(b, 2 * n_tiles, d)
    z_edge = _inproj_nt(x_edge, n1g2, sh1, sc1, wt, 2 * n_tiles)
    inside = jnp.ones((2 * n_tiles,), F32).at[0].set(0.0).at[2 * n_tiles - 1].set(0.0)
    halo = z_edge[:, 0].astype(F32) * inside
    conv_params = jnp.repeat(jnp.concatenate([conv_w.T, conv_b[:, None]], axis=1), LANES, axis=1)

    z4 = _inproj_nt(x, n1g2, sh1, sc1, wt, DFT_MINOR, conv_params, halo)
    y4 = _hyena_core(z4, kf, fbias.T, tabs["f1_half"], tabs["twr"], tabs["twi"], tabs["f2e"], tabs["f2c"],
                     tabs["g1e"])
    x = _proj_res([y4], w_out.astype(BF16), x, g1, transposed=True, tn_pref=d)
    return _ffn(x, n2g.reshape(1, d), sh2, sc2, g2, wg, wu, wd)


def _rms_kernel(x_ref, g_ref, o_ref):
    x = x_ref[...]
    ms = jnp.mean(x * x, axis=-1, keepdims=True)
    o_ref[...] = x * lax.rsqrt(ms + NORM_EPS) * g_ref[...]


def _final_norm(x, g):
    b, s, d = x.shape
    tm = _tile(s, 1024)
    return pl.pallas_call(
        _rms_kernel,
        grid=(b, s // tm),
        in_specs=[pl.BlockSpec((None, tm, d), lambda bb, i: (bb, i, 0)),
                  pl.BlockSpec((1, d), lambda bb, i: (0, 0))],
        out_specs=pl.BlockSpec((None, tm, d), lambda bb, i: (bb, i, 0)),
        out_shape=jax.ShapeDtypeStruct((b, s, d), F32),
        compiler_params=_params("parallel", "parallel"),
        name="final_norm",
    )(x, g.reshape(1, d))


def kernel(x, c, ctx, c_ctx, ada_w, ada_b, norm1, norm2, ffn_w_gate, ffn_w_up, ffn_w_down, e_w_in, e_w_out, e_lambda, e_subln, e_sgu_norm_g, e_sgu_norm_b, e_sgu_w, e_sgu_b, o_w_in, o_conv_w, o_conv_b, o_filt_w1, o_filt_b1, o_filt_w2, o_filt_b2, o_filt_w3, o_filt_b3, o_filt_freq, o_filt_w4, o_filt_bias, o_w_out, final_norm):
    b, s, d = x.shape
    depth = ada_w.shape[0]
    assert b == 2, "the long convolution packs exactly two batches into one complex signal"
    assert depth == 2, "odd layers here never carry the context stream"
    cond_t = jnp.zeros((d, 8), F32).at[:, :b].set(c.T).at[:, b].set(c_ctx)
    mod_all = _adaln_all(cond_t, b + 1, ada_w, ada_b)
    xc = ctx
    for i in range(depth):
        j = i // 2
        parts = jnp.split(mod_all[i], 6, axis=-1)
        mods = [p[:b, None, :] for p in parts]
        cmods = [jnp.broadcast_to(p[b:b + 1, None, :], (b, 1, d)) for p in parts]
        wg, wu, wd = (ffn_w_gate[i].astype(BF16), ffn_w_up[i].astype(BF16), ffn_w_down[i].astype(BF16))
        if i % 2 == 0:
            x, xc = _even_layer(x, xc, mods, cmods, norm1[i], norm2[i], e_w_in[j], e_w_out[j], e_lambda[j],
                                e_subln[j], e_sgu_norm_g[j], e_sgu_norm_b[j], e_sgu_w[j], e_sgu_b[j],
                                wg, wu, wd, i)
        else:
            x = _odd_layer(x, mods, norm1[i], norm2[i], o_w_in[j], o_conv_w[j], o_conv_b[j], o_filt_w1[j],
                           o_filt_b1[j], o_filt_w2[j], o_filt_b2[j], o_filt_w3[j], o_filt_b3[j],
                           o_filt_freq[j], o_filt_w4[j], o_filt_bias[j], o_w_out[j], wg, wu, wd)
    return _final_norm(x, final_norm)
```

```python
import functools
import math

import numpy as np
import jax
import jax.numpy as jnp
from jax import lax
from jax.experimental import pallas as pl
from jax.experimental.pallas import tpu as pltpu

F32 = jnp.float32
BF16 = jnp.bfloat16
HIGHEST = lax.Precision.HIGHEST

GRID_W = 64
NORM_EPS = 1e-6
DIFF_HEADS = 8
DIFF_QK_DIM = 64
DIFF_V_DIM = 2 * DIFF_QK_DIM
DIFF_SCALE = DIFF_QK_DIM ** -0.5
A_WIDTH = DIFF_HEADS * DIFF_V_DIM
Q_COLS = DIFF_HEADS * 2 * DIFF_QK_DIM
ROPE_THETA = 10000.0
ROPE_AXIS_DIM = DIFF_QK_DIM // 2
SUBLN_EPS = 1e-5
SGU_GROUPS = 8
SGU_CHUNK = 128
SGU_CH = 128
B_WIDTH = SGU_GROUPS * SGU_CH
LN_EPS = 1e-5
HYENA_ORDER = 2
SHORT_CONV = 3
FILTER_EMB = 33
FILTER_BANDS = (FILTER_EMB - 1) // 2
FILTER_HIDDEN = 64
DECAY_TARGET = 1e-2
MAX_DECAY = math.log(DECAY_TARGET) / 0.3
MIN_DECAY = math.log(DECAY_TARGET) / 1.5

LANES = 128
ONES_ROWS = 16
ATTN_SUB = 256
DFT_MINOR = 256
VMEM_LIMIT = 48 * 1024 * 1024
VMEM_LIMIT_WIDE = 56 * 1024 * 1024

NT_DIMS = (((1,), (1,)), ((), ()))
TN_DIMS = (((0,), (0,)), ((), ()))


def _params(*sem, vmem=VMEM_LIMIT):
    return pltpu.CompilerParams(dimension_semantics=sem, vmem_limit_bytes=vmem)


def _tile(n, pref):
    return pref if n % pref == 0 else n


def _adaln_kernel(ct_ref, wa_ref, wb_ref, b_ref, o_ref, *, n_rows):
    k = pl.program_id(1)
    a = ct_ref[...]
    a = a * jax.nn.sigmoid(a)
    half = wa_ref.shape[0]
    rows = [jnp.sum(wa_ref[...] * a[:half, r:r + 1], axis=0, keepdims=True)
            + jnp.sum(wb_ref[...] * a[half:, r:r + 1], axis=0, keepdims=True) for r in range(n_rows)]
    rows.append(jnp.zeros((o_ref.shape[0] - n_rows, wa_ref.shape[1]), F32))
    part = jnp.concatenate(rows, axis=0)

    @pl.when(k == 0)
    def _():
        valid = lax.broadcasted_iota(jnp.int32, part.shape, 0) < n_rows
        o_ref[...] = part + jnp.where(valid, b_ref[...], 0.0)

    @pl.when(k > 0)
    def _():
        o_ref[...] += part


def _adaln_all(cond_t, n_rows, ada_w, ada_b):
    depth, d, n6 = ada_w.shape
    tk = _tile(d, 256)
    return pl.pallas_call(
        functools.partial(_adaln_kernel, n_rows=n_rows),
        grid=(depth, d // tk),
        in_specs=[pl.BlockSpec((tk, 8), lambda l, k: (k, 0)),
                  pl.BlockSpec((None, tk // 2, n6), lambda l, k: (l, 2 * k, 0)),
                  pl.BlockSpec((None, tk // 2, n6), lambda l, k: (l, 2 * k + 1, 0)),
                  pl.BlockSpec((None, 1, n6), lambda l, k: (l, 0, 0))],
        out_specs=pl.BlockSpec((None, 8, n6), lambda l, k: (l, 0, 0)),
        out_shape=jax.ShapeDtypeStruct((depth, 8, n6), F32),
        compiler_params=_params("parallel", "arbitrary"),
        name="adaln",
    )(cond_t, ada_w, ada_w, ada_b.reshape(depth, 1, n6))


def _norm_mod_rows(x, g, sh, sc):
    ms = jnp.mean(x * x, axis=-1, keepdims=True)
    y = x * lax.rsqrt(ms + NORM_EPS) * g
    return (y * (1.0 + sc) + sh).astype(BF16)


def _norm_mod(x_ref, g_ref, sh_ref, sc_ref):
    return _norm_mod_rows(x_ref[...], g_ref[...], sh_ref[...], sc_ref[...])


def _inproj_kernel(x_ref, g_ref, sh_ref, sc_ref, w_ref, cos_ref, sin_ref, o_ref, hs_ref, *,
                   n_q, n_qk, rope):
    j = pl.program_id(2)

    @pl.when(j == 0)
    def _():
        hs_ref[...] = _norm_mod(x_ref, g_ref, sh_ref, sc_ref)

    tm, tn = o_ref.shape
    rc = min(tm, 256)

    def by_row_chunks(epilogue):
        for r in range(tm // rc):
            rows = pl.ds(r * rc, rc)
            acc = jnp.dot(hs_ref[rows, :], w_ref[...], preferred_element_type=F32)
            o_ref[rows, :] = epilogue(acc, rows).astype(o_ref.dtype)

    def qk_epilogue(a, rows):
        if rope:
            lane = lax.broadcasted_iota(jnp.int32, a.shape, 1)
            first = (lane & 31) < 16
            partner = jnp.where(first, pltpu.roll(a, tn - 16, 1), pltpu.roll(a, 16, 1))
            reps = tn // LANES
            a = (a * jnp.tile(cos_ref[rows, :], (1, reps))
                 + partner * jnp.tile(sin_ref[rows, :], (1, reps)))
        return jnp.where(j < n_q, a * DIFF_SCALE, a)

    @pl.when(j < n_qk)
    def _():
        by_row_chunks(qk_epilogue)

    @pl.when(j >= n_qk)
    def _():
        by_row_chunks(lambda a, rows: jax.nn.gelu(a))


def _inproj(x, g, sh, sc, w, cos_t, sin_t, rope):
    b, s, d = x.shape
    n = w.shape[1]
    tm = _tile(s, 1024)
    tn = 512
    kern = functools.partial(_inproj_kernel, n_q=Q_COLS // tn, n_qk=2 * Q_COLS // tn, rope=rope)
    return pl.pallas_call(
        kern,
        grid=(b, s // tm, n // tn),
        in_specs=[pl.BlockSpec((None, tm, d), lambda bb, i, j: (bb, i, 0)),
                  pl.BlockSpec((1, d), lambda bb, i, j: (0, 0)),
                  pl.BlockSpec((None, 1, d), lambda bb, i, j: (bb, 0, 0)),
                  pl.BlockSpec((None, 1, d), lambda bb, i, j: (bb, 0, 0)),
                  pl.BlockSpec((d, tn), lambda bb, i, j: (0, j)),
                  pl.BlockSpec((tm, LANES), lambda bb, i, j: (i, 0)),
                  pl.BlockSpec((tm, LANES), lambda bb, i, j: (i, 0))],
        out_specs=pl.BlockSpec((None, tm, tn), lambda bb, i, j: (bb, i, j)),
        out_shape=jax.ShapeDtypeStruct((b, s, n), BF16),
        scratch_shapes=[pltpu.VMEM((tm, d), BF16)],
        compiler_params=_params("parallel", "parallel", "arbitrary"),
        name="inproj",
    )(x, g, sh, sc, w, cos_t, sin_t)


def _inproj_nt_kernel(x_ref, g_ref, sh_ref, sc_ref, wt_ref, *rest, tl, conv):
    if conv:
        cw_ref, halo_ref, o_ref, hs_ref = rest
    else:
        o_ref, hs_ref = rest
    j = pl.program_id(2)

    @pl.when(j == 0)
    def _():
        hs_ref[...] = _norm_mod(x_ref, g_ref, sh_ref, sc_ref)

    n_chunks = o_ref.shape[0]
    if not conv:
        acc = lax.dot_general(wt_ref[...], hs_ref[...], NT_DIMS, preferred_element_type=F32)
        for c in range(n_chunks):
            o_ref[c] = acc[:, c * tl:(c + 1) * tl].astype(o_ref.dtype)
        return

    acc = jnp.concatenate(
        [lax.dot_general(wt_ref[...], hs_ref[c * tl:(c + 1) * tl, :], NT_DIMS, preferred_element_type=F32)
         for c in range(n_chunks)], axis=1)
    tn, tm = acc.shape
    halo = halo_ref[...]
    col = lax.broadcasted_iota(jnp.int32, halo.shape, 1)
    tile = pl.program_id(1)
    before = jnp.sum(jnp.where(col == 2 * tile, halo, 0.0), axis=1, keepdims=True)
    after = jnp.sum(jnp.where(col == 2 * tile + 1, halo, 0.0), axis=1, keepdims=True)
    lane = lax.broadcasted_iota(jnp.int32, (tn, LANES), 1)
    prev = pltpu.roll(acc, 1, 1)
    prev = jnp.concatenate([jnp.where(lane == 0, before, prev[:, :LANES]), prev[:, LANES:]], axis=1)
    nxt = pltpu.roll(acc, tm - 1, 1)
    nxt = jnp.concatenate([nxt[:, :tm - LANES],
                           jnp.where(lane == LANES - 1, after, nxt[:, tm - LANES:])], axis=1)

    def tap(k):
        return jnp.tile(cw_ref[:, k * LANES:(k + 1) * LANES], (1, tm // LANES))

    out = tap(3) + prev * tap(0) + acc * tap(1) + nxt * tap(2)
    for c in range(n_chunks):
        o_ref[c] = out[:, c * tl:(c + 1) * tl].astype(o_ref.dtype)


def _inproj_nt(x, g, sh, sc, wt, tl, conv_params=None, halo=None):
    b, s, d = x.shape
    n = wt.shape[0]
    tm = _tile(s, 1024)
    tn = 512
    conv = conv_params is not None
    kern = functools.partial(_inproj_nt_kernel, tl=tl, conv=conv)
    in_specs = [pl.BlockSpec((None, tm, d), lambda bb, i, j: (bb, i, 0)),
                pl.BlockSpec((1, d), lambda bb, i, j: (0, 0)),
                pl.BlockSpec((None, 1, d), lambda bb, i, j: (bb, 0, 0)),
                pl.BlockSpec((None, 1, d), lambda bb, i, j: (bb, 0, 0)),
                pl.BlockSpec((tn, d), lambda bb, i, j: (j, 0))]
    args = [x, g, sh, sc, wt]
    if conv:
        in_specs += [pl.BlockSpec((tn, 4 * LANES), lambda bb, i, j: (j, 0)),
                     pl.BlockSpec((None, tn, halo.shape[2]), lambda bb, i, j: (bb, j, 0))]
        args += [conv_params, halo]
    return pl.pallas_call(
        kern,
        grid=(b, s // tm, n // tn),
        in_specs=in_specs,
        out_specs=pl.BlockSpec((None, tm // tl, tn, tl), lambda bb, i, j: (bb, i, j, 0)),
        out_shape=jax.ShapeDtypeStruct((b, s // tl, n, tl), BF16),
        scratch_shapes=[pltpu.VMEM((tm, d), BF16)],
        compiler_params=_params("parallel", "parallel", "arbitrary"),
        name="inproj_nt",
    )(*args)


def _ffn_up_kernel(hs_ref, wg_ref, wu_ref, o_ref):
    hs = hs_ref[...]
    gate = jnp.dot(hs, wg_ref[...], preferred_element_type=F32)
    up = jnp.dot(hs, wu_ref[...], preferred_element_type=F32)
    o_ref[...] = (gate * jax.nn.sigmoid(gate) * up).astype(o_ref.dtype)


def _ffn_up(hs, wg, wu):
    b, s, d = hs.shape
    n = wg.shape[1]
    tm = _tile(s, 1024)
    tn = 512
    return pl.pallas_call(
        _ffn_up_kernel,
        grid=(b, s // tm, n // tn),
        in_specs=[pl.BlockSpec((None, tm, d), lambda bb, i, j: (bb, i, 0)),
                  pl.BlockSpec((d, tn), lambda bb, i, j: (0, j)),
                  pl.BlockSpec((d, tn), lambda bb, i, j: (0, j))],
        out_specs=pl.BlockSpec((None, tm, tn), lambda bb, i, j: (bb, i, j)),
        out_shape=jax.ShapeDtypeStruct((b, s, n), BF16),
        compiler_params=_params("parallel", "parallel", "parallel"),
        name="ffn_up",
    )(hs, wg, wu)


def _proj_res_kernel(*refs, ksizes, transposed, next_norm):
    n = len(ksizes)
    a_refs = refs[:n]
    if next_norm:
        w_ref, x_ref, gate_ref, ng_ref, nsh_ref, nsc_ref, o_ref, hs_ref = refs[n:]
    else:
        w_ref, x_ref, gate_ref, o_ref = refs[n:]

    def emit(rows, acc):
        y = x_ref[rows, :] + gate_ref[...] * acc
        o_ref[rows, :] = y
        if next_norm:
            hs_ref[rows, :] = _norm_mod_rows(y, ng_ref[...], nsh_ref[...], nsc_ref[...])

    if transposed:
        (a_ref,) = a_refs
        tl = a_ref.shape[2]
        for c in range(a_ref.shape[0]):
            emit(slice(c * tl, (c + 1) * tl),
                 lax.dot_general(a_ref[c], w_ref[...], TN_DIMS, preferred_element_type=F32))
        return
    acc = None
    off = 0
    for a_ref, ks in zip(a_refs, ksizes):
        part = jnp.dot(a_ref[...], w_ref[off:off + ks, :], preferred_element_type=F32)
        acc = part if acc is None else acc + part
        off += ks
    emit(slice(None), acc)


def _proj_res(a_list, w, x, gate, transposed=False, tm_pref=512, tn_pref=512, vmem=VMEM_LIMIT, next_norm=None):
    b, s, d = x.shape
    ksizes = tuple(a.shape[2] for a in a_list)
    ktot = sum(ksizes)
    tm = _tile(s, tm_pref)
    tn = _tile(d, tn_pref)
    if transposed:
        tl = a_list[0].shape[3]
        a_specs = [pl.BlockSpec((None, tm // tl, ktot, tl), lambda bb, i, j: (bb, i, 0, 0))]
    else:
        a_specs = [pl.BlockSpec((None, tm, ks), lambda bb, i, j: (bb, i, 0)) for ks in ksizes]
    in_specs = a_specs + [pl.BlockSpec((ktot, tn), lambda bb, i, j: (0, j)),
                          pl.BlockSpec((None, tm, tn), lambda bb, i, j: (bb, i, j)),
                          pl.BlockSpec((None, 1, tn), lambda bb, i, j: (bb, 0, j))]
    args = [*a_list, w, x, gate]
    out_spec = pl.BlockSpec((None, tm, tn), lambda bb, i, j: (bb, i, j))
    out_specs, out_shape = out_spec, jax.ShapeDtypeStruct((b, s, d), F32)
    if next_norm is not None:
        assert tn == d
        vmem = VMEM_LIMIT_WIDE
        in_specs += [pl.BlockSpec((1, d), lambda bb, i, j: (0, 0)),
                     pl.BlockSpec((None, 1, d), lambda bb, i, j: (bb, 0, 0)),
                     pl.BlockSpec((None, 1, d), lambda bb, i, j: (bb, 0, 0))]
        args += list(next_norm)
        out_specs, out_shape = [out_spec, out_spec], [out_shape, jax.ShapeDtypeStruct((b, s, d), BF16)]
    kern = functools.partial(_proj_res_kernel, ksizes=ksizes, transposed=transposed,
                             next_norm=next_norm is not None)
    return pl.pallas_call(
        kern,
        grid=(b, s // tm, d // tn),
        in_specs=in_specs,
        out_specs=out_specs,
        out_shape=out_shape,
        compiler_params=_params("parallel", "parallel", "parallel", vmem=vmem),
        name="proj_res",
    )(*args)


def _attn_kernel(*refs, n_chunks, lam_init):
    if n_chunks:
        lam_ref, q_ref, kc_ref, vct_ref, k_ref, vt_ref, g_ref, o_ref, acc_ref = refs[:9]
        n_slots = (len(refs) - 9) // 2
        slots = tuple(zip(refs[9:9 + n_slots], refs[9 + n_slots:]))
    else:
        lam_ref, q_ref, kc_ref, vct_ref, g_ref, o_ref, acc_ref = refs
    n_sub = acc_ref.shape[0]
    tq = acc_ref.shape[-1]
    dv = DIFF_V_DIM
    q = q_ref[...]
    qm = [(q[s * tq:(s + 1) * tq, :DIFF_QK_DIM], q[s * tq:(s + 1) * tq, DIFF_QK_DIM:]) for s in range(n_sub)]

    def scores(kblk, sub):
        s_pair = tuple(lax.dot_general(kblk[:, m * DIFF_QK_DIM:(m + 1) * DIFF_QK_DIM], qm[sub][m], NT_DIMS,
                                       preferred_element_type=F32) for m in range(2))
        return s_pair, tuple(jnp.max(s, axis=0, keepdims=True) for s in s_pair)

    def with_ones(vtblk):
        return jnp.concatenate([vtblk, jnp.ones((ONES_ROWS, vtblk.shape[1]), BF16)], axis=0)

    def absorb(scored, vext, m_pair, sub):
        s_pair, smax_pair = scored
        out = []
        for m in range(2):
            m_old = m_pair[m]
            m_new = jnp.maximum(m_old, smax_pair[m])
            alpha = jnp.exp(m_old - m_new)
            p = jnp.exp((s_pair[m] - m_new).astype(BF16))
            pv = jnp.dot(vext, p, preferred_element_type=F32)
            acc_ref[sub, m] = alpha * acc_ref[sub, m] + pv
            out.append(m_new)
        return tuple(out)

    def store(slot, scored):
        for m in range(2):
            slot[0][m] = scored[0][m]
            slot[1][m] = scored[1][m]

    def load(slot):
        return (slot[0][0], slot[0][1]), (slot[1][0], slot[1][1])

    acc_ref[...] = jnp.zeros_like(acc_ref)
    init = jnp.full((1, tq), -1e30, F32)
    kc, vc = kc_ref[...], with_ones(vct_ref[0])
    m_state = tuple(absorb(scores(kc, s), vc, (init, init), s) for s in range(n_sub))
    if n_chunks:
        group = len(slots) // (2 * n_sub)
        assert n_chunks % (2 * group) == 0

        def slot(half, k, sub):
            return slots[(half * group + k) * n_sub + sub]

        def half_trip(half, base, ms, lookahead):
            ms = list(ms)
            for k in range(group):
                vext = with_ones(vt_ref[base + k])
                for s in range(n_sub):
                    if lookahead:
                        store(slot(1 - half, k, s), scores(k_ref[base + group + k], s))
                    ms[s] = absorb(load(slot(half, k, s)), vext, ms[s], s)
            return tuple(ms)

        def trip(j, ms, lookahead):
            base = 2 * group * j
            ms = half_trip(0, base, ms, True)
            return half_trip(1, base + group, ms, lookahead)

        for k in range(group):
            for s in range(n_sub):
                store(slot(0, k, s), scores(k_ref[k], s))
        n_trips = n_chunks // (2 * group)
        m_state = lax.fori_loop(0, n_trips - 1, lambda j, ms: trip(j, ms, True), m_state)
        m_state = trip(n_trips - 1, m_state, False)

    lp = lam_ref[...]
    lam = (jnp.exp(jnp.sum(lp[0:1] * lp[1:2], axis=-1, keepdims=True))
           - jnp.exp(jnp.sum(lp[2:3] * lp[3:4], axis=-1, keepdims=True)) + lam_init)
    for s in range(n_sub):
        acc0, acc1 = acc_ref[s, 0], acc_ref[s, 1]
        o = acc0[:dv] / acc0[dv:dv + 1] - lam * (acc1[:dv] / acc1[dv:dv + 1])
        ot = o.T
        ms = jnp.mean(ot * ot, axis=-1, keepdims=True)
        on = ot * lax.rsqrt(ms + SUBLN_EPS) * g_ref[...] * (1.0 - lam_init)
        o_ref[s * tq:(s + 1) * tq, :] = on.astype(o_ref.dtype)


def _attention(lam_p, subln, q_arr, kc_arr, vct_arr, k4=None, vt4=None, *, lam_init):
    b, sq = q_arr.shape[0], q_arr.shape[1]
    sc = kc_arr.shape[1]
    h = DIFF_HEADS
    dv = DIFF_V_DIM
    n_sub = next(n for n in (4, 2, 1) if sq % (n * ATTN_SUB) == 0)
    tq = n_sub * ATTN_SUB
    n_chunks = 0 if k4 is None else k4.shape[1]
    in_specs = [pl.BlockSpec((4, DIFF_QK_DIM), lambda bb, hh, i: (0, 0)),
                pl.BlockSpec((None, tq, dv), lambda bb, hh, i: (bb, i, hh)),
                pl.BlockSpec((None, sc, dv), lambda bb, hh, i: (bb, 0, h + hh)),
                pl.BlockSpec((None, 1, dv, sc), lambda bb, hh, i: (bb, 0, hh, 0))]
    args = [lam_p, q_arr, kc_arr, vct_arr]
    if n_chunks:
        tk = k4.shape[2]
        in_specs += [pl.BlockSpec((None, n_chunks, tk, dv), lambda bb, hh, i: (bb, 0, 0, h + hh)),
                     pl.BlockSpec((None, n_chunks, dv, tk), lambda bb, hh, i: (bb, 0, hh, 0))]
        args += [k4, vt4]
    in_specs.append(pl.BlockSpec((1, dv), lambda bb, hh, i: (0, 0)))
    args.append(subln)
    kern = functools.partial(_attn_kernel, n_chunks=n_chunks, lam_init=lam_init)
    scratch = [pltpu.VMEM((n_sub, 2, dv + ONES_ROWS, ATTN_SUB), F32)]
    if n_chunks:
        n_slots = 8 if n_chunks % 16 == 0 else 2 * n_sub
        scratch += [pltpu.VMEM((2, tk, ATTN_SUB), F32) for _ in range(n_slots)]
        scratch += [pltpu.VMEM((2, 1, ATTN_SUB), F32) for _ in range(n_slots)]
    return pl.pallas_call(
        kern,
        grid=(b, h, sq // tq),
        in_specs=in_specs,
        out_specs=pl.BlockSpec((None, tq, dv), lambda bb, hh, i: (bb, i, hh)),
        out_shape=jax.ShapeDtypeStruct((b, sq, A_WIDTH), BF16),
        scratch_shapes=scratch,
        compiler_params=_params("parallel", "parallel", "parallel"),
        name="diff_attn",
    )(*args)


def _sgu_kernel(u_ref, g_ref, ng_ref, nb_ref, w_ref, bs_ref, o_ref):
    for gi in range(SGU_GROUPS):
        cols = slice(gi * SGU_CH, (gi + 1) * SGU_CH)
        w = w_ref[gi]
        for c in range(u_ref.shape[0] // SGU_CHUNK):
            sl = slice(c * SGU_CHUNK, (c + 1) * SGU_CHUNK)
            gg = g_ref[sl, cols].astype(F32)
            mu = jnp.mean(gg, axis=-1, keepdims=True)
            dev = gg - mu
            var = jnp.mean(dev * dev, axis=-1, keepdims=True)
            vv = dev * lax.rsqrt(var + LN_EPS) * ng_ref[gi] + nb_ref[gi]
            mixed = jnp.dot(w, vv.astype(BF16), preferred_element_type=F32) + bs_ref[gi]
            o_ref[sl, cols] = (u_ref[sl, cols].astype(F32) * mixed).astype(o_ref.dtype)


def _sgu(qkug, norm_g, norm_b, w_s, b_s):
    b, s = qkug.shape[0], qkug.shape[1]
    tm = _tile(s, 1024)
    gcount = SGU_GROUPS
    ublk = 2 * Q_COLS // B_WIDTH
    full = lambda shape: pl.BlockSpec(shape, lambda bb, i: (0,) * len(shape))
    return pl.pallas_call(
        _sgu_kernel,
        grid=(b, s // tm),
        in_specs=[pl.BlockSpec((None, tm, B_WIDTH), lambda bb, i: (bb, i, ublk)),
                  pl.BlockSpec((None, tm, B_WIDTH), lambda bb, i: (bb, i, ublk + 1)),
                  full((gcount, 1, SGU_CH)), full((gcount, 1, SGU_CH)),
                  full((gcount, SGU_CHUNK, SGU_CHUNK)), full((gcount, SGU_CHUNK, 1))],
        out_specs=pl.BlockSpec((None, tm, B_WIDTH), lambda bb, i: (bb, i, 0)),
        out_shape=jax.ShapeDtypeStruct((b, s, B_WIDTH), BF16),
        compiler_params=_params("parallel", "parallel"),
        name="sgu",
    )(qkug, qkug, norm_g.reshape(gcount, 1, SGU_CH), norm_b.reshape(gcount, 1, SGU_CH),
      w_s.astype(BF16), b_s.reshape(gcount, SGU_CHUNK, 1))


def _filter_mlp_kernel(f_ref, w1_ref, b1_ref, w2_ref, b2_ref, w3_ref, b3_ref, fr_ref, o_ref):
    def lin(a, w_ref, b_ref):
        return jnp.dot(a, w_ref[...], preferred_element_type=F32, precision=HIGHEST) + b_ref[...]
    fr = fr_ref[...]
    hcur = jnp.sin(fr[0:1] * lin(f_ref[...], w1_ref, b1_ref))
    hcur = jnp.sin(fr[1:2] * lin(hcur, w2_ref, b2_ref))
    o_ref[...] = jnp.sin(fr[2:3] * lin(hcur, w3_ref, b3_ref))


def _filter_mlp(feats, w1, b1, w2, b2, w3, b3, freq):
    rows, emb = feats.shape
    hid = FILTER_HIDDEN
    tr = _tile(rows, 2048)
    full = lambda shape: pl.BlockSpec(shape, lambda i: (0,) * len(shape))
    return pl.pallas_call(
        _filter_mlp_kernel,
        grid=(rows // tr,),
        in_specs=[pl.BlockSpec((tr, emb), lambda i: (i, 0)),
                  full((emb, hid)), full((1, hid)), full((hid, hid)), full((1, hid)),
                  full((hid, hid)), full((1, hid)), full((3, hid))],
        out_specs=pl.BlockSpec((tr, hid), lambda i: (i, 0)),
        out_shape=jax.ShapeDtypeStruct((rows, hid), F32),
        compiler_params=_params("parallel"),
        name="filter_mlp",
    )(feats, w1, b1.reshape(1, hid), w2, b2.reshape(1, hid), w3, b3.reshape(1, hid), freq)


def _filter_raw_kernel(w4t_ref, h_ref, t_ref, delta_ref, o_ref, *, zero_tile):
    rt = pl.program_id(2)

    def split(a):
        hi = a.astype(BF16)
        return hi, (a - hi.astype(F32)).astype(BF16)

    def nt(a, bm):
        return lax.dot_general(a, bm, NT_DIMS, preferred_element_type=F32)

    w_hi, w_lo = split(w4t_ref[...])
    h_hi, h_lo = split(h_ref[...])
    raw = nt(w_hi, h_hi) + (nt(w_hi, h_lo) + nt(w_lo, h_hi))
    raw = raw * jnp.exp(-(delta_ref[...] * t_ref[...]))
    for c in range(o_ref.shape[0]):
        o_ref[c] = raw[:, c * DFT_MINOR:(c + 1) * DFT_MINOR].astype(o_ref.dtype)

    @pl.when(rt == zero_tile)
    def _():
        col = lax.broadcasted_iota(jnp.int32, (raw.shape[0], DFT_MINOR), 1)
        o_ref[0] = jnp.where(col == 0, 0.0, raw[:, :DFT_MINOR]).astype(o_ref.dtype)


def _filter_raw(w4t, h3, t_row, deltas, seq):
    c = w4t.shape[2]
    rows = h3.shape[0]
    tr = _tile(rows // 2, 2048)
    tc = _tile(c, 512)
    half_tiles = seq // tr
    kern = functools.partial(_filter_raw_kernel, zero_tile=half_tiles)
    return pl.pallas_call(
        kern,
        grid=(HYENA_ORDER, c // tc, rows // tr),
        in_specs=[pl.BlockSpec((None, None, tc, FILTER_HIDDEN),
                               lambda n, ci, rt: (n, rt // half_tiles, ci, 0)),
                  pl.BlockSpec((tr, FILTER_HIDDEN), lambda n, ci, rt: (rt, 0)),
                  pl.BlockSpec((1, tr), lambda n, ci, rt: (0, rt)),
                  pl.BlockSpec((tc, 1), lambda n, ci, rt: (ci, 0))],
        out_specs=pl.BlockSpec((None, tr // DFT_MINOR, tc, DFT_MINOR), lambda n, ci, rt: (n, rt, ci, 0)),
        out_shape=jax.ShapeDtypeStruct((HYENA_ORDER, rows // DFT_MINOR, c, DFT_MINOR), BF16),
        compiler_params=_params("parallel", "parallel", "parallel"),
        name="filter_raw",
    )(w4t, h3, t_row, deltas)


def _cmul(ar, ai, br, bi):
    return ar * br - ai * bi, ar * bi + ai * br


def _store_complex(ref, c, n1, re, im):
    r0 = pl.multiple_of(c * n1, n1)
    ref[pl.ds(r0, n1), :DFT_MINOR] = re.astype(ref.dtype)
    ref[pl.ds(r0, n1), DFT_MINOR:] = im.astype(ref.dtype)


def _load_complex(ref, c, n1):
    r0 = pl.multiple_of(c * n1, n1)
    tile = ref[pl.ds(r0, n1), :]
    return tile[:, :DFT_MINOR], tile[:, DFT_MINOR:]


def _filter_spec_kernel(k_ref, f1_ref, twr_ref, twi_ref, f2_ref, o_ref, a2_ref, *, inv_n):
    tc, n1, n2 = k_ref.shape

    def left(c, carry):
        k = k_ref[c].astype(F32)
        nrm = jnp.sum(jnp.sum(jnp.abs(k), axis=1, keepdims=True), axis=0, keepdims=True)
        kn = (k * (inv_n / nrm)).astype(BF16)
        a = jnp.dot(f1_ref[...], kn, preferred_element_type=F32)
        ar, ai = _cmul(a[:n1], a[n1:], twr_ref[...], twi_ref[...])
        _store_complex(a2_ref, c, n1, ar, ai)
        return carry

    lax.fori_loop(0, tc, left, 0, unroll=8)
    z = jnp.dot(a2_ref[...], f2_ref[...], preferred_element_type=F32).reshape(tc, n1, 2 * n2)
    o_ref[:, 0] = z[:, :, :n2].astype(o_ref.dtype)
    o_ref[:, 1] = z[:, :, n2:].astype(o_ref.dtype)


def _filter_spec(k2, f1_full, twr, twi, f2e):
    nc, n1, n2 = k2.shape
    tc = 16
    kern = functools.partial(_filter_spec_kernel, inv_n=1.0 / (n1 * n2))
    full = lambda shape: pl.BlockSpec(shape, lambda i: (0,) * len(shape))
    return pl.pallas_call(
        kern,
        grid=(nc // tc,),
        in_specs=[pl.BlockSpec((tc, n1, n2), lambda i: (i, 0, 0)),
                  full((2 * n1, n1)), full((n1, n2)), full((n1, n2)), full((2 * n2, 2 * n2))],
        out_specs=pl.BlockSpec((tc, 2, n1, n2), lambda i: (i, 0, 0, 0)),
        out_shape=jax.ShapeDtypeStruct((nc, 2, n1, n2), BF16),
        scratch_shapes=[pltpu.VMEM((tc * n1, 2 * n2), BF16)],
        compiler_params=_params("parallel"),
        name="filter_spec",
    )(k2, f1_full, twr, twi, f2e)


def _hyena_kernel(zv_ref, z1_ref, z2_ref, kf_ref, fb_ref, f1_ref, twr_ref, twi_ref, f2_ref, f2c_ref,
                  g1_ref, o_ref, sig_ref, a2_ref, z_ref):
    nb, hr, tc, n2 = zv_ref.shape
    n1 = 2 * hr
    for part, ref in enumerate((zv_ref, z1_ref, z2_ref)):
        for bb in range(nb):
            sig_ref[part, bb] = pltpu.einshape("tcl->ctl", ref[bb].astype(F32))

    for n in range(HYENA_ORDER):
        def fwd_left(c, carry):
            xs = jnp.concatenate([sig_ref[0, 0, c], sig_ref[0, 1, c]], axis=0).astype(BF16)
            a = jnp.dot(f1_ref[...], xs, preferred_element_type=F32)
            ar, ai = _cmul(a[:n1], a[n1:], twr_ref[...], twi_ref[...])
            _store_complex(a2_ref, c, n1, ar, ai)
            return carry

        lax.fori_loop(0, tc, fwd_left, 0, unroll=8)
        z_ref[...] = jnp.dot(a2_ref[...], f2_ref[...], preferred_element_type=F32)

        def spectrum(c, carry):
            zr, zi = _load_complex(z_ref, c, n1)
            wr, wi = _cmul(zr, zi, kf_ref[n, c, 0].astype(F32), kf_ref[n, c, 1].astype(F32))
            _store_complex(a2_ref, c, n1, wr, wi)
            return carry

        lax.fori_loop(0, tc, spectrum, 0, unroll=4)
        z_ref[...] = jnp.dot(a2_ref[...], f2c_ref[...], preferred_element_type=F32)

        def inv_left(c, carry):
            br, bi = _load_complex(z_ref, c, n1)
            br, bi = _cmul(br, bi, twr_ref[...], -twi_ref[...])
            bs = jnp.concatenate([br, bi], axis=0).astype(BF16)
            y = jnp.dot(g1_ref[...], bs, preferred_element_type=F32)
            fb = fb_ref[pl.ds(c, 1), n:n + 1]
            for bb in range(nb):
                sig_ref[0, bb, c] = sig_ref[1 + n, bb, c] * (y[bb * hr:(bb + 1) * hr] + sig_ref[0, bb, c] * fb)
            return carry

        lax.fori_loop(0, tc, inv_left, 0, unroll=8)

    for bb in range(nb):
        o_ref[bb] = pltpu.einshape("ctl->tcl", sig_ref[0, bb]).astype(o_ref.dtype)


def _hyena_core(z4, kf, fbias, f1h, twr, twi, f2e, f2c, g1e):
    nb, hr, c3, n2 = z4.shape
    c = c3 // 3
    n1 = 2 * hr
    tc = 16
    nct = c // tc
    full = lambda shape: pl.BlockSpec(shape, lambda i: (0,) * len(shape))
    zspec = lambda part: pl.BlockSpec((nb, hr, tc, n2), lambda i: (0, 0, i + part * nct, 0))
    return pl.pallas_call(
        _hyena_kernel,
        grid=(nct,),
        in_specs=[zspec(0), zspec(1), zspec(2),
                  pl.BlockSpec((HYENA_ORDER, tc, 2, n1, n2), lambda i: (0, i, 0, 0, 0)),
                  pl.BlockSpec((tc, HYENA_ORDER), lambda i: (i, 0)),
                  full((2 * n1, n1)), full((n1, n2)), full((n1, n2)),
                  full((2 * n2, 2 * n2)), full((2 * n2, 2 * n2)), full((n1, 2 * n1))],
        out_specs=pl.BlockSpec((nb, hr, tc, n2), lambda i: (0, 0, i, 0)),
        out_shape=jax.ShapeDtypeStruct((nb, hr, c, n2), BF16),
        scratch_shapes=[pltpu.VMEM((3, nb, tc, hr, n2), F32),
                        pltpu.VMEM((tc * n1, 2 * n2), BF16),
                        pltpu.VMEM((tc * n1, 2 * n2), F32)],
        compiler_params=_params("parallel"),
        name="hyena_core",
    )(z4, z4, z4, kf, fbias, f1h, twr, twi, f2e, f2c, g1e)


def _dft_tables(n1):
    n2 = DFT_MINOR
    n = n1 * n2
    a1 = 2.0 * np.pi * np.outer(np.arange(n1), np.arange(n1)) / n1
    f1r, f1i = np.cos(a1), -np.sin(a1)
    a2 = 2.0 * np.pi * np.outer(np.arange(n2), np.arange(n2)) / n2
    f2r, f2i = np.cos(a2), -np.sin(a2)
    at = 2.0 * np.pi * np.outer(np.arange(n1), np.arange(n2)) / n
    twr, twi = np.cos(at), -np.sin(at)
    hr = n1 // 2
    f1_full = np.concatenate([f1r, f1i], axis=0)
    f1_half = np.block([[f1r[:, :hr], -f1i[:, :hr]], [f1i[:, :hr], f1r[:, :hr]]])
    f2e = np.block([[f2r, f2i], [-f2i, f2r]])
    f2c = np.block([[f2r, -f2i], [f2i, f2r]])
    gr, gi = f1r[:hr, :], -f1i[:hr, :]
    g1e = np.block([[gr, -gi], [gi, gr]])
    bf = lambda m: jnp.asarray(m, F32).astype(BF16)
    return dict(f1_full=bf(f1_full), f1_half=bf(f1_half), f2e=bf(f2e), f2c=bf(f2c), g1e=bf(g1e),
                twr=jnp.asarray(twr, F32), twi=jnp.asarray(twi, F32))


def _hyena_spectra(seq, fw1, fb1, fw2, fb2, fw3, fb3, ffreq, fw4, tabs):
    l = seq
    width = fw4.shape[1] // (2 * HYENA_ORDER)
    t_fwd = jnp.linspace(0.0, 1.0, l, dtype=F32)
    w = 2.0 * math.pi * jnp.arange(l, dtype=F32)[:, None] / l
    f = jnp.linspace(1e-4, FILTER_BANDS - 1, FILTER_BANDS, dtype=F32)[None, :]
    feats = jnp.concatenate([t_fwd[:, None], jnp.cos(f * w), -jnp.sin(f * w)], axis=-1)

    def two_sided(a):
        return jnp.concatenate([a, a[:1], a[1:][::-1]], axis=0)

    feats = two_sided(feats)
    t_lin = two_sided(t_fwd)
    emb_pad = FILTER_HIDDEN
    feats = jnp.pad(feats, ((0, 0), (0, emb_pad - FILTER_EMB)))
    w1p = jnp.pad(fw1, ((0, emb_pad - FILTER_EMB), (0, 0)))
    h3 = _filter_mlp(feats, w1p, fb1, fw2, fb2, fw3, fb3, ffreq)
    w4t = fw4.reshape(FILTER_HIDDEN, HYENA_ORDER, 2, width).transpose(1, 2, 3, 0)
    deltas = jnp.abs(jnp.linspace(MIN_DECAY, MAX_DECAY, width, dtype=F32))[:, None]
    raw = _filter_raw(w4t, h3, t_lin[None, :], deltas, l)
    n1 = raw.shape[1]
    k2 = raw.transpose(0, 2, 1, 3).reshape(HYENA_ORDER * width, n1, DFT_MINOR)
    kf = _filter_spec(k2, tabs["f1_full"], tabs["twr"], tabs["twi"], tabs["f2e"])
    return kf.reshape(HYENA_ORDER, width, 2, n1, DFT_MINOR)


def _rope_tables(n_tokens):
    tok = jnp.arange(n_tokens)
    row = (tok // GRID_W).astype(F32)
    col = (tok % GRID_W).astype(F32)
    half = ROPE_AXIS_DIM // 2
    inv = 1.0 / (ROPE_THETA ** (jnp.arange(0, ROPE_AXIS_DIM, 2, dtype=F32) / ROPE_AXIS_DIM))
    ang_r = row[:, None] * inv
    ang_c = col[:, None] * inv
    cos64 = jnp.concatenate([jnp.cos(ang_r), jnp.cos(ang_r), jnp.cos(ang_c), jnp.cos(ang_c)], axis=-1)
    sin64 = jnp.concatenate([-jnp.sin(ang_r), jnp.sin(ang_r), -jnp.sin(ang_c), jnp.sin(ang_c)], axis=-1)
    assert cos64.shape[1] == 4 * half == DIFF_QK_DIM
    return jnp.tile(cos64, (1, 2)), jnp.tile(sin64, (1, 2))


def _ffn(x, hs, gate, wg, wu, wd):
    return _proj_res([_ffn_up(hs, wg, wu)], wd, x, gate, tm_pref=1024, vmem=VMEM_LIMIT_WIDE)


def _even_layer(x, xc, mods, cmods, n1g, n2g, w_in, w_out, lam_p, subln, sgu_ng, sgu_nb, sgu_w, sgu_b,
                wg, wu, wd, layer_idx):
    b, s, d = x.shape
    lam_init = 0.8 - 0.6 * math.exp(-0.3 * layer_idx)
    sh1, sc1, g1, sh2, sc2, g2 = mods
    csh1, csc1, cg1, csh2, csc2, cg2 = cmods
    o_k, o_v, o_u = Q_COLS, 2 * Q_COLS, 2 * Q_COLS + A_WIDTH
    w_qkug = jnp.concatenate([w_in[:, :o_v], w_in[:, o_u:]], axis=1).astype(BF16)
    w_vt = w_in[:, o_v:o_u].T.astype(BF16)
    wo = w_out.astype(BF16)
    n1g2 = n1g.reshape(1, d)
    cos_t, sin_t = _rope_tables(s)
    sc_len = xc.shape[1]

    qkug = _inproj(x, n1g2, sh1, sc1, w_qkug, cos_t, sin_t, rope=True)
    cqkug = _inproj(xc, n1g2, csh1, csc1, w_qkug, cos_t[:sc_len], sin_t[:sc_len], rope=False)
    tk = _tile(s, 512)
    vt4 = _inproj_nt(x, n1g2, sh1, sc1, w_vt, tk)
    cvt4 = _inproj_nt(xc, n1g2, csh1, csc1, w_vt, sc_len)
    k4 = qkug.reshape(b, s // tk, tk, qkug.shape[2])
    subln2 = subln.reshape(1, DIFF_V_DIM)

    a_l = _attention(lam_p, subln2, qkug, cqkug, cvt4, k4, vt4, lam_init=lam_init)
    s_l = _sgu(qkug, sgu_ng, sgu_nb, sgu_w, sgu_b)
    n2g2 = n2g.reshape(1, d)
    x, hs = _proj_res([a_l, s_l], wo, x, g1, tn_pref=d, next_norm=(n2g2, sh2, sc2))
    x = _ffn(x, hs, g2, wg, wu, wd)

    a_c = _attention(lam_p, subln2, cqkug, cqkug, cvt4, lam_init=lam_init)
    s_c = _sgu(cqkug, sgu_ng, sgu_nb, sgu_w, sgu_b)
    xc, hs_c = _proj_res([a_c, s_c], wo, xc, cg1, tn_pref=d, next_norm=(n2g2, csh2, csc2))
    xc = _ffn(xc, hs_c, cg2, wg, wu, wd)
    return x, xc


def _odd_layer(x, mods, n1g, n2g, w_in, conv_w, conv_b, fw1, fb1, fw2, fb2, fw3, fb3, ffreq, fw4, fbias,
               w_out, wg, wu, wd):
    b, s, d = x.shape
    sh1, sc1, g1, sh2, sc2, g2 = mods
    width = w_out.shape[0]
    n1 = 2 * s // DFT_MINOR
    tabs = _dft_tables(n1)
    kf = _hyena_spectra(s, fw1, fb1, fw2, fb2, fw3, fb3, ffreq, fw4, tabs)

    n1g2 = n1g.reshape(1, d)
    wt = w_in.T.astype(BF16)
    tm = _tile(s, 1024)
    n_tiles = s // tm
    x_tiles = x.reshape(b, n_tiles, tm, d)
    firsts, lasts = x_tiles[:, :, 0, :], x_tiles[:, :, tm - 1, :]
    before = jnp.concatenate([lasts[:, :1], lasts[:, :-1]], axis=1)
    after = jnp.concatenate([firsts[:, 1:], firsts[:, -1:]], axis=1)
    x_edge = jnp.stack([before, after], axis=2).reshape(b, 2 * n_tiles, d)
    z_edge = _inproj_nt(x_edge, n1g2, sh1, sc1, wt, 2 * n_tiles)
    inside = jnp.ones((2 * n_tiles,), F32).at[0].set(0.0).at[2 * n_tiles - 1].set(0.0)
    halo = z_edge[:, 0].astype(F32) * inside
    conv_params = jnp.repeat(jnp.concatenate([conv_w.T, conv_b[:, None]], axis=1), LANES, axis=1)

    z4 = _inproj_nt(x, n1g2, sh1, sc1, wt, DFT_MINOR, conv_params, halo)
    y4 = _hyena_core(z4, kf, fbias.T, tabs["f1_half"], tabs["twr"], tabs["twi"], tabs["f2e"], tabs["f2c"],
                     tabs["g1e"])
    x, hs = _proj_res([y4], w_out.astype(BF16), x, g1, transposed=True, tn_pref=d,
                      next_norm=(n2g.reshape(1, d), sh2, sc2))
    return _ffn(x, hs, g2, wg, wu, wd)


def _rms_kernel(x_ref, g_ref, o_ref):
    x = x_ref[...]
    ms = jnp.mean(x * x, axis=-1, keepdims=True)
    o_ref[...] = x * lax.rsqrt(ms + NORM_EPS) * g_ref[...]


def _final_norm(x, g):
    b, s, d = x.shape
    tm = _tile(s, 1024)
    return pl.pallas_call(
        _rms_kernel,
        grid=(b, s // tm),
        in_specs=[pl.BlockSpec((None, tm, d), lambda bb, i: (bb, i, 0)),
                  pl.BlockSpec((1, d), lambda bb, i: (0, 0))],
        out_specs=pl.BlockSpec((None, tm, d), lambda bb, i: (bb, i, 0)),
        out_shape=jax.ShapeDtypeStruct((b, s, d), F32),
        compiler_params=_params("parallel", "parallel"),
        name="final_norm",
    )(x, g.reshape(1, d))


def kernel(x, c, ctx, c_ctx, ada_w, ada_b, norm1, norm2, ffn_w_gate, ffn_w_up, ffn_w_down, e_w_in, e_w_out, e_lambda, e_subln, e_sgu_norm_g, e_sgu_norm_b, e_sgu_w, e_sgu_b, o_w_in, o_conv_w, o_conv_b, o_filt_w1, o_filt_b1, o_filt_w2, o_filt_b2, o_filt_w3, o_filt_b3, o_filt_freq, o_filt_w4, o_filt_bias, o_w_out, final_norm):
    b, s, d = x.shape
    depth = ada_w.shape[0]
    assert b == 2, "the long convolution packs exactly two batches into one complex signal"
    assert depth == 2, "odd layers here never carry the context stream"
    cond_t = jnp.zeros((d, 8), F32).at[:, :b].set(c.T).at[:, b].set(c_ctx)
    mod_all = _adaln_all(cond_t, b + 1, ada_w, ada_b)
    xc = ctx
    for i in range(depth):
        j = i // 2
        parts = jnp.split(mod_all[i], 6, axis=-1)
        mods = [p[:b, None, :] for p in parts]
        cmods = [jnp.broadcast_to(p[b:b + 1, None, :], (b, 1, d)) for p in parts]
        wg, wu, wd = (ffn_w_gate[i].astype(BF16), ffn_w_up[i].astype(BF16), ffn_w_down[i].astype(BF16))
        if i % 2 == 0:
            x, xc = _even_layer(x, xc, mods, cmods, norm1[i], norm2[i], e_w_in[j], e_w_out[j], e_lambda[j],
                                e_subln[j], e_sgu_norm_g[j], e_sgu_norm_b[j], e_sgu_w[j], e_sgu_b[j],
                                wg, wu, wd, i)
        else:
            x = _odd_layer(x, mods, norm1[i], norm2[i], o_w_in[j], o_conv_w[j], o_conv_b[j], o_filt_w1[j],
                           o_filt_b1[j], o_filt_w2[j], o_filt_b2[j], o_filt_w3[j], o_filt_b3[j],
                           o_filt_freq[j], o_filt_w4[j], o_filt_bias[j], o_w_out[j], wg, wu, wd)
    return _final_norm(x, final_norm)
```

```python
import functools
import math

import numpy as np
import jax
import jax.numpy as jnp
from jax import lax
from jax.experimental import pallas as pl
from jax.experimental.pallas import tpu as pltpu

F32 = jnp.float32
BF16 = jnp.bfloat16
HIGHEST = lax.Precision.HIGHEST

GRID_W = 64
NORM_EPS = 1e-6
DIFF_HEADS = 8
DIFF_QK_DIM = 64
DIFF_V_DIM = 2 * DIFF_QK_DIM
DIFF_SCALE = DIFF_QK_DIM ** -0.5
A_WIDTH = DIFF_HEADS * DIFF_V_DIM
Q_COLS = DIFF_HEADS * 2 * DIFF_QK_DIM
ROPE_THETA = 10000.0
ROPE_AXIS_DIM = DIFF_QK_DIM // 2
SUBLN_EPS = 1e-5
SGU_GROUPS = 8
SGU_CHUNK = 128
SGU_CH = 128
B_WIDTH = SGU_GROUPS * SGU_CH
LN_EPS = 1e-5
HYENA_ORDER = 2
SHORT_CONV = 3
FILTER_EMB = 33
FILTER_BANDS = (FILTER_EMB - 1) // 2
FILTER_HIDDEN = 64
DECAY_TARGET = 1e-2
MAX_DECAY = math.log(DECAY_TARGET) / 0.3
MIN_DECAY = math.log(DECAY_TARGET) / 1.5

LANES = 128
ONES_ROWS = 16
ATTN_SUB = 256
DFT_MINOR = 256
VMEM_LIMIT = 48 * 1024 * 1024
VMEM_LIMIT_WIDE = 56 * 1024 * 1024

NT_DIMS = (((1,), (1,)), ((), ()))
TN_DIMS = (((0,), (0,)), ((), ()))


def _params(*sem, vmem=VMEM_LIMIT):
    return pltpu.CompilerParams(dimension_semantics=sem, vmem_limit_bytes=vmem)


def _tile(n, pref):
    return pref if n % pref == 0 else n


def _adaln_kernel(ct_ref, wa_ref, wb_ref, b_ref, o_ref, *, n_rows):
    k = pl.program_id(1)
    a = ct_ref[...]
    a = a * jax.nn.sigmoid(a)
    half = wa_ref.shape[0]
    rows = [jnp.sum(wa_ref[...] * a[:half, r:r + 1], axis=0, keepdims=True)
            + jnp.sum(wb_ref[...] * a[half:, r:r + 1], axis=0, keepdims=True) for r in range(n_rows)]
    rows.append(jnp.zeros((o_ref.shape[0] - n_rows, wa_ref.shape[1]), F32))
    part = jnp.concatenate(rows, axis=0)

    @pl.when(k == 0)
    def _():
        valid = lax.broadcasted_iota(jnp.int32, part.shape, 0) < n_rows
        o_ref[...] = part + jnp.where(valid, b_ref[...], 0.0)

    @pl.when(k > 0)
    def _():
        o_ref[...] += part


def _adaln_all(cond_t, n_rows, ada_w, ada_b):
    depth, d, n6 = ada_w.shape
    tk = _tile(d, 256)
    return pl.pallas_call(
        functools.partial(_adaln_kernel, n_rows=n_rows),
        grid=(depth, d // tk),
        in_specs=[pl.BlockSpec((tk, 8), lambda l, k: (k, 0)),
                  pl.BlockSpec((None, tk // 2, n6), lambda l, k: (l, 2 * k, 0)),
                  pl.BlockSpec((None, tk // 2, n6), lambda l, k: (l, 2 * k + 1, 0)),
                  pl.BlockSpec((None, 1, n6), lambda l, k: (l, 0, 0))],
        out_specs=pl.BlockSpec((None, 8, n6), lambda l, k: (l, 0, 0)),
        out_shape=jax.ShapeDtypeStruct((depth, 8, n6), F32),
        compiler_params=_params("parallel", "arbitrary"),
        name="adaln",
    )(cond_t, ada_w, ada_w, ada_b.reshape(depth, 1, n6))


def _norm_mod_rows(x, g, sh, sc):
    ms = jnp.mean(x * x, axis=-1, keepdims=True)
    y = x * lax.rsqrt(ms + NORM_EPS) * g
    return (y * (1.0 + sc) + sh).astype(BF16)


def _norm_mod(x_ref, g_ref, sh_ref, sc_ref):
    return _norm_mod_rows(x_ref[...], g_ref[...], sh_ref[...], sc_ref[...])


def _inproj_kernel(x_ref, g_ref, sh_ref, sc_ref, w_ref, cos_ref, sin_ref, o_ref, hs_ref, *,
                   n_q, n_qk, rope):
    j = pl.program_id(2)

    @pl.when(j == 0)
    def _():
        hs_ref[...] = _norm_mod(x_ref, g_ref, sh_ref, sc_ref)

    tm, tn = o_ref.shape
    rc = min(tm, 256)

    def by_row_chunks(epilogue):
        for r in range(tm // rc):
            rows = pl.ds(r * rc, rc)
            acc = jnp.dot(hs_ref[rows, :], w_ref[...], preferred_element_type=F32)
            o_ref[rows, :] = epilogue(acc, rows).astype(o_ref.dtype)

    def qk_epilogue(a, rows):
        if rope:
            lane = lax.broadcasted_iota(jnp.int32, a.shape, 1)
            first = (lane & 31) < 16
            partner = jnp.where(first, pltpu.roll(a, tn - 16, 1), pltpu.roll(a, 16, 1))
            reps = tn // LANES
            a = (a * jnp.tile(cos_ref[rows, :], (1, reps))
                 + partner * jnp.tile(sin_ref[rows, :], (1, reps)))
        return jnp.where(j < n_q, a * DIFF_SCALE, a)

    @pl.when(j < n_qk)
    def _():
        by_row_chunks(qk_epilogue)

    @pl.when(j >= n_qk)
    def _():
        by_row_chunks(lambda a, rows: jax.nn.gelu(a))


def _inproj(x, g, sh, sc, w, cos_t, sin_t, rope):
    b, s, d = x.shape
    n = w.shape[1]
    tm = _tile(s, 1024)
    tn = 512
    kern = functools.partial(_inproj_kernel, n_q=Q_COLS // tn, n_qk=2 * Q_COLS // tn, rope=rope)
    return pl.pallas_call(
        kern,
        grid=(b, s // tm, n // tn),
        in_specs=[pl.BlockSpec((None, tm, d), lambda bb, i, j: (bb, i, 0)),
                  pl.BlockSpec((1, d), lambda bb, i, j: (0, 0)),
                  pl.BlockSpec((None, 1, d), lambda bb, i, j: (bb, 0, 0)),
                  pl.BlockSpec((None, 1, d), lambda bb, i, j: (bb, 0, 0)),
                  pl.BlockSpec((d, tn), lambda bb, i, j: (0, j)),
                  pl.BlockSpec((tm, LANES), lambda bb, i, j: (i, 0)),
                  pl.BlockSpec((tm, LANES), lambda bb, i, j: (i, 0))],
        out_specs=pl.BlockSpec((None, tm, tn), lambda bb, i, j: (bb, i, j)),
        out_shape=jax.ShapeDtypeStruct((b, s, n), BF16),
        scratch_shapes=[pltpu.VMEM((tm, d), BF16)],
        compiler_params=_params("parallel", "parallel", "arbitrary"),
        name="inproj",
    )(x, g, sh, sc, w, cos_t, sin_t)


def _inproj_nt_kernel(x_ref, g_ref, sh_ref, sc_ref, wt_ref, *rest, tl, conv):
    if conv:
        cw_ref, halo_ref, o_ref, hs_ref = rest
    else:
        o_ref, hs_ref = rest
    j = pl.program_id(2)

    @pl.when(j == 0)
    def _():
        hs_ref[...] = _norm_mod(x_ref, g_ref, sh_ref, sc_ref)

    n_chunks = o_ref.shape[0]
    if not conv:
        acc = lax.dot_general(wt_ref[...], hs_ref[...], NT_DIMS, preferred_element_type=F32)
        for c in range(n_chunks):
            o_ref[c] = acc[:, c * tl:(c + 1) * tl].astype(o_ref.dtype)
        return

    acc = jnp.concatenate(
        [lax.dot_general(wt_ref[...], hs_ref[c * tl:(c + 1) * tl, :], NT_DIMS, preferred_element_type=F32)
         for c in range(n_chunks)], axis=1)
    tn, tm = acc.shape
    halo = halo_ref[...]
    col = lax.broadcasted_iota(jnp.int32, halo.shape, 1)
    tile = pl.program_id(1)
    before = jnp.sum(jnp.where(col == 2 * tile, halo, 0.0), axis=1, keepdims=True)
    after = jnp.sum(jnp.where(col == 2 * tile + 1, halo, 0.0), axis=1, keepdims=True)
    lane = lax.broadcasted_iota(jnp.int32, (tn, LANES), 1)
    prev = pltpu.roll(acc, 1, 1)
    prev = jnp.concatenate([jnp.where(lane == 0, before, prev[:, :LANES]), prev[:, LANES:]], axis=1)
    nxt = pltpu.roll(acc, tm - 1, 1)
    nxt = jnp.concatenate([nxt[:, :tm - LANES],
                           jnp.where(lane == LANES - 1, after, nxt[:, tm - LANES:])], axis=1)

    def tap(k):
        return jnp.tile(cw_ref[:, k * LANES:(k + 1) * LANES], (1, tm // LANES))

    out = tap(3) + prev * tap(0) + acc * tap(1) + nxt * tap(2)
    for c in range(n_chunks):
        o_ref[c] = out[:, c * tl:(c + 1) * tl].astype(o_ref.dtype)


def _inproj_nt(x, g, sh, sc, wt, tl, conv_params=None, halo=None):
    b, s, d = x.shape
    n = wt.shape[0]
    tm = _tile(s, 1024)
    tn = 512
    conv = conv_params is not None
    kern = functools.partial(_inproj_nt_kernel, tl=tl, conv=conv)
    in_specs = [pl.BlockSpec((None, tm, d), lambda bb, i, j: (bb, i, 0)),
                pl.BlockSpec((1, d), lambda bb, i, j: (0, 0)),
                pl.BlockSpec((None, 1, d), lambda bb, i, j: (bb, 0, 0)),
                pl.BlockSpec((None, 1, d), lambda bb, i, j: (bb, 0, 0)),
                pl.BlockSpec((tn, d), lambda bb, i, j: (j, 0))]
    args = [x, g, sh, sc, wt]
    if conv:
        in_specs += [pl.BlockSpec((tn, 4 * LANES), lambda bb, i, j: (j, 0)),
                     pl.BlockSpec((None, tn, halo.shape[2]), lambda bb, i, j: (bb, j, 0))]
        args += [conv_params, halo]
    return pl.pallas_call(
        kern,
        grid=(b, s // tm, n // tn),
        in_specs=in_specs,
        out_specs=pl.BlockSpec((None, tm // tl, tn, tl), lambda bb, i, j: (bb, i, j, 0)),
        out_shape=jax.ShapeDtypeStruct((b, s // tl, n, tl), BF16),
        scratch_shapes=[pltpu.VMEM((tm, d), BF16)],
        compiler_params=_params("parallel", "parallel", "arbitrary"),
        name="inproj_nt",
    )(*args)


def _ffn_up_kernel(hs_ref, wg_ref, wu_ref, o_ref):
    hs = hs_ref[...]
    gate = jnp.dot(hs, wg_ref[...], preferred_element_type=F32)
    up = jnp.dot(hs, wu_ref[...], preferred_element_type=F32)
    o_ref[...] = (gate * jax.nn.sigmoid(gate) * up).astype(o_ref.dtype)


def _ffn_up(hs, wg, wu):
    b, s, d = hs.shape
    n = wg.shape[1]
    tm = _tile(s, 1024)
    tn = 512
    return pl.pallas_call(
        _ffn_up_kernel,
        grid=(b, s // tm, n // tn),
        in_specs=[pl.BlockSpec((None, tm, d), lambda bb, i, j: (bb, i, 0)),
                  pl.BlockSpec((d, tn), lambda bb, i, j: (0, j)),
                  pl.BlockSpec((d, tn), lambda bb, i, j: (0, j))],
        out_specs=pl.BlockSpec((None, tm, tn), lambda bb, i, j: (bb, i, j)),
        out_shape=jax.ShapeDtypeStruct((b, s, n), BF16),
        compiler_params=_params("parallel", "parallel", "parallel"),
        name="ffn_up",
    )(hs, wg, wu)


def _proj_res_kernel(*refs, ksizes, transposed, next_norm):
    n = len(ksizes)
    a_refs = refs[:n]
    if next_norm:
        w_ref, x_ref, gate_ref, ng_ref, nsh_ref, nsc_ref, o_ref, hs_ref = refs[n:]
    else:
        w_ref, x_ref, gate_ref, o_ref = refs[n:]

    def emit(rows, acc):
        y = x_ref[rows, :] + gate_ref[...] * acc
        o_ref[rows, :] = y
        if next_norm:
            hs_ref[rows, :] = _norm_mod_rows(y, ng_ref[...], nsh_ref[...], nsc_ref[...])

    if transposed:
        (a_ref,) = a_refs
        tl = a_ref.shape[2]
        for c in range(a_ref.shape[0]):
            emit(slice(c * tl, (c + 1) * tl),
                 lax.dot_general(a_ref[c], w_ref[...], TN_DIMS, preferred_element_type=F32))
        return
    acc = None
    off = 0
    for a_ref, ks in zip(a_refs, ksizes):
        part = jnp.dot(a_ref[...], w_ref[off:off + ks, :], preferred_element_type=F32)
        acc = part if acc is None else acc + part
        off += ks
    emit(slice(None), acc)


def _proj_res(a_list, w, x, gate, transposed=False, tm_pref=512, tn_pref=512, vmem=VMEM_LIMIT, next_norm=None):
    b, s, d = x.shape
    ksizes = tuple(a.shape[2] for a in a_list)
    ktot = sum(ksizes)
    tm = _tile(s, tm_pref)
    tn = _tile(d, tn_pref)
    if transposed:
        tl = a_list[0].shape[3]
        a_specs = [pl.BlockSpec((None, tm // tl, ktot, tl), lambda bb, i, j: (bb, i, 0, 0))]
    else:
        a_specs = [pl.BlockSpec((None, tm, ks), lambda bb, i, j: (bb, i, 0)) for ks in ksizes]
    in_specs = a_specs + [pl.BlockSpec((ktot, tn), lambda bb, i, j: (0, j)),
                          pl.BlockSpec((None, tm, tn), lambda bb, i, j: (bb, i, j)),
                          pl.BlockSpec((None, 1, tn), lambda bb, i, j: (bb, 0, j))]
    args = [*a_list, w, x, gate]
    out_spec = pl.BlockSpec((None, tm, tn), lambda bb, i, j: (bb, i, j))
    out_specs, out_shape = out_spec, jax.ShapeDtypeStruct((b, s, d), F32)
    if next_norm is not None:
        assert tn == d
        vmem = VMEM_LIMIT_WIDE
        in_specs += [pl.BlockSpec((1, d), lambda bb, i, j: (0, 0)),
                     pl.BlockSpec((None, 1, d), lambda bb, i, j: (bb, 0, 0)),
                     pl.BlockSpec((None, 1, d), lambda bb, i, j: (bb, 0, 0))]
        args += list(next_norm)
        out_specs, out_shape = [out_spec, out_spec], [out_shape, jax.ShapeDtypeStruct((b, s, d), BF16)]
    kern = functools.partial(_proj_res_kernel, ksizes=ksizes, transposed=transposed,
                             next_norm=next_norm is not None)
    return pl.pallas_call(
        kern,
        grid=(b, s // tm, d // tn),
        in_specs=in_specs,
        out_specs=out_specs,
        out_shape=out_shape,
        compiler_params=_params("parallel", "parallel", "parallel", vmem=vmem),
        name="proj_res",
    )(*args)


def _attn_kernel(*refs, n_chunks, lam_init):
    if n_chunks:
        lam_ref, q_ref, kc_ref, vct_ref, k_ref, vt_ref, g_ref, o_ref, acc_ref = refs[:9]
        n_slots = (len(refs) - 9) // 2
        slots = tuple(zip(refs[9:9 + n_slots], refs[9 + n_slots:]))
    else:
        lam_ref, q_ref, kc_ref, vct_ref, g_ref, o_ref, acc_ref = refs
    n_sub = acc_ref.shape[0]
    tq = acc_ref.shape[-1]
    dv = DIFF_V_DIM
    q = q_ref[...]
    qm = [(q[s * tq:(s + 1) * tq, :DIFF_QK_DIM], q[s * tq:(s + 1) * tq, DIFF_QK_DIM:]) for s in range(n_sub)]

    def scores(kblk, sub):
        s_pair = tuple(lax.dot_general(kblk[:, m * DIFF_QK_DIM:(m + 1) * DIFF_QK_DIM], qm[sub][m], NT_DIMS,
                                       preferred_element_type=F32) for m in range(2))
        return s_pair, tuple(jnp.max(s, axis=0, keepdims=True) for s in s_pair)

    def with_ones(vtblk):
        return jnp.concatenate([vtblk, jnp.ones((ONES_ROWS, vtblk.shape[1]), BF16)], axis=0)

    def absorb(scored, vext, m_pair, sub):
        s_pair, smax_pair = scored
        out = []
        for m in range(2):
            m_old = m_pair[m]
            m_new = jnp.maximum(m_old, smax_pair[m])
            alpha = jnp.exp(m_old - m_new)
            p = jnp.exp((s_pair[m] - m_new).astype(BF16))
            pv = jnp.dot(vext, p, preferred_element_type=F32)
            acc_ref[sub, m] = alpha * acc_ref[sub, m] + pv
            out.append(m_new)
        return tuple(out)

    def store(slot, scored):
        for m in range(2):
            slot[0][m] = scored[0][m]
            slot[1][m] = scored[1][m]

    def load(slot):
        return (slot[0][0], slot[0][1]), (slot[1][0], slot[1][1])

    acc_ref[...] = jnp.zeros_like(acc_ref)
    init = jnp.full((1, tq), -1e30, F32)
    kc, vc = kc_ref[...], with_ones(vct_ref[0])
    m_state = tuple(absorb(scores(kc, s), vc, (init, init), s) for s in range(n_sub))
    if n_chunks:
        group = len(slots) // (2 * n_sub)
        assert n_chunks % (2 * group) == 0

        def slot(half, k, sub):
            return slots[(half * group + k) * n_sub + sub]

        def half_trip(half, base, ms, lookahead):
            ms = list(ms)
            for k in range(group):
                vext = with_ones(vt_ref[base + k])
                for s in range(n_sub):
                    if lookahead:
                        store(slot(1 - half, k, s), scores(k_ref[base + group + k], s))
                    ms[s] = absorb(load(slot(half, k, s)), vext, ms[s], s)
            return tuple(ms)

        def trip(j, ms, lookahead):
            base = 2 * group * j
            ms = half_trip(0, base, ms, True)
            return half_trip(1, base + group, ms, lookahead)

        for k in range(group):
            for s in range(n_sub):
                store(slot(0, k, s), scores(k_ref[k], s))
        n_trips = n_chunks // (2 * group)
        m_state = lax.fori_loop(0, n_trips - 1, lambda j, ms: trip(j, ms, True), m_state)
        m_state = trip(n_trips - 1, m_state, False)

    lp = lam_ref[...]
    lam = (jnp.exp(jnp.sum(lp[0:1] * lp[1:2], axis=-1, keepdims=True))
           - jnp.exp(jnp.sum(lp[2:3] * lp[3:4], axis=-1, keepdims=True)) + lam_init)
    for s in range(n_sub):
        acc0, acc1 = acc_ref[s, 0], acc_ref[s, 1]
        o = acc0[:dv] / acc0[dv:dv + 1] - lam * (acc1[:dv] / acc1[dv:dv + 1])
        ot = o.T
        ms = jnp.mean(ot * ot, axis=-1, keepdims=True)
        on = ot * lax.rsqrt(ms + SUBLN_EPS) * g_ref[...] * (1.0 - lam_init)
        o_ref[s * tq:(s + 1) * tq, :] = on.astype(o_ref.dtype)


def _attention(lam_p, subln, q_arr, kc_arr, vct_arr, k4=None, vt4=None, *, lam_init):
    b, sq = q_arr.shape[0], q_arr.shape[1]
    sc = kc_arr.shape[1]
    h = DIFF_HEADS
    dv = DIFF_V_DIM
    n_sub = next(n for n in (4, 2, 1) if sq % (n * ATTN_SUB) == 0)
    tq = n_sub * ATTN_SUB
    n_chunks = 0 if k4 is None else k4.shape[1]
    in_specs = [pl.BlockSpec((4, DIFF_QK_DIM), lambda bb, hh, i: (0, 0)),
                pl.BlockSpec((None, tq, dv), lambda bb, hh, i: (bb, i, hh)),
                pl.BlockSpec((None, sc, dv), lambda bb, hh, i: (bb, 0, h + hh)),
                pl.BlockSpec((None, 1, dv, sc), lambda bb, hh, i: (bb, 0, hh, 0))]
    args = [lam_p, q_arr, kc_arr, vct_arr]
    if n_chunks:
        tk = k4.shape[2]
        in_specs += [pl.BlockSpec((None, n_chunks, tk, dv), lambda bb, hh, i: (bb, 0, 0, h + hh)),
                     pl.BlockSpec((None, n_chunks, dv, tk), lambda bb, hh, i: (bb, 0, hh, 0))]
        args += [k4, vt4]
    in_specs.append(pl.BlockSpec((1, dv), lambda bb, hh, i: (0, 0)))
    args.append(subln)
    kern = functools.partial(_attn_kernel, n_chunks=n_chunks, lam_init=lam_init)
    scratch = [pltpu.VMEM((n_sub, 2, dv + ONES_ROWS, ATTN_SUB), F32)]
    if n_chunks:
        n_slots = 8 if n_chunks % 16 == 0 else 2 * n_sub
        scratch += [pltpu.VMEM((2, tk, ATTN_SUB), F32) for _ in range(n_slots)]
        scratch += [pltpu.VMEM((2, 1, ATTN_SUB), F32) for _ in range(n_slots)]
    return pl.pallas_call(
        kern,
        grid=(b, h, sq // tq),
        in_specs=in_specs,
        out_specs=pl.BlockSpec((None, tq, dv), lambda bb, hh, i: (bb, i, hh)),
        out_shape=jax.ShapeDtypeStruct((b, sq, A_WIDTH), BF16),
        scratch_shapes=scratch,
        compiler_params=_params("parallel", "parallel", "parallel"),
        name="diff_attn",
    )(*args)


def _sgu_kernel(u_ref, g_ref, ng_ref, nb_ref, w_ref, bs_ref, o_ref):
    for gi in range(SGU_GROUPS):
        cols = slice(gi * SGU_CH, (gi + 1) * SGU_CH)
        w = w_ref[gi]
        for c in range(u_ref.shape[0] // SGU_CHUNK):
            sl = slice(c * SGU_CHUNK, (c + 1) * SGU_CHUNK)
            gg = g_ref[sl, cols].astype(F32)
            mu = jnp.mean(gg, axis=-1, keepdims=True)
            dev = gg - mu
            var = jnp.mean(dev * dev, axis=-1, keepdims=True)
            vv = dev * lax.rsqrt(var + LN_EPS) * ng_ref[gi] + nb_ref[gi]
            mixed = jnp.dot(w, vv.astype(BF16), preferred_element_type=F32) + bs_ref[gi]
            o_ref[sl, cols] = (u_ref[sl, cols].astype(F32) * mixed).astype(o_ref.dtype)


def _sgu(qkug, norm_g, norm_b, w_s, b_s):
    b, s = qkug.shape[0], qkug.shape[1]
    tm = _tile(s, 1024)
    gcount = SGU_GROUPS
    ublk = 2 * Q_COLS // B_WIDTH
    full = lambda shape: pl.BlockSpec(shape, lambda bb, i: (0,) * len(shape))
    return pl.pallas_call(
        _sgu_kernel,
        grid=(b, s // tm),
        in_specs=[pl.BlockSpec((None, tm, B_WIDTH), lambda bb, i: (bb, i, ublk)),
                  pl.BlockSpec((None, tm, B_WIDTH), lambda bb, i: (bb, i, ublk + 1)),
                  full((gcount, 1, SGU_CH)), full((gcount, 1, SGU_CH)),
                  full((gcount, SGU_CHUNK, SGU_CHUNK)), full((gcount, SGU_CHUNK, 1))],
        out_specs=pl.BlockSpec((None, tm, B_WIDTH), lambda bb, i: (bb, i, 0)),
        out_shape=jax.ShapeDtypeStruct((b, s, B_WIDTH), BF16),
        compiler_params=_params("parallel", "parallel"),
        name="sgu",
    )(qkug, qkug, norm_g.reshape(gcount, 1, SGU_CH), norm_b.reshape(gcount, 1, SGU_CH),
      w_s.astype(BF16), b_s.reshape(gcount, SGU_CHUNK, 1))


def _filter_mlp_kernel(f_ref, w1_ref, b1_ref, w2_ref, b2_ref, w3_ref, b3_ref, fr_ref, o_ref):
    def lin(a, w_ref, b_ref):
        return jnp.dot(a, w_ref[...], preferred_element_type=F32, precision=HIGHEST) + b_ref[...]
    fr = fr_ref[...]
    hcur = jnp.sin(fr[0:1] * lin(f_ref[...], w1_ref, b1_ref))
    hcur = jnp.sin(fr[1:2] * lin(hcur, w2_ref, b2_ref))
    o_ref[...] = jnp.sin(fr[2:3] * lin(hcur, w3_ref, b3_ref))


def _filter_mlp(feats, w1, b1, w2, b2, w3, b3, freq):
    rows, emb = feats.shape
    hid = FILTER_HIDDEN
    tr = _tile(rows, 2048)
    full = lambda shape: pl.BlockSpec(shape, lambda i: (0,) * len(shape))
    return pl.pallas_call(
        _filter_mlp_kernel,
        grid=(rows // tr,),
        in_specs=[pl.BlockSpec((tr, emb), lambda i: (i, 0)),
                  full((emb, hid)), full((1, hid)), full((hid, hid)), full((1, hid)),
                  full((hid, hid)), full((1, hid)), full((3, hid))],
        out_specs=pl.BlockSpec((tr, hid), lambda i: (i, 0)),
        out_shape=jax.ShapeDtypeStruct((rows, hid), F32),
        compiler_params=_params("parallel"),
        name="filter_mlp",
    )(feats, w1, b1.reshape(1, hid), w2, b2.reshape(1, hid), w3, b3.reshape(1, hid), freq)


def _filter_raw_kernel(w4t_ref, h_ref, t_ref, delta_ref, o_ref, *, zero_tile):
    rt = pl.program_id(2)

    def split(a):
        hi = a.astype(BF16)
        return hi, (a - hi.astype(F32)).astype(BF16)

    def nt(a, bm):
        return lax.dot_general(a, bm, NT_DIMS, preferred_element_type=F32)

    w_hi, w_lo = split(w4t_ref[...])
    h_hi, h_lo = split(h_ref[...])
    raw = nt(w_hi, h_hi) + (nt(w_hi, h_lo) + nt(w_lo, h_hi))
    raw = raw * jnp.exp(-(delta_ref[...] * t_ref[...]))
    for c in range(o_ref.shape[0]):
        o_ref[c] = raw[:, c * DFT_MINOR:(c + 1) * DFT_MINOR].astype(o_ref.dtype)

    @pl.when(rt == zero_tile)
    def _():
        col = lax.broadcasted_iota(jnp.int32, (raw.shape[0], DFT_MINOR), 1)
        o_ref[0] = jnp.where(col == 0, 0.0, raw[:, :DFT_MINOR]).astype(o_ref.dtype)


def _filter_raw(w4t, h3, t_row, deltas, seq):
    c = w4t.shape[2]
    rows = h3.shape[0]
    tr = _tile(rows // 2, 2048)
    tc = _tile(c, 512)
    half_tiles = seq // tr
    kern = functools.partial(_filter_raw_kernel, zero_tile=half_tiles)
    return pl.pallas_call(
        kern,
        grid=(HYENA_ORDER, c // tc, rows // tr),
        in_specs=[pl.BlockSpec((None, None, tc, FILTER_HIDDEN),
                               lambda n, ci, rt: (n, rt // half_tiles, ci, 0)),
                  pl.BlockSpec((tr, FILTER_HIDDEN), lambda n, ci, rt: (rt, 0)),
                  pl.BlockSpec((1, tr), lambda n, ci, rt: (0, rt)),
                  pl.BlockSpec((tc, 1), lambda n, ci, rt: (ci, 0))],
        out_specs=pl.BlockSpec((None, tr // DFT_MINOR, tc, DFT_MINOR), lambda n, ci, rt: (n, rt, ci, 0)),
        out_shape=jax.ShapeDtypeStruct((HYENA_ORDER, rows // DFT_MINOR, c, DFT_MINOR), BF16),
        compiler_params=_params("parallel", "parallel", "parallel"),
        name="filter_raw",
    )(w4t, h3, t_row, deltas)


def _cmul(ar, ai, br, bi):
    return ar * br - ai * bi, ar * bi + ai * br


def _store_complex(ref, c, n1, re, im):
    r0 = pl.multiple_of(c * n1, n1)
    ref[pl.ds(r0, n1), :DFT_MINOR] = re.astype(ref.dtype)
    ref[pl.ds(r0, n1), DFT_MINOR:] = im.astype(ref.dtype)


def _load_complex(ref, c, n1):
    r0 = pl.multiple_of(c * n1, n1)
    tile = ref[pl.ds(r0, n1), :]
    return tile[:, :DFT_MINOR], tile[:, DFT_MINOR:]


def _filter_spec_kernel(k_ref, f1_ref, twr_ref, twi_ref, f2_ref, o_ref, a2_ref, *, inv_n):
    tc, n1, n2 = k_ref.shape

    def left(c, carry):
        k = k_ref[c].astype(F32)
        nrm = jnp.sum(jnp.sum(jnp.abs(k), axis=1, keepdims=True), axis=0, keepdims=True)
        kn = (k * (inv_n / nrm)).astype(BF16)
        a = jnp.dot(f1_ref[...], kn, preferred_element_type=F32)
        ar, ai = _cmul(a[:n1], a[n1:], twr_ref[...], twi_ref[...])
        _store_complex(a2_ref, c, n1, ar, ai)
        return carry

    lax.fori_loop(0, tc, left, 0, unroll=True)
    z = jnp.dot(a2_ref[...], f2_ref[...], preferred_element_type=F32).reshape(tc, n1, 2 * n2)
    o_ref[:, 0] = z[:, :, :n2].astype(o_ref.dtype)
    o_ref[:, 1] = z[:, :, n2:].astype(o_ref.dtype)


def _filter_spec(k2, f1_full, twr, twi, f2e):
    nc, n1, n2 = k2.shape
    tc = 16
    kern = functools.partial(_filter_spec_kernel, inv_n=1.0 / (n1 * n2))
    full = lambda shape: pl.BlockSpec(shape, lambda i: (0,) * len(shape))
    return pl.pallas_call(
        kern,
        grid=(nc // tc,),
        in_specs=[pl.BlockSpec((tc, n1, n2), lambda i: (i, 0, 0)),
                  full((2 * n1, n1)), full((n1, n2)), full((n1, n2)), full((2 * n2, 2 * n2))],
        out_specs=pl.BlockSpec((tc, 2, n1, n2), lambda i: (i, 0, 0, 0)),
        out_shape=jax.ShapeDtypeStruct((nc, 2, n1, n2), BF16),
        scratch_shapes=[pltpu.VMEM((tc * n1, 2 * n2), BF16)],
        compiler_params=_params("parallel"),
        name="filter_spec",
    )(k2, f1_full, twr, twi, f2e)


def _hyena_kernel(zv_ref, z1_ref, z2_ref, kf_ref, fb_ref, f1_ref, twr_ref, twi_ref, f2_ref, f2c_ref,
                  g1_ref, o_ref, sig_ref, a2_ref, z_ref):
    nb, hr, tc, n2 = zv_ref.shape
    n1 = 2 * hr
    for part, ref in enumerate((zv_ref, z1_ref, z2_ref)):
        for bb in range(nb):
            sig_ref[part, bb] = pltpu.einshape("tcl->ctl", ref[bb].astype(F32))

    for n in range(HYENA_ORDER):
        def fwd_left(c, carry):
            xs = jnp.concatenate([sig_ref[0, 0, c], sig_ref[0, 1, c]], axis=0).astype(BF16)
            a = jnp.dot(f1_ref[...], xs, preferred_element_type=F32)
            ar, ai = _cmul(a[:n1], a[n1:], twr_ref[...], twi_ref[...])
            _store_complex(a2_ref, c, n1, ar, ai)
            return carry

        lax.fori_loop(0, tc, fwd_left, 0, unroll=True)
        z_ref[...] = jnp.dot(a2_ref[...], f2_ref[...], preferred_element_type=F32)

        def spectrum(c, carry):
            zr, zi = _load_complex(z_ref, c, n1)
            wr, wi = _cmul(zr, zi, kf_ref[n, c, 0].astype(F32), kf_ref[n, c, 1].astype(F32))
            _store_complex(a2_ref, c, n1, wr, wi)
            return carry

        lax.fori_loop(0, tc, spectrum, 0, unroll=True)
        z_ref[...] = jnp.dot(a2_ref[...], f2c_ref[...], preferred_element_type=F32)

        def inv_left(c, carry):
            br, bi = _load_complex(z_ref, c, n1)
            br, bi = _cmul(br, bi, twr_ref[...], -twi_ref[...])
            bs = jnp.concatenate([br, bi], axis=0).astype(BF16)
            y = jnp.dot(g1_ref[...], bs, preferred_element_type=F32)
            fb = fb_ref[pl.ds(c, 1), n:n + 1]
            for bb in range(nb):
                sig_ref[0, bb, c] = sig_ref[1 + n, bb, c] * (y[bb * hr:(bb + 1) * hr] + sig_ref[0, bb, c] * fb)
            return carry

        lax.fori_loop(0, tc, inv_left, 0, unroll=True)

    for bb in range(nb):
        o_ref[bb] = pltpu.einshape("ctl->tcl", sig_ref[0, bb]).astype(o_ref.dtype)


def _hyena_core(z4, kf, fbias, f1h, twr, twi, f2e, f2c, g1e):
    nb, hr, c3, n2 = z4.shape
    c = c3 // 3
    n1 = 2 * hr
    tc = 16
    nct = c // tc
    full = lambda shape: pl.BlockSpec(shape, lambda i: (0,) * len(shape))
    zspec = lambda part: pl.BlockSpec((nb, hr, tc, n2), lambda i: (0, 0, i + part * nct, 0))
    return pl.pallas_call(
        _hyena_kernel,
        grid=(nct,),
        in_specs=[zspec(0), zspec(1), zspec(2),
                  pl.BlockSpec((HYENA_ORDER, tc, 2, n1, n2), lambda i: (0, i, 0, 0, 0)),
                  pl.BlockSpec((tc, HYENA_ORDER), lambda i: (i, 0)),
                  full((2 * n1, n1)), full((n1, n2)), full((n1, n2)),
                  full((2 * n2, 2 * n2)), full((2 * n2, 2 * n2)), full((n1, 2 * n1))],
        out_specs=pl.BlockSpec((nb, hr, tc, n2), lambda i: (0, 0, i, 0)),
        out_shape=jax.ShapeDtypeStruct((nb, hr, c, n2), BF16),
        scratch_shapes=[pltpu.VMEM((3, nb, tc, hr, n2), F32),
                        pltpu.VMEM((tc * n1, 2 * n2), BF16),
                        pltpu.VMEM((tc * n1, 2 * n2), F32)],
        compiler_params=_params("parallel"),
        name="hyena_core",
    )(z4, z4, z4, kf, fbias, f1h, twr, twi, f2e, f2c, g1e)


def _dft_tables(n1):
    n2 = DFT_MINOR
    n = n1 * n2
    a1 = 2.0 * np.pi * np.outer(np.arange(n1), np.arange(n1)) / n1
    f1r, f1i = np.cos(a1), -np.sin(a1)
    a2 = 2.0 * np.pi * np.outer(np.arange(n2), np.arange(n2)) / n2
    f2r, f2i = np.cos(a2), -np.sin(a2)
    at = 2.0 * np.pi * np.outer(np.arange(n1), np.arange(n2)) / n
    twr, twi = np.cos(at), -np.sin(at)
    hr = n1 // 2
    f1_full = np.concatenate([f1r, f1i], axis=0)
    f1_half = np.block([[f1r[:, :hr], -f1i[:, :hr]], [f1i[:, :hr], f1r[:, :hr]]])
    f2e = np.block([[f2r, f2i], [-f2i, f2r]])
    f2c = np.block([[f2r, -f2i], [f2i, f2r]])
    gr, gi = f1r[:hr, :], -f1i[:hr, :]
    g1e = np.block([[gr, -gi], [gi, gr]])
    bf = lambda m: jnp.asarray(m, F32).astype(BF16)
    return dict(f1_full=bf(f1_full), f1_half=bf(f1_half), f2e=bf(f2e), f2c=bf(f2c), g1e=bf(g1e),
                twr=jnp.asarray(twr, F32), twi=jnp.asarray(twi, F32))


def _hyena_spectra(seq, fw1, fb1, fw2, fb2, fw3, fb3, ffreq, fw4, tabs):
    l = seq
    width = fw4.shape[1] // (2 * HYENA_ORDER)
    t_fwd = jnp.linspace(0.0, 1.0, l, dtype=F32)
    w = 2.0 * math.pi * jnp.arange(l, dtype=F32)[:, None] / l
    f = jnp.linspace(1e-4, FILTER_BANDS - 1, FILTER_BANDS, dtype=F32)[None, :]
    feats = jnp.concatenate([t_fwd[:, None], jnp.cos(f * w), -jnp.sin(f * w)], axis=-1)

    def two_sided(a):
        return jnp.concatenate([a, a[:1], a[1:][::-1]], axis=0)

    t_lin = two_sided(t_fwd)
    emb_pad = FILTER_HIDDEN
    feats = jnp.pad(feats, ((0, 0), (0, emb_pad - FILTER_EMB)))
    w1p = jnp.pad(fw1, ((0, emb_pad - FILTER_EMB), (0, 0)))
    h3 = two_sided(_filter_mlp(feats, w1p, fb1, fw2, fb2, fw3, fb3, ffreq))
    w4t = fw4.reshape(FILTER_HIDDEN, HYENA_ORDER, 2, width).transpose(1, 2, 3, 0)
    deltas = jnp.abs(jnp.linspace(MIN_DECAY, MAX_DECAY, width, dtype=F32))[:, None]
    raw = _filter_raw(w4t, h3, t_lin[None, :], deltas, l)
    n1 = raw.shape[1]
    k2 = raw.transpose(0, 2, 1, 3).reshape(HYENA_ORDER * width, n1, DFT_MINOR)
    kf = _filter_spec(k2, tabs["f1_full"], tabs["twr"], tabs["twi"], tabs["f2e"])
    return kf.reshape(HYENA_ORDER, width, 2, n1, DFT_MINOR)


def _rope_tables(n_tokens):
    tok = jnp.arange(n_tokens)
    row = (tok // GRID_W).astype(F32)
    col = (tok % GRID_W).astype(F32)
    half = ROPE_AXIS_DIM // 2
    inv = 1.0 / (ROPE_THETA ** (jnp.arange(0, ROPE_AXIS_DIM, 2, dtype=F32) / ROPE_AXIS_DIM))
    ang_r = row[:, None] * inv
    ang_c = col[:, None] * inv
    cos64 = jnp.concatenate([jnp.cos(ang_r), jnp.cos(ang_r), jnp.cos(ang_c), jnp.cos(ang_c)], axis=-1)
    sin64 = jnp.concatenate([-jnp.sin(ang_r), jnp.sin(ang_r), -jnp.sin(ang_c), jnp.sin(ang_c)], axis=-1)
    assert cos64.shape[1] == 4 * half == DIFF_QK_DIM
    return jnp.tile(cos64, (1, 2)), jnp.tile(sin64, (1, 2))


def _ffn(x, hs, gate, wg, wu, wd):
    return _proj_res([_ffn_up(hs, wg, wu)], wd, x, gate, tm_pref=1024, vmem=VMEM_LIMIT_WIDE)


def _even_layer(x, xc, mods, cmods, n1g, n2g, w_in, w_out, lam_p, subln, sgu_ng, sgu_nb, sgu_w, sgu_b,
                wg, wu, wd, layer_idx):
    b, s, d = x.shape
    lam_init = 0.8 - 0.6 * math.exp(-0.3 * layer_idx)
    sh1, sc1, g1, sh2, sc2, g2 = mods
    csh1, csc1, cg1, csh2, csc2, cg2 = cmods
    o_k, o_v, o_u = Q_COLS, 2 * Q_COLS, 2 * Q_COLS + A_WIDTH
    w_qkug = jnp.concatenate([w_in[:, :o_v], w_in[:, o_u:]], axis=1).astype(BF16)
    w_vt = w_in[:, o_v:o_u].T.astype(BF16)
    wo = w_out.astype(BF16)
    n1g2 = n1g.reshape(1, d)
    cos_t, sin_t = _rope_tables(s)
    sc_len = xc.shape[1]

    qkug = _inproj(x, n1g2, sh1, sc1, w_qkug, cos_t, sin_t, rope=True)
    cqkug = _inproj(xc, n1g2, csh1, csc1, w_qkug, cos_t[:sc_len], sin_t[:sc_len], rope=False)
    tk = _tile(s, 512)
    vt4 = _inproj_nt(x, n1g2, sh1, sc1, w_vt, tk)
    cvt4 = _inproj_nt(xc, n1g2, csh1, csc1, w_vt, sc_len)
    k4 = qkug.reshape(b, s // tk, tk, qkug.shape[2])
    subln2 = subln.reshape(1, DIFF_V_DIM)

    a_l = _attention(lam_p, subln2, qkug, cqkug, cvt4, k4, vt4, lam_init=lam_init)
    s_l = _sgu(qkug, sgu_ng, sgu_nb, sgu_w, sgu_b)
    n2g2 = n2g.reshape(1, d)
    x, hs = _proj_res([a_l, s_l], wo, x, g1, tn_pref=d, next_norm=(n2g2, sh2, sc2))
    x = _ffn(x, hs, g2, wg, wu, wd)

    a_c = _attention(lam_p, subln2, cqkug, cqkug, cvt4, lam_init=lam_init)
    s_c = _sgu(cqkug, sgu_ng, sgu_nb, sgu_w, sgu_b)
    xc, hs_c = _proj_res([a_c, s_c], wo, xc, cg1, tn_pref=d, next_norm=(n2g2, csh2, csc2))
    xc = _ffn(xc, hs_c, cg2, wg, wu, wd)
    return x, xc


def _odd_layer(x, mods, n1g, n2g, w_in, conv_w, conv_b, fw1, fb1, fw2, fb2, fw3, fb3, ffreq, fw4, fbias,
               w_out, wg, wu, wd):
    b, s, d = x.shape
    sh1, sc1, g1, sh2, sc2, g2 = mods
    width = w_out.shape[0]
    n1 = 2 * s // DFT_MINOR
    tabs = _dft_tables(n1)
    kf = _hyena_spectra(s, fw1, fb1, fw2, fb2, fw3, fb3, ffreq, fw4, tabs)

    n1g2 = n1g.reshape(1, d)
    wt = w_in.T.astype(BF16)
    tm = _tile(s, 1024)
    n_tiles = s // tm
    x_tiles = x.reshape(b, n_tiles, tm, d)
    firsts, lasts = x_tiles[:, :, 0, :], x_tiles[:, :, tm - 1, :]
    before = jnp.concatenate([lasts[:, :1], lasts[:, :-1]], axis=1)
    after = jnp.concatenate([firsts[:, 1:], firsts[:, -1:]], axis=1)
    x_edge = jnp.stack([before, after], axis=2).reshape(b, 2 * n_tiles, d)
    z_edge = _inproj_nt(x_edge, n1g2, sh1, sc1, wt, 2 * n_tiles)
    inside = jnp.ones((2 * n_tiles,), F32).at[0].set(0.0).at[2 * n_tiles - 1].set(0.0)
    halo = z_edge[:, 0].astype(F32) * inside
    conv_params = jnp.repeat(jnp.concatenate([conv_w.T, conv_b[:, None]], axis=1), LANES, axis=1)

    z4 = _inproj_nt(x, n1g2, sh1, sc1, wt, DFT_MINOR, conv_params, halo)
    y4 = _hyena_core(z4, kf, fbias.T, tabs["f1_half"], tabs["twr"], tabs["twi"], tabs["f2e"], tabs["f2c"],
                     tabs["g1e"])
    x, hs = _proj_res([y4], w_out.astype(BF16), x, g1, transposed=True, tn_pref=d,
                      next_norm=(n2g.reshape(1, d), sh2, sc2))
    return _ffn(x, hs, g2, wg, wu, wd)


def _rms_kernel(x_ref, g_ref, o_ref):
    x = x_ref[...]
    ms = jnp.mean(x * x, axis=-1, keepdims=True)
    o_ref[...] = x * lax.rsqrt(ms + NORM_EPS) * g_ref[...]


def _final_norm(x, g):
    b, s, d = x.shape
    tm = _tile(s, 1024)
    return pl.pallas_call(
        _rms_kernel,
        grid=(b, s // tm),
        in_specs=[pl.BlockSpec((None, tm, d), lambda bb, i: (bb, i, 0)),
                  pl.BlockSpec((1, d), lambda bb, i: (0, 0))],
        out_specs=pl.BlockSpec((None, tm, d), lambda bb, i: (bb, i, 0)),
        out_shape=jax.ShapeDtypeStruct((b, s, d), F32),
        compiler_params=_params("parallel", "parallel"),
        name="final_norm",
    )(x, g.reshape(1, d))


def kernel(x, c, ctx, c_ctx, ada_w, ada_b, norm1, norm2, ffn_w_gate, ffn_w_up, ffn_w_down, e_w_in, e_w_out, e_lambda, e_subln, e_sgu_norm_g, e_sgu_norm_b, e_sgu_w, e_sgu_b, o_w_in, o_conv_w, o_conv_b, o_filt_w1, o_filt_b1, o_filt_w2, o_filt_b2, o_filt_w3, o_filt_b3, o_filt_freq, o_filt_w4, o_filt_bias, o_w_out, final_norm):
    b, s, d = x.shape
    depth = ada_w.shape[0]
    assert b == 2, "the long convolution packs exactly two batches into one complex signal"
    assert depth == 2, "odd layers here never carry the context stream"
    cond_t = jnp.zeros((d, 8), F32).at[:, :b].set(c.T).at[:, b].set(c_ctx)
    mod_all = _adaln_all(cond_t, b + 1, ada_w, ada_b)
    xc = ctx
    for i in range(depth):
        j = i // 2
        parts = jnp.split(mod_all[i], 6, axis=-1)
        mods = [p[:b, None, :] for p in parts]
        cmods = [jnp.broadcast_to(p[b:b + 1, None, :], (b, 1, d)) for p in parts]
        wg, wu, wd = (ffn_w_gate[i].astype(BF16), ffn_w_up[i].astype(BF16), ffn_w_down[i].astype(BF16))
        if i % 2 == 0:
            x, xc = _even_layer(x, xc, mods, cmods, norm1[i], norm2[i], e_w_in[j], e_w_out[j], e_lambda[j],
                                e_subln[j], e_sgu_norm_g[j], e_sgu_norm_b[j], e_sgu_w[j], e_sgu_b[j],
                                wg, wu, wd, i)
        else:
            x = _odd_layer(x, mods, norm1[i], norm2[i], o_w_in[j], o_conv_w[j], o_conv_b[j], o_filt_w1[j],
                           o_filt_b1[j], o_filt_w2[j], o_filt_b2[j], o_filt_w3[j], o_filt_b3[j],
                           o_filt_freq[j], o_filt_w4[j], o_filt_bias[j], o_w_out[j], wg, wu, wd)
    return _final_norm(x, final_norm)
```

```python
import functools
import math

import numpy as np
import jax
import jax.numpy as jnp
from jax import lax
from jax.experimental import pallas as pl
from jax.experimental.pallas import tpu as pltpu

F32 = jnp.float32
BF16 = jnp.bfloat16
HIGHEST = lax.Precision.HIGHEST

GRID_W = 64
NORM_EPS = 1e-6
DIFF_HEADS = 8
DIFF_QK_DIM = 64
DIFF_V_DIM = 2 * DIFF_QK_DIM
DIFF_SCALE = DIFF_QK_DIM ** -0.5
A_WIDTH = DIFF_HEADS * DIFF_V_DIM
Q_COLS = DIFF_HEADS * 2 * DIFF_QK_DIM
ROPE_THETA = 10000.0
ROPE_AXIS_DIM = DIFF_QK_DIM // 2
SUBLN_EPS = 1e-5
SGU_GROUPS = 8
SGU_CHUNK = 128
SGU_CH = 128
B_WIDTH = SGU_GROUPS * SGU_CH
LN_EPS = 1e-5
HYENA_ORDER = 2
SHORT_CONV = 3
FILTER_EMB = 33
FILTER_BANDS = (FILTER_EMB - 1) // 2
FILTER_HIDDEN = 64
DECAY_TARGET = 1e-2
MAX_DECAY = math.log(DECAY_TARGET) / 0.3
MIN_DECAY = math.log(DECAY_TARGET) / 1.5

LANES = 128
ONES_ROWS = 16
ATTN_SUB = 256
DFT_MINOR = 256
VMEM_LIMIT = 48 * 1024 * 1024
VMEM_LIMIT_WIDE = 56 * 1024 * 1024

NT_DIMS = (((1,), (1,)), ((), ()))
TN_DIMS = (((0,), (0,)), ((), ()))


def _params(*sem, vmem=VMEM_LIMIT):
    return pltpu.CompilerParams(dimension_semantics=sem, vmem_limit_bytes=vmem)


def _tile(n, pref):
    return pref if n % pref == 0 else n


def _adaln_kernel(ct_ref, wa_ref, wb_ref, b_ref, o_ref, *, n_rows):
    k = pl.program_id(1)
    a = ct_ref[...]
    a = a * jax.nn.sigmoid(a)
    half = wa_ref.shape[0]
    rows = [jnp.sum(wa_ref[...] * a[:half, r:r + 1], axis=0, keepdims=True)
            + jnp.sum(wb_ref[...] * a[half:, r:r + 1], axis=0, keepdims=True) for r in range(n_rows)]
    rows.append(jnp.zeros((o_ref.shape[0] - n_rows, wa_ref.shape[1]), F32))
    part = jnp.concatenate(rows, axis=0)

    @pl.when(k == 0)
    def _():
        valid = lax.broadcasted_iota(jnp.int32, part.shape, 0) < n_rows
        o_ref[...] = part + jnp.where(valid, b_ref[...], 0.0)

    @pl.when(k > 0)
    def _():
        o_ref[...] += part


def _adaln_all(cond_t, n_rows, ada_w, ada_b):
    depth, d, n6 = ada_w.shape
    tk = _tile(d, 256)
    return pl.pallas_call(
        functools.partial(_adaln_kernel, n_rows=n_rows),
        grid=(depth, d // tk),
        in_specs=[pl.BlockSpec((tk, 8), lambda l, k: (k, 0)),
                  pl.BlockSpec((None, tk // 2, n6), lambda l, k: (l, 2 * k, 0)),
                  pl.BlockSpec((None, tk // 2, n6), lambda l, k: (l, 2 * k + 1, 0)),
                  pl.BlockSpec((None, 1, n6), lambda l, k: (l, 0, 0))],
        out_specs=pl.BlockSpec((None, 8, n6), lambda l, k: (l, 0, 0)),
        out_shape=jax.ShapeDtypeStruct((depth, 8, n6), F32),
        compiler_params=_params("parallel", "arbitrary"),
        name="adaln",
    )(cond_t, ada_w, ada_w, ada_b.reshape(depth, 1, n6))


def _norm_mod_rows(x, g, sh, sc):
    ms = jnp.mean(x * x, axis=-1, keepdims=True)
    y = x * lax.rsqrt(ms + NORM_EPS) * g
    return (y * (1.0 + sc) + sh).astype(BF16)


def _norm_mod(x_ref, g_ref, sh_ref, sc_ref):
    return _norm_mod_rows(x_ref[...], g_ref[...], sh_ref[...], sc_ref[...])


def _inproj_kernel(x_ref, g_ref, sh_ref, sc_ref, w_ref, cos_ref, sin_ref, o_ref, hs_ref, *,
                   n_q, n_qk, rope):
    j = pl.program_id(2)

    @pl.when(j == 0)
    def _():
        hs_ref[...] = _norm_mod(x_ref, g_ref, sh_ref, sc_ref)

    tm, tn = o_ref.shape
    rc = min(tm, 256)

    def by_row_chunks(epilogue):
        for r in range(tm // rc):
            rows = pl.ds(r * rc, rc)
            acc = jnp.dot(hs_ref[rows, :], w_ref[...], preferred_element_type=F32)
            o_ref[rows, :] = epilogue(acc, rows).astype(o_ref.dtype)

    def qk_epilogue(a, rows):
        if rope:
            lane = lax.broadcasted_iota(jnp.int32, a.shape, 1)
            first = (lane & 31) < 16
            partner = jnp.where(first, pltpu.roll(a, tn - 16, 1), pltpu.roll(a, 16, 1))
            reps = tn // LANES
            a = (a * jnp.tile(cos_ref[rows, :], (1, reps))
                 + partner * jnp.tile(sin_ref[rows, :], (1, reps)))
        return jnp.where(j < n_q, a * DIFF_SCALE, a)

    @pl.when(j < n_qk)
    def _():
        by_row_chunks(qk_epilogue)

    @pl.when(j >= n_qk)
    def _():
        by_row_chunks(lambda a, rows: jax.nn.gelu(a))


def _inproj(x, g, sh, sc, w, cos_t, sin_t, rope):
    b, s, d = x.shape
    n = w.shape[1]
    tm = _tile(s, 1024)
    tn = 512
    kern = functools.partial(_inproj_kernel, n_q=Q_COLS // tn, n_qk=2 * Q_COLS // tn, rope=rope)
    return pl.pallas_call(
        kern,
        grid=(b, s // tm, n // tn),
        in_specs=[pl.BlockSpec((None, tm, d), lambda bb, i, j: (bb, i, 0)),
                  pl.BlockSpec((1, d), lambda bb, i, j: (0, 0)),
                  pl.BlockSpec((None, 1, d), lambda bb, i, j: (bb, 0, 0)),
                  pl.BlockSpec((None, 1, d), lambda bb, i, j: (bb, 0, 0)),
                  pl.BlockSpec((d, tn), lambda bb, i, j: (0, j)),
                  pl.BlockSpec((tm, LANES), lambda bb, i, j: (i, 0)),
                  pl.BlockSpec((tm, LANES), lambda bb, i, j: (i, 0))],
        out_specs=pl.BlockSpec((None, tm, tn), lambda bb, i, j: (bb, i, j)),
        out_shape=jax.ShapeDtypeStruct((b, s, n), BF16),
        scratch_shapes=[pltpu.VMEM((tm, d), BF16)],
        compiler_params=_params("parallel", "parallel", "arbitrary"),
        name="inproj",
    )(x, g, sh, sc, w, cos_t, sin_t)


def _inproj_nt_kernel(x_ref, g_ref, sh_ref, sc_ref, wt_ref, *rest, tl, conv):
    if conv:
        cw_ref, halo_ref, o_ref, hs_ref = rest
    else:
        o_ref, hs_ref = rest
    j = pl.program_id(2)

    @pl.when(j == 0)
    def _():
        hs_ref[...] = _norm_mod(x_ref, g_ref, sh_ref, sc_ref)

    n_chunks = o_ref.shape[0]
    if not conv:
        acc = lax.dot_general(wt_ref[...], hs_ref[...], NT_DIMS, preferred_element_type=F32)
        for c in range(n_chunks):
            o_ref[c] = acc[:, c * tl:(c + 1) * tl].astype(o_ref.dtype)
        return

    acc = jnp.concatenate(
        [lax.dot_general(wt_ref[...], hs_ref[c * tl:(c + 1) * tl, :], NT_DIMS, preferred_element_type=F32)
         for c in range(n_chunks)], axis=1)
    tn, tm = acc.shape
    halo = halo_ref[...]
    col = lax.broadcasted_iota(jnp.int32, halo.shape, 1)
    tile = pl.program_id(1)
    before = jnp.sum(jnp.where(col == 2 * tile, halo, 0.0), axis=1, keepdims=True)
    after = jnp.sum(jnp.where(col == 2 * tile + 1, halo, 0.0), axis=1, keepdims=True)
    lane = lax.broadcasted_iota(jnp.int32, (tn, LANES), 1)
    prev = pltpu.roll(acc, 1, 1)
    prev = jnp.concatenate([jnp.where(lane == 0, before, prev[:, :LANES]), prev[:, LANES:]], axis=1)
    nxt = pltpu.roll(acc, tm - 1, 1)
    nxt = jnp.concatenate([nxt[:, :tm - LANES],
                           jnp.where(lane == LANES - 1, after, nxt[:, tm - LANES:])], axis=1)

    def tap(k):
        return jnp.tile(cw_ref[:, k * LANES:(k + 1) * LANES], (1, tm // LANES))

    out = tap(3) + prev * tap(0) + acc * tap(1) + nxt * tap(2)
    for c in range(n_chunks):
        o_ref[c] = out[:, c * tl:(c + 1) * tl].astype(o_ref.dtype)


def _inproj_nt(x, g, sh, sc, wt, tl, conv_params=None, halo=None):
    b, s, d = x.shape
    n = wt.shape[0]
    tm = _tile(s, 1024)
    tn = 512
    conv = conv_params is not None
    kern = functools.partial(_inproj_nt_kernel, tl=tl, conv=conv)
    in_specs = [pl.BlockSpec((None, tm, d), lambda bb, i, j: (bb, i, 0)),
                pl.BlockSpec((1, d), lambda bb, i, j: (0, 0)),
                pl.BlockSpec((None, 1, d), lambda bb, i, j: (bb, 0, 0)),
                pl.BlockSpec((None, 1, d), lambda bb, i, j: (bb, 0, 0)),
                pl.BlockSpec((tn, d), lambda bb, i, j: (j, 0))]
    args = [x, g, sh, sc, wt]
    if conv:
        in_specs += [pl.BlockSpec((tn, 4 * LANES), lambda bb, i, j: (j, 0)),
                     pl.BlockSpec((None, tn, halo.shape[2]), lambda bb, i, j: (bb, j, 0))]
        args += [conv_params, halo]
    return pl.pallas_call(
        kern,
        grid=(b, s // tm, n // tn),
        in_specs=in_specs,
        out_specs=pl.BlockSpec((None, tm // tl, tn, tl), lambda bb, i, j: (bb, i, j, 0)),
        out_shape=jax.ShapeDtypeStruct((b, s // tl, n, tl), BF16),
        scratch_shapes=[pltpu.VMEM((tm, d), BF16)],
        compiler_params=_params("parallel", "parallel", "arbitrary"),
        name="inproj_nt",
    )(*args)


def _ffn_up_kernel(hs_ref, wg_ref, wu_ref, o_ref):
    hs = hs_ref[...]
    gate = jnp.dot(hs, wg_ref[...], preferred_element_type=F32)
    up = jnp.dot(hs, wu_ref[...], preferred_element_type=F32)
    o_ref[...] = (gate * jax.nn.sigmoid(gate) * up).astype(o_ref.dtype)


def _ffn_up(hs, wg, wu):
    b, s, d = hs.shape
    n = wg.shape[1]
    tm = _tile(s, 1024)
    tn = 512
    return pl.pallas_call(
        _ffn_up_kernel,
        grid=(b, s // tm, n // tn),
        in_specs=[pl.BlockSpec((None, tm, d), lambda bb, i, j: (bb, i, 0)),
                  pl.BlockSpec((d, tn), lambda bb, i, j: (0, j)),
                  pl.BlockSpec((d, tn), lambda bb, i, j: (0, j))],
        out_specs=pl.BlockSpec((None, tm, tn), lambda bb, i, j: (bb, i, j)),
        out_shape=jax.ShapeDtypeStruct((b, s, n), BF16),
        compiler_params=_params("parallel", "parallel", "parallel"),
        name="ffn_up",
    )(hs, wg, wu)


def _proj_res_kernel(*refs, ksizes, transposed, next_norm):
    n = len(ksizes)
    a_refs = refs[:n]
    if next_norm:
        w_ref, x_ref, gate_ref, ng_ref, nsh_ref, nsc_ref, o_ref, hs_ref = refs[n:]
    else:
        w_ref, x_ref, gate_ref, o_ref = refs[n:]

    def emit(rows, acc):
        y = x_ref[rows, :] + gate_ref[...] * acc
        o_ref[rows, :] = y
        if next_norm:
            hs_ref[rows, :] = _norm_mod_rows(y, ng_ref[...], nsh_ref[...], nsc_ref[...])

    if transposed:
        (a_ref,) = a_refs
        tl = a_ref.shape[2]
        for c in range(a_ref.shape[0]):
            emit(slice(c * tl, (c + 1) * tl),
                 lax.dot_general(a_ref[c], w_ref[...], TN_DIMS, preferred_element_type=F32))
        return
    acc = None
    off = 0
    for a_ref, ks in zip(a_refs, ksizes):
        part = jnp.dot(a_ref[...], w_ref[off:off + ks, :], preferred_element_type=F32)
        acc = part if acc is None else acc + part
        off += ks
    emit(slice(None), acc)


def _proj_res(a_list, w, x, gate, transposed=False, tm_pref=512, tn_pref=512, vmem=VMEM_LIMIT, next_norm=None):
    b, s, d = x.shape
    ksizes = tuple(a.shape[2] for a in a_list)
    ktot = sum(ksizes)
    tm = _tile(s, tm_pref)
    tn = _tile(d, tn_pref)
    if transposed:
        tl = a_list[0].shape[3]
        a_specs = [pl.BlockSpec((None, tm // tl, ktot, tl), lambda bb, i, j: (bb, i, 0, 0))]
    else:
        a_specs = [pl.BlockSpec((None, tm, ks), lambda bb, i, j: (bb, i, 0)) for ks in ksizes]
    in_specs = a_specs + [pl.BlockSpec((ktot, tn), lambda bb, i, j: (0, j)),
                          pl.BlockSpec((None, tm, tn), lambda bb, i, j: (bb, i, j)),
                          pl.BlockSpec((None, 1, tn), lambda bb, i, j: (bb, 0, j))]
    args = [*a_list, w, x, gate]
    out_spec = pl.BlockSpec((None, tm, tn), lambda bb, i, j: (bb, i, j))
    out_specs, out_shape = out_spec, jax.ShapeDtypeStruct((b, s, d), F32)
    if next_norm is not None:
        assert tn == d
        vmem = VMEM_LIMIT_WIDE
        in_specs += [pl.BlockSpec((1, d), lambda bb, i, j: (0, 0)),
                     pl.BlockSpec((None, 1, d), lambda bb, i, j: (bb, 0, 0)),
                     pl.BlockSpec((None, 1, d), lambda bb, i, j: (bb, 0, 0))]
        args += list(next_norm)
        out_specs, out_shape = [out_spec, out_spec], [out_shape, jax.ShapeDtypeStruct((b, s, d), BF16)]
    kern = functools.partial(_proj_res_kernel, ksizes=ksizes, transposed=transposed,
                             next_norm=next_norm is not None)
    return pl.pallas_call(
        kern,
        grid=(b, s // tm, d // tn),
        in_specs=in_specs,
        out_specs=out_specs,
        out_shape=out_shape,
        compiler_params=_params("parallel", "parallel", "parallel", vmem=vmem),
        name="proj_res",
    )(*args)


def _attn_kernel(*refs, n_chunks, lam_init):
    if n_chunks:
        lam_ref, q_ref, kc_ref, vct_ref, k_ref, vt_ref, g_ref, o_ref, acc_ref = refs[:9]
        n_slots = (len(refs) - 9) // 2
        slots = tuple(zip(refs[9:9 + n_slots], refs[9 + n_slots:]))
    else:
        lam_ref, q_ref, kc_ref, vct_ref, g_ref, o_ref, acc_ref = refs
    n_sub = acc_ref.shape[0]
    tq = acc_ref.shape[-1]
    dv = DIFF_V_DIM
    q = q_ref[...]
    qm = [(q[s * tq:(s + 1) * tq, :DIFF_QK_DIM], q[s * tq:(s + 1) * tq, DIFF_QK_DIM:]) for s in range(n_sub)]

    def scores(kblk, sub):
        s_pair = tuple(lax.dot_general(kblk[:, m * DIFF_QK_DIM:(m + 1) * DIFF_QK_DIM], qm[sub][m], NT_DIMS,
                                       preferred_element_type=F32) for m in range(2))
        return s_pair, tuple(jnp.max(s, axis=0, keepdims=True) for s in s_pair)

    def with_ones(vtblk):
        return jnp.concatenate([vtblk, jnp.ones((ONES_ROWS, vtblk.shape[1]), BF16)], axis=0)

    def absorb(scored, vext, m_pair, sub):
        s_pair, smax_pair = scored
        out = []
        for m in range(2):
            m_old = m_pair[m]
            m_new = jnp.maximum(m_old, smax_pair[m])
            alpha = jnp.exp(m_old - m_new)
            p = jnp.exp((s_pair[m] - m_new).astype(BF16))
            pv = jnp.dot(vext, p, preferred_element_type=F32)
            acc_ref[sub, m] = alpha * acc_ref[sub, m] + pv
            out.append(m_new)
        return tuple(out)

    def store(slot, scored):
        for m in range(2):
            slot[0][m] = scored[0][m]
            slot[1][m] = scored[1][m]

    def load(slot):
        return (slot[0][0], slot[0][1]), (slot[1][0], slot[1][1])

    acc_ref[...] = jnp.zeros_like(acc_ref)
    init = jnp.full((1, tq), -1e30, F32)
    kc, vc = kc_ref[...], with_ones(vct_ref[0])
    m_state = tuple(absorb(scores(kc, s), vc, (init, init), s) for s in range(n_sub))
    if n_chunks:
        group = len(slots) // (2 * n_sub)
        assert n_chunks % (2 * group) == 0

        def slot(half, k, sub):
            return slots[(half * group + k) * n_sub + sub]

        def half_trip(half, base, ms, lookahead):
            ms = list(ms)
            for k in range(group):
                vext = with_ones(vt_ref[base + k])
                for s in range(n_sub):
                    if lookahead:
                        store(slot(1 - half, k, s), scores(k_ref[base + group + k], s))
                    ms[s] = absorb(load(slot(half, k, s)), vext, ms[s], s)
            return tuple(ms)

        def trip(j, ms, lookahead):
            base = 2 * group * j
            ms = half_trip(0, base, ms, True)
            return half_trip(1, base + group, ms, lookahead)

        for k in range(group):
            for s in range(n_sub):
                store(slot(0, k, s), scores(k_ref[k], s))
        n_trips = n_chunks // (2 * group)
        unroll = next(u for u in (5, 4, 3, 2, 1) if (n_trips - 1) % u == 0)
        m_state = lax.fori_loop(0, n_trips - 1, lambda j, ms: trip(j, ms, True), m_state, unroll=unroll)
        m_state = trip(n_trips - 1, m_state, False)

    lp = lam_ref[...]
    lam = (jnp.exp(jnp.sum(lp[0:1] * lp[1:2], axis=-1, keepdims=True))
           - jnp.exp(jnp.sum(lp[2:3] * lp[3:4], axis=-1, keepdims=True)) + lam_init)
    for s in range(n_sub):
        acc0, acc1 = acc_ref[s, 0], acc_ref[s, 1]
        o = acc0[:dv] / acc0[dv:dv + 1] - lam * (acc1[:dv] / acc1[dv:dv + 1])
        ot = o.T
        ms = jnp.mean(ot * ot, axis=-1, keepdims=True)
        on = ot * lax.rsqrt(ms + SUBLN_EPS) * g_ref[...] * (1.0 - lam_init)
        o_ref[s * tq:(s + 1) * tq, :] = on.astype(o_ref.dtype)


def _attention(lam_p, subln, q_arr, kc_arr, vct_arr, k4=None, vt4=None, *, lam_init):
    b, sq = q_arr.shape[0], q_arr.shape[1]
    sc = kc_arr.shape[1]
    h = DIFF_HEADS
    dv = DIFF_V_DIM
    n_sub = next(n for n in (4, 2, 1) if sq % (n * ATTN_SUB) == 0)
    tq = n_sub * ATTN_SUB
    n_chunks = 0 if k4 is None else k4.shape[1]
    in_specs = [pl.BlockSpec((4, DIFF_QK_DIM), lambda bb, hh, i: (0, 0)),
                pl.BlockSpec((None, tq, dv), lambda bb, hh, i: (bb, i, hh)),
                pl.BlockSpec((None, sc, dv), lambda bb, hh, i: (bb, 0, h + hh)),
                pl.BlockSpec((None, 1, dv, sc), lambda bb, hh, i: (bb, 0, hh, 0))]
    args = [lam_p, q_arr, kc_arr, vct_arr]
    if n_chunks:
        tk = k4.shape[2]
        in_specs += [pl.BlockSpec((None, n_chunks, tk, dv), lambda bb, hh, i: (bb, 0, 0, h + hh)),
                     pl.BlockSpec((None, n_chunks, dv, tk), lambda bb, hh, i: (bb, 0, hh, 0))]
        args += [k4, vt4]
    in_specs.append(pl.BlockSpec((1, dv), lambda bb, hh, i: (0, 0)))
    args.append(subln)
    kern = functools.partial(_attn_kernel, n_chunks=n_chunks, lam_init=lam_init)
    scratch = [pltpu.VMEM((n_sub, 2, dv + ONES_ROWS, ATTN_SUB), F32)]
    if n_chunks:
        n_slots = 8 if n_chunks % 16 == 0 else 2 * n_sub
        scratch += [pltpu.VMEM((2, tk, ATTN_SUB), F32) for _ in range(n_slots)]
        scratch += [pltpu.VMEM((2, 1, ATTN_SUB), F32) for _ in range(n_slots)]
    return pl.pallas_call(
        kern,
        grid=(b, h, sq // tq),
        in_specs=in_specs,
        out_specs=pl.BlockSpec((None, tq, dv), lambda bb, hh, i: (bb, i, hh)),
        out_shape=jax.ShapeDtypeStruct((b, sq, A_WIDTH), BF16),
        scratch_shapes=scratch,
        compiler_params=_params("parallel", "parallel", "parallel"),
        name="diff_attn",
    )(*args)


def _sgu_kernel(u_ref, g_ref, ng_ref, nb_ref, w_ref, bs_ref, o_ref):
    for gi in range(SGU_GROUPS):
        cols = slice(gi * SGU_CH, (gi + 1) * SGU_CH)
        w = w_ref[gi]
        for c in range(u_ref.shape[0] // SGU_CHUNK):
            sl = slice(c * SGU_CHUNK, (c + 1) * SGU_CHUNK)
            gg = g_ref[sl, cols].astype(F32)
            mu = jnp.mean(gg, axis=-1, keepdims=True)
            dev = gg - mu
            var = jnp.mean(dev * dev, axis=-1, keepdims=True)
            vv = dev * lax.rsqrt(var + LN_EPS) * ng_ref[gi] + nb_ref[gi]
            mixed = jnp.dot(w, vv.astype(BF16), preferred_element_type=F32) + bs_ref[gi]
            o_ref[sl, cols] = (u_ref[sl, cols].astype(F32) * mixed).astype(o_ref.dtype)


def _sgu(qkug, norm_g, norm_b, w_s, b_s):
    b, s = qkug.shape[0], qkug.shape[1]
    tm = _tile(s, 1024)
    gcount = SGU_GROUPS
    ublk = 2 * Q_COLS // B_WIDTH
    full = lambda shape: pl.BlockSpec(shape, lambda bb, i: (0,) * len(shape))
    return pl.pallas_call(
        _sgu_kernel,
        grid=(b, s // tm),
        in_specs=[pl.BlockSpec((None, tm, B_WIDTH), lambda bb, i: (bb, i, ublk)),
                  pl.BlockSpec((None, tm, B_WIDTH), lambda bb, i: (bb, i, ublk + 1)),
                  full((gcount, 1, SGU_CH)), full((gcount, 1, SGU_CH)),
                  full((gcount, SGU_CHUNK, SGU_CHUNK)), full((gcount, SGU_CHUNK, 1))],
        out_specs=pl.BlockSpec((None, tm, B_WIDTH), lambda bb, i: (bb, i, 0)),
        out_shape=jax.ShapeDtypeStruct((b, s, B_WIDTH), BF16),
        compiler_params=_params("parallel", "parallel"),
        name="sgu",
    )(qkug, qkug, norm_g.reshape(gcount, 1, SGU_CH), norm_b.reshape(gcount, 1, SGU_CH),
      w_s.astype(BF16), b_s.reshape(gcount, SGU_CHUNK, 1))


def _filter_mlp_kernel(f_ref, w1_ref, b1_ref, w2_ref, b2_ref, w3_ref, b3_ref, fr_ref, o_ref):
    def lin(a, w_ref, b_ref):
        return jnp.dot(a, w_ref[...], preferred_element_type=F32, precision=HIGHEST) + b_ref[...]
    fr = fr_ref[...]
    hcur = jnp.sin(fr[0:1] * lin(f_ref[...], w1_ref, b1_ref))
    hcur = jnp.sin(fr[1:2] * lin(hcur, w2_ref, b2_ref))
    o_ref[...] = jnp.sin(fr[2:3] * lin(hcur, w3_ref, b3_ref))


def _filter_mlp(feats, w1, b1, w2, b2, w3, b3, freq):
    rows, emb = feats.shape
    hid = FILTER_HIDDEN
    tr = _tile(rows, 2048)
    full = lambda shape: pl.BlockSpec(shape, lambda i: (0,) * len(shape))
    return pl.pallas_call(
        _filter_mlp_kernel,
        grid=(rows // tr,),
        in_specs=[pl.BlockSpec((tr, emb), lambda i: (i, 0)),
                  full((emb, hid)), full((1, hid)), full((hid, hid)), full((1, hid)),
                  full((hid, hid)), full((1, hid)), full((3, hid))],
        out_specs=pl.BlockSpec((tr, hid), lambda i: (i, 0)),
        out_shape=jax.ShapeDtypeStruct((rows, hid), F32),
        compiler_params=_params("parallel"),
        name="filter_mlp",
    )(feats, w1, b1.reshape(1, hid), w2, b2.reshape(1, hid), w3, b3.reshape(1, hid), freq)


def _filter_raw_kernel(w4t_ref, h_ref, t_ref, delta_ref, o_ref, *, zero_tile):
    rt = pl.program_id(2)

    def split(a):
        hi = a.astype(BF16)
        return hi, (a - hi.astype(F32)).astype(BF16)

    def nt(a, bm):
        return lax.dot_general(a, bm, NT_DIMS, preferred_element_type=F32)

    w_hi, w_lo = split(w4t_ref[...])
    h_hi, h_lo = split(h_ref[...])
    raw = nt(w_hi, h_hi) + (nt(w_hi, h_lo) + nt(w_lo, h_hi))
    raw = raw * jnp.exp(-(delta_ref[...] * t_ref[...]))
    for c in range(o_ref.shape[0]):
        o_ref[c] = raw[:, c * DFT_MINOR:(c + 1) * DFT_MINOR].astype(o_ref.dtype)

    @pl.when(rt == zero_tile)
    def _():
        col = lax.broadcasted_iota(jnp.int32, (raw.shape[0], DFT_MINOR), 1)
        o_ref[0] = jnp.where(col == 0, 0.0, raw[:, :DFT_MINOR]).astype(o_ref.dtype)


def _filter_raw(w4t, h3, t_row, deltas, seq):
    c = w4t.shape[2]
    rows = h3.shape[0]
    tr = _tile(rows // 2, 2048)
    tc = _tile(c, 512)
    half_tiles = seq // tr
    kern = functools.partial(_filter_raw_kernel, zero_tile=half_tiles)
    return pl.pallas_call(
        kern,
        grid=(HYENA_ORDER, c // tc, rows // tr),
        in_specs=[pl.BlockSpec((None, None, tc, FILTER_HIDDEN),
                               lambda n, ci, rt: (n, rt // half_tiles, ci, 0)),
                  pl.BlockSpec((tr, FILTER_HIDDEN), lambda n, ci, rt: (rt, 0)),
                  pl.BlockSpec((1, tr), lambda n, ci, rt: (0, rt)),
                  pl.BlockSpec((tc, 1), lambda n, ci, rt: (ci, 0))],
        out_specs=pl.BlockSpec((None, tr // DFT_MINOR, tc, DFT_MINOR), lambda n, ci, rt: (n, rt, ci, 0)),
        out_shape=jax.ShapeDtypeStruct((HYENA_ORDER, rows // DFT_MINOR, c, DFT_MINOR), BF16),
        compiler_params=_params("parallel", "parallel", "parallel"),
        name="filter_raw",
    )(w4t, h3, t_row, deltas)


def _cmul(ar, ai, br, bi):
    return ar * br - ai * bi, ar * bi + ai * br


def _store_complex(ref, c, n1, re, im):
    r0 = pl.multiple_of(c * n1, n1)
    ref[pl.ds(r0, n1), :DFT_MINOR] = re.astype(ref.dtype)
    ref[pl.ds(r0, n1), DFT_MINOR:] = im.astype(ref.dtype)


def _load_complex(ref, c, n1):
    r0 = pl.multiple_of(c * n1, n1)
    tile = ref[pl.ds(r0, n1), :]
    return tile[:, :DFT_MINOR], tile[:, DFT_MINOR:]


def _filter_spec_kernel(k_ref, f1_ref, twr_ref, twi_ref, f2_ref, o_ref, a2_ref, *, inv_n):
    tc, n1, n2 = k_ref.shape

    def left(c, carry):
        k = k_ref[c].astype(F32)
        nrm = jnp.sum(jnp.sum(jnp.abs(k), axis=1, keepdims=True), axis=0, keepdims=True)
        kn = (k * (inv_n / nrm)).astype(BF16)
        a = jnp.dot(f1_ref[...], kn, preferred_element_type=F32)
        ar, ai = _cmul(a[:n1], a[n1:], twr_ref[...], twi_ref[...])
        _store_complex(a2_ref, c, n1, ar, ai)
        return carry

    lax.fori_loop(0, tc, left, 0, unroll=True)
    z = jnp.dot(a2_ref[...], f2_ref[...], preferred_element_type=F32).reshape(tc, n1, 2 * n2)
    o_ref[:, 0] = z[:, :, :n2].astype(o_ref.dtype)
    o_ref[:, 1] = z[:, :, n2:].astype(o_ref.dtype)


def _filter_spec(k2, f1_full, twr, twi, f2e):
    nc, n1, n2 = k2.shape
    tc = 16
    kern = functools.partial(_filter_spec_kernel, inv_n=1.0 / (n1 * n2))
    full = lambda shape: pl.BlockSpec(shape, lambda i: (0,) * len(shape))
    return pl.pallas_call(
        kern,
        grid=(nc // tc,),
        in_specs=[pl.BlockSpec((tc, n1, n2), lambda i: (i, 0, 0)),
                  full((2 * n1, n1)), full((n1, n2)), full((n1, n2)), full((2 * n2, 2 * n2))],
        out_specs=pl.BlockSpec((tc, 2, n1, n2), lambda i: (i, 0, 0, 0)),
        out_shape=jax.ShapeDtypeStruct((nc, 2, n1, n2), BF16),
        scratch_shapes=[pltpu.VMEM((tc * n1, 2 * n2), BF16)],
        compiler_params=_params("parallel"),
        name="filter_spec",
    )(k2, f1_full, twr, twi, f2e)


def _hyena_kernel(zv_ref, z1_ref, z2_ref, kf_ref, fb_ref, f1_ref, twr_ref, twi_ref, f2_ref, f2c_ref,
                  g1_ref, o_ref, sig_ref, a2_ref, z_ref):
    nb, hr, tc, n2 = zv_ref.shape
    n1 = 2 * hr
    for part, ref in enumerate((zv_ref, z1_ref, z2_ref)):
        for bb in range(nb):
            sig_ref[part, bb] = pltpu.einshape("tcl->ctl", ref[bb].astype(F32))

    for n in range(HYENA_ORDER):
        def fwd_left(c, carry):
            xs = jnp.concatenate([sig_ref[0, 0, c], sig_ref[0, 1, c]], axis=0).astype(BF16)
            a = jnp.dot(f1_ref[...], xs, preferred_element_type=F32)
            ar, ai = _cmul(a[:n1], a[n1:], twr_ref[...], twi_ref[...])
            _store_complex(a2_ref, c, n1, ar, ai)
            return carry

        lax.fori_loop(0, tc, fwd_left, 0, unroll=True)
        z_ref[...] = jnp.dot(a2_ref[...], f2_ref[...], preferred_element_type=F32)

        def spectrum(c, carry):
            zr, zi = _load_complex(z_ref, c, n1)
            wr, wi = _cmul(zr, zi, kf_ref[n, c, 0].astype(F32), kf_ref[n, c, 1].astype(F32))
            _store_complex(a2_ref, c, n1, wr, wi)
            return carry

        lax.fori_loop(0, tc, spectrum, 0, unroll=True)
        z_ref[...] = jnp.dot(a2_ref[...], f2c_ref[...], preferred_element_type=F32)

        def inv_left(c, carry):
            br, bi = _load_complex(z_ref, c, n1)
            br, bi = _cmul(br, bi, twr_ref[...], -twi_ref[...])
            bs = jnp.concatenate([br, bi], axis=0).astype(BF16)
            y = jnp.dot(g1_ref[...], bs, preferred_element_type=F32)
            fb = fb_ref[pl.ds(c, 1), n:n + 1]
            for bb in range(nb):
                sig_ref[0, bb, c] = sig_ref[1 + n, bb, c] * (y[bb * hr:(bb + 1) * hr] + sig_ref[0, bb, c] * fb)
            return carry

        lax.fori_loop(0, tc, inv_left, 0, unroll=True)

    for bb in range(nb):
        o_ref[bb] = pltpu.einshape("ctl->tcl", sig_ref[0, bb]).astype(o_ref.dtype)


def _hyena_core(z4, kf, fbias, f1h, twr, twi, f2e, f2c, g1e):
    nb, hr, c3, n2 = z4.shape
    c = c3 // 3
    n1 = 2 * hr
    tc = 16
    nct = c // tc
    full = lambda shape: pl.BlockSpec(shape, lambda i: (0,) * len(shape))
    zspec = lambda part: pl.BlockSpec((nb, hr, tc, n2), lambda i: (0, 0, i + part * nct, 0))
    return pl.pallas_call(
        _hyena_kernel,
        grid=(nct,),
        in_specs=[zspec(0), zspec(1), zspec(2),
                  pl.BlockSpec((HYENA_ORDER, tc, 2, n1, n2), lambda i: (0, i, 0, 0, 0)),
                  pl.BlockSpec((tc, HYENA_ORDER), lambda i: (i, 0)),
                  full((2 * n1, n1)), full((n1, n2)), full((n1, n2)),
                  full((2 * n2, 2 * n2)), full((2 * n2, 2 * n2)), full((n1, 2 * n1))],
        out_specs=pl.BlockSpec((nb, hr, tc, n2), lambda i: (0, 0, i, 0)),
        out_shape=jax.ShapeDtypeStruct((nb, hr, c, n2), BF16),
        scratch_shapes=[pltpu.VMEM((3, nb, tc, hr, n2), F32),
                        pltpu.VMEM((tc * n1, 2 * n2), BF16),
                        pltpu.VMEM((tc * n1, 2 * n2), F32)],
        compiler_params=_params("parallel"),
        name="hyena_core",
    )(z4, z4, z4, kf, fbias, f1h, twr, twi, f2e, f2c, g1e)


def _dft_tables(n1):
    n2 = DFT_MINOR
    n = n1 * n2
    a1 = 2.0 * np.pi * np.outer(np.arange(n1), np.arange(n1)) / n1
    f1r, f1i = np.cos(a1), -np.sin(a1)
    a2 = 2.0 * np.pi * np.outer(np.arange(n2), np.arange(n2)) / n2
    f2r, f2i = np.cos(a2), -np.sin(a2)
    at = 2.0 * np.pi * np.outer(np.arange(n1), np.arange(n2)) / n
    twr, twi = np.cos(at), -np.sin(at)
    hr = n1 // 2
    f1_full = np.concatenate([f1r, f1i], axis=0)
    f1_half = np.block([[f1r[:, :hr], -f1i[:, :hr]], [f1i[:, :hr], f1r[:, :hr]]])
    f2e = np.block([[f2r, f2i], [-f2i, f2r]])
    f2c = np.block([[f2r, -f2i], [f2i, f2r]])
    gr, gi = f1r[:hr, :], -f1i[:hr, :]
    g1e = np.block([[gr, -gi], [gi, gr]])
    bf = lambda m: jnp.asarray(m, F32).astype(BF16)
    return dict(f1_full=bf(f1_full), f1_half=bf(f1_half), f2e=bf(f2e), f2c=bf(f2c), g1e=bf(g1e),
                twr=jnp.asarray(twr, F32), twi=jnp.asarray(twi, F32))


def _hyena_spectra(seq, fw1, fb1, fw2, fb2, fw3, fb3, ffreq, fw4, tabs):
    l = seq
    width = fw4.shape[1] // (2 * HYENA_ORDER)
    t_fwd = jnp.linspace(0.0, 1.0, l, dtype=F32)
    w = 2.0 * math.pi * jnp.arange(l, dtype=F32)[:, None] / l
    f = jnp.linspace(1e-4, FILTER_BANDS - 1, FILTER_BANDS, dtype=F32)[None, :]
    feats = jnp.concatenate([t_fwd[:, None], jnp.cos(f * w), -jnp.sin(f * w)], axis=-1)

    def two_sided(a):
        return jnp.concatenate([a, a[:1], a[1:][::-1]], axis=0)

    t_lin = two_sided(t_fwd)
    emb_pad = FILTER_HIDDEN
    feats = jnp.pad(feats, ((0, 0), (0, emb_pad - FILTER_EMB)))
    w1p = jnp.pad(fw1, ((0, emb_pad - FILTER_EMB), (0, 0)))
    h3 = two_sided(_filter_mlp(feats, w1p, fb1, fw2, fb2, fw3, fb3, ffreq))
    w4t = fw4.reshape(FILTER_HIDDEN, HYENA_ORDER, 2, width).transpose(1, 2, 3, 0)
    deltas = jnp.abs(jnp.linspace(MIN_DECAY, MAX_DECAY, width, dtype=F32))[:, None]
    raw = _filter_raw(w4t, h3, t_lin[None, :], deltas, l)
    n1 = raw.shape[1]
    k2 = raw.transpose(0, 2, 1, 3).reshape(HYENA_ORDER * width, n1, DFT_MINOR)
    kf = _filter_spec(k2, tabs["f1_full"], tabs["twr"], tabs["twi"], tabs["f2e"])
    return kf.reshape(HYENA_ORDER, width, 2, n1, DFT_MINOR)


def _rope_tables(n_tokens):
    tok = jnp.arange(n_tokens)
    row = (tok // GRID_W).astype(F32)
    col = (tok % GRID_W).astype(F32)
    half = ROPE_AXIS_DIM // 2
    inv = 1.0 / (ROPE_THETA ** (jnp.arange(0, ROPE_AXIS_DIM, 2, dtype=F32) / ROPE_AXIS_DIM))
    ang_r = row[:, None] * inv
    ang_c = col[:, None] * inv
    cos64 = jnp.concatenate([jnp.cos(ang_r), jnp.cos(ang_r), jnp.cos(ang_c), jnp.cos(ang_c)], axis=-1)
    sin64 = jnp.concatenate([-jnp.sin(ang_r), jnp.sin(ang_r), -jnp.sin(ang_c), jnp.sin(ang_c)], axis=-1)
    assert cos64.shape[1] == 4 * half == DIFF_QK_DIM
    return jnp.tile(cos64, (1, 2)), jnp.tile(sin64, (1, 2))


def _ffn(x, hs, gate, wg, wu, wd):
    return _proj_res([_ffn_up(hs, wg, wu)], wd, x, gate, tm_pref=1024, vmem=VMEM_LIMIT_WIDE)


def _even_layer(x, xc, mods, cmods, n1g, n2g, w_in, w_out, lam_p, subln, sgu_ng, sgu_nb, sgu_w, sgu_b,
                wg, wu, wd, layer_idx):
    b, s, d = x.shape
    lam_init = 0.8 - 0.6 * math.exp(-0.3 * layer_idx)
    sh1, sc1, g1, sh2, sc2, g2 = mods
    csh1, csc1, cg1, csh2, csc2, cg2 = cmods
    o_k, o_v, o_u = Q_COLS, 2 * Q_COLS, 2 * Q_COLS + A_WIDTH
    w_qkug = jnp.concatenate([w_in[:, :o_v], w_in[:, o_u:]], axis=1).astype(BF16)
    w_vt = w_in[:, o_v:o_u].T.astype(BF16)
    wo = w_out.astype(BF16)
    n1g2 = n1g.reshape(1, d)
    cos_t, sin_t = _rope_tables(s)
    sc_len = xc.shape[1]

    qkug = _inproj(x, n1g2, sh1, sc1, w_qkug, cos_t, sin_t, rope=True)
    cqkug = _inproj(xc, n1g2, csh1, csc1, w_qkug, cos_t[:sc_len], sin_t[:sc_len], rope=False)
    tk = _tile(s, 512)
    vt4 = _inproj_nt(x, n1g2, sh1, sc1, w_vt, tk)
    cvt4 = _inproj_nt(xc, n1g2, csh1, csc1, w_vt, sc_len)
    k4 = qkug.reshape(b, s // tk, tk, qkug.shape[2])
    subln2 = subln.reshape(1, DIFF_V_DIM)

    a_l = _attention(lam_p, subln2, qkug, cqkug, cvt4, k4, vt4, lam_init=lam_init)
    s_l = _sgu(qkug, sgu_ng, sgu_nb, sgu_w, sgu_b)
    n2g2 = n2g.reshape(1, d)
    x, hs = _proj_res([a_l, s_l], wo, x, g1, tn_pref=d, next_norm=(n2g2, sh2, sc2))
    x = _ffn(x, hs, g2, wg, wu, wd)

    a_c = _attention(lam_p, subln2, cqkug, cqkug, cvt4, lam_init=lam_init)
    s_c = _sgu(cqkug, sgu_ng, sgu_nb, sgu_w, sgu_b)
    xc, hs_c = _proj_res([a_c, s_c], wo, xc, cg1, tn_pref=d, next_norm=(n2g2, csh2, csc2))
    xc = _ffn(xc, hs_c, cg2, wg, wu, wd)
    return x, xc


def _odd_layer(x, mods, n1g, n2g, w_in, conv_w, conv_b, fw1, fb1, fw2, fb2, fw3, fb3, ffreq, fw4, fbias,
               w_out, wg, wu, wd):
    b, s, d = x.shape
    sh1, sc1, g1, sh2, sc2, g2 = mods
    width = w_out.shape[0]
    n1 = 2 * s // DFT_MINOR
    tabs = _dft_tables(n1)
    kf = _hyena_spectra(s, fw1, fb1, fw2, fb2, fw3, fb3, ffreq, fw4, tabs)

    n1g2 = n1g.reshape(1, d)
    wt = w_in.T.astype(BF16)
    tm = _tile(s, 1024)
    n_tiles = s // tm
    x_tiles = x.reshape(b, n_tiles, tm, d)
    firsts, lasts = x_tiles[:, :, 0, :], x_tiles[:, :, tm - 1, :]
    before = jnp.concatenate([lasts[:, :1], lasts[:, :-1]], axis=1)
    after = jnp.concatenate([firsts[:, 1:], firsts[:, -1:]], axis=1)
    x_edge = jnp.stack([before, after], axis=2).reshape(b, 2 * n_tiles, d)
    z_edge = _inproj_nt(x_edge, n1g2, sh1, sc1, wt, 2 * n_tiles)
    inside = jnp.ones((2 * n_tiles,), F32).at[0].set(0.0).at[2 * n_tiles - 1].set(0.0)
    halo = z_edge[:, 0].astype(F32) * inside
    conv_params = jnp.repeat(jnp.concatenate([conv_w.T, conv_b[:, None]], axis=1), LANES, axis=1)

    z4 = _inproj_nt(x, n1g2, sh1, sc1, wt, DFT_MINOR, conv_params, halo)
    y4 = _hyena_core(z4, kf, fbias.T, tabs["f1_half"], tabs["twr"], tabs["twi"], tabs["f2e"], tabs["f2c"],
                     tabs["g1e"])
    x, hs = _proj_res([y4], w_out.astype(BF16), x, g1, transposed=True, tn_pref=d,
                      next_norm=(n2g.reshape(1, d), sh2, sc2))
    return _ffn(x, hs, g2, wg, wu, wd)


def _rms_kernel(x_ref, g_ref, o_ref):
    x = x_ref[...]
    ms = jnp.mean(x * x, axis=-1, keepdims=True)
    o_ref[...] = x * lax.rsqrt(ms + NORM_EPS) * g_ref[...]


def _final_norm(x, g):
    b, s, d = x.shape
    tm = _tile(s, 1024)
    return pl.pallas_call(
        _rms_kernel,
        grid=(b, s // tm),
        in_specs=[pl.BlockSpec((None, tm, d), lambda bb, i: (bb, i, 0)),
                  pl.BlockSpec((1, d), lambda bb, i: (0, 0))],
        out_specs=pl.BlockSpec((None, tm, d), lambda bb, i: (bb, i, 0)),
        out_shape=jax.ShapeDtypeStruct((b, s, d), F32),
        compiler_params=_params("parallel", "parallel"),
        name="final_norm",
    )(x, g.reshape(1, d))


def kernel(x, c, ctx, c_ctx, ada_w, ada_b, norm1, norm2, ffn_w_gate, ffn_w_up, ffn_w_down, e_w_in, e_w_out, e_lambda, e_subln, e_sgu_norm_g, e_sgu_norm_b, e_sgu_w, e_sgu_b, o_w_in, o_conv_w, o_conv_b, o_filt_w1, o_filt_b1, o_filt_w2, o_filt_b2, o_filt_w3, o_filt_b3, o_filt_freq, o_filt_w4, o_filt_bias, o_w_out, final_norm):
    b, s, d = x.shape
    depth = ada_w.shape[0]
    assert b == 2, "the long convolution packs exactly two batches into one complex signal"
    assert depth == 2, "odd layers here never carry the context stream"
    cond_t = jnp.zeros((d, 8), F32).at[:, :b].set(c.T).at[:, b].set(c_ctx)
    mod_all = _adaln_all(cond_t, b + 1, ada_w, ada_b)
    xc = ctx
    for i in range(depth):
        j = i // 2
        parts = jnp.split(mod_all[i], 6, axis=-1)
        mods = [p[:b, None, :] for p in parts]
        cmods = [jnp.broadcast_to(p[b:b + 1, None, :], (b, 1, d)) for p in parts]
        wg, wu, wd = (ffn_w_gate[i].astype(BF16), ffn_w_up[i].astype(BF16), ffn_w_down[i].astype(BF16))
        if i % 2 == 0:
            x, xc = _even_layer(x, xc, mods, cmods, norm1[i], norm2[i], e_w_in[j], e_w_out[j], e_lambda[j],
                                e_subln[j], e_sgu_norm_g[j], e_sgu_norm_b[j], e_sgu_w[j], e_sgu_b[j],
                                wg, wu, wd, i)
        else:
            x = _odd_layer(x, mods, norm1[i], norm2[i], o_w_in[j], o_conv_w[j], o_conv_b[j], o_filt_w1[j],
                           o_filt_b1[j], o_filt_w2[j], o_filt_b2[j], o_filt_w3[j], o_filt_b3[j],
                           o_filt_freq[j], o_filt_w4[j], o_filt_bias[j], o_w_out[j], wg, wu, wd)
    return _final_norm(x, final_norm)
```

```python
import functools
import math

import numpy as np
import jax
import jax.numpy as jnp
from jax import lax
from jax.experimental import pallas as pl
from jax.experimental.pallas import tpu as pltpu

F32 = jnp.float32
BF16 = jnp.bfloat16
HIGHEST = lax.Precision.HIGHEST

GRID_W = 64
NORM_EPS = 1e-6
DIFF_HEADS = 8
DIFF_QK_DIM = 64
DIFF_V_DIM = 2 * DIFF_QK_DIM
DIFF_SCALE = DIFF_QK_DIM ** -0.5
A_WIDTH = DIFF_HEADS * DIFF_V_DIM
Q_COLS = DIFF_HEADS * 2 * DIFF_QK_DIM
ROPE_THETA = 10000.0
ROPE_AXIS_DIM = DIFF_QK_DIM // 2
SUBLN_EPS = 1e-5
SGU_GROUPS = 8
SGU_CHUNK = 128
SGU_CH = 128
B_WIDTH = SGU_GROUPS * SGU_CH
LN_EPS = 1e-5
HYENA_ORDER = 2
SHORT_CONV = 3
FILTER_EMB = 33
FILTER_BANDS = (FILTER_EMB - 1) // 2
FILTER_HIDDEN = 64
DECAY_TARGET = 1e-2
MAX_DECAY = math.log(DECAY_TARGET) / 0.3
MIN_DECAY = math.log(DECAY_TARGET) / 1.5

LANES = 128
ONES_ROWS = 16
ATTN_SUB = 256
DFT_MINOR = 256
VMEM_LIMIT = 48 * 1024 * 1024
VMEM_LIMIT_WIDE = 56 * 1024 * 1024

NT_DIMS = (((1,), (1,)), ((), ()))
TN_DIMS = (((0,), (0,)), ((), ()))


def _params(*sem, vmem=VMEM_LIMIT):
    return pltpu.CompilerParams(dimension_semantics=sem, vmem_limit_bytes=vmem)


def _tile(n, pref):
    return pref if n % pref == 0 else n


def _adaln_kernel(ct_ref, wa_ref, wb_ref, b_ref, o_ref, *, n_rows):
    k = pl.program_id(1)
    a = ct_ref[...]
    a = a * jax.nn.sigmoid(a)
    half = wa_ref.shape[0]
    rows = [jnp.sum(wa_ref[...] * a[:half, r:r + 1], axis=0, keepdims=True)
            + jnp.sum(wb_ref[...] * a[half:, r:r + 1], axis=0, keepdims=True) for r in range(n_rows)]
    rows.append(jnp.zeros((o_ref.shape[0] - n_rows, wa_ref.shape[1]), F32))
    part = jnp.concatenate(rows, axis=0)

    @pl.when(k == 0)
    def _():
        valid = lax.broadcasted_iota(jnp.int32, part.shape, 0) < n_rows
        o_ref[...] = part + jnp.where(valid, b_ref[...], 0.0)

    @pl.when(k > 0)
    def _():
        o_ref[...] += part


def _adaln_all(cond_t, n_rows, ada_w, ada_b):
    depth, d, n6 = ada_w.shape
    tk = _tile(d, 256)
    return pl.pallas_call(
        functools.partial(_adaln_kernel, n_rows=n_rows),
        grid=(depth, d // tk),
        in_specs=[pl.BlockSpec((tk, 8), lambda l, k: (k, 0)),
                  pl.BlockSpec((None, tk // 2, n6), lambda l, k: (l, 2 * k, 0)),
                  pl.BlockSpec((None, tk // 2, n6), lambda l, k: (l, 2 * k + 1, 0)),
                  pl.BlockSpec((None, 1, n6), lambda l, k: (l, 0, 0))],
        out_specs=pl.BlockSpec((None, 8, n6), lambda l, k: (l, 0, 0)),
        out_shape=jax.ShapeDtypeStruct((depth, 8, n6), F32),
        compiler_params=_params("parallel", "arbitrary"),
        name="adaln",
    )(cond_t, ada_w, ada_w, ada_b.reshape(depth, 1, n6))


def _norm_mod_rows(x, g, sh, sc):
    ms = jnp.mean(x * x, axis=-1, keepdims=True)
    y = x * lax.rsqrt(ms + NORM_EPS) * g
    return (y * (1.0 + sc) + sh).astype(BF16)


def _norm_mod(x_ref, g_ref, sh_ref, sc_ref):
    return _norm_mod_rows(x_ref[...], g_ref[...], sh_ref[...], sc_ref[...])


def _inproj_kernel(x_ref, g_ref, sh_ref, sc_ref, w_ref, cos_ref, sin_ref, o_ref, hs_ref, *,
                   n_q, n_qk, rope):
    j = pl.program_id(2)

    @pl.when(j == 0)
    def _():
        hs_ref[...] = _norm_mod(x_ref, g_ref, sh_ref, sc_ref)

    tm, tn = o_ref.shape
    rc = min(tm, 256)

    def by_row_chunks(epilogue):
        for r in range(tm // rc):
            rows = pl.ds(r * rc, rc)
            acc = jnp.dot(hs_ref[rows, :], w_ref[...], preferred_element_type=F32)
            o_ref[rows, :] = epilogue(acc, rows).astype(o_ref.dtype)

    def qk_epilogue(a, rows):
        if rope:
            lane = lax.broadcasted_iota(jnp.int32, a.shape, 1)
            first = (lane & 31) < 16
            partner = jnp.where(first, pltpu.roll(a, tn - 16, 1), pltpu.roll(a, 16, 1))
            reps = tn // LANES
            a = (a * jnp.tile(cos_ref[rows, :], (1, reps))
                 + partner * jnp.tile(sin_ref[rows, :], (1, reps)))
        return jnp.where(j < n_q, a * DIFF_SCALE, a)

    @pl.when(j < n_qk)
    def _():
        by_row_chunks(qk_epilogue)

    @pl.when(j >= n_qk)
    def _():
        by_row_chunks(lambda a, rows: jax.nn.gelu(a))


def _inproj(x, g, sh, sc, w, cos_t, sin_t, rope):
    b, s, d = x.shape
    n = w.shape[1]
    tm = _tile(s, 1024)
    tn = 512
    kern = functools.partial(_inproj_kernel, n_q=Q_COLS // tn, n_qk=2 * Q_COLS // tn, rope=rope)
    return pl.pallas_call(
        kern,
        grid=(b, s // tm, n // tn),
        in_specs=[pl.BlockSpec((None, tm, d), lambda bb, i, j: (bb, i, 0)),
                  pl.BlockSpec((1, d), lambda bb, i, j: (0, 0)),
                  pl.BlockSpec((None, 1, d), lambda bb, i, j: (bb, 0, 0)),
                  pl.BlockSpec((None, 1, d), lambda bb, i, j: (bb, 0, 0)),
                  pl.BlockSpec((d, tn), lambda bb, i, j: (0, j)),
                  pl.BlockSpec((tm, LANES), lambda bb, i, j: (i, 0)),
                  pl.BlockSpec((tm, LANES), lambda bb, i, j: (i, 0))],
        out_specs=[pl.BlockSpec((None, tm, tn), lambda bb, i, j: (bb, i, j)),
                   pl.BlockSpec((None, tm, d), lambda bb, i, j: (bb, i, 0))],
        out_shape=[jax.ShapeDtypeStruct((b, s, n), BF16), jax.ShapeDtypeStruct((b, s, d), BF16)],
        compiler_params=_params("parallel", "parallel", "arbitrary"),
        name="inproj",
    )(x, g, sh, sc, w, cos_t, sin_t)


def _inproj_nt_kernel(*refs, tl, conv, normalized):
    if normalized:
        hs_ref, wt_ref, o_ref = refs
    else:
        x_ref, g_ref, sh_ref, sc_ref, wt_ref = refs[:5]
        if conv:
            cw_ref, halo_ref, o_ref, hs_ref = refs[5:]
        else:
            o_ref, hs_ref = refs[5:]

        @pl.when(pl.program_id(2) == 0)
        def _():
            hs_ref[...] = _norm_mod(x_ref, g_ref, sh_ref, sc_ref)

    n_chunks = o_ref.shape[0]
    if not conv:
        acc = lax.dot_general(wt_ref[...], hs_ref[...], NT_DIMS, preferred_element_type=F32)
        for c in range(n_chunks):
            o_ref[c] = acc[:, c * tl:(c + 1) * tl].astype(o_ref.dtype)
        return

    acc = jnp.concatenate(
        [lax.dot_general(wt_ref[...], hs_ref[c * tl:(c + 1) * tl, :], NT_DIMS, preferred_element_type=F32)
         for c in range(n_chunks)], axis=1)
    tn, tm = acc.shape
    halo = halo_ref[...]
    col = lax.broadcasted_iota(jnp.int32, halo.shape, 1)
    tile = pl.program_id(1)
    before = jnp.sum(jnp.where(col == 2 * tile, halo, 0.0), axis=1, keepdims=True)
    after = jnp.sum(jnp.where(col == 2 * tile + 1, halo, 0.0), axis=1, keepdims=True)
    lane = lax.broadcasted_iota(jnp.int32, (tn, LANES), 1)
    prev = pltpu.roll(acc, 1, 1)
    prev = jnp.concatenate([jnp.where(lane == 0, before, prev[:, :LANES]), prev[:, LANES:]], axis=1)
    nxt = pltpu.roll(acc, tm - 1, 1)
    nxt = jnp.concatenate([nxt[:, :tm - LANES],
                           jnp.where(lane == LANES - 1, after, nxt[:, tm - LANES:])], axis=1)

    def tap(k):
        return jnp.tile(cw_ref[:, k * LANES:(k + 1) * LANES], (1, tm // LANES))

    out = tap(3) + prev * tap(0) + acc * tap(1) + nxt * tap(2)
    for c in range(n_chunks):
        o_ref[c] = out[:, c * tl:(c + 1) * tl].astype(o_ref.dtype)


def _inproj_nt(x, g, sh, sc, wt, tl, conv_params=None, halo=None):
    b, s, d = x.shape
    n = wt.shape[0]
    tm = _tile(s, 1024)
    tn = 512
    conv = conv_params is not None
    normalized = g is None
    assert not (normalized and conv)
    kern = functools.partial(_inproj_nt_kernel, tl=tl, conv=conv, normalized=normalized)
    row_spec = pl.BlockSpec((None, tm, d), lambda bb, i, j: (bb, i, 0))
    w_spec = pl.BlockSpec((tn, d), lambda bb, i, j: (j, 0))
    if normalized:
        in_specs, args, scratch = [row_spec, w_spec], [x, wt], []
    else:
        in_specs = [row_spec,
                    pl.BlockSpec((1, d), lambda bb, i, j: (0, 0)),
                    pl.BlockSpec((None, 1, d), lambda bb, i, j: (bb, 0, 0)),
                    pl.BlockSpec((None, 1, d), lambda bb, i, j: (bb, 0, 0)),
                    w_spec]
        args, scratch = [x, g, sh, sc, wt], [pltpu.VMEM((tm, d), BF16)]
    if conv:
        in_specs += [pl.BlockSpec((tn, 4 * LANES), lambda bb, i, j: (j, 0)),
                     pl.BlockSpec((None, tn, halo.shape[2]), lambda bb, i, j: (bb, j, 0))]
        args += [conv_params, halo]
    return pl.pallas_call(
        kern,
        grid=(b, s // tm, n // tn),
        in_specs=in_specs,
        out_specs=pl.BlockSpec((None, tm // tl, tn, tl), lambda bb, i, j: (bb, i, j, 0)),
        out_shape=jax.ShapeDtypeStruct((b, s // tl, n, tl), BF16),
        scratch_shapes=scratch,
        compiler_params=_params("parallel", "parallel", "arbitrary"),
        name="inproj_nt",
    )(*args)


def _ffn_up_kernel(hs_ref, wg_ref, wu_ref, o_ref):
    hs = hs_ref[...]
    gate = jnp.dot(hs, wg_ref[...], preferred_element_type=F32)
    up = jnp.dot(hs, wu_ref[...], preferred_element_type=F32)
    o_ref[...] = (gate * jax.nn.sigmoid(gate) * up).astype(o_ref.dtype)


def _ffn_up(hs, wg, wu):
    b, s, d = hs.shape
    n = wg.shape[1]
    tm = _tile(s, 1024)
    tn = 512
    return pl.pallas_call(
        _ffn_up_kernel,
        grid=(b, s // tm, n // tn),
        in_specs=[pl.BlockSpec((None, tm, d), lambda bb, i, j: (bb, i, 0)),
                  pl.BlockSpec((d, tn), lambda bb, i, j: (0, j)),
                  pl.BlockSpec((d, tn), lambda bb, i, j: (0, j))],
        out_specs=pl.BlockSpec((None, tm, tn), lambda bb, i, j: (bb, i, j)),
        out_shape=jax.ShapeDtypeStruct((b, s, n), BF16),
        compiler_params=_params("parallel", "parallel", "parallel"),
        name="ffn_up",
    )(hs, wg, wu)


def _proj_res_kernel(*refs, ksizes, transposed, next_norm):
    n = len(ksizes)
    a_refs = refs[:n]
    if next_norm:
        w_ref, x_ref, gate_ref, ng_ref, nsh_ref, nsc_ref, o_ref, hs_ref = refs[n:]
    else:
        w_ref, x_ref, gate_ref, o_ref = refs[n:]

    def emit(rows, acc):
        y = x_ref[rows, :] + gate_ref[...] * acc
        o_ref[rows, :] = y
        if next_norm:
            hs_ref[rows, :] = _norm_mod_rows(y, ng_ref[...], nsh_ref[...], nsc_ref[...])

    if transposed:
        (a_ref,) = a_refs
        tl = a_ref.shape[2]
        for c in range(a_ref.shape[0]):
            emit(slice(c * tl, (c + 1) * tl),
                 lax.dot_general(a_ref[c], w_ref[...], TN_DIMS, preferred_element_type=F32))
        return
    acc = None
    off = 0
    for a_ref, ks in zip(a_refs, ksizes):
        part = jnp.dot(a_ref[...], w_ref[off:off + ks, :], preferred_element_type=F32)
        acc = part if acc is None else acc + part
        off += ks
    emit(slice(None), acc)


def _proj_res(a_list, w, x, gate, transposed=False, tm_pref=512, tn_pref=512, vmem=VMEM_LIMIT, next_norm=None):
    b, s, d = x.shape
    ksizes = tuple(a.shape[2] for a in a_list)
    ktot = sum(ksizes)
    tm = _tile(s, tm_pref)
    tn = _tile(d, tn_pref)
    if transposed:
        tl = a_list[0].shape[3]
        a_specs = [pl.BlockSpec((None, tm // tl, ktot, tl), lambda bb, i, j: (bb, i, 0, 0))]
    else:
        a_specs = [pl.BlockSpec((None, tm, ks), lambda bb, i, j: (bb, i, 0)) for ks in ksizes]
    in_specs = a_specs + [pl.BlockSpec((ktot, tn), lambda bb, i, j: (0, j)),
                          pl.BlockSpec((None, tm, tn), lambda bb, i, j: (bb, i, j)),
                          pl.BlockSpec((None, 1, tn), lambda bb, i, j: (bb, 0, j))]
    args = [*a_list, w, x, gate]
    out_spec = pl.BlockSpec((None, tm, tn), lambda bb, i, j: (bb, i, j))
    out_specs, out_shape = out_spec, jax.ShapeDtypeStruct((b, s, d), F32)
    if next_norm is not None:
        assert tn == d
        vmem = VMEM_LIMIT_WIDE
        in_specs += [pl.BlockSpec((1, d), lambda bb, i, j: (0, 0)),
                     pl.BlockSpec((None, 1, d), lambda bb, i, j: (bb, 0, 0)),
                     pl.BlockSpec((None, 1, d), lambda bb, i, j: (bb, 0, 0))]
        args += list(next_norm)
        out_specs, out_shape = [out_spec, out_spec], [out_shape, jax.ShapeDtypeStruct((b, s, d), BF16)]
    kern = functools.partial(_proj_res_kernel, ksizes=ksizes, transposed=transposed,
                             next_norm=next_norm is not None)
    return pl.pallas_call(
        kern,
        grid=(b, s // tm, d // tn),
        in_specs=in_specs,
        out_specs=out_specs,
        out_shape=out_shape,
        compiler_params=_params("parallel", "parallel", "parallel", vmem=vmem),
        name="proj_res",
    )(*args)


def _attn_kernel(*refs, n_chunks, lam_init):
    if n_chunks:
        lam_ref, q_ref, kc_ref, vct_ref, k_ref, vt_ref, g_ref, o_ref, acc_ref = refs[:9]
        n_slots = (len(refs) - 9) // 2
        slots = tuple(zip(refs[9:9 + n_slots], refs[9 + n_slots:]))
    else:
        lam_ref, q_ref, kc_ref, vct_ref, g_ref, o_ref, acc_ref = refs
    n_sub = acc_ref.shape[0]
    tq = acc_ref.shape[-1]
    dv = DIFF_V_DIM
    q = q_ref[...]
    qm = [(q[s * tq:(s + 1) * tq, :DIFF_QK_DIM], q[s * tq:(s + 1) * tq, DIFF_QK_DIM:]) for s in range(n_sub)]

    def scores(kblk, sub):
        s_pair = tuple(lax.dot_general(kblk[:, m * DIFF_QK_DIM:(m + 1) * DIFF_QK_DIM], qm[sub][m], NT_DIMS,
                                       preferred_element_type=F32) for m in range(2))
        return s_pair, tuple(jnp.max(s, axis=0, keepdims=True) for s in s_pair)

    def with_ones(vtblk):
        return jnp.concatenate([vtblk, jnp.ones((ONES_ROWS, vtblk.shape[1]), BF16)], axis=0)

    def absorb(scored, vext, m_pair, sub):
        s_pair, smax_pair = scored
        out = []
        for m in range(2):
            m_old = m_pair[m]
            m_new = jnp.maximum(m_old, smax_pair[m])
            alpha = jnp.exp(m_old - m_new)
            p = jnp.exp((s_pair[m] - m_new).astype(BF16))
            pv = jnp.dot(vext, p, preferred_element_type=F32)
            acc_ref[sub, m] = alpha * acc_ref[sub, m] + pv
            out.append(m_new)
        return tuple(out)

    def store(slot, scored):
        for m in range(2):
            slot[0][m] = scored[0][m]
            slot[1][m] = scored[1][m]

    def load(slot):
        return (slot[0][0], slot[0][1]), (slot[1][0], slot[1][1])

    acc_ref[...] = jnp.zeros_like(acc_ref)
    init = jnp.full((1, tq), -1e30, F32)
    kc, vc = kc_ref[...], with_ones(vct_ref[0])
    m_state = tuple(absorb(scores(kc, s), vc, (init, init), s) for s in range(n_sub))
    if n_chunks:
        group = len(slots) // (2 * n_sub)
        assert n_chunks % (2 * group) == 0

        def slot(half, k, sub):
            return slots[(half * group + k) * n_sub + sub]

        def half_trip(half, base, ms, lookahead):
            ms = list(ms)
            for k in range(group):
                vext = with_ones(vt_ref[base + k])
                for s in range(n_sub):
                    if lookahead:
                        store(slot(1 - half, k, s), scores(k_ref[base + group + k], s))
                    ms[s] = absorb(load(slot(half, k, s)), vext, ms[s], s)
            return tuple(ms)

        def trip(j, ms, lookahead):
            base = 2 * group * j
            ms = half_trip(0, base, ms, True)
            return half_trip(1, base + group, ms, lookahead)

        for k in range(group):
            for s in range(n_sub):
                store(slot(0, k, s), scores(k_ref[k], s))
        n_trips = n_chunks // (2 * group)
        unroll = next(u for u in (5, 4, 3, 2, 1) if (n_trips - 1) % u == 0)
        m_state = lax.fori_loop(0, n_trips - 1, lambda j, ms: trip(j, ms, True), m_state, unroll=unroll)
        m_state = trip(n_trips - 1, m_state, False)

    lp = lam_ref[...]
    lam = (jnp.exp(jnp.sum(lp[0:1] * lp[1:2], axis=-1, keepdims=True))
           - jnp.exp(jnp.sum(lp[2:3] * lp[3:4], axis=-1, keepdims=True)) + lam_init)
    for s in range(n_sub):
        acc0, acc1 = acc_ref[s, 0], acc_ref[s, 1]
        o = acc0[:dv] / acc0[dv:dv + 1] - lam * (acc1[:dv] / acc1[dv:dv + 1])
        ot = o.T
        ms = jnp.mean(ot * ot, axis=-1, keepdims=True)
        on = ot * lax.rsqrt(ms + SUBLN_EPS) * g_ref[...] * (1.0 - lam_init)
        o_ref[s * tq:(s + 1) * tq, :] = on.astype(o_ref.dtype)


def _attention(lam_p, subln, q_arr, kc_arr, vct_arr, k4=None, vt4=None, *, lam_init):
    b, sq = q_arr.shape[0], q_arr.shape[1]
    sc = kc_arr.shape[1]
    h = DIFF_HEADS
    dv = DIFF_V_DIM
    n_sub = next(n for n in (4, 2, 1) if sq % (n * ATTN_SUB) == 0)
    tq = n_sub * ATTN_SUB
    n_chunks = 0 if k4 is None else k4.shape[1]
    in_specs = [pl.BlockSpec((4, DIFF_QK_DIM), lambda bb, hh, i: (0, 0)),
                pl.BlockSpec((None, tq, dv), lambda bb, hh, i: (bb, i, hh)),
                pl.BlockSpec((None, sc, dv), lambda bb, hh, i: (bb, 0, h + hh)),
                pl.BlockSpec((None, 1, dv, sc), lambda bb, hh, i: (bb, 0, hh, 0))]
    args = [lam_p, q_arr, kc_arr, vct_arr]
    if n_chunks:
        tk = k4.shape[2]
        in_specs += [pl.BlockSpec((None, n_chunks, tk, dv), lambda bb, hh, i: (bb, 0, 0, h + hh)),
                     pl.BlockSpec((None, n_chunks, dv, tk), lambda bb, hh, i: (bb, 0, hh, 0))]
        args += [k4, vt4]
    in_specs.append(pl.BlockSpec((1, dv), lambda bb, hh, i: (0, 0)))
    args.append(subln)
    kern = functools.partial(_attn_kernel, n_chunks=n_chunks, lam_init=lam_init)
    scratch = [pltpu.VMEM((n_sub, 2, dv + ONES_ROWS, ATTN_SUB), F32)]
    if n_chunks:
        n_slots = 8 if n_chunks % 16 == 0 else 2 * n_sub
        scratch += [pltpu.VMEM((2, tk, ATTN_SUB), F32) for _ in range(n_slots)]
        scratch += [pltpu.VMEM((2, 1, ATTN_SUB), F32) for _ in range(n_slots)]
    return pl.pallas_call(
        kern,
        grid=(b, h, sq // tq),
        in_specs=in_specs,
        out_specs=pl.BlockSpec((None, tq, dv), lambda bb, hh, i: (bb, i, hh)),
        out_shape=jax.ShapeDtypeStruct((b, sq, A_WIDTH), BF16),
        scratch_shapes=scratch,
        compiler_params=_params("parallel", "parallel", "parallel"),
        name="diff_attn",
    )(*args)


def _sgu_kernel(u_ref, g_ref, ng_ref, nb_ref, w_ref, bs_ref, o_ref):
    for gi in range(SGU_GROUPS):
        cols = slice(gi * SGU_CH, (gi + 1) * SGU_CH)
        w = w_ref[gi]
        for c in range(u_ref.shape[0] // SGU_CHUNK):
            sl = slice(c * SGU_CHUNK, (c + 1) * SGU_CHUNK)
            gg = g_ref[sl, cols].astype(F32)
            mu = jnp.mean(gg, axis=-1, keepdims=True)
            dev = gg - mu
            var = jnp.mean(dev * dev, axis=-1, keepdims=True)
            vv = dev * lax.rsqrt(var + LN_EPS) * ng_ref[gi] + nb_ref[gi]
            mixed = jnp.dot(w, vv.astype(BF16), preferred_element_type=F32) + bs_ref[gi]
            o_ref[sl, cols] = (u_ref[sl, cols].astype(F32) * mixed).astype(o_ref.dtype)


def _sgu(qkug, norm_g, norm_b, w_s, b_s):
    b, s = qkug.shape[0], qkug.shape[1]
    tm = _tile(s, 1024)
    gcount = SGU_GROUPS
    ublk = 2 * Q_COLS // B_WIDTH
    full = lambda shape: pl.BlockSpec(shape, lambda bb, i: (0,) * len(shape))
    return pl.pallas_call(
        _sgu_kernel,
        grid=(b, s // tm),
        in_specs=[pl.BlockSpec((None, tm, B_WIDTH), lambda bb, i: (bb, i, ublk)),
                  pl.BlockSpec((None, tm, B_WIDTH), lambda bb, i: (bb, i, ublk + 1)),
                  full((gcount, 1, SGU_CH)), full((gcount, 1, SGU_CH)),
                  full((gcount, SGU_CHUNK, SGU_CHUNK)), full((gcount, SGU_CHUNK, 1))],
        out_specs=pl.BlockSpec((None, tm, B_WIDTH), lambda bb, i: (bb, i, 0)),
        out_shape=jax.ShapeDtypeStruct((b, s, B_WIDTH), BF16),
        compiler_params=_params("parallel", "parallel"),
        name="sgu",
    )(qkug, qkug, norm_g.reshape(gcount, 1, SGU_CH), norm_b.reshape(gcount, 1, SGU_CH),
      w_s.astype(BF16), b_s.reshape(gcount, SGU_CHUNK, 1))


def _filter_mlp_kernel(f_ref, w1_ref, b1_ref, w2_ref, b2_ref, w3_ref, b3_ref, fr_ref, o_ref):
    def lin(a, w_ref, b_ref):
        return jnp.dot(a, w_ref[...], preferred_element_type=F32, precision=HIGHEST) + b_ref[...]
    fr = fr_ref[...]
    hcur = jnp.sin(fr[0:1] * lin(f_ref[...], w1_ref, b1_ref))
    hcur = jnp.sin(fr[1:2] * lin(hcur, w2_ref, b2_ref))
    o_ref[...] = jnp.sin(fr[2:3] * lin(hcur, w3_ref, b3_ref))


def _filter_mlp(feats, w1, b1, w2, b2, w3, b3, freq):
    rows, emb = feats.shape
    hid = FILTER_HIDDEN
    tr = _tile(rows, 2048)
    full = lambda shape: pl.BlockSpec(shape, lambda i: (0,) * len(shape))
    return pl.pallas_call(
        _filter_mlp_kernel,
        grid=(rows // tr,),
        in_specs=[pl.BlockSpec((tr, emb), lambda i: (i, 0)),
                  full((emb, hid)), full((1, hid)), full((hid, hid)), full((1, hid)),
                  full((hid, hid)), full((1, hid)), full((3, hid))],
        out_specs=pl.BlockSpec((tr, hid), lambda i: (i, 0)),
        out_shape=jax.ShapeDtypeStruct((rows, hid), F32),
        compiler_params=_params("parallel"),
        name="filter_mlp",
    )(feats, w1, b1.reshape(1, hid), w2, b2.reshape(1, hid), w3, b3.reshape(1, hid), freq)


def _filter_raw_kernel(w4t_ref, h_ref, t_ref, delta_ref, o_ref, *, zero_tile):
    rt = pl.program_id(2)

    def split(a):
        hi = a.astype(BF16)
        return hi, (a - hi.astype(F32)).astype(BF16)

    def nt(a, bm):
        return lax.dot_general(a, bm, NT_DIMS, preferred_element_type=F32)

    w_hi, w_lo = split(w4t_ref[...])
    h_hi, h_lo = split(h_ref[...])
    raw = nt(w_hi, h_hi) + (nt(w_hi, h_lo) + nt(w_lo, h_hi))
    raw = raw * jnp.exp(-(delta_ref[...] * t_ref[...]))
    for c in range(o_ref.shape[0]):
        o_ref[c] = raw[:, c * DFT_MINOR:(c + 1) * DFT_MINOR].astype(o_ref.dtype)

    @pl.when(rt == zero_tile)
    def _():
        col = lax.broadcasted_iota(jnp.int32, (raw.shape[0], DFT_MINOR), 1)
        o_ref[0] = jnp.where(col == 0, 0.0, raw[:, :DFT_MINOR]).astype(o_ref.dtype)


def _filter_raw(w4t, h3, t_row, deltas, seq):
    c = w4t.shape[2]
    rows = h3.shape[0]
    tr = _tile(rows // 2, 2048)
    tc = _tile(c, 512)
    half_tiles = seq // tr
    kern = functools.partial(_filter_raw_kernel, zero_tile=half_tiles)
    return pl.pallas_call(
        kern,
        grid=(HYENA_ORDER, c // tc, rows // tr),
        in_specs=[pl.BlockSpec((None, None, tc, FILTER_HIDDEN),
                               lambda n, ci, rt: (n, rt // half_tiles, ci, 0)),
                  pl.BlockSpec((tr, FILTER_HIDDEN), lambda n, ci, rt: (rt, 0)),
                  pl.BlockSpec((1, tr), lambda n, ci, rt: (0, rt)),
                  pl.BlockSpec((tc, 1), lambda n, ci, rt: (ci, 0))],
        out_specs=pl.BlockSpec((None, tr // DFT_MINOR, tc, DFT_MINOR), lambda n, ci, rt: (n, rt, ci, 0)),
        out_shape=jax.ShapeDtypeStruct((HYENA_ORDER, rows // DFT_MINOR, c, DFT_MINOR), BF16),
        compiler_params=_params("parallel", "parallel", "parallel"),
        name="filter_raw",
    )(w4t, h3, t_row, deltas)


def _cmul(ar, ai, br, bi):
    return ar * br - ai * bi, ar * bi + ai * br


def _store_complex(ref, c, n1, re, im):
    r0 = pl.multiple_of(c * n1, n1)
    ref[pl.ds(r0, n1), :DFT_MINOR] = re.astype(ref.dtype)
    ref[pl.ds(r0, n1), DFT_MINOR:] = im.astype(ref.dtype)


def _load_complex(ref, c, n1):
    r0 = pl.multiple_of(c * n1, n1)
    tile = ref[pl.ds(r0, n1), :]
    return tile[:, :DFT_MINOR], tile[:, DFT_MINOR:]


def _filter_spec_kernel(k_ref, f1_ref, twr_ref, twi_ref, f2_ref, o_ref, a2_ref, *, inv_n):
    tc, n1, n2 = k_ref.shape

    def left(c, carry):
        k = k_ref[c].astype(F32)
        nrm = jnp.sum(jnp.sum(jnp.abs(k), axis=1, keepdims=True), axis=0, keepdims=True)
        kn = (k * (inv_n / nrm)).astype(BF16)
        a = jnp.dot(f1_ref[...], kn, preferred_element_type=F32)
        ar, ai = _cmul(a[:n1], a[n1:], twr_ref[...], twi_ref[...])
        _store_complex(a2_ref, c, n1, ar, ai)
        return carry

    lax.fori_loop(0, tc, left, 0, unroll=True)
    z = jnp.dot(a2_ref[...], f2_ref[...], preferred_element_type=F32).reshape(tc, n1, 2 * n2)
    o_ref[:, 0] = z[:, :, :n2].astype(o_ref.dtype)
    o_ref[:, 1] = z[:, :, n2:].astype(o_ref.dtype)


def _filter_spec(k2, f1_full, twr, twi, f2e):
    nc, n1, n2 = k2.shape
    tc = 16
    kern = functools.partial(_filter_spec_kernel, inv_n=1.0 / (n1 * n2))
    full = lambda shape: pl.BlockSpec(shape, lambda i: (0,) * len(shape))
    return pl.pallas_call(
        kern,
        grid=(nc // tc,),
        in_specs=[pl.BlockSpec((tc, n1, n2), lambda i: (i, 0, 0)),
                  full((2 * n1, n1)), full((n1, n2)), full((n1, n2)), full((2 * n2, 2 * n2))],
        out_specs=pl.BlockSpec((tc, 2, n1, n2), lambda i: (i, 0, 0, 0)),
        out_shape=jax.ShapeDtypeStruct((nc, 2, n1, n2), BF16),
        scratch_shapes=[pltpu.VMEM((tc * n1, 2 * n2), BF16)],
        compiler_params=_params("parallel"),
        name="filter_spec",
    )(k2, f1_full, twr, twi, f2e)


def _hyena_kernel(zv_ref, z1_ref, z2_ref, kf_ref, fb_ref, f1_ref, twr_ref, twi_ref, f2_ref, f2c_ref,
                  g1_ref, o_ref, sig_ref, a2_ref, z_ref):
    nb, hr, tc, n2 = zv_ref.shape
    n1 = 2 * hr
    for part, ref in enumerate((zv_ref, z1_ref, z2_ref)):
        for bb in range(nb):
            sig_ref[part, bb] = pltpu.einshape("tcl->ctl", ref[bb].astype(F32))

    for n in range(HYENA_ORDER):
        def fwd_left(c, carry):
            xs = jnp.concatenate([sig_ref[0, 0, c], sig_ref[0, 1, c]], axis=0).astype(BF16)
            a = jnp.dot(f1_ref[...], xs, preferred_element_type=F32)
            ar, ai = _cmul(a[:n1], a[n1:], twr_ref[...], twi_ref[...])
            _store_complex(a2_ref, c, n1, ar, ai)
            return carry

        lax.fori_loop(0, tc, fwd_left, 0, unroll=True)
        z_ref[...] = jnp.dot(a2_ref[...], f2_ref[...], preferred_element_type=F32)

        def spectrum(c, carry):
            zr, zi = _load_complex(z_ref, c, n1)
            wr, wi = _cmul(zr, zi, kf_ref[n, c, 0].astype(F32), kf_ref[n, c, 1].astype(F32))
            _store_complex(a2_ref, c, n1, wr, wi)
            return carry

        lax.fori_loop(0, tc, spectrum, 0, unroll=True)
        z_ref[...] = jnp.dot(a2_ref[...], f2c_ref[...], preferred_element_type=F32)

        def inv_left(c, carry):
            br, bi = _load_complex(z_ref, c, n1)
            br, bi = _cmul(br, bi, twr_ref[...], -twi_ref[...])
            bs = jnp.concatenate([br, bi], axis=0).astype(BF16)
            y = jnp.dot(g1_ref[...], bs, preferred_element_type=F32)
            fb = fb_ref[pl.ds(c, 1), n:n + 1]
            for bb in range(nb):
                sig_ref[0, bb, c] = sig_ref[1 + n, bb, c] * (y[bb * hr:(bb + 1) * hr] + sig_ref[0, bb, c] * fb)
            return carry

        lax.fori_loop(0, tc, inv_left, 0, unroll=True)

    for bb in range(nb):
        o_ref[bb] = pltpu.einshape("ctl->tcl", sig_ref[0, bb]).astype(o_ref.dtype)


def _hyena_core(z4, kf, fbias, f1h, twr, twi, f2e, f2c, g1e):
    nb, hr, c3, n2 = z4.shape
    c = c3 // 3
    n1 = 2 * hr
    tc = 16
    nct = c // tc
    full = lambda shape: pl.BlockSpec(shape, lambda i: (0,) * len(shape))
    zspec = lambda part: pl.BlockSpec((nb, hr, tc, n2), lambda i: (0, 0, i + part * nct, 0))
    return pl.pallas_call(
        _hyena_kernel,
        grid=(nct,),
        in_specs=[zspec(0), zspec(1), zspec(2),
                  pl.BlockSpec((HYENA_ORDER, tc, 2, n1, n2), lambda i: (0, i, 0, 0, 0)),
                  pl.BlockSpec((tc, HYENA_ORDER), lambda i: (i, 0)),
                  full((2 * n1, n1)), full((n1, n2)), full((n1, n2)),
                  full((2 * n2, 2 * n2)), full((2 * n2, 2 * n2)), full((n1, 2 * n1))],
        out_specs=pl.BlockSpec((nb, hr, tc, n2), lambda i: (0, 0, i, 0)),
        out_shape=jax.ShapeDtypeStruct((nb, hr, c, n2), BF16),
        scratch_shapes=[pltpu.VMEM((3, nb, tc, hr, n2), F32),
                        pltpu.VMEM((tc * n1, 2 * n2), BF16),
                        pltpu.VMEM((tc * n1, 2 * n2), F32)],
        compiler_params=_params("parallel"),
        name="hyena_core",
    )(z4, z4, z4, kf, fbias, f1h, twr, twi, f2e, f2c, g1e)


def _dft_tables(n1):
    n2 = DFT_MINOR
    n = n1 * n2
    a1 = 2.0 * np.pi * np.outer(np.arange(n1), np.arange(n1)) / n1
    f1r, f1i = np.cos(a1), -np.sin(a1)
    a2 = 2.0 * np.pi * np.outer(np.arange(n2), np.arange(n2)) / n2
    f2r, f2i = np.cos(a2), -np.sin(a2)
    at = 2.0 * np.pi * np.outer(np.arange(n1), np.arange(n2)) / n
    twr, twi = np.cos(at), -np.sin(at)
    hr = n1 // 2
    f1_full = np.concatenate([f1r, f1i], axis=0)
    f1_half = np.block([[f1r[:, :hr], -f1i[:, :hr]], [f1i[:, :hr], f1r[:, :hr]]])
    f2e = np.block([[f2r, f2i], [-f2i, f2r]])
    f2c = np.block([[f2r, -f2i], [f2i, f2r]])
    gr, gi = f1r[:hr, :], -f1i[:hr, :]
    g1e = np.block([[gr, -gi], [gi, gr]])
    bf = lambda m: jnp.asarray(m, F32).astype(BF16)
    return dict(f1_full=bf(f1_full), f1_half=bf(f1_half), f2e=bf(f2e), f2c=bf(f2c), g1e=bf(g1e),
                twr=jnp.asarray(twr, F32), twi=jnp.asarray(twi, F32))


def _hyena_spectra(seq, fw1, fb1, fw2, fb2, fw3, fb3, ffreq, fw4, tabs):
    l = seq
    width = fw4.shape[1] // (2 * HYENA_ORDER)
    t_fwd = jnp.linspace(0.0, 1.0, l, dtype=F32)
    w = 2.0 * math.pi * jnp.arange(l, dtype=F32)[:, None] / l
    f = jnp.linspace(1e-4, FILTER_BANDS - 1, FILTER_BANDS, dtype=F32)[None, :]
    feats = jnp.concatenate([t_fwd[:, None], jnp.cos(f * w), -jnp.sin(f * w)], axis=-1)

    def two_sided(a):
        return jnp.concatenate([a, a[:1], a[1:][::-1]], axis=0)

    t_lin = two_sided(t_fwd)
    emb_pad = FILTER_HIDDEN
    feats = jnp.pad(feats, ((0, 0), (0, emb_pad - FILTER_EMB)))
    w1p = jnp.pad(fw1, ((0, emb_pad - FILTER_EMB), (0, 0)))
    h3 = two_sided(_filter_mlp(feats, w1p, fb1, fw2, fb2, fw3, fb3, ffreq))
    w4t = fw4.reshape(FILTER_HIDDEN, HYENA_ORDER, 2, width).transpose(1, 2, 3, 0)
    deltas = jnp.abs(jnp.linspace(MIN_DECAY, MAX_DECAY, width, dtype=F32))[:, None]
    raw = _filter_raw(w4t, h3, t_lin[None, :], deltas, l)
    n1 = raw.shape[1]
    k2 = raw.transpose(0, 2, 1, 3).reshape(HYENA_ORDER * width, n1, DFT_MINOR)
    kf = _filter_spec(k2, tabs["f1_full"], tabs["twr"], tabs["twi"], tabs["f2e"])
    return kf.reshape(HYENA_ORDER, width, 2, n1, DFT_MINOR)


def _rope_tables(n_tokens):
    tok = jnp.arange(n_tokens)
    row = (tok // GRID_W).astype(F32)
    col = (tok % GRID_W).astype(F32)
    half = ROPE_AXIS_DIM // 2
    inv = 1.0 / (ROPE_THETA ** (jnp.arange(0, ROPE_AXIS_DIM, 2, dtype=F32) / ROPE_AXIS_DIM))
    ang_r = row[:, None] * inv
    ang_c = col[:, None] * inv
    cos64 = jnp.concatenate([jnp.cos(ang_r), jnp.cos(ang_r), jnp.cos(ang_c), jnp.cos(ang_c)], axis=-1)
    sin64 = jnp.concatenate([-jnp.sin(ang_r), jnp.sin(ang_r), -jnp.sin(ang_c), jnp.sin(ang_c)], axis=-1)
    assert cos64.shape[1] == 4 * half == DIFF_QK_DIM
    return jnp.tile(cos64, (1, 2)), jnp.tile(sin64, (1, 2))


def _ffn(x, hs, gate, wg, wu, wd):
    return _proj_res([_ffn_up(hs, wg, wu)], wd, x, gate, tm_pref=1024, vmem=VMEM_LIMIT_WIDE)


def _even_layer(x, xc, mods, cmods, n1g, n2g, w_in, w_out, lam_p, subln, sgu_ng, sgu_nb, sgu_w, sgu_b,
                wg, wu, wd, layer_idx):
    b, s, d = x.shape
    lam_init = 0.8 - 0.6 * math.exp(-0.3 * layer_idx)
    sh1, sc1, g1, sh2, sc2, g2 = mods
    csh1, csc1, cg1, csh2, csc2, cg2 = cmods
    o_k, o_v, o_u = Q_COLS, 2 * Q_COLS, 2 * Q_COLS + A_WIDTH
    w_qkug = jnp.concatenate([w_in[:, :o_v], w_in[:, o_u:]], axis=1).astype(BF16)
    w_vt = w_in[:, o_v:o_u].T.astype(BF16)
    wo = w_out.astype(BF16)
    n1g2 = n1g.reshape(1, d)
    cos_t, sin_t = _rope_tables(s)
    sc_len = xc.shape[1]

    qkug, hs1 = _inproj(x, n1g2, sh1, sc1, w_qkug, cos_t, sin_t, rope=True)
    cqkug, chs1 = _inproj(xc, n1g2, csh1, csc1, w_qkug, cos_t[:sc_len], sin_t[:sc_len], rope=False)
    tk = _tile(s, 512)
    vt4 = _inproj_nt(hs1, None, None, None, w_vt, tk)
    cvt4 = _inproj_nt(chs1, None, None, None, w_vt, sc_len)
    k4 = qkug.reshape(b, s // tk, tk, qkug.shape[2])
    subln2 = subln.reshape(1, DIFF_V_DIM)

    a_l = _attention(lam_p, subln2, qkug, cqkug, cvt4, k4, vt4, lam_init=lam_init)
    s_l = _sgu(qkug, sgu_ng, sgu_nb, sgu_w, sgu_b)
    n2g2 = n2g.reshape(1, d)
    x, hs = _proj_res([a_l, s_l], wo, x, g1, tn_pref=d, next_norm=(n2g2, sh2, sc2))
    x = _ffn(x, hs, g2, wg, wu, wd)

    a_c = _attention(lam_p, subln2, cqkug, cqkug, cvt4, lam_init=lam_init)
    s_c = _sgu(cqkug, sgu_ng, sgu_nb, sgu_w, sgu_b)
    xc, hs_c = _proj_res([a_c, s_c], wo, xc, cg1, tn_pref=d, next_norm=(n2g2, csh2, csc2))
    xc = _ffn(xc, hs_c, cg2, wg, wu, wd)
    return x, xc


def _odd_layer(x, mods, n1g, n2g, w_in, conv_w, conv_b, fw1, fb1, fw2, fb2, fw3, fb3, ffreq, fw4, fbias,
               w_out, wg, wu, wd):
    b, s, d = x.shape
    sh1, sc1, g1, sh2, sc2, g2 = mods
    width = w_out.shape[0]
    n1 = 2 * s // DFT_MINOR
    tabs = _dft_tables(n1)
    kf = _hyena_spectra(s, fw1, fb1, fw2, fb2, fw3, fb3, ffreq, fw4, tabs)

    n1g2 = n1g.reshape(1, d)
    wt = w_in.T.astype(BF16)
    tm = _tile(s, 1024)
    n_tiles = s // tm
    x_tiles = x.reshape(b, n_tiles, tm, d)
    firsts, lasts = x_tiles[:, :, 0, :], x_tiles[:, :, tm - 1, :]
    before = jnp.concatenate([lasts[:, :1], lasts[:, :-1]], axis=1)
    after = jnp.concatenate([firsts[:, 1:], firsts[:, -1:]], axis=1)
    x_edge = jnp.stack([before, after], axis=2).reshape(b, 2 * n_tiles, d)
    z_edge = _inproj_nt(x_edge, n1g2, sh1, sc1, wt, 2 * n_tiles)
    inside = jnp.ones((2 * n_tiles,), F32).at[0].set(0.0).at[2 * n_tiles - 1].set(0.0)
    halo = z_edge[:, 0].astype(F32) * inside
    conv_params = jnp.repeat(jnp.concatenate([conv_w.T, conv_b[:, None]], axis=1), LANES, axis=1)

    z4 = _inproj_nt(x, n1g2, sh1, sc1, wt, DFT_MINOR, conv_params, halo)
    y4 = _hyena_core(z4, kf, fbias.T, tabs["f1_half"], tabs["twr"], tabs["twi"], tabs["f2e"], tabs["f2c"],
                     tabs["g1e"])
    x, hs = _proj_res([y4], w_out.astype(BF16), x, g1, transposed=True, tn_pref=d,
                      next_norm=(n2g.reshape(1, d), sh2, sc2))
    return _ffn(x, hs, g2, wg, wu, wd)


def _rms_kernel(x_ref, g_ref, o_ref):
    x = x_ref[...]
    ms = jnp.mean(x * x, axis=-1, keepdims=True)
    o_ref[...] = x * lax.rsqrt(ms + NORM_EPS) * g_ref[...]


def _final_norm(x, g):
    b, s, d = x.shape
    tm = _tile(s, 1024)
    return pl.pallas_call(
        _rms_kernel,
        grid=(b, s // tm),
        in_specs=[pl.BlockSpec((None, tm, d), lambda bb, i: (bb, i, 0)),
                  pl.BlockSpec((1, d), lambda bb, i: (0, 0))],
        out_specs=pl.BlockSpec((None, tm, d), lambda bb, i: (bb, i, 0)),
        out_shape=jax.ShapeDtypeStruct((b, s, d), F32),
        compiler_params=_params("parallel", "parallel"),
        name="final_norm",
    )(x, g.reshape(1, d))


def kernel(x, c, ctx, c_ctx, ada_w, ada_b, norm1, norm2, ffn_w_gate, ffn_w_up, ffn_w_down, e_w_in, e_w_out, e_lambda, e_subln, e_sgu_norm_g, e_sgu_norm_b, e_sgu_w, e_sgu_b, o_w_in, o_conv_w, o_conv_b, o_filt_w1, o_filt_b1, o_filt_w2, o_filt_b2, o_filt_w3, o_filt_b3, o_filt_freq, o_filt_w4, o_filt_bias, o_w_out, final_norm):
    b, s, d = x.shape
    depth = ada_w.shape[0]
    assert b == 2, "the long convolution packs exactly two batches into one complex signal"
    assert depth == 2, "odd layers here never carry the context stream"
    cond_t = jnp.zeros((d, 8), F32).at[:, :b].set(c.T).at[:, b].set(c_ctx)
    mod_all = _adaln_all(cond_t, b + 1, ada_w, ada_b)
    xc = ctx
    for i in range(depth):
        j = i // 2
        parts = jnp.split(mod_all[i], 6, axis=-1)
        mods = [p[:b, None, :] for p in parts]
        cmods = [jnp.broadcast_to(p[b:b + 1, None, :], (b, 1, d)) for p in parts]
        wg, wu, wd = (ffn_w_gate[i].astype(BF16), ffn_w_up[i].astype(BF16), ffn_w_down[i].astype(BF16))
        if i % 2 == 0:
            x, xc = _even_layer(x, xc, mods, cmods, norm1[i], norm2[i], e_w_in[j], e_w_out[j], e_lambda[j],
                                e_subln[j], e_sgu_norm_g[j], e_sgu_norm_b[j], e_sgu_w[j], e_sgu_b[j],
                                wg, wu, wd, i)
        else:
            x = _odd_layer(x, mods, norm1[i], norm2[i], o_w_in[j], o_conv_w[j], o_conv_b[j], o_filt_w1[j],
                           o_filt_b1[j], o_filt_w2[j], o_filt_b2[j], o_filt_w3[j], o_filt_b3[j],
                           o_filt_freq[j], o_filt_w4[j], o_filt_bias[j], o_w_out[j], wg, wu, wd)
    return _final_norm(x, final_norm)
```

```python
import functools
import math

import numpy as np
import jax
import jax.numpy as jnp
from jax import lax
from jax.experimental import pallas as pl
from jax.experimental.pallas import tpu as pltpu

F32 = jnp.float32
BF16 = jnp.bfloat16
HIGHEST = lax.Precision.HIGHEST

GRID_W = 64
NORM_EPS = 1e-6
DIFF_HEADS = 8
DIFF_QK_DIM = 64
DIFF_V_DIM = 2 * DIFF_QK_DIM
DIFF_SCALE = DIFF_QK_DIM ** -0.5
A_WIDTH = DIFF_HEADS * DIFF_V_DIM
Q_COLS = DIFF_HEADS * 2 * DIFF_QK_DIM
ROPE_THETA = 10000.0
ROPE_AXIS_DIM = DIFF_QK_DIM // 2
SUBLN_EPS = 1e-5
SGU_GROUPS = 8
SGU_CHUNK = 128
SGU_CH = 128
B_WIDTH = SGU_GROUPS * SGU_CH
LN_EPS = 1e-5
HYENA_ORDER = 2
SHORT_CONV = 3
FILTER_EMB = 33
FILTER_BANDS = (FILTER_EMB - 1) // 2
FILTER_HIDDEN = 64
DECAY_TARGET = 1e-2
MAX_DECAY = math.log(DECAY_TARGET) / 0.3
MIN_DECAY = math.log(DECAY_TARGET) / 1.5

LANES = 128
ONES_ROWS = 16
ATTN_SUB = 256
DFT_MINOR = 256
VMEM_LIMIT = 48 * 1024 * 1024
VMEM_LIMIT_WIDE = 56 * 1024 * 1024

NT_DIMS = (((1,), (1,)), ((), ()))
TN_DIMS = (((0,), (0,)), ((), ()))


def _params(*sem, vmem=VMEM_LIMIT):
    return pltpu.CompilerParams(dimension_semantics=sem, vmem_limit_bytes=vmem)


def _tile(n, pref):
    return pref if n % pref == 0 else n


def _adaln_kernel(ct_ref, wa_ref, wb_ref, b_ref, o_ref, *, n_rows):
    k = pl.program_id(1)
    a = ct_ref[...]
    a = a * jax.nn.sigmoid(a)
    half = wa_ref.shape[0]
    rows = [jnp.sum(wa_ref[...] * a[:half, r:r + 1], axis=0, keepdims=True)
            + jnp.sum(wb_ref[...] * a[half:, r:r + 1], axis=0, keepdims=True) for r in range(n_rows)]
    rows.append(jnp.zeros((o_ref.shape[0] - n_rows, wa_ref.shape[1]), F32))
    part = jnp.concatenate(rows, axis=0)

    @pl.when(k == 0)
    def _():
        valid = lax.broadcasted_iota(jnp.int32, part.shape, 0) < n_rows
        o_ref[...] = part + jnp.where(valid, b_ref[...], 0.0)

    @pl.when(k > 0)
    def _():
        o_ref[...] += part


def _adaln_all(cond_t, n_rows, ada_w, ada_b):
    depth, d, n6 = ada_w.shape
    tk = _tile(d, 256)
    return pl.pallas_call(
        functools.partial(_adaln_kernel, n_rows=n_rows),
        grid=(depth, d // tk),
        in_specs=[pl.BlockSpec((tk, 8), lambda l, k: (k, 0)),
                  pl.BlockSpec((None, tk // 2, n6), lambda l, k: (l, 2 * k, 0)),
                  pl.BlockSpec((None, tk // 2, n6), lambda l, k: (l, 2 * k + 1, 0)),
                  pl.BlockSpec((None, 1, n6), lambda l, k: (l, 0, 0))],
        out_specs=pl.BlockSpec((None, 8, n6), lambda l, k: (l, 0, 0)),
        out_shape=jax.ShapeDtypeStruct((depth, 8, n6), F32),
        compiler_params=_params("parallel", "arbitrary"),
        name="adaln",
    )(cond_t, ada_w, ada_w, ada_b.reshape(depth, 1, n6))


def _norm_mod_rows(x, g, sh, sc):
    ms = jnp.mean(x * x, axis=-1, keepdims=True)
    y = x * lax.rsqrt(ms + NORM_EPS) * g
    return (y * (1.0 + sc) + sh).astype(BF16)


def _norm_mod(x_ref, g_ref, sh_ref, sc_ref):
    return _norm_mod_rows(x_ref[...], g_ref[...], sh_ref[...], sc_ref[...])


def _inproj_kernel(x_ref, g_ref, sh_ref, sc_ref, w_ref, cos_ref, sin_ref, o_ref, hs_ref, *,
                   n_q, n_qk, rope):
    j = pl.program_id(2)

    @pl.when(j == 0)
    def _():
        hs_ref[...] = _norm_mod(x_ref, g_ref, sh_ref, sc_ref)

    tm, tn = o_ref.shape
    rc = min(tm, 256)

    def by_row_chunks(epilogue):
        for r in range(tm // rc):
            rows = pl.ds(r * rc, rc)
            acc = jnp.dot(hs_ref[rows, :], w_ref[...], preferred_element_type=F32)
            o_ref[rows, :] = epilogue(acc, rows).astype(o_ref.dtype)

    def qk_epilogue(a, rows):
        if rope:
            lane = lax.broadcasted_iota(jnp.int32, a.shape, 1)
            first = (lane & 31) < 16
            partner = jnp.where(first, pltpu.roll(a, tn - 16, 1), pltpu.roll(a, 16, 1))
            reps = tn // LANES
            a = (a * jnp.tile(cos_ref[rows, :], (1, reps))
                 + partner * jnp.tile(sin_ref[rows, :], (1, reps)))
        return jnp.where(j < n_q, a * DIFF_SCALE, a)

    @pl.when(j < n_qk)
    def _():
        by_row_chunks(qk_epilogue)

    @pl.when(j >= n_qk)
    def _():
        by_row_chunks(lambda a, rows: jax.nn.gelu(a))


def _inproj(x, g, sh, sc, w, cos_t, sin_t, rope):
    b, s, d = x.shape
    n = w.shape[1]
    tm = _tile(s, 1024)
    tn = 512
    kern = functools.partial(_inproj_kernel, n_q=Q_COLS // tn, n_qk=2 * Q_COLS // tn, rope=rope)
    return pl.pallas_call(
        kern,
        grid=(b, s // tm, n // tn),
        in_specs=[pl.BlockSpec((None, tm, d), lambda bb, i, j: (bb, i, 0)),
                  pl.BlockSpec((1, d), lambda bb, i, j: (0, 0)),
                  pl.BlockSpec((None, 1, d), lambda bb, i, j: (bb, 0, 0)),
                  pl.BlockSpec((None, 1, d), lambda bb, i, j: (bb, 0, 0)),
                  pl.BlockSpec((d, tn), lambda bb, i, j: (0, j)),
                  pl.BlockSpec((tm, LANES), lambda bb, i, j: (i, 0)),
                  pl.BlockSpec((tm, LANES), lambda bb, i, j: (i, 0))],
        out_specs=[pl.BlockSpec((None, tm, tn), lambda bb, i, j: (bb, i, j)),
                   pl.BlockSpec((None, tm, d), lambda bb, i, j: (bb, i, 0))],
        out_shape=[jax.ShapeDtypeStruct((b, s, n), BF16), jax.ShapeDtypeStruct((b, s, d), BF16)],
        compiler_params=_params("parallel", "parallel", "arbitrary"),
        name="inproj",
    )(x, g, sh, sc, w, cos_t, sin_t)


def _inproj_nt_kernel(*refs, tl, conv, normalized):
    if normalized:
        hs_ref, wt_ref, o_ref = refs
    else:
        x_ref, g_ref, sh_ref, sc_ref, wt_ref = refs[:5]
        if conv:
            cw_ref, halo_ref, o_ref, hs_ref = refs[5:]
        else:
            o_ref, hs_ref = refs[5:]

        @pl.when(pl.program_id(2) == 0)
        def _():
            hs_ref[...] = _norm_mod(x_ref, g_ref, sh_ref, sc_ref)

    n_chunks = o_ref.shape[0]
    if not conv:
        acc = lax.dot_general(wt_ref[...], hs_ref[...], NT_DIMS, preferred_element_type=F32)
        for c in range(n_chunks):
            o_ref[c] = acc[:, c * tl:(c + 1) * tl].astype(o_ref.dtype)
        return

    acc = jnp.concatenate(
        [lax.dot_general(wt_ref[...], hs_ref[c * tl:(c + 1) * tl, :], NT_DIMS, preferred_element_type=F32)
         for c in range(n_chunks)], axis=1)
    tn, tm = acc.shape
    halo = halo_ref[...]
    col = lax.broadcasted_iota(jnp.int32, halo.shape, 1)
    tile = pl.program_id(1)
    before = jnp.sum(jnp.where(col == 2 * tile, halo, 0.0), axis=1, keepdims=True)
    after = jnp.sum(jnp.where(col == 2 * tile + 1, halo, 0.0), axis=1, keepdims=True)
    lane = lax.broadcasted_iota(jnp.int32, (tn, LANES), 1)
    prev = pltpu.roll(acc, 1, 1)
    prev = jnp.concatenate([jnp.where(lane == 0, before, prev[:, :LANES]), prev[:, LANES:]], axis=1)
    nxt = pltpu.roll(acc, tm - 1, 1)
    nxt = jnp.concatenate([nxt[:, :tm - LANES],
                           jnp.where(lane == LANES - 1, after, nxt[:, tm - LANES:])], axis=1)

    def tap(k):
        return jnp.tile(cw_ref[:, k * LANES:(k + 1) * LANES], (1, tm // LANES))

    out = tap(3) + prev * tap(0) + acc * tap(1) + nxt * tap(2)
    for c in range(n_chunks):
        o_ref[c] = out[:, c * tl:(c + 1) * tl].astype(o_ref.dtype)


def _inproj_nt(x, g, sh, sc, wt, tl, conv_params=None, halo=None):
    b, s, d = x.shape
    n = wt.shape[0]
    tm = _tile(s, 1024)
    tn = 512
    conv = conv_params is not None
    normalized = g is None
    assert not (normalized and conv)
    kern = functools.partial(_inproj_nt_kernel, tl=tl, conv=conv, normalized=normalized)
    row_spec = pl.BlockSpec((None, tm, d), lambda bb, i, j: (bb, i, 0))
    w_spec = pl.BlockSpec((tn, d), lambda bb, i, j: (j, 0))
    if normalized:
        in_specs, args, scratch = [row_spec, w_spec], [x, wt], []
    else:
        in_specs = [row_spec,
                    pl.BlockSpec((1, d), lambda bb, i, j: (0, 0)),
                    pl.BlockSpec((None, 1, d), lambda bb, i, j: (bb, 0, 0)),
                    pl.BlockSpec((None, 1, d), lambda bb, i, j: (bb, 0, 0)),
                    w_spec]
        args, scratch = [x, g, sh, sc, wt], [pltpu.VMEM((tm, d), BF16)]
    if conv:
        in_specs += [pl.BlockSpec((tn, 4 * LANES), lambda bb, i, j: (j, 0)),
                     pl.BlockSpec((None, tn, halo.shape[2]), lambda bb, i, j: (bb, j, 0))]
        args += [conv_params, halo]
    return pl.pallas_call(
        kern,
        grid=(b, s // tm, n // tn),
        in_specs=in_specs,
        out_specs=pl.BlockSpec((None, tm // tl, tn, tl), lambda bb, i, j: (bb, i, j, 0)),
        out_shape=jax.ShapeDtypeStruct((b, s // tl, n, tl), BF16),
        scratch_shapes=scratch,
        compiler_params=_params("parallel", "parallel", "arbitrary"),
        name="inproj_nt",
    )(*args)


def _ffn_up_kernel(hs_ref, wg_ref, wu_ref, o_ref):
    hs = hs_ref[...]
    gate = jnp.dot(hs, wg_ref[...], preferred_element_type=F32)
    up = jnp.dot(hs, wu_ref[...], preferred_element_type=F32)
    o_ref[...] = (gate * jax.nn.sigmoid(gate) * up).astype(o_ref.dtype)


def _ffn_up(hs, wg, wu):
    b, s, d = hs.shape
    n = wg.shape[1]
    tm = _tile(s, 1024)
    tn = 512
    return pl.pallas_call(
        _ffn_up_kernel,
        grid=(b, s // tm, n // tn),
        in_specs=[pl.BlockSpec((None, tm, d), lambda bb, i, j: (bb, i, 0)),
                  pl.BlockSpec((d, tn), lambda bb, i, j: (0, j)),
                  pl.BlockSpec((d, tn), lambda bb, i, j: (0, j))],
        out_specs=pl.BlockSpec((None, tm, tn), lambda bb, i, j: (bb, i, j)),
        out_shape=jax.ShapeDtypeStruct((b, s, n), BF16),
        compiler_params=_params("parallel", "parallel", "parallel"),
        name="ffn_up",
    )(hs, wg, wu)


def _proj_res_kernel(*refs, ksizes, transposed, next_norm):
    n = len(ksizes)
    a_refs = refs[:n]
    if next_norm:
        w_ref, x_ref, gate_ref, ng_ref, nsh_ref, nsc_ref, o_ref, hs_ref = refs[n:]
    else:
        w_ref, x_ref, gate_ref, o_ref = refs[n:]

    def emit(rows, acc):
        y = x_ref[rows, :] + gate_ref[...] * acc
        o_ref[rows, :] = y
        if next_norm:
            hs_ref[rows, :] = _norm_mod_rows(y, ng_ref[...], nsh_ref[...], nsc_ref[...])

    if transposed:
        (a_ref,) = a_refs
        tl = a_ref.shape[2]
        for c in range(a_ref.shape[0]):
            emit(slice(c * tl, (c + 1) * tl),
                 lax.dot_general(a_ref[c], w_ref[...], TN_DIMS, preferred_element_type=F32))
        return
    acc = None
    off = 0
    for a_ref, ks in zip(a_refs, ksizes):
        part = jnp.dot(a_ref[...], w_ref[off:off + ks, :], preferred_element_type=F32)
        acc = part if acc is None else acc + part
        off += ks
    emit(slice(None), acc)


def _proj_res(a_list, w, x, gate, transposed=False, tm_pref=512, tn_pref=512, vmem=VMEM_LIMIT, next_norm=None):
    b, s, d = x.shape
    ksizes = tuple(a.shape[2] for a in a_list)
    ktot = sum(ksizes)
    tm = _tile(s, tm_pref)
    tn = _tile(d, tn_pref)
    if transposed:
        tl = a_list[0].shape[3]
        a_specs = [pl.BlockSpec((None, tm // tl, ktot, tl), lambda bb, i, j: (bb, i, 0, 0))]
    else:
        a_specs = [pl.BlockSpec((None, tm, ks), lambda bb, i, j: (bb, i, 0)) for ks in ksizes]
    in_specs = a_specs + [pl.BlockSpec((ktot, tn), lambda bb, i, j: (0, j)),
                          pl.BlockSpec((None, tm, tn), lambda bb, i, j: (bb, i, j)),
                          pl.BlockSpec((None, 1, tn), lambda bb, i, j: (bb, 0, j))]
    args = [*a_list, w, x, gate]
    out_spec = pl.BlockSpec((None, tm, tn), lambda bb, i, j: (bb, i, j))
    out_specs, out_shape = out_spec, jax.ShapeDtypeStruct((b, s, d), F32)
    if next_norm is not None:
        assert tn == d
        vmem = VMEM_LIMIT_WIDE
        in_specs += [pl.BlockSpec((1, d), lambda bb, i, j: (0, 0)),
                     pl.BlockSpec((None, 1, d), lambda bb, i, j: (bb, 0, 0)),
                     pl.BlockSpec((None, 1, d), lambda bb, i, j: (bb, 0, 0))]
        args += list(next_norm)
        out_specs, out_shape = [out_spec, out_spec], [out_shape, jax.ShapeDtypeStruct((b, s, d), BF16)]
    kern = functools.partial(_proj_res_kernel, ksizes=ksizes, transposed=transposed,
                             next_norm=next_norm is not None)
    return pl.pallas_call(
        kern,
        grid=(b, s // tm, d // tn),
        in_specs=in_specs,
        out_specs=out_specs,
        out_shape=out_shape,
        compiler_params=_params("parallel", "parallel", "parallel", vmem=vmem),
        name="proj_res",
    )(*args)


def _attn_kernel(*refs, n_chunks, lam_init):
    if n_chunks:
        lam_ref, q_ref, kc_ref, vct_ref, k_ref, vt_ref, g_ref, o_ref, acc_ref = refs[:9]
        n_slots = (len(refs) - 9) // 2
        slots = tuple(zip(refs[9:9 + n_slots], refs[9 + n_slots:]))
    else:
        lam_ref, q_ref, kc_ref, vct_ref, g_ref, o_ref, acc_ref = refs
    n_sub = acc_ref.shape[0]
    tq = acc_ref.shape[-1]
    dv = DIFF_V_DIM
    q = q_ref[...]
    qm = [(q[s * tq:(s + 1) * tq, :DIFF_QK_DIM], q[s * tq:(s + 1) * tq, DIFF_QK_DIM:]) for s in range(n_sub)]

    def scores(kblk, sub):
        s_pair = tuple(lax.dot_general(kblk[:, m * DIFF_QK_DIM:(m + 1) * DIFF_QK_DIM], qm[sub][m], NT_DIMS,
                                       preferred_element_type=F32) for m in range(2))
        return s_pair, tuple(jnp.max(s, axis=0, keepdims=True) for s in s_pair)

    def with_ones(vtblk):
        return jnp.concatenate([vtblk, jnp.ones((ONES_ROWS, vtblk.shape[1]), BF16)], axis=0)

    def absorb(scored, vext, m_pair, sub):
        s_pair, smax_pair = scored
        out = []
        for m in range(2):
            m_old = m_pair[m]
            m_new = jnp.maximum(m_old, smax_pair[m])
            alpha = jnp.exp(m_old - m_new)
            p = jnp.exp((s_pair[m] - m_new).astype(BF16))
            pv = jnp.dot(vext, p, preferred_element_type=F32)
            acc_ref[sub, m] = alpha * acc_ref[sub, m] + pv
            out.append(m_new)
        return tuple(out)

    def store(slot, scored):
        for m in range(2):
            slot[0][m] = scored[0][m]
            slot[1][m] = scored[1][m]

    def load(slot):
        return (slot[0][0], slot[0][1]), (slot[1][0], slot[1][1])

    acc_ref[...] = jnp.zeros_like(acc_ref)
    init = jnp.full((1, tq), -1e30, F32)
    kc, vc = kc_ref[...], with_ones(vct_ref[0])
    m_state = tuple(absorb(scores(kc, s), vc, (init, init), s) for s in range(n_sub))
    if n_chunks:
        group = len(slots) // (2 * n_sub)
        assert n_chunks % (2 * group) == 0

        def slot(half, k, sub):
            return slots[(half * group + k) * n_sub + sub]

        def half_trip(half, base, ms, lookahead):
            ms = list(ms)
            for k in range(group):
                vext = with_ones(vt_ref[base + k])
                for s in range(n_sub):
                    if lookahead:
                        store(slot(1 - half, k, s), scores(k_ref[base + group + k], s))
                    ms[s] = absorb(load(slot(half, k, s)), vext, ms[s], s)
            return tuple(ms)

        def trip(j, ms, lookahead):
            base = 2 * group * j
            ms = half_trip(0, base, ms, True)
            return half_trip(1, base + group, ms, lookahead)

        for k in range(group):
            for s in range(n_sub):
                store(slot(0, k, s), scores(k_ref[k], s))
        n_trips = n_chunks // (2 * group)
        unroll = next(u for u in (5, 4, 3, 2, 1) if (n_trips - 1) % u == 0)
        m_state = lax.fori_loop(0, n_trips - 1, lambda j, ms: trip(j, ms, True), m_state, unroll=unroll)
        m_state = trip(n_trips - 1, m_state, False)

    lp = lam_ref[...]
    lam = (jnp.exp(jnp.sum(lp[0:1] * lp[1:2], axis=-1, keepdims=True))
           - jnp.exp(jnp.sum(lp[2:3] * lp[3:4], axis=-1, keepdims=True)) + lam_init)
    for s in range(n_sub):
        acc0, acc1 = acc_ref[s, 0], acc_ref[s, 1]
        o = acc0[:dv] / acc0[dv:dv + 1] - lam * (acc1[:dv] / acc1[dv:dv + 1])
        ot = o.T
        ms = jnp.mean(ot * ot, axis=-1, keepdims=True)
        on = ot * lax.rsqrt(ms + SUBLN_EPS) * g_ref[...] * (1.0 - lam_init)
        o_ref[s * tq:(s + 1) * tq, :] = on.astype(o_ref.dtype)


def _attention(lam_p, subln, q_arr, kc_arr, vct_arr, k4=None, vt4=None, *, lam_init):
    b, sq = q_arr.shape[0], q_arr.shape[1]
    sc = kc_arr.shape[1]
    h = DIFF_HEADS
    dv = DIFF_V_DIM
    n_sub = next(n for n in (8, 4, 2, 1) if sq % (n * ATTN_SUB) == 0)
    tq = n_sub * ATTN_SUB
    n_chunks = 0 if k4 is None else k4.shape[1]
    in_specs = [pl.BlockSpec((4, DIFF_QK_DIM), lambda bb, hh, i: (0, 0)),
                pl.BlockSpec((None, tq, dv), lambda bb, hh, i: (bb, i, hh)),
                pl.BlockSpec((None, sc, dv), lambda bb, hh, i: (bb, 0, h + hh)),
                pl.BlockSpec((None, 1, dv, sc), lambda bb, hh, i: (bb, 0, hh, 0))]
    args = [lam_p, q_arr, kc_arr, vct_arr]
    if n_chunks:
        tk = k4.shape[2]
        in_specs += [pl.BlockSpec((None, n_chunks, tk, dv), lambda bb, hh, i: (bb, 0, 0, h + hh)),
                     pl.BlockSpec((None, n_chunks, dv, tk), lambda bb, hh, i: (bb, 0, hh, 0))]
        args += [k4, vt4]
    in_specs.append(pl.BlockSpec((1, dv), lambda bb, hh, i: (0, 0)))
    args.append(subln)
    kern = functools.partial(_attn_kernel, n_chunks=n_chunks, lam_init=lam_init)
    scratch = [pltpu.VMEM((n_sub, 2, dv + ONES_ROWS, ATTN_SUB), F32)]
    if n_chunks:
        n_slots = 2 * n_sub
        scratch += [pltpu.VMEM((2, tk, ATTN_SUB), F32) for _ in range(n_slots)]
        scratch += [pltpu.VMEM((2, 1, ATTN_SUB), F32) for _ in range(n_slots)]
    return pl.pallas_call(
        kern,
        grid=(b, h, sq // tq),
        in_specs=in_specs,
        out_specs=pl.BlockSpec((None, tq, dv), lambda bb, hh, i: (bb, i, hh)),
        out_shape=jax.ShapeDtypeStruct((b, sq, A_WIDTH), BF16),
        scratch_shapes=scratch,
        compiler_params=_params("parallel", "parallel", "parallel"),
        name="diff_attn",
    )(*args)


def _sgu_kernel(u_ref, g_ref, ng_ref, nb_ref, w_ref, bs_ref, o_ref):
    for gi in range(SGU_GROUPS):
        cols = slice(gi * SGU_CH, (gi + 1) * SGU_CH)
        w = w_ref[gi]
        for c in range(u_ref.shape[0] // SGU_CHUNK):
            sl = slice(c * SGU_CHUNK, (c + 1) * SGU_CHUNK)
            gg = g_ref[sl, cols].astype(F32)
            mu = jnp.mean(gg, axis=-1, keepdims=True)
            dev = gg - mu
            var = jnp.mean(dev * dev, axis=-1, keepdims=True)
            vv = dev * lax.rsqrt(var + LN_EPS) * ng_ref[gi] + nb_ref[gi]
            mixed = jnp.dot(w, vv.astype(BF16), preferred_element_type=F32) + bs_ref[gi]
            o_ref[sl, cols] = (u_ref[sl, cols].astype(F32) * mixed).astype(o_ref.dtype)


def _sgu(qkug, norm_g, norm_b, w_s, b_s):
    b, s = qkug.shape[0], qkug.shape[1]
    tm = _tile(s, 1024)
    gcount = SGU_GROUPS
    ublk = 2 * Q_COLS // B_WIDTH
    full = lambda shape: pl.BlockSpec(shape, lambda bb, i: (0,) * len(shape))
    return pl.pallas_call(
        _sgu_kernel,
        grid=(b, s // tm),
        in_specs=[pl.BlockSpec((None, tm, B_WIDTH), lambda bb, i: (bb, i, ublk)),
                  pl.BlockSpec((None, tm, B_WIDTH), lambda bb, i: (bb, i, ublk + 1)),
                  full((gcount, 1, SGU_CH)), full((gcount, 1, SGU_CH)),
                  full((gcount, SGU_CHUNK, SGU_CHUNK)), full((gcount, SGU_CHUNK, 1))],
        out_specs=pl.BlockSpec((None, tm, B_WIDTH), lambda bb, i: (bb, i, 0)),
        out_shape=jax.ShapeDtypeStruct((b, s, B_WIDTH), BF16),
        compiler_params=_params("parallel", "parallel"),
        name="sgu",
    )(qkug, qkug, norm_g.reshape(gcount, 1, SGU_CH), norm_b.reshape(gcount, 1, SGU_CH),
      w_s.astype(BF16), b_s.reshape(gcount, SGU_CHUNK, 1))


def _filter_mlp_kernel(f_ref, w1_ref, b1_ref, w2_ref, b2_ref, w3_ref, b3_ref, fr_ref, o_ref):
    def lin(a, w_ref, b_ref):
        return jnp.dot(a, w_ref[...], preferred_element_type=F32, precision=HIGHEST) + b_ref[...]
    fr = fr_ref[...]
    hcur = jnp.sin(fr[0:1] * lin(f_ref[...], w1_ref, b1_ref))
    hcur = jnp.sin(fr[1:2] * lin(hcur, w2_ref, b2_ref))
    o_ref[...] = jnp.sin(fr[2:3] * lin(hcur, w3_ref, b3_ref))


def _filter_mlp(feats, w1, b1, w2, b2, w3, b3, freq):
    rows, emb = feats.shape
    hid = FILTER_HIDDEN
    tr = _tile(rows, 2048)
    full = lambda shape: pl.BlockSpec(shape, lambda i: (0,) * len(shape))
    return pl.pallas_call(
        _filter_mlp_kernel,
        grid=(rows // tr,),
        in_specs=[pl.BlockSpec((tr, emb), lambda i: (i, 0)),
                  full((emb, hid)), full((1, hid)), full((hid, hid)), full((1, hid)),
                  full((hid, hid)), full((1, hid)), full((3, hid))],
        out_specs=pl.BlockSpec((tr, hid), lambda i: (i, 0)),
        out_shape=jax.ShapeDtypeStruct((rows, hid), F32),
        compiler_params=_params("parallel"),
        name="filter_mlp",
    )(feats, w1, b1.reshape(1, hid), w2, b2.reshape(1, hid), w3, b3.reshape(1, hid), freq)


def _filter_raw_kernel(w4t_ref, h_ref, t_ref, delta_ref, o_ref, *, zero_tile):
    rt = pl.program_id(2)

    def split(a):
        hi = a.astype(BF16)
        return hi, (a - hi.astype(F32)).astype(BF16)

    def nt(a, bm):
        return lax.dot_general(a, bm, NT_DIMS, preferred_element_type=F32)

    w_hi, w_lo = split(w4t_ref[...])
    h_hi, h_lo = split(h_ref[...])
    raw = nt(w_hi, h_hi) + (nt(w_hi, h_lo) + nt(w_lo, h_hi))
    raw = raw * jnp.exp(-(delta_ref[...] * t_ref[...]))
    for c in range(o_ref.shape[0]):
        o_ref[c] = raw[:, c * DFT_MINOR:(c + 1) * DFT_MINOR].astype(o_ref.dtype)

    @pl.when(rt == zero_tile)
    def _():
        col = lax.broadcasted_iota(jnp.int32, (raw.shape[0], DFT_MINOR), 1)
        o_ref[0] = jnp.where(col == 0, 0.0, raw[:, :DFT_MINOR]).astype(o_ref.dtype)


def _filter_raw(w4t, h3, t_row, deltas, seq):
    c = w4t.shape[2]
    rows = h3.shape[0]
    tr = _tile(rows // 2, 2048)
    tc = _tile(c, 512)
    half_tiles = seq // tr
    kern = functools.partial(_filter_raw_kernel, zero_tile=half_tiles)
    return pl.pallas_call(
        kern,
        grid=(HYENA_ORDER, c // tc, rows // tr),
        in_specs=[pl.BlockSpec((None, None, tc, FILTER_HIDDEN),
                               lambda n, ci, rt: (n, rt // half_tiles, ci, 0)),
                  pl.BlockSpec((tr, FILTER_HIDDEN), lambda n, ci, rt: (rt, 0)),
                  pl.BlockSpec((1, tr), lambda n, ci, rt: (0, rt)),
                  pl.BlockSpec((tc, 1), lambda n, ci, rt: (ci, 0))],
        out_specs=pl.BlockSpec((None, tr // DFT_MINOR, tc, DFT_MINOR), lambda n, ci, rt: (n, rt, ci, 0)),
        out_shape=jax.ShapeDtypeStruct((HYENA_ORDER, rows // DFT_MINOR, c, DFT_MINOR), BF16),
        compiler_params=_params("parallel", "parallel", "parallel"),
        name="filter_raw",
    )(w4t, h3, t_row, deltas)


def _cmul(ar, ai, br, bi):
    return ar * br - ai * bi, ar * bi + ai * br


def _store_complex(ref, c, n1, re, im):
    r0 = pl.multiple_of(c * n1, n1)
    ref[pl.ds(r0, n1), :DFT_MINOR] = re.astype(ref.dtype)
    ref[pl.ds(r0, n1), DFT_MINOR:] = im.astype(ref.dtype)


def _load_complex(ref, c, n1):
    r0 = pl.multiple_of(c * n1, n1)
    tile = ref[pl.ds(r0, n1), :]
    return tile[:, :DFT_MINOR], tile[:, DFT_MINOR:]


def _filter_spec_kernel(k_ref, f1_ref, twr_ref, twi_ref, f2_ref, o_ref, a2_ref, *, inv_n):
    tc, n1, n2 = k_ref.shape

    def left(c, carry):
        k = k_ref[c].astype(F32)
        nrm = jnp.sum(jnp.sum(jnp.abs(k), axis=1, keepdims=True), axis=0, keepdims=True)
        kn = (k * (inv_n / nrm)).astype(BF16)
        a = jnp.dot(f1_ref[...], kn, preferred_element_type=F32)
        ar, ai = _cmul(a[:n1], a[n1:], twr_ref[...], twi_ref[...])
        _store_complex(a2_ref, c, n1, ar, ai)
        return carry

    lax.fori_loop(0, tc, left, 0, unroll=True)
    z = jnp.dot(a2_ref[...], f2_ref[...], preferred_element_type=F32).reshape(tc, n1, 2 * n2)
    o_ref[:, 0] = z[:, :, :n2].astype(o_ref.dtype)
    o_ref[:, 1] = z[:, :, n2:].astype(o_ref.dtype)


def _filter_spec(k2, f1_full, twr, twi, f2e):
    nc, n1, n2 = k2.shape
    tc = 16
    kern = functools.partial(_filter_spec_kernel, inv_n=1.0 / (n1 * n2))
    full = lambda shape: pl.BlockSpec(shape, lambda i: (0,) * len(shape))
    return pl.pallas_call(
        kern,
        grid=(nc // tc,),
        in_specs=[pl.BlockSpec((tc, n1, n2), lambda i: (i, 0, 0)),
                  full((2 * n1, n1)), full((n1, n2)), full((n1, n2)), full((2 * n2, 2 * n2))],
        out_specs=pl.BlockSpec((tc, 2, n1, n2), lambda i: (i, 0, 0, 0)),
        out_shape=jax.ShapeDtypeStruct((nc, 2, n1, n2), BF16),
        scratch_shapes=[pltpu.VMEM((tc * n1, 2 * n2), BF16)],
        compiler_params=_params("parallel"),
        name="filter_spec",
    )(k2, f1_full, twr, twi, f2e)


def _hyena_kernel(zv_ref, z1_ref, z2_ref, kf_ref, fb_ref, f1_ref, twr_ref, twi_ref, f2_ref, f2c_ref,
                  g1_ref, o_ref, sig_ref, a2_ref, z_ref):
    nb, hr, tc, n2 = zv_ref.shape
    n1 = 2 * hr
    for part, ref in enumerate((zv_ref, z1_ref, z2_ref)):
        for bb in range(nb):
            sig_ref[part, bb] = pltpu.einshape("tcl->ctl", ref[bb].astype(F32))

    for n in range(HYENA_ORDER):
        def fwd_left(c, carry):
            xs = jnp.concatenate([sig_ref[0, 0, c], sig_ref[0, 1, c]], axis=0).astype(BF16)
            a = jnp.dot(f1_ref[...], xs, preferred_element_type=F32)
            ar, ai = _cmul(a[:n1], a[n1:], twr_ref[...], twi_ref[...])
            _store_complex(a2_ref, c, n1, ar, ai)
            return carry

        lax.fori_loop(0, tc, fwd_left, 0, unroll=True)
        z_ref[...] = jnp.dot(a2_ref[...], f2_ref[...], preferred_element_type=F32)

        def spectrum(c, carry):
            zr, zi = _load_complex(z_ref, c, n1)
            wr, wi = _cmul(zr, zi, kf_ref[n, c, 0].astype(F32), kf_ref[n, c, 1].astype(F32))
            _store_complex(a2_ref, c, n1, wr, wi)
            return carry

        lax.fori_loop(0, tc, spectrum, 0, unroll=True)
        z_ref[...] = jnp.dot(a2_ref[...], f2c_ref[...], preferred_element_type=F32)

        def inv_left(c, carry):
            br, bi = _load_complex(z_ref, c, n1)
            br, bi = _cmul(br, bi, twr_ref[...], -twi_ref[...])
            bs = jnp.concatenate([br, bi], axis=0).astype(BF16)
            y = jnp.dot(g1_ref[...], bs, preferred_element_type=F32)
            fb = fb_ref[pl.ds(c, 1), n:n + 1]
            for bb in range(nb):
                sig_ref[0, bb, c] = sig_ref[1 + n, bb, c] * (y[bb * hr:(bb + 1) * hr] + sig_ref[0, bb, c] * fb)
            return carry

        lax.fori_loop(0, tc, inv_left, 0, unroll=True)

    for bb in range(nb):
        o_ref[bb] = pltpu.einshape("ctl->tcl", sig_ref[0, bb]).astype(o_ref.dtype)


def _hyena_core(z4, kf, fbias, f1h, twr, twi, f2e, f2c, g1e):
    nb, hr, c3, n2 = z4.shape
    c = c3 // 3
    n1 = 2 * hr
    tc = 16
    nct = c // tc
    full = lambda shape: pl.BlockSpec(shape, lambda i: (0,) * len(shape))
    zspec = lambda part: pl.BlockSpec((nb, hr, tc, n2), lambda i: (0, 0, i + part * nct, 0))
    return pl.pallas_call(
        _hyena_kernel,
        grid=(nct,),
        in_specs=[zspec(0), zspec(1), zspec(2),
                  pl.BlockSpec((HYENA_ORDER, tc, 2, n1, n2), lambda i: (0, i, 0, 0, 0)),
                  pl.BlockSpec((tc, HYENA_ORDER), lambda i: (i, 0)),
                  full((2 * n1, n1)), full((n1, n2)), full((n1, n2)),
                  full((2 * n2, 2 * n2)), full((2 * n2, 2 * n2)), full((n1, 2 * n1))],
        out_specs=pl.BlockSpec((nb, hr, tc, n2), lambda i: (0, 0, i, 0)),
        out_shape=jax.ShapeDtypeStruct((nb, hr, c, n2), BF16),
        scratch_shapes=[pltpu.VMEM((3, nb, tc, hr, n2), F32),
                        pltpu.VMEM((tc * n1, 2 * n2), BF16),
                        pltpu.VMEM((tc * n1, 2 * n2), F32)],
        compiler_params=_params("parallel"),
        name="hyena_core",
    )(z4, z4, z4, kf, fbias, f1h, twr, twi, f2e, f2c, g1e)


def _dft_tables(n1):
    n2 = DFT_MINOR
    n = n1 * n2
    a1 = 2.0 * np.pi * np.outer(np.arange(n1), np.arange(n1)) / n1
    f1r, f1i = np.cos(a1), -np.sin(a1)
    a2 = 2.0 * np.pi * np.outer(np.arange(n2), np.arange(n2)) / n2
    f2r, f2i = np.cos(a2), -np.sin(a2)
    at = 2.0 * np.pi * np.outer(np.arange(n1), np.arange(n2)) / n
    twr, twi = np.cos(at), -np.sin(at)
    hr = n1 // 2
    f1_full = np.concatenate([f1r, f1i], axis=0)
    f1_half = np.block([[f1r[:, :hr], -f1i[:, :hr]], [f1i[:, :hr], f1r[:, :hr]]])
    f2e = np.block([[f2r, f2i], [-f2i, f2r]])
    f2c = np.block([[f2r, -f2i], [f2i, f2r]])
    gr, gi = f1r[:hr, :], -f1i[:hr, :]
    g1e = np.block([[gr, -gi], [gi, gr]])
    bf = lambda m: jnp.asarray(m, F32).astype(BF16)
    return dict(f1_full=bf(f1_full), f1_half=bf(f1_half), f2e=bf(f2e), f2c=bf(f2c), g1e=bf(g1e),
                twr=jnp.asarray(twr, F32), twi=jnp.asarray(twi, F32))


def _hyena_spectra(seq, fw1, fb1, fw2, fb2, fw3, fb3, ffreq, fw4, tabs):
    l = seq
    width = fw4.shape[1] // (2 * HYENA_ORDER)
    t_fwd = jnp.linspace(0.0, 1.0, l, dtype=F32)
    w = 2.0 * math.pi * jnp.arange(l, dtype=F32)[:, None] / l
    f = jnp.linspace(1e-4, FILTER_BANDS - 1, FILTER_BANDS, dtype=F32)[None, :]
    feats = jnp.concatenate([t_fwd[:, None], jnp.cos(f * w), -jnp.sin(f * w)], axis=-1)

    def two_sided(a):
        return jnp.concatenate([a, a[:1], a[1:][::-1]], axis=0)

    t_lin = two_sided(t_fwd)
    emb_pad = FILTER_HIDDEN
    feats = jnp.pad(feats, ((0, 0), (0, emb_pad - FILTER_EMB)))
    w1p = jnp.pad(fw1, ((0, emb_pad - FILTER_EMB), (0, 0)))
    h3 = two_sided(_filter_mlp(feats, w1p, fb1, fw2, fb2, fw3, fb3, ffreq))
    w4t = fw4.reshape(FILTER_HIDDEN, HYENA_ORDER, 2, width).transpose(1, 2, 3, 0)
    deltas = jnp.abs(jnp.linspace(MIN_DECAY, MAX_DECAY, width, dtype=F32))[:, None]
    raw = _filter_raw(w4t, h3, t_lin[None, :], deltas, l)
    n1 = raw.shape[1]
    k2 = raw.transpose(0, 2, 1, 3).reshape(HYENA_ORDER * width, n1, DFT_MINOR)
    kf = _filter_spec(k2, tabs["f1_full"], tabs["twr"], tabs["twi"], tabs["f2e"])
    return kf.reshape(HYENA_ORDER, width, 2, n1, DFT_MINOR)


def _rope_tables(n_tokens):
    tok = jnp.arange(n_tokens)
    row = (tok // GRID_W).astype(F32)
    col = (tok % GRID_W).astype(F32)
    half = ROPE_AXIS_DIM // 2
    inv = 1.0 / (ROPE_THETA ** (jnp.arange(0, ROPE_AXIS_DIM, 2, dtype=F32) / ROPE_AXIS_DIM))
    ang_r = row[:, None] * inv
    ang_c = col[:, None] * inv
    cos64 = jnp.concatenate([jnp.cos(ang_r), jnp.cos(ang_r), jnp.cos(ang_c), jnp.cos(ang_c)], axis=-1)
    sin64 = jnp.concatenate([-jnp.sin(ang_r), jnp.sin(ang_r), -jnp.sin(ang_c), jnp.sin(ang_c)], axis=-1)
    assert cos64.shape[1] == 4 * half == DIFF_QK_DIM
    return jnp.tile(cos64, (1, 2)), jnp.tile(sin64, (1, 2))


def _ffn(x, hs, gate, wg, wu, wd):
    return _proj_res([_ffn_up(hs, wg, wu)], wd, x, gate, tm_pref=1024, vmem=VMEM_LIMIT_WIDE)


def _even_layer(x, xc, mods, cmods, n1g, n2g, w_in, w_out, lam_p, subln, sgu_ng, sgu_nb, sgu_w, sgu_b,
                wg, wu, wd, layer_idx):
    b, s, d = x.shape
    lam_init = 0.8 - 0.6 * math.exp(-0.3 * layer_idx)
    sh1, sc1, g1, sh2, sc2, g2 = mods
    csh1, csc1, cg1, csh2, csc2, cg2 = cmods
    o_k, o_v, o_u = Q_COLS, 2 * Q_COLS, 2 * Q_COLS + A_WIDTH
    w_qkug = jnp.concatenate([w_in[:, :o_v], w_in[:, o_u:]], axis=1).astype(BF16)
    w_vt = w_in[:, o_v:o_u].T.astype(BF16)
    wo = w_out.astype(BF16)
    n1g2 = n1g.reshape(1, d)
    cos_t, sin_t = _rope_tables(s)
    sc_len = xc.shape[1]

    qkug, hs1 = _inproj(x, n1g2, sh1, sc1, w_qkug, cos_t, sin_t, rope=True)
    cqkug, chs1 = _inproj(xc, n1g2, csh1, csc1, w_qkug, cos_t[:sc_len], sin_t[:sc_len], rope=False)
    tk = _tile(s, 512)
    vt4 = _inproj_nt(hs1, None, None, None, w_vt, tk)
    cvt4 = _inproj_nt(chs1, None, None, None, w_vt, sc_len)
    k4 = qkug.reshape(b, s // tk, tk, qkug.shape[2])
    subln2 = subln.reshape(1, DIFF_V_DIM)

    a_l = _attention(lam_p, subln2, qkug, cqkug, cvt4, k4, vt4, lam_init=lam_init)
    s_l = _sgu(qkug, sgu_ng, sgu_nb, sgu_w, sgu_b)
    n2g2 = n2g.reshape(1, d)
    x, hs = _proj_res([a_l, s_l], wo, x, g1, tn_pref=d, next_norm=(n2g2, sh2, sc2))
    x = _ffn(x, hs, g2, wg, wu, wd)

    a_c = _attention(lam_p, subln2, cqkug, cqkug, cvt4, lam_init=lam_init)
    s_c = _sgu(cqkug, sgu_ng, sgu_nb, sgu_w, sgu_b)
    xc, hs_c = _proj_res([a_c, s_c], wo, xc, cg1, tn_pref=d, next_norm=(n2g2, csh2, csc2))
    xc = _ffn(xc, hs_c, cg2, wg, wu, wd)
    return x, xc


def _odd_layer(x, mods, n1g, n2g, w_in, conv_w, conv_b, fw1, fb1, fw2, fb2, fw3, fb3, ffreq, fw4, fbias,
               w_out, wg, wu, wd):
    b, s, d = x.shape
    sh1, sc1, g1, sh2, sc2, g2 = mods
    width = w_out.shape[0]
    n1 = 2 * s // DFT_MINOR
    tabs = _dft_tables(n1)
    kf = _hyena_spectra(s, fw1, fb1, fw2, fb2, fw3, fb3, ffreq, fw4, tabs)

    n1g2 = n1g.reshape(1, d)
    wt = w_in.T.astype(BF16)
    tm = _tile(s, 1024)
    n_tiles = s // tm
    x_tiles = x.reshape(b, n_tiles, tm, d)
    firsts, lasts = x_tiles[:, :, 0, :], x_tiles[:, :, tm - 1, :]
    before = jnp.concatenate([lasts[:, :1], lasts[:, :-1]], axis=1)
    after = jnp.concatenate([firsts[:, 1:], firsts[:, -1:]], axis=1)
    x_edge = jnp.stack([before, after], axis=2).reshape(b, 2 * n_tiles, d)
    z_edge = _inproj_nt(x_edge, n1g2, sh1, sc1, wt, 2 * n_tiles)
    inside = jnp.ones((2 * n_tiles,), F32).at[0].set(0.0).at[2 * n_tiles - 1].set(0.0)
    halo = z_edge[:, 0].astype(F32) * inside
    conv_params = jnp.repeat(jnp.concatenate([conv_w.T, conv_b[:, None]], axis=1), LANES, axis=1)

    z4 = _inproj_nt(x, n1g2, sh1, sc1, wt, DFT_MINOR, conv_params, halo)
    y4 = _hyena_core(z4, kf, fbias.T, tabs["f1_half"], tabs["twr"], tabs["twi"], tabs["f2e"], tabs["f2c"],
                     tabs["g1e"])
    x, hs = _proj_res([y4], w_out.astype(BF16), x, g1, transposed=True, tn_pref=d,
                      next_norm=(n2g.reshape(1, d), sh2, sc2))
    return _ffn(x, hs, g2, wg, wu, wd)


def _rms_kernel(x_ref, g_ref, o_ref):
    x = x_ref[...]
    ms = jnp.mean(x * x, axis=-1, keepdims=True)
    o_ref[...] = x * lax.rsqrt(ms + NORM_EPS) * g_ref[...]


def _final_norm(x, g):
    b, s, d = x.shape
    tm = _tile(s, 1024)
    return pl.pallas_call(
        _rms_kernel,
        grid=(b, s // tm),
        in_specs=[pl.BlockSpec((None, tm, d), lambda bb, i: (bb, i, 0)),
                  pl.BlockSpec((1, d), lambda bb, i: (0, 0))],
        out_specs=pl.BlockSpec((None, tm, d), lambda bb, i: (bb, i, 0)),
        out_shape=jax.ShapeDtypeStruct((b, s, d), F32),
        compiler_params=_params("parallel", "parallel"),
        name="final_norm",
    )(x, g.reshape(1, d))


def kernel(x, c, ctx, c_ctx, ada_w, ada_b, norm1, norm2, ffn_w_gate, ffn_w_up, ffn_w_down, e_w_in, e_w_out, e_lambda, e_subln, e_sgu_norm_g, e_sgu_norm_b, e_sgu_w, e_sgu_b, o_w_in, o_conv_w, o_conv_b, o_filt_w1, o_filt_b1, o_filt_w2, o_filt_b2, o_filt_w3, o_filt_b3, o_filt_freq, o_filt_w4, o_filt_bias, o_w_out, final_norm):
    b, s, d = x.shape
    depth = ada_w.shape[0]
    assert b == 2, "the long convolution packs exactly two batches into one complex signal"
    assert depth == 2, "odd layers here never carry the context stream"
    cond_t = jnp.zeros((d, 8), F32).at[:, :b].set(c.T).at[:, b].set(c_ctx)
    mod_all = _adaln_all(cond_t, b + 1, ada_w, ada_b)
    xc = ctx
    for i in range(depth):
        j = i // 2
        parts = jnp.split(mod_all[i], 6, axis=-1)
        mods = [p[:b, None, :] for p in parts]
        cmods = [jnp.broadcast_to(p[b:b + 1, None, :], (b, 1, d)) for p in parts]
        wg, wu, wd = (ffn_w_gate[i].astype(BF16), ffn_w_up[i].astype(BF16), ffn_w_down[i].astype(BF16))
        if i % 2 == 0:
            x, xc = _even_layer(x, xc, mods, cmods, norm1[i], norm2[i], e_w_in[j], e_w_out[j], e_lambda[j],
                                e_subln[j], e_sgu_norm_g[j], e_sgu_norm_b[j], e_sgu_w[j], e_sgu_b[j],
                                wg, wu, wd, i)
        else:
            x = _odd_layer(x, mods, norm1[i], norm2[i], o_w_in[j], o_conv_w[j], o_conv_b[j], o_filt_w1[j],
                           o_filt_b1[j], o_filt_w2[j], o_filt_b2[j], o_filt_w3[j], o_filt_b3[j],
                           o_filt_freq[j], o_filt_w4[j], o_filt_bias[j], o_w_out[j], wg, wu, wd)
    return _final_norm(x, final_norm)
```

```python
import functools
import math

import numpy as np
import jax
import jax.numpy as jnp
from jax import lax
from jax.experimental import pallas as pl
from jax.experimental.pallas import tpu as pltpu

F32 = jnp.float32
BF16 = jnp.bfloat16
HIGHEST = lax.Precision.HIGHEST

GRID_W = 64
NORM_EPS = 1e-6
DIFF_HEADS = 8
DIFF_QK_DIM = 64
DIFF_V_DIM = 2 * DIFF_QK_DIM
DIFF_SCALE = DIFF_QK_DIM ** -0.5
A_WIDTH = DIFF_HEADS * DIFF_V_DIM
Q_COLS = DIFF_HEADS * 2 * DIFF_QK_DIM
ROPE_THETA = 10000.0
ROPE_AXIS_DIM = DIFF_QK_DIM // 2
SUBLN_EPS = 1e-5
SGU_GROUPS = 8
SGU_CHUNK = 128
SGU_CH = 128
B_WIDTH = SGU_GROUPS * SGU_CH
LN_EPS = 1e-5
HYENA_ORDER = 2
SHORT_CONV = 3
FILTER_EMB = 33
FILTER_BANDS = (FILTER_EMB - 1) // 2
FILTER_HIDDEN = 64
DECAY_TARGET = 1e-2
MAX_DECAY = math.log(DECAY_TARGET) / 0.3
MIN_DECAY = math.log(DECAY_TARGET) / 1.5

LANES = 128
ONES_ROWS = 16
ATTN_SUB = 256
DFT_MINOR = 256
VMEM_LIMIT = 48 * 1024 * 1024
VMEM_LIMIT_WIDE = 56 * 1024 * 1024

NT_DIMS = (((1,), (1,)), ((), ()))
TN_DIMS = (((0,), (0,)), ((), ()))


def _params(*sem, vmem=VMEM_LIMIT):
    return pltpu.CompilerParams(dimension_semantics=sem, vmem_limit_bytes=vmem)


def _tile(n, pref):
    return pref if n % pref == 0 else n


def _adaln_kernel(ct_ref, wa_ref, wb_ref, b_ref, o_ref, *, n_rows):
    k = pl.program_id(1)
    a = ct_ref[...]
    a = a * jax.nn.sigmoid(a)
    half = wa_ref.shape[0]
    rows = [jnp.sum(wa_ref[...] * a[:half, r:r + 1], axis=0, keepdims=True)
            + jnp.sum(wb_ref[...] * a[half:, r:r + 1], axis=0, keepdims=True) for r in range(n_rows)]
    rows.append(jnp.zeros((o_ref.shape[0] - n_rows, wa_ref.shape[1]), F32))
    part = jnp.concatenate(rows, axis=0)

    @pl.when(k == 0)
    def _():
        valid = lax.broadcasted_iota(jnp.int32, part.shape, 0) < n_rows
        o_ref[...] = part + jnp.where(valid, b_ref[...], 0.0)

    @pl.when(k > 0)
    def _():
        o_ref[...] += part


def _adaln_all(cond_t, n_rows, ada_w, ada_b):
    depth, d, n6 = ada_w.shape
    tk = _tile(d, 256)
    return pl.pallas_call(
        functools.partial(_adaln_kernel, n_rows=n_rows),
        grid=(depth, d // tk),
        in_specs=[pl.BlockSpec((tk, 8), lambda l, k: (k, 0)),
                  pl.BlockSpec((None, tk // 2, n6), lambda l, k: (l, 2 * k, 0)),
                  pl.BlockSpec((None, tk // 2, n6), lambda l, k: (l, 2 * k + 1, 0)),
                  pl.BlockSpec((None, 1, n6), lambda l, k: (l, 0, 0))],
        out_specs=pl.BlockSpec((None, 8, n6), lambda l, k: (l, 0, 0)),
        out_shape=jax.ShapeDtypeStruct((depth, 8, n6), F32),
        compiler_params=_params("parallel", "arbitrary"),
        name="adaln",
    )(cond_t, ada_w, ada_w, ada_b.reshape(depth, 1, n6))


def _norm_mod_rows(x, g, sh, sc):
    ms = jnp.mean(x * x, axis=-1, keepdims=True)
    y = x * lax.rsqrt(ms + NORM_EPS) * g
    return (y * (1.0 + sc) + sh).astype(BF16)


def _norm_mod(x_ref, g_ref, sh_ref, sc_ref):
    return _norm_mod_rows(x_ref[...], g_ref[...], sh_ref[...], sc_ref[...])


def _inproj_kernel(x_ref, g_ref, sh_ref, sc_ref, w_ref, cos_ref, sin_ref, o_ref, hs_ref, *,
                   n_q, n_qk, rope):
    j = pl.program_id(2)

    @pl.when(j == 0)
    def _():
        hs_ref[...] = _norm_mod(x_ref, g_ref, sh_ref, sc_ref)

    tm, tn = o_ref.shape
    rc = min(tm, 256)

    def by_row_chunks(epilogue):
        for r in range(tm // rc):
            rows = pl.ds(r * rc, rc)
            acc = jnp.dot(hs_ref[rows, :], w_ref[...], preferred_element_type=F32)
            o_ref[rows, :] = epilogue(acc, rows).astype(o_ref.dtype)

    def qk_epilogue(a, rows):
        if rope:
            lane = lax.broadcasted_iota(jnp.int32, a.shape, 1)
            first = (lane & 31) < 16
            partner = jnp.where(first, pltpu.roll(a, tn - 16, 1), pltpu.roll(a, 16, 1))
            reps = tn // LANES
            a = (a * jnp.tile(cos_ref[rows, :], (1, reps))
                 + partner * jnp.tile(sin_ref[rows, :], (1, reps)))
        return jnp.where(j < n_q, a * DIFF_SCALE, a)

    @pl.when(j < n_qk)
    def _():
        by_row_chunks(qk_epilogue)

    @pl.when(j >= n_qk)
    def _():
        by_row_chunks(lambda a, rows: jax.nn.gelu(a))


def _inproj(x, g, sh, sc, w, cos_t, sin_t, rope):
    b, s, d = x.shape
    n = w.shape[1]
    tm = _tile(s, 1024)
    tn = 512
    kern = functools.partial(_inproj_kernel, n_q=Q_COLS // tn, n_qk=2 * Q_COLS // tn, rope=rope)
    return pl.pallas_call(
        kern,
        grid=(b, s // tm, n // tn),
        in_specs=[pl.BlockSpec((None, tm, d), lambda bb, i, j: (bb, i, 0)),
                  pl.BlockSpec((1, d), lambda bb, i, j: (0, 0)),
                  pl.BlockSpec((None, 1, d), lambda bb, i, j: (bb, 0, 0)),
                  pl.BlockSpec((None, 1, d), lambda bb, i, j: (bb, 0, 0)),
                  pl.BlockSpec((d, tn), lambda bb, i, j: (0, j)),
                  pl.BlockSpec((tm, LANES), lambda bb, i, j: (i, 0)),
                  pl.BlockSpec((tm, LANES), lambda bb, i, j: (i, 0))],
        out_specs=[pl.BlockSpec((None, tm, tn), lambda bb, i, j: (bb, i, j)),
                   pl.BlockSpec((None, tm, d), lambda bb, i, j: (bb, i, 0))],
        out_shape=[jax.ShapeDtypeStruct((b, s, n), BF16), jax.ShapeDtypeStruct((b, s, d), BF16)],
        compiler_params=_params("parallel", "parallel", "arbitrary"),
        name="inproj",
    )(x, g, sh, sc, w, cos_t, sin_t)


def _inproj_nt_kernel(*refs, tl, conv, normalized):
    if normalized:
        hs_ref, wt_ref, o_ref = refs
    else:
        x_ref, g_ref, sh_ref, sc_ref, wt_ref = refs[:5]
        if conv:
            cw_ref, halo_ref, o_ref, hs_ref = refs[5:]
        else:
            o_ref, hs_ref = refs[5:]

        @pl.when(pl.program_id(2) == 0)
        def _():
            hs_ref[...] = _norm_mod(x_ref, g_ref, sh_ref, sc_ref)

    n_chunks = o_ref.shape[0]
    if not conv:
        acc = lax.dot_general(wt_ref[...], hs_ref[...], NT_DIMS, preferred_element_type=F32)
        for c in range(n_chunks):
            o_ref[c] = acc[:, c * tl:(c + 1) * tl].astype(o_ref.dtype)
        return

    acc = jnp.concatenate(
        [lax.dot_general(wt_ref[...], hs_ref[c * tl:(c + 1) * tl, :], NT_DIMS, preferred_element_type=F32)
         for c in range(n_chunks)], axis=1)
    tn, tm = acc.shape
    halo = halo_ref[...]
    col = lax.broadcasted_iota(jnp.int32, halo.shape, 1)
    tile = pl.program_id(1)
    before = jnp.sum(jnp.where(col == 2 * tile, halo, 0.0), axis=1, keepdims=True)
    after = jnp.sum(jnp.where(col == 2 * tile + 1, halo, 0.0), axis=1, keepdims=True)
    lane = lax.broadcasted_iota(jnp.int32, (tn, LANES), 1)
    prev = pltpu.roll(acc, 1, 1)
    prev = jnp.concatenate([jnp.where(lane == 0, before, prev[:, :LANES]), prev[:, LANES:]], axis=1)
    nxt = pltpu.roll(acc, tm - 1, 1)
    nxt = jnp.concatenate([nxt[:, :tm - LANES],
                           jnp.where(lane == LANES - 1, after, nxt[:, tm - LANES:])], axis=1)

    def tap(k):
        return jnp.tile(cw_ref[:, k * LANES:(k + 1) * LANES], (1, tm // LANES))

    out = tap(3) + prev * tap(0) + acc * tap(1) + nxt * tap(2)
    for c in range(n_chunks):
        o_ref[c] = out[:, c * tl:(c + 1) * tl].astype(o_ref.dtype)


def _inproj_nt(x, g, sh, sc, wt, tl, conv_params=None, halo=None):
    b, s, d = x.shape
    n = wt.shape[0]
    tm = _tile(s, 1024)
    tn = 512
    conv = conv_params is not None
    normalized = g is None
    assert not (normalized and conv)
    kern = functools.partial(_inproj_nt_kernel, tl=tl, conv=conv, normalized=normalized)
    row_spec = pl.BlockSpec((None, tm, d), lambda bb, i, j: (bb, i, 0))
    w_spec = pl.BlockSpec((tn, d), lambda bb, i, j: (j, 0))
    if normalized:
        in_specs, args, scratch = [row_spec, w_spec], [x, wt], []
    else:
        in_specs = [row_spec,
                    pl.BlockSpec((1, d), lambda bb, i, j: (0, 0)),
                    pl.BlockSpec((None, 1, d), lambda bb, i, j: (bb, 0, 0)),
                    pl.BlockSpec((None, 1, d), lambda bb, i, j: (bb, 0, 0)),
                    w_spec]
        args, scratch = [x, g, sh, sc, wt], [pltpu.VMEM((tm, d), BF16)]
    if conv:
        in_specs += [pl.BlockSpec((tn, 4 * LANES), lambda bb, i, j: (j, 0)),
                     pl.BlockSpec((None, tn, halo.shape[2]), lambda bb, i, j: (bb, j, 0))]
        args += [conv_params, halo]
    return pl.pallas_call(
        kern,
        grid=(b, s // tm, n // tn),
        in_specs=in_specs,
        out_specs=pl.BlockSpec((None, tm // tl, tn, tl), lambda bb, i, j: (bb, i, j, 0)),
        out_shape=jax.ShapeDtypeStruct((b, s // tl, n, tl), BF16),
        scratch_shapes=scratch,
        compiler_params=_params("parallel", "parallel", "arbitrary"),
        name="inproj_nt",
    )(*args)


def _ffn_up_kernel(hs_ref, wg_ref, wu_ref, o_ref):
    hs = hs_ref[...]
    gate = jnp.dot(hs, wg_ref[...], preferred_element_type=F32)
    up = jnp.dot(hs, wu_ref[...], preferred_element_type=F32)
    o_ref[...] = (gate * jax.nn.sigmoid(gate) * up).astype(o_ref.dtype)


def _ffn_up(hs, wg, wu):
    b, s, d = hs.shape
    n = wg.shape[1]
    tm = _tile(s, 1024)
    tn = 512
    return pl.pallas_call(
        _ffn_up_kernel,
        grid=(b, s // tm, n // tn),
        in_specs=[pl.BlockSpec((None, tm, d), lambda bb, i, j: (bb, i, 0)),
                  pl.BlockSpec((d, tn), lambda bb, i, j: (0, j)),
                  pl.BlockSpec((d, tn), lambda bb, i, j: (0, j))],
        out_specs=pl.BlockSpec((None, tm, tn), lambda bb, i, j: (bb, i, j)),
        out_shape=jax.ShapeDtypeStruct((b, s, n), BF16),
        compiler_params=_params("parallel", "parallel", "parallel"),
        name="ffn_up",
    )(hs, wg, wu)


def _proj_res_kernel(*refs, ksizes, transposed, next_norm):
    n = len(ksizes)
    a_refs = refs[:n]
    if next_norm:
        w_ref, x_ref, gate_ref, ng_ref, nsh_ref, nsc_ref, o_ref, hs_ref = refs[n:]
    else:
        w_ref, x_ref, gate_ref, o_ref = refs[n:]

    def emit(rows, acc):
        y = x_ref[rows, :] + gate_ref[...] * acc
        o_ref[rows, :] = y
        if next_norm:
            hs_ref[rows, :] = _norm_mod_rows(y, ng_ref[...], nsh_ref[...], nsc_ref[...])

    if transposed:
        (a_ref,) = a_refs
        tl = a_ref.shape[2]
        for c in range(a_ref.shape[0]):
            emit(slice(c * tl, (c + 1) * tl),
                 lax.dot_general(a_ref[c], w_ref[...], TN_DIMS, preferred_element_type=F32))
        return
    acc = None
    off = 0
    for a_ref, ks in zip(a_refs, ksizes):
        part = jnp.dot(a_ref[...], w_ref[off:off + ks, :], preferred_element_type=F32)
        acc = part if acc is None else acc + part
        off += ks
    emit(slice(None), acc)


def _proj_res(a_list, w, x, gate, transposed=False, tm_pref=512, tn_pref=512, vmem=VMEM_LIMIT, next_norm=None):
    b, s, d = x.shape
    ksizes = tuple(a.shape[2] for a in a_list)
    ktot = sum(ksizes)
    tm = _tile(s, tm_pref)
    tn = _tile(d, tn_pref)
    if transposed:
        tl = a_list[0].shape[3]
        a_specs = [pl.BlockSpec((None, tm // tl, ktot, tl), lambda bb, i, j: (bb, i, 0, 0))]
    else:
        a_specs = [pl.BlockSpec((None, tm, ks), lambda bb, i, j: (bb, i, 0)) for ks in ksizes]
    in_specs = a_specs + [pl.BlockSpec((ktot, tn), lambda bb, i, j: (0, j)),
                          pl.BlockSpec((None, tm, tn), lambda bb, i, j: (bb, i, j)),
                          pl.BlockSpec((None, 1, tn), lambda bb, i, j: (bb, 0, j))]
    args = [*a_list, w, x, gate]
    out_spec = pl.BlockSpec((None, tm, tn), lambda bb, i, j: (bb, i, j))
    out_specs, out_shape = out_spec, jax.ShapeDtypeStruct((b, s, d), F32)
    if next_norm is not None:
        assert tn == d
        vmem = VMEM_LIMIT_WIDE
        in_specs += [pl.BlockSpec((1, d), lambda bb, i, j: (0, 0)),
                     pl.BlockSpec((None, 1, d), lambda bb, i, j: (bb, 0, 0)),
                     pl.BlockSpec((None, 1, d), lambda bb, i, j: (bb, 0, 0))]
        args += list(next_norm)
        out_specs, out_shape = [out_spec, out_spec], [out_shape, jax.ShapeDtypeStruct((b, s, d), BF16)]
    kern = functools.partial(_proj_res_kernel, ksizes=ksizes, transposed=transposed,
                             next_norm=next_norm is not None)
    return pl.pallas_call(
        kern,
        grid=(b, s // tm, d // tn),
        in_specs=in_specs,
        out_specs=out_specs,
        out_shape=out_shape,
        compiler_params=_params("parallel", "parallel", "parallel", vmem=vmem),
        name="proj_res",
    )(*args)


def _attn_kernel(*refs, n_chunks, lam_init):
    if n_chunks:
        lam_ref, q_ref, kc_ref, vct_ref, k_ref, vt_ref, g_ref, o_ref, acc_ref = refs[:9]
        n_slots = (len(refs) - 9) // 2
        slots = tuple(zip(refs[9:9 + n_slots], refs[9 + n_slots:]))
    else:
        lam_ref, q_ref, kc_ref, vct_ref, g_ref, o_ref, acc_ref = refs
    n_sub = acc_ref.shape[0]
    tq = acc_ref.shape[-1]
    dv = DIFF_V_DIM
    q = q_ref[...]
    qm = [(q[s * tq:(s + 1) * tq, :DIFF_QK_DIM], q[s * tq:(s + 1) * tq, DIFF_QK_DIM:]) for s in range(n_sub)]

    def scores(kblk, sub):
        s_pair = tuple(lax.dot_general(kblk[:, m * DIFF_QK_DIM:(m + 1) * DIFF_QK_DIM], qm[sub][m], NT_DIMS,
                                       preferred_element_type=F32) for m in range(2))
        return s_pair, tuple(jnp.max(s, axis=0, keepdims=True) for s in s_pair)

    def with_ones(vtblk):
        return jnp.concatenate([vtblk, jnp.ones((ONES_ROWS, vtblk.shape[1]), BF16)], axis=0)

    def absorb(scored, vext, m_pair, sub):
        s_pair, smax_pair = scored
        out = []
        for m in range(2):
            m_old = m_pair[m]
            m_new = jnp.maximum(m_old, smax_pair[m])
            alpha = jnp.exp(m_old - m_new)
            p = jnp.exp((s_pair[m] - m_new).astype(BF16))
            pv = jnp.dot(vext, p, preferred_element_type=F32)
            acc_ref[sub, m] = alpha * acc_ref[sub, m] + pv
            out.append(m_new)
        return tuple(out)

    def store(slot, scored):
        for m in range(2):
            slot[0][m] = scored[0][m]
            slot[1][m] = scored[1][m]

    def load(slot):
        return (slot[0][0], slot[0][1]), (slot[1][0], slot[1][1])

    acc_ref[...] = jnp.zeros_like(acc_ref)
    init = jnp.full((1, tq), -1e30, F32)
    kc, vc = kc_ref[...], with_ones(vct_ref[0])
    m_state = tuple(absorb(scores(kc, s), vc, (init, init), s) for s in range(n_sub))
    if n_chunks:
        group = len(slots) // (2 * n_sub)
        assert n_chunks % (2 * group) == 0

        def slot(half, k, sub):
            return slots[(half * group + k) * n_sub + sub]

        def half_trip(half, base, ms, lookahead):
            ms = list(ms)
            for k in range(group):
                vext = with_ones(vt_ref[base + k])
                for s in range(n_sub):
                    if lookahead:
                        store(slot(1 - half, k, s), scores(k_ref[base + group + k], s))
                    ms[s] = absorb(load(slot(half, k, s)), vext, ms[s], s)
            return tuple(ms)

        def trip(j, ms, lookahead):
            base = 2 * group * j
            ms = half_trip(0, base, ms, True)
            return half_trip(1, base + group, ms, lookahead)

        for k in range(group):
            for s in range(n_sub):
                store(slot(0, k, s), scores(k_ref[k], s))
        n_trips = n_chunks // (2 * group)
        unroll = next(u for u in (5, 4, 3, 2, 1) if (n_trips - 1) % u == 0)
        m_state = lax.fori_loop(0, n_trips - 1, lambda j, ms: trip(j, ms, True), m_state, unroll=unroll)
        m_state = trip(n_trips - 1, m_state, False)

    lp = lam_ref[...]
    lam = (jnp.exp(jnp.sum(lp[0:1] * lp[1:2], axis=-1, keepdims=True))
           - jnp.exp(jnp.sum(lp[2:3] * lp[3:4], axis=-1, keepdims=True)) + lam_init)
    for s in range(n_sub):
        acc0, acc1 = acc_ref[s, 0], acc_ref[s, 1]
        o = acc0[:dv] / acc0[dv:dv + 1] - lam * (acc1[:dv] / acc1[dv:dv + 1])
        ot = o.T
        ms = jnp.mean(ot * ot, axis=-1, keepdims=True)
        on = ot * lax.rsqrt(ms + SUBLN_EPS) * g_ref[...] * (1.0 - lam_init)
        o_ref[s * tq:(s + 1) * tq, :] = on.astype(o_ref.dtype)


def _attention(lam_p, subln, q_arr, kc_arr, vct_arr, k4=None, vt4=None, *, lam_init):
    b, sq = q_arr.shape[0], q_arr.shape[1]
    sc = kc_arr.shape[1]
    h = DIFF_HEADS
    dv = DIFF_V_DIM
    n_sub = next(n for n in (8, 4, 2, 1) if sq % (n * ATTN_SUB) == 0)
    tq = n_sub * ATTN_SUB
    n_chunks = 0 if k4 is None else k4.shape[1]
    in_specs = [pl.BlockSpec((4, DIFF_QK_DIM), lambda bb, hh, i: (0, 0)),
                pl.BlockSpec((None, tq, dv), lambda bb, hh, i: (bb, i, hh)),
                pl.BlockSpec((None, sc, dv), lambda bb, hh, i: (bb, 0, h + hh)),
                pl.BlockSpec((None, 1, dv, sc), lambda bb, hh, i: (bb, 0, hh, 0))]
    args = [lam_p, q_arr, kc_arr, vct_arr]
    if n_chunks:
        tk = k4.shape[2]
        in_specs += [pl.BlockSpec((None, n_chunks, tk, dv), lambda bb, hh, i: (bb, 0, 0, h + hh)),
                     pl.BlockSpec((None, n_chunks, dv, tk), lambda bb, hh, i: (bb, 0, hh, 0))]
        args += [k4, vt4]
    in_specs.append(pl.BlockSpec((1, dv), lambda bb, hh, i: (0, 0)))
    args.append(subln)
    kern = functools.partial(_attn_kernel, n_chunks=n_chunks, lam_init=lam_init)
    scratch = [pltpu.VMEM((n_sub, 2, dv + ONES_ROWS, ATTN_SUB), F32)]
    if n_chunks:
        n_slots = 2 * n_sub
        scratch += [pltpu.VMEM((2, tk, ATTN_SUB), F32) for _ in range(n_slots)]
        scratch += [pltpu.VMEM((2, 1, ATTN_SUB), F32) for _ in range(n_slots)]
    return pl.pallas_call(
        kern,
        grid=(b, h, sq // tq),
        in_specs=in_specs,
        out_specs=pl.BlockSpec((None, tq, dv), lambda bb, hh, i: (bb, i, hh)),
        out_shape=jax.ShapeDtypeStruct((b, sq, A_WIDTH), BF16),
        scratch_shapes=scratch,
        compiler_params=_params("parallel", "parallel", "parallel"),
        name="diff_attn",
    )(*args)


def _sgu_kernel(u_ref, g_ref, ng_ref, nb_ref, w_ref, bs_ref, o_ref):
    for gi in range(SGU_GROUPS):
        cols = slice(gi * SGU_CH, (gi + 1) * SGU_CH)
        w = w_ref[gi]
        for c in range(u_ref.shape[0] // SGU_CHUNK):
            sl = slice(c * SGU_CHUNK, (c + 1) * SGU_CHUNK)
            gg = g_ref[sl, cols].astype(F32)
            mu = jnp.mean(gg, axis=-1, keepdims=True)
            dev = gg - mu
            var = jnp.mean(dev * dev, axis=-1, keepdims=True)
            vv = dev * lax.rsqrt(var + LN_EPS) * ng_ref[gi] + nb_ref[gi]
            mixed = jnp.dot(w, vv.astype(BF16), preferred_element_type=F32) + bs_ref[gi]
            o_ref[sl, cols] = (u_ref[sl, cols].astype(F32) * mixed).astype(o_ref.dtype)


def _sgu(qkug, norm_g, norm_b, w_s, b_s):
    b, s = qkug.shape[0], qkug.shape[1]
    tm = _tile(s, 1024)
    gcount = SGU_GROUPS
    ublk = 2 * Q_COLS // B_WIDTH
    full = lambda shape: pl.BlockSpec(shape, lambda bb, i: (0,) * len(shape))
    return pl.pallas_call(
        _sgu_kernel,
        grid=(b, s // tm),
        in_specs=[pl.BlockSpec((None, tm, B_WIDTH), lambda bb, i: (bb, i, ublk)),
                  pl.BlockSpec((None, tm, B_WIDTH), lambda bb, i: (bb, i, ublk + 1)),
                  full((gcount, 1, SGU_CH)), full((gcount, 1, SGU_CH)),
                  full((gcount, SGU_CHUNK, SGU_CHUNK)), full((gcount, SGU_CHUNK, 1))],
        out_specs=pl.BlockSpec((None, tm, B_WIDTH), lambda bb, i: (bb, i, 0)),
        out_shape=jax.ShapeDtypeStruct((b, s, B_WIDTH), BF16),
        compiler_params=_params("parallel", "parallel"),
        name="sgu",
    )(qkug, qkug, norm_g.reshape(gcount, 1, SGU_CH), norm_b.reshape(gcount, 1, SGU_CH),
      w_s.astype(BF16), b_s.reshape(gcount, SGU_CHUNK, 1))


def _filter_mlp_kernel(f_ref, w1_ref, b1_ref, w2_ref, b2_ref, w3_ref, b3_ref, fr_ref, o_ref):
    def lin(a, w_ref, b_ref):
        return jnp.dot(a, w_ref[...], preferred_element_type=F32, precision=HIGHEST) + b_ref[...]
    fr = fr_ref[...]
    hcur = jnp.sin(fr[0:1] * lin(f_ref[...], w1_ref, b1_ref))
    hcur = jnp.sin(fr[1:2] * lin(hcur, w2_ref, b2_ref))
    o_ref[...] = jnp.sin(fr[2:3] * lin(hcur, w3_ref, b3_ref))


def _filter_mlp(feats, w1, b1, w2, b2, w3, b3, freq):
    rows, emb = feats.shape
    hid = FILTER_HIDDEN
    tr = _tile(rows, 2048)
    full = lambda shape: pl.BlockSpec(shape, lambda i: (0,) * len(shape))
    return pl.pallas_call(
        _filter_mlp_kernel,
        grid=(rows // tr,),
        in_specs=[pl.BlockSpec((tr, emb), lambda i: (i, 0)),
                  full((emb, hid)), full((1, hid)), full((hid, hid)), full((1, hid)),
                  full((hid, hid)), full((1, hid)), full((3, hid))],
        out_specs=pl.BlockSpec((tr, hid), lambda i: (i, 0)),
        out_shape=jax.ShapeDtypeStruct((rows, hid), F32),
        compiler_params=_params("parallel"),
        name="filter_mlp",
    )(feats, w1, b1.reshape(1, hid), w2, b2.reshape(1, hid), w3, b3.reshape(1, hid), freq)


def _filter_raw_kernel(w4t_ref, h_ref, t_ref, delta_ref, o_ref, *, zero_tile):
    rt = pl.program_id(2)

    def split(a):
        hi = a.astype(BF16)
        return hi, (a - hi.astype(F32)).astype(BF16)

    def nt(a, bm):
        return lax.dot_general(a, bm, NT_DIMS, preferred_element_type=F32)

    w_hi, w_lo = split(w4t_ref[...])
    h_hi, h_lo = split(h_ref[...])
    raw = nt(w_hi, h_hi) + (nt(w_hi, h_lo) + nt(w_lo, h_hi))
    raw = raw * jnp.exp(-(delta_ref[...] * t_ref[...]))
    for c in range(o_ref.shape[0]):
        o_ref[c] = raw[:, c * DFT_MINOR:(c + 1) * DFT_MINOR].astype(o_ref.dtype)

    @pl.when(rt == zero_tile)
    def _():
        col = lax.broadcasted_iota(jnp.int32, (raw.shape[0], DFT_MINOR), 1)
        o_ref[0] = jnp.where(col == 0, 0.0, raw[:, :DFT_MINOR]).astype(o_ref.dtype)


def _filter_raw(w4t, h3, t_row, deltas, seq):
    c = w4t.shape[2]
    rows = h3.shape[0]
    tr = _tile(rows // 2, 2048)
    tc = _tile(c, 512)
    half_tiles = seq // tr
    kern = functools.partial(_filter_raw_kernel, zero_tile=half_tiles)
    return pl.pallas_call(
        kern,
        grid=(HYENA_ORDER, c // tc, rows // tr),
        in_specs=[pl.BlockSpec((None, None, tc, FILTER_HIDDEN),
                               lambda n, ci, rt: (n, rt // half_tiles, ci, 0)),
                  pl.BlockSpec((tr, FILTER_HIDDEN), lambda n, ci, rt: (rt, 0)),
                  pl.BlockSpec((1, tr), lambda n, ci, rt: (0, rt)),
                  pl.BlockSpec((tc, 1), lambda n, ci, rt: (ci, 0))],
        out_specs=pl.BlockSpec((None, tr // DFT_MINOR, tc, DFT_MINOR), lambda n, ci, rt: (n, rt, ci, 0)),
        out_shape=jax.ShapeDtypeStruct((HYENA_ORDER, rows // DFT_MINOR, c, DFT_MINOR), BF16),
        compiler_params=_params("parallel", "parallel", "parallel"),
        name="filter_raw",
    )(w4t, h3, t_row, deltas)


def _cmul(ar, ai, br, bi):
    return ar * br - ai * bi, ar * bi + ai * br


def _store_complex(ref, c, n1, re, im):
    r0 = pl.multiple_of(c * n1, n1)
    ref[pl.ds(r0, n1), :DFT_MINOR] = re.astype(ref.dtype)
    ref[pl.ds(r0, n1), DFT_MINOR:] = im.astype(ref.dtype)


def _load_complex(ref, c, n1):
    r0 = pl.multiple_of(c * n1, n1)
    tile = ref[pl.ds(r0, n1), :]
    return tile[:, :DFT_MINOR], tile[:, DFT_MINOR:]


def _filter_spec_kernel(k_ref, f1_ref, twr_ref, twi_ref, f2_ref, o_ref, a2_ref, *, inv_n):
    tc, n1, n2 = k_ref.shape

    def left(c, carry):
        k = k_ref[c].astype(F32)
        nrm = jnp.sum(jnp.sum(jnp.abs(k), axis=1, keepdims=True), axis=0, keepdims=True)
        kn = (k * (inv_n / nrm)).astype(BF16)
        a = jnp.dot(f1_ref[...], kn, preferred_element_type=F32)
        ar, ai = _cmul(a[:n1], a[n1:], twr_ref[...], twi_ref[...])
        _store_complex(a2_ref, c, n1, ar, ai)
        return carry

    lax.fori_loop(0, tc, left, 0, unroll=True)
    z = jnp.dot(a2_ref[...], f2_ref[...], preferred_element_type=F32).reshape(tc, n1, 2 * n2)
    o_ref[:, 0] = z[:, :, :n2].astype(o_ref.dtype)
    o_ref[:, 1] = z[:, :, n2:].astype(o_ref.dtype)


def _filter_spec(k2, f1_full, twr, twi, f2e):
    nc, n1, n2 = k2.shape
    tc = 16
    kern = functools.partial(_filter_spec_kernel, inv_n=1.0 / (n1 * n2))
    full = lambda shape: pl.BlockSpec(shape, lambda i: (0,) * len(shape))
    return pl.pallas_call(
        kern,
        grid=(nc // tc,),
        in_specs=[pl.BlockSpec((tc, n1, n2), lambda i: (i, 0, 0)),
                  full((2 * n1, n1)), full((n1, n2)), full((n1, n2)), full((2 * n2, 2 * n2))],
        out_specs=pl.BlockSpec((tc, 2, n1, n2), lambda i: (i, 0, 0, 0)),
        out_shape=jax.ShapeDtypeStruct((nc, 2, n1, n2), BF16),
        scratch_shapes=[pltpu.VMEM((tc * n1, 2 * n2), BF16)],
        compiler_params=_params("parallel"),
        name="filter_spec",
    )(k2, f1_full, twr, twi, f2e)


def _hyena_kernel(zv_ref, z1_ref, z2_ref, kf_ref, fb_ref, f1_ref, twr_ref, twi_ref, f2_ref, f2c_ref,
                  g1_ref, o_ref, sig_ref, a2_ref, z_ref):
    nb, hr, tc, n2 = zv_ref.shape
    n1 = 2 * hr
    for part, ref in enumerate((zv_ref, z1_ref, z2_ref)):
        for bb in range(nb):
            sig_ref[part, bb] = pltpu.einshape("tcl->ctl", ref[bb].astype(F32))

    for n in range(HYENA_ORDER):
        def fwd_left(c, carry):
            xs = jnp.concatenate([sig_ref[0, 0, c], sig_ref[0, 1, c]], axis=0).astype(BF16)
            a = jnp.dot(f1_ref[...], xs, preferred_element_type=F32)
            ar, ai = _cmul(a[:n1], a[n1:], twr_ref[...], twi_ref[...])
            _store_complex(a2_ref, c, n1, ar, ai)
            return carry

        lax.fori_loop(0, tc, fwd_left, 0, unroll=True)
        z_ref[...] = jnp.dot(a2_ref[...], f2_ref[...], preferred_element_type=F32)

        def spectrum(c, carry):
            zr, zi = _load_complex(z_ref, c, n1)
            wr, wi = _cmul(zr, zi, kf_ref[n, c, 0].astype(F32), kf_ref[n, c, 1].astype(F32))
            _store_complex(a2_ref, c, n1, wr, wi)
            return carry

        lax.fori_loop(0, tc, spectrum, 0, unroll=True)
        z_ref[...] = jnp.dot(a2_ref[...], f2c_ref[...], preferred_element_type=F32)

        def inv_left(c, carry):
            br, bi = _load_complex(z_ref, c, n1)
            br, bi = _cmul(br, bi, twr_ref[...], -twi_ref[...])
            bs = jnp.concatenate([br, bi], axis=0).astype(BF16)
            y = jnp.dot(g1_ref[...], bs, preferred_element_type=F32)
            fb = fb_ref[pl.ds(c, 1), n:n + 1]
            for bb in range(nb):
                sig_ref[0, bb, c] = sig_ref[1 + n, bb, c] * (y[bb * hr:(bb + 1) * hr] + sig_ref[0, bb, c] * fb)
            return carry

        lax.fori_loop(0, tc, inv_left, 0, unroll=True)

    for bb in range(nb):
        o_ref[bb] = pltpu.einshape("ctl->tcl", sig_ref[0, bb]).astype(o_ref.dtype)


def _hyena_core(z4, kf, fbias, f1h, twr, twi, f2e, f2c, g1e):
    nb, hr, c3, n2 = z4.shape
    c = c3 // 3
    n1 = 2 * hr
    tc = 8
    nct = c // tc
    full = lambda shape: pl.BlockSpec(shape, lambda i: (0,) * len(shape))
    zspec = lambda part: pl.BlockSpec((nb, hr, tc, n2), lambda i: (0, 0, i + part * nct, 0))
    return pl.pallas_call(
        _hyena_kernel,
        grid=(nct,),
        in_specs=[zspec(0), zspec(1), zspec(2),
                  pl.BlockSpec((HYENA_ORDER, tc, 2, n1, n2), lambda i: (0, i, 0, 0, 0)),
                  pl.BlockSpec((tc, HYENA_ORDER), lambda i: (i, 0)),
                  full((2 * n1, n1)), full((n1, n2)), full((n1, n2)),
                  full((2 * n2, 2 * n2)), full((2 * n2, 2 * n2)), full((n1, 2 * n1))],
        out_specs=pl.BlockSpec((nb, hr, tc, n2), lambda i: (0, 0, i, 0)),
        out_shape=jax.ShapeDtypeStruct((nb, hr, c, n2), BF16),
        scratch_shapes=[pltpu.VMEM((3, nb, tc, hr, n2), F32),
                        pltpu.VMEM((tc * n1, 2 * n2), BF16),
                        pltpu.VMEM((tc * n1, 2 * n2), F32)],
        compiler_params=_params("parallel"),
        name="hyena_core",
    )(z4, z4, z4, kf, fbias, f1h, twr, twi, f2e, f2c, g1e)


def _dft_tables(n1):
    n2 = DFT_MINOR
    n = n1 * n2
    a1 = 2.0 * np.pi * np.outer(np.arange(n1), np.arange(n1)) / n1
    f1r, f1i = np.cos(a1), -np.sin(a1)
    a2 = 2.0 * np.pi * np.outer(np.arange(n2), np.arange(n2)) / n2
    f2r, f2i = np.cos(a2), -np.sin(a2)
    at = 2.0 * np.pi * np.outer(np.arange(n1), np.arange(n2)) / n
    twr, twi = np.cos(at), -np.sin(at)
    hr = n1 // 2
    f1_full = np.concatenate([f1r, f1i], axis=0)
    f1_half = np.block([[f1r[:, :hr], -f1i[:, :hr]], [f1i[:, :hr], f1r[:, :hr]]])
    f2e = np.block([[f2r, f2i], [-f2i, f2r]])
    f2c = np.block([[f2r, -f2i], [f2i, f2r]])
    gr, gi = f1r[:hr, :], -f1i[:hr, :]
    g1e = np.block([[gr, -gi], [gi, gr]])
    bf = lambda m: jnp.asarray(m, F32).astype(BF16)
    return dict(f1_full=bf(f1_full), f1_half=bf(f1_half), f2e=bf(f2e), f2c=bf(f2c), g1e=bf(g1e),
                twr=jnp.asarray(twr, F32), twi=jnp.asarray(twi, F32))


def _hyena_spectra(seq, fw1, fb1, fw2, fb2, fw3, fb3, ffreq, fw4, tabs):
    l = seq
    width = fw4.shape[1] // (2 * HYENA_ORDER)
    t_fwd = jnp.linspace(0.0, 1.0, l, dtype=F32)
    w = 2.0 * math.pi * jnp.arange(l, dtype=F32)[:, None] / l
    f = jnp.linspace(1e-4, FILTER_BANDS - 1, FILTER_BANDS, dtype=F32)[None, :]
    feats = jnp.concatenate([t_fwd[:, None], jnp.cos(f * w), -jnp.sin(f * w)], axis=-1)

    def two_sided(a):
        return jnp.concatenate([a, a[:1], a[1:][::-1]], axis=0)

    t_lin = two_sided(t_fwd)
    emb_pad = FILTER_HIDDEN
    feats = jnp.pad(feats, ((0, 0), (0, emb_pad - FILTER_EMB)))
    w1p = jnp.pad(fw1, ((0, emb_pad - FILTER_EMB), (0, 0)))
    h3 = two_sided(_filter_mlp(feats, w1p, fb1, fw2, fb2, fw3, fb3, ffreq))
    w4t = fw4.reshape(FILTER_HIDDEN, HYENA_ORDER, 2, width).transpose(1, 2, 3, 0)
    deltas = jnp.abs(jnp.linspace(MIN_DECAY, MAX_DECAY, width, dtype=F32))[:, None]
    raw = _filter_raw(w4t, h3, t_lin[None, :], deltas, l)
    n1 = raw.shape[1]
    k2 = raw.transpose(0, 2, 1, 3).reshape(HYENA_ORDER * width, n1, DFT_MINOR)
    kf = _filter_spec(k2, tabs["f1_full"], tabs["twr"], tabs["twi"], tabs["f2e"])
    return kf.reshape(HYENA_ORDER, width, 2, n1, DFT_MINOR)


def _rope_tables(n_tokens):
    tok = jnp.arange(n_tokens)
    row = (tok // GRID_W).astype(F32)
    col = (tok % GRID_W).astype(F32)
    half = ROPE_AXIS_DIM // 2
    inv = 1.0 / (ROPE_THETA ** (jnp.arange(0, ROPE_AXIS_DIM, 2, dtype=F32) / ROPE_AXIS_DIM))
    ang_r = row[:, None] * inv
    ang_c = col[:, None] * inv
    cos64 = jnp.concatenate([jnp.cos(ang_r), jnp.cos(ang_r), jnp.cos(ang_c), jnp.cos(ang_c)], axis=-1)
    sin64 = jnp.concatenate([-jnp.sin(ang_r), jnp.sin(ang_r), -jnp.sin(ang_c), jnp.sin(ang_c)], axis=-1)
    assert cos64.shape[1] == 4 * half == DIFF_QK_DIM
    return jnp.tile(cos64, (1, 2)), jnp.tile(sin64, (1, 2))


def _ffn(x, hs, gate, wg, wu, wd):
    return _proj_res([_ffn_up(hs, wg, wu)], wd, x, gate, tm_pref=1024, vmem=VMEM_LIMIT_WIDE)


def _even_layer(x, xc, mods, cmods, n1g, n2g, w_in, w_out, lam_p, subln, sgu_ng, sgu_nb, sgu_w, sgu_b,
                wg, wu, wd, layer_idx):
    b, s, d = x.shape
    lam_init = 0.8 - 0.6 * math.exp(-0.3 * layer_idx)
    sh1, sc1, g1, sh2, sc2, g2 = mods
    csh1, csc1, cg1, csh2, csc2, cg2 = cmods
    o_k, o_v, o_u = Q_COLS, 2 * Q_COLS, 2 * Q_COLS + A_WIDTH
    w_qkug = jnp.concatenate([w_in[:, :o_v], w_in[:, o_u:]], axis=1).astype(BF16)
    w_vt = w_in[:, o_v:o_u].T.astype(BF16)
    wo = w_out.astype(BF16)
    n1g2 = n1g.reshape(1, d)
    cos_t, sin_t = _rope_tables(s)
    sc_len = xc.shape[1]

    qkug, hs1 = _inproj(x, n1g2, sh1, sc1, w_qkug, cos_t, sin_t, rope=True)
    cqkug, chs1 = _inproj(xc, n1g2, csh1, csc1, w_qkug, cos_t[:sc_len], sin_t[:sc_len], rope=False)
    tk = _tile(s, 512)
    vt4 = _inproj_nt(hs1, None, None, None, w_vt, tk)
    cvt4 = _inproj_nt(chs1, None, None, None, w_vt, sc_len)
    k4 = qkug.reshape(b, s // tk, tk, qkug.shape[2])
    subln2 = subln.reshape(1, DIFF_V_DIM)

    a_l = _attention(lam_p, subln2, qkug, cqkug, cvt4, k4, vt4, lam_init=lam_init)
    s_l = _sgu(qkug, sgu_ng, sgu_nb, sgu_w, sgu_b)
    n2g2 = n2g.reshape(1, d)
    x, hs = _proj_res([a_l, s_l], wo, x, g1, tn_pref=d, next_norm=(n2g2, sh2, sc2))
    x = _ffn(x, hs, g2, wg, wu, wd)

    a_c = _attention(lam_p, subln2, cqkug, cqkug, cvt4, lam_init=lam_init)
    s_c = _sgu(cqkug, sgu_ng, sgu_nb, sgu_w, sgu_b)
    xc, hs_c = _proj_res([a_c, s_c], wo, xc, cg1, tn_pref=d, next_norm=(n2g2, csh2, csc2))
    xc = _ffn(xc, hs_c, cg2, wg, wu, wd)
    return x, xc


def _odd_layer(x, mods, n1g, n2g, w_in, conv_w, conv_b, fw1, fb1, fw2, fb2, fw3, fb3, ffreq, fw4, fbias,
               w_out, wg, wu, wd):
    b, s, d = x.shape
    sh1, sc1, g1, sh2, sc2, g2 = mods
    width = w_out.shape[0]
    n1 = 2 * s // DFT_MINOR
    tabs = _dft_tables(n1)
    kf = _hyena_spectra(s, fw1, fb1, fw2, fb2, fw3, fb3, ffreq, fw4, tabs)

    n1g2 = n1g.reshape(1, d)
    wt = w_in.T.astype(BF16)
    tm = _tile(s, 1024)
    n_tiles = s // tm
    x_tiles = x.reshape(b, n_tiles, tm, d)
    firsts, lasts = x_tiles[:, :, 0, :], x_tiles[:, :, tm - 1, :]
    before = jnp.concatenate([lasts[:, :1], lasts[:, :-1]], axis=1)
    after = jnp.concatenate([firsts[:, 1:], firsts[:, -1:]], axis=1)
    x_edge = jnp.stack([before, after], axis=2).reshape(b, 2 * n_tiles, d)
    z_edge = _inproj_nt(x_edge, n1g2, sh1, sc1, wt, 2 * n_tiles)
    inside = jnp.ones((2 * n_tiles,), F32).at[0].set(0.0).at[2 * n_tiles - 1].set(0.0)
    halo = z_edge[:, 0].astype(F32) * inside
    conv_params = jnp.repeat(jnp.concatenate([conv_w.T, conv_b[:, None]], axis=1), LANES, axis=1)

    z4 = _inproj_nt(x, n1g2, sh1, sc1, wt, DFT_MINOR, conv_params, halo)
    y4 = _hyena_core(z4, kf, fbias.T, tabs["f1_half"], tabs["twr"], tabs["twi"], tabs["f2e"], tabs["f2c"],
                     tabs["g1e"])
    x, hs = _proj_res([y4], w_out.astype(BF16), x, g1, transposed=True, tn_pref=d,
                      next_norm=(n2g.reshape(1, d), sh2, sc2))
    return _ffn(x, hs, g2, wg, wu, wd)


def _rms_kernel(x_ref, g_ref, o_ref):
    x = x_ref[...]
    ms = jnp.mean(x * x, axis=-1, keepdims=True)
    o_ref[...] = x * lax.rsqrt(ms + NORM_EPS) * g_ref[...]


def _final_norm(x, g):
    b, s, d = x.shape
    tm = _tile(s, 1024)
    return pl.pallas_call(
        _rms_kernel,
        grid=(b, s // tm),
        in_specs=[pl.BlockSpec((None, tm, d), lambda bb, i: (bb, i, 0)),
                  pl.BlockSpec((1, d), lambda bb, i: (0, 0))],
        out_specs=pl.BlockSpec((None, tm, d), lambda bb, i: (bb, i, 0)),
        out_shape=jax.ShapeDtypeStruct((b, s, d), F32),
        compiler_params=_params("parallel", "parallel"),
        name="final_norm",
    )(x, g.reshape(1, d))


def kernel(x, c, ctx, c_ctx, ada_w, ada_b, norm1, norm2, ffn_w_gate, ffn_w_up, ffn_w_down, e_w_in, e_w_out, e_lambda, e_subln, e_sgu_norm_g, e_sgu_norm_b, e_sgu_w, e_sgu_b, o_w_in, o_conv_w, o_conv_b, o_filt_w1, o_filt_b1, o_filt_w2, o_filt_b2, o_filt_w3, o_filt_b3, o_filt_freq, o_filt_w4, o_filt_bias, o_w_out, final_norm):
    b, s, d = x.shape
    depth = ada_w.shape[0]
    assert b == 2, "the long convolution packs exactly two batches into one complex signal"
    assert depth == 2, "odd layers here never carry the context stream"
    cond_t = jnp.zeros((d, 8), F32).at[:, :b].set(c.T).at[:, b].set(c_ctx)
    mod_all = _adaln_all(cond_t, b + 1, ada_w, ada_b)
    xc = ctx
    for i in range(depth):
        j = i // 2
        parts = jnp.split(mod_all[i], 6, axis=-1)
        mods = [p[:b, None, :] for p in parts]
        cmods = [jnp.broadcast_to(p[b:b + 1, None, :], (b, 1, d)) for p in parts]
        wg, wu, wd = (ffn_w_gate[i].astype(BF16), ffn_w_up[i].astype(BF16), ffn_w_down[i].astype(BF16))
        if i % 2 == 0:
            x, xc = _even_layer(x, xc, mods, cmods, norm1[i], norm2[i], e_w_in[j], e_w_out[j], e_lambda[j],
                                e_subln[j], e_sgu_norm_g[j], e_sgu_norm_b[j], e_sgu_w[j], e_sgu_b[j],
                                wg, wu, wd, i)
        else:
            x = _odd_layer(x, mods, norm1[i], norm2[i], o_w_in[j], o_conv_w[j], o_conv_b[j], o_filt_w1[j],
                           o_filt_b1[j], o_filt_w2[j], o_filt_b2[j], o_filt_w3[j], o_filt_b3[j],
                           o_filt_freq[j], o_filt_w4[j], o_filt_bias[j], o_w_out[j], wg, wu, wd)
    return _final_norm(x, final_norm)
```
